```python
import jax, jax.numpy as jnp
from jax import lax
import numpy as np

D_MODEL = 1024
BATCH = 16
SEQ = 256
DEPTH = 2
DEC_BATCH = 8
DEC_SEQ = 1024
PAST_LEN = 512

GRID_W = 64
N_MIXERS = 2
N_POOL_LAYERS = (DEPTH + 1) // 2
N_ATTN_LAYERS = DEPTH // 2
POOL_SIZES = (2, 4, 8, 16)
N_POOL_GROUPS = 4
POOL_GROUP_DIM = D_MODEL // N_POOL_GROUPS
N_HEADS = 16
HEAD_DIM = D_MODEL // N_HEADS
WIN_H = 8
WIN_W = 16
REL_H = 2 * WIN_H - 1
REL_W = 2 * WIN_W - 1
NEG_INF = -1e30
N_EXPERTS = 16
N_EXPERT_GROUPS = 4
EXPERTS_PER_GROUP = N_EXPERTS // N_EXPERT_GROUPS
TOPK_GROUP = 1
TOP_K = 2
D_FF_EXPERT = 512
ALPHA = (2.0 * DEPTH) ** 0.25
BETA = (8.0 * DEPTH) ** -0.25
LN_EPS = 1e-5

kernel_name = "hybrid_pool_natten_moe_diffusion_step"


def layer_norm(x, g, b):
    xf = x.astype(jnp.float32)
    mu = jnp.mean(xf, axis=-1, keepdims=True)
    var = jnp.mean(jnp.square(xf - mu), axis=-1, keepdims=True)
    return ((xf - mu) * lax.rsqrt(var + LN_EPS) * g.astype(jnp.float32) + b.astype(jnp.float32)).astype(x.dtype)


def ada_modulation(cond, w, b):
    m = jax.nn.silu(cond) @ w + b
    return jnp.split(m[:, None, :], 6, axis=-1)


def modulate(x, shift, scale):
    return x * (1.0 + scale) + shift


def post_norm(x, out, g, b):
    return layer_norm(ALPHA * x + out, g, b)


def centred_pool_minus_self(x, w):
    n = x.shape[1]
    cs = jnp.concatenate([jnp.zeros_like(x[:, :1]), jnp.cumsum(x, axis=1)], axis=1)
    t = jnp.arange(n)
    lo = jnp.clip(t - w // 2, 0, n)
    hi = jnp.clip(t - w // 2 + w, 0, n)
    cnt = (hi - lo).astype(x.dtype)
    return (cs[:, hi] - cs[:, lo]) / cnt[None, :, None] - x


def pool_mixer(h, w_groups, scale):
    b, n, _ = h.shape
    hg = h.astype(jnp.float32).reshape(b, n, N_POOL_GROUPS, POOL_GROUP_DIM)
    pooled = jnp.stack([centred_pool_minus_self(hg[:, :, g], POOL_SIZES[g]) for g in range(N_POOL_GROUPS)], axis=2)
    out = jnp.einsum('bngc,gcd->bngd', pooled.astype(h.dtype), w_groups).reshape(b, n, D_MODEL)
    return out * scale


def project_qkv(h, w_qkv):
    b, n, _ = h.shape
    qkv = (h @ w_qkv).reshape(b, n, 3, N_HEADS, HEAD_DIM)
    return qkv[:, :, 0], qkv[:, :, 1], qkv[:, :, 2]


def context_attention(q, k, v):
    b, _, n, _ = q.shape
    s = jnp.einsum('bhqd,bhkd->bhqk', q, k).astype(jnp.float32) * (HEAD_DIM ** -0.5)
    p = jax.nn.softmax(s, axis=-1).astype(v.dtype)
    return jnp.einsum('bhqk,bhkd->bqhd', p, v).reshape(b, n, D_MODEL)


def latent_neighbourhood_attention(q, k, v, ctx_k, ctx_v, rel_bias):
    b, n, _, _ = q.shape
    rows = n // GRID_W
    wh = min(WIN_H, rows)
    r = jnp.arange(rows)
    row_start = jnp.clip(r - wh // 2, 0, rows - wh)
    row_idx = row_start[:, None] + jnp.arange(wh)[None, :]
    qg = q.reshape(b, rows, GRID_W, N_HEADS, HEAD_DIM)
    kg = k.reshape(b, rows, GRID_W, N_HEADS, HEAD_DIM)[:, row_idx]
    vg = v.reshape(b, rows, GRID_W, N_HEADS, HEAD_DIM)[:, row_idx]
    col = jnp.arange(GRID_W)
    col_start = jnp.clip(col - WIN_W // 2, 0, GRID_W - WIN_W)
    col_ok = (col[None, :] >= col_start[:, None]) & (col[None, :] < col_start[:, None] + WIN_W)
    scale = HEAD_DIM ** -0.5
    s_lat = jnp.einsum('brqhd,brikhd->bhrqik', qg, kg).astype(jnp.float32) * scale
    dy = row_idx - r[:, None] + (WIN_H - 1)
    dx = jnp.clip(col[None, :] - col[:, None] + (WIN_W - 1), 0, REL_W - 1)
    bias = rel_bias.astype(jnp.float32)[:, dy[:, None, :, None], dx[None, :, None, :]]
    s_lat = jnp.where(col_ok[:, None, :], s_lat + bias[None], NEG_INF)
    s_lat = s_lat.reshape(b, N_HEADS, n, wh * GRID_W)
    qh = q.transpose(0, 2, 1, 3)
    s_ctx = jnp.einsum('bhnd,bhld->bhnl', qh, ctx_k).astype(jnp.float32) * scale
    p = jax.nn.softmax(jnp.concatenate([s_lat, s_ctx], axis=-1), axis=-1).astype(v.dtype)
    p_lat = p[..., : wh * GRID_W].reshape(b, N_HEADS, rows, GRID_W, wh, GRID_W)
    p_ctx = p[..., wh * GRID_W:]
    o_lat = jnp.einsum('bhrqik,brikhd->brqhd', p_lat, vg).reshape(b, n, N_HEADS, HEAD_DIM)
    o_ctx = jnp.einsum('bhnl,bhld->bnhd', p_ctx, ctx_v)
    return (o_lat + o_ctx).reshape(b, n, D_MODEL)


def grouped_moe(h, w_router, b_router, w_gate, w_up, w_down):
    b, n, d = h.shape
    t = h.reshape(-1, d)
    ntok = t.shape[0]
    s = jax.nn.sigmoid((t @ w_router).astype(jnp.float32))
    sel = s + b_router.astype(jnp.float32)
    grp = sel.reshape(ntok, N_EXPERT_GROUPS, EXPERTS_PER_GROUP)
    grp_score = jnp.sum(lax.top_k(grp, 2)[0], axis=-1)
    _, gidx = lax.top_k(grp_score, TOPK_GROUP)
    gmask = jnp.sum(jax.nn.one_hot(gidx, N_EXPERT_GROUPS, dtype=jnp.float32), axis=1) > 0
    emask = jnp.repeat(gmask, EXPERTS_PER_GROUP, axis=1)
    _, eidx = lax.top_k(jnp.where(emask, sel, NEG_INF), TOP_K)
    wsel = jnp.take_along_axis(s, eidx, axis=1)
    wsel = wsel / jnp.sum(wsel, axis=-1, keepdims=True)
    combine = jnp.sum(jax.nn.one_hot(eidx, N_EXPERTS, dtype=jnp.float32) * wsel[..., None], axis=1).astype(t.dtype)
    out = jnp.zeros_like(t)
    for e in range(N_EXPERTS):
        he = jax.nn.silu(t @ w_gate[e]) * (t @ w_up[e])
        out = out + combine[:, e:e + 1] * (he @ w_down[e])
    return out.reshape(b, n, d)


def setup_inputs(seed: int = 0) -> dict:
    key = jax.random.key(seed)
    ks = jax.random.split(key, 24)
    nrm = jax.random.normal
    f32 = jnp.float32
    d, e, ff = D_MODEL, N_EXPERTS, D_FF_EXPERT
    qkv_scale = jnp.concatenate([jnp.ones((2 * d,), f32), jnp.full((d,), BETA, f32)])
    return {
        "x_prompt": nrm(ks[0], (BATCH, SEQ, d), f32),
        "x_sample": nrm(ks[1], (DEC_BATCH, DEC_SEQ, d), f32),
        "cache_k": nrm(ks[2], (DEC_BATCH, N_ATTN_LAYERS, N_HEADS, PAST_LEN, HEAD_DIM), f32),
        "cache_v": nrm(ks[3], (DEC_BATCH, N_ATTN_LAYERS, N_HEADS, PAST_LEN, HEAD_DIM), f32) * BETA,
        "c": nrm(ks[4], (DEC_BATCH, d), f32),
        "c_ctx": nrm(ks[5], (d,), f32),
        "w_ada": nrm(ks[6], (DEPTH, d, 6 * d), f32) * (0.2 * d ** -0.5),
        "b_ada": nrm(ks[7], (DEPTH, 6 * d), f32) * 0.02,
        "ln1_g": 1.0 + 0.02 * nrm(ks[8], (DEPTH, d), f32),
        "ln1_b": 0.02 * nrm(ks[9], (DEPTH, d), f32),
        "ln2_g": 1.0 + 0.02 * nrm(ks[10], (DEPTH, d), f32),
        "ln2_b": 0.02 * nrm(ks[11], (DEPTH, d), f32),
        "pool_w": nrm(ks[12], (N_POOL_LAYERS, N_POOL_GROUPS, POOL_GROUP_DIM, POOL_GROUP_DIM), f32) * (BETA * POOL_GROUP_DIM ** -0.5),
        "pool_scale": 1.0 + 0.02 * nrm(ks[13], (N_POOL_LAYERS, d), f32),
        "w_qkv": nrm(ks[14], (N_ATTN_LAYERS, d, 3 * d), f32) * (d ** -0.5) * qkv_scale,
        "w_o": nrm(ks[15], (N_ATTN_LAYERS, d, d), f32) * (BETA * d ** -0.5),
        "rel_bias": 0.02 * nrm(ks[16], (N_ATTN_LAYERS, N_HEADS, REL_H, REL_W), f32),
        "w_router": nrm(ks[17], (d, e), f32) * (d ** -0.5),
        "b_router": 0.01 * nrm(ks[18], (e,), f32),
        "w_gate": nrm(ks[19], (DEPTH, e, d, ff), f32) * (BETA * d ** -0.5),
        "w_up": nrm(ks[20], (DEPTH, e, d, ff), f32) * (BETA * d ** -0.5),
        "w_down": nrm(ks[21], (DEPTH, e, ff, d), f32) * (BETA * ff ** -0.5),
    }


def reference(x_prompt, x_sample, cache_k, cache_v, c, c_ctx, w_ada, b_ada, ln1_g, ln1_b, ln2_g, ln2_b,
              pool_w, pool_scale, w_qkv, w_o, rel_bias, w_router, b_router, w_gate, w_up, w_down):
    x = x_prompt
    new_k, new_v = [], []
    for i in range(DEPTH):
        j = i // N_MIXERS
        sh1, sc1, g1, sh2, sc2, g2 = ada_modulation(c_ctx[None, :], w_ada[i], b_ada[i])
        h = modulate(x, sh1, sc1)
        if i % N_MIXERS == 0:
            out = pool_mixer(h, pool_w[j], pool_scale[j])
        else:
            q, k, v = project_qkv(h, w_qkv[j])
            k_t = k.transpose(0, 2, 1, 3)
            v_t = v.transpose(0, 2, 1, 3)
            new_k.append(k_t)
            new_v.append(v_t)
            out = context_attention(q.transpose(0, 2, 1, 3), k_t, v_t) @ w_o[j]
        x = post_norm(x, g1 * out, ln1_g[i], ln1_b[i])
        h = modulate(x, sh2, sc2)
        x = post_norm(x, g2 * grouped_moe(h, w_router, b_router, w_gate[i], w_up[i], w_down[i]), ln2_g[i], ln2_b[i])
    y_prompt = x
    new_cache_k = jnp.stack(new_k, axis=1)
    new_cache_v = jnp.stack(new_v, axis=1)

    x = x_sample
    for i in range(DEPTH):
        j = i // N_MIXERS
        sh1, sc1, g1, sh2, sc2, g2 = ada_modulation(c, w_ada[i], b_ada[i])
        h = modulate(x, sh1, sc1)
        if i % N_MIXERS == 0:
            out = pool_mixer(h, pool_w[j], pool_scale[j])
        else:
            q, k, v = project_qkv(h, w_qkv[j])
            out = latent_neighbourhood_attention(q, k, v, cache_k[:, j], cache_v[:, j], rel_bias[j]) @ w_o[j]
        x = post_norm(x, g1 * out, ln1_g[i], ln1_b[i])
        h = modulate(x, sh2, sc2)
        x = post_norm(x, g2 * grouped_moe(h, w_router, b_router, w_gate[i], w_up[i], w_down[i]), ln2_g[i], ln2_b[i])
    y_sample = x
    return (y_prompt, y_sample, new_cache_k, new_cache_v)
```

```python
import functools

import jax
import jax.numpy as jnp
from jax import lax
from jax.experimental import pallas as pl
from jax.experimental.pallas import tpu as pltpu

F32 = jnp.float32
BF16 = jnp.bfloat16

D_MODEL = 1024
BATCH = 16
SEQ = 256
DEC_BATCH = 8
DEC_SEQ = 1024
PAST_LEN = 512
GRID_W = 64
GRID_ROWS = DEC_SEQ // GRID_W
POOL_SIZES = (2, 4, 8, 16)
POOL_GROUP_DIM = D_MODEL // len(POOL_SIZES)
POOL_HALO = 8
N_HEADS = 16
HEAD_DIM = 64
WIN_H = 8
WIN_W = 16
N_EXPERTS = 16
EXPERTS_PER_GROUP = 4
N_EXPERT_GROUPS = 4
D_FF = 512
ALPHA = (2.0 * 2) ** 0.25
LN_EPS = 1e-5
NEG_INF = -1e30

T_CTX = BATCH * SEQ
T_LAT = DEC_BATCH * DEC_SEQ
T_ALL = T_CTX + T_LAT
N_COND = 16
TILE = 256
N_TILES = T_ALL // TILE
CTX_TILES = T_CTX // TILE
TILES_PER_LAT_SEQ = DEC_SEQ // TILE
MOE_TILE = 256
MOE_ROWS = 2 * T_ALL + N_EXPERTS * MOE_TILE
MOE_TILES = MOE_ROWS // MOE_TILE
LN_TILE = 512
HEAD_PAIRS = N_HEADS // 2
LANES = 128
VMEM_LIMIT = 56 * 1024 * 1024


def _cond_row(t, tile):
    ctx_tiles = T_CTX // tile
    per_seq = DEC_SEQ // tile
    return jnp.maximum(t - ctx_tiles + per_seq, 0) // per_seq


def _params(*sem):
    return pltpu.CompilerParams(dimension_semantics=sem, vmem_limit_bytes=VMEM_LIMIT)


def _ada_kernel(cond_ref, w_ref, b_ref, o_ref):
    cnd = cond_ref[...]
    act = cnd * jax.nn.sigmoid(cnd)
    o_ref[0] = jnp.dot(act, w_ref[0], precision=lax.Precision.HIGHEST,
                       preferred_element_type=F32) + b_ref[0]


def _ada(cond, w_ada, b_ada):
    depth, d, n = w_ada.shape
    bn = 1536
    return pl.pallas_call(
        _ada_kernel,
        grid=(depth, n // bn),
        in_specs=[
            pl.BlockSpec((N_COND, d), lambda i, j: (0, 0)),
            pl.BlockSpec((1, d, bn), lambda i, j: (i, 0, j)),
            pl.BlockSpec((1, 1, bn), lambda i, j: (i, 0, j)),
        ],
        out_specs=pl.BlockSpec((1, N_COND, bn), lambda i, j: (i, 0, j)),
        out_shape=jax.ShapeDtypeStruct((depth, N_COND, n), F32),
        compiler_params=_params("arbitrary", "arbitrary"),
        name="ada",
    )(cond, w_ada, b_ada.reshape(depth, 1, n))


def _post_norm(x, upd, g, b):
    y = ALPHA * x + upd
    mu = jnp.mean(y, axis=-1, keepdims=True)
    yc = y - mu
    var = jnp.mean(yc * yc, axis=-1, keepdims=True)
    return yc * lax.rsqrt(var + LN_EPS) * g + b


def _route(h2, wrt, br, route_ref):
    logits = lax.dot_general(wrt, h2, (((1,), (1,)), ((), ())), precision=lax.Precision.HIGHEST,
                             preferred_element_type=F32)
    aff = jax.nn.sigmoid(logits)
    sel = aff + br
    sel_rows = [sel[e:e + 1, :] for e in range(N_EXPERTS)]
    aff_rows = [aff[e:e + 1, :] for e in range(N_EXPERTS)]

    def group_score(g):
        r = sel_rows[g * EXPERTS_PER_GROUP:(g + 1) * EXPERTS_PER_GROUP]
        best = None
        for i in range(EXPERTS_PER_GROUP):
            for j in range(i + 1, EXPERTS_PER_GROUP):
                pair = r[i] + r[j]
                best = pair if best is None else jnp.maximum(best, pair)
        return best

    best = group_score(0)
    gidx = jnp.zeros_like(best, dtype=jnp.int32)
    for g in range(1, N_EXPERT_GROUPS):
        sc = group_score(g)
        better = sc > best
        gidx = jnp.where(better, g, gidx)
        best = jnp.where(better, sc, best)

    def pick_group(rows, j):
        out = rows[j]
        for g in range(1, N_EXPERT_GROUPS):
            out = jnp.where(gidx == g, rows[g * EXPERTS_PER_GROUP + j], out)
        return out

    cand = [pick_group(sel_rows, j) for j in range(EXPERTS_PER_GROUP)]
    cand_aff = [pick_group(aff_rows, j) for j in range(EXPERTS_PER_GROUP)]

    def argmax_first(vals):
        bv, bi, ba = vals[0], jnp.zeros_like(gidx), cand_aff[0]
        for j in range(1, EXPERTS_PER_GROUP):
            better = vals[j] > bv
            bv = jnp.where(better, vals[j], bv)
            bi = jnp.where(better, j, bi)
            ba = jnp.where(better, cand_aff[j], ba)
        return bi, ba

    i1, a1 = argmax_first(cand)
    rest = [jnp.where(i1 == j, -jnp.inf, cand[j]) for j in range(EXPERTS_PER_GROUP)]
    i2, a2 = argmax_first(rest)
    denom = a1 + a2
    base = gidx * EXPERTS_PER_GROUP
    route_ref[0, 0:1, :] = (base + i1).astype(F32)
    route_ref[0, 1:2, :] = (base + i2).astype(F32)
    route_ref[0, 2:3, :] = a1 / denom
    route_ref[0, 3:4, :] = a2 / denom
    route_ref[0, 4:8, :] = jnp.zeros((4, TILE), F32)


def _epilogue(x, mix, m, lng, lnb, wrt, br, x1_ref, h2_ref, route_ref):
    g1, sh2, sc2 = m[2:3, :], m[3:4, :], m[4:5, :]
    x1 = _post_norm(x, g1 * mix, lng, lnb)
    h2 = x1 * (1.0 + sc2) + sh2
    x1_ref[...] = x1
    h2_ref[...] = h2
    _route(h2, wrt, br, route_ref)


_EPI_IN_SPECS = [
    pl.BlockSpec((1, D_MODEL), lambda t: (0, 0)),
    pl.BlockSpec((1, D_MODEL), lambda t: (0, 0)),
    pl.BlockSpec((N_EXPERTS, D_MODEL), lambda t: (0, 0)),
    pl.BlockSpec((N_EXPERTS, 1), lambda t: (0, 0)),
]
_EPI_OUT_SPECS = [
    pl.BlockSpec((TILE, D_MODEL), lambda t: (t, 0)),
    pl.BlockSpec((TILE, D_MODEL), lambda t: (t, 0)),
    pl.BlockSpec((1, 8, TILE), lambda t: (t, 0, 0)),
]
_EPI_OUT_SHAPE = [
    jax.ShapeDtypeStruct((T_ALL, D_MODEL), F32),
    jax.ShapeDtypeStruct((T_ALL, D_MODEL), F32),
    jax.ShapeDtypeStruct((N_TILES, 8, TILE), F32),
]


def _mods_spec(layer):
    return pl.BlockSpec((1, 1, 6, D_MODEL), lambda t: (layer, _cond_row(t, TILE), 0, 0))


def _pool_kernel(xp_ref, x_ref, xn_ref, mods_ref, pw_ref, ps_ref, lng_ref, lnb_ref, wrt_ref, br_ref,
                 x1_ref, h2_ref, route_ref):
    t = pl.program_id(0)
    m = mods_ref[0, 0]
    sh1, sc1 = m[0:1, :], m[1:2, :]
    in_lat = t >= CTX_TILES
    sub = (t - CTX_TILES) % TILES_PER_LAT_SEQ
    is_first = jnp.logical_or(jnp.logical_not(in_lat), sub == 0)
    is_last = jnp.logical_or(jnp.logical_not(in_lat), sub == TILES_PER_LAT_SEQ - 1)
    seq_len = jnp.where(in_lat, DEC_SEQ, SEQ)
    pos0 = jnp.where(in_lat, sub * TILE, 0)

    x = x_ref[...]
    h = x * (1.0 + sc1) + sh1
    hp = jnp.where(is_first, 0.0, xp_ref[...] * (1.0 + sc1) + sh1)
    hn = jnp.where(is_last, 0.0, xn_ref[...] * (1.0 + sc1) + sh1)
    hext = jnp.concatenate([hp, h, hn], axis=0)
    ext = TILE + 2 * POOL_HALO
    pos = pos0 + lax.broadcasted_iota(jnp.int32, (TILE, 1), 0)

    outs = []
    for g, w in enumerate(POOL_SIZES):
        lo_c, hi_c = g * POOL_GROUP_DIM, (g + 1) * POOL_GROUP_DIM
        a = hext[:, lo_c:hi_c]
        k = 1
        while k < w:
            a = a + pltpu.roll(a, ext - k, axis=0)
            k *= 2
        off = POOL_HALO - w // 2
        win = pltpu.roll(a, ext - off, axis=0)[:TILE] if off else a[:TILE]
        lo = jnp.maximum(pos - w // 2, 0)
        hi = jnp.minimum(pos - w // 2 + w, seq_len)
        cnt = (hi - lo).astype(F32)
        pooled = win / cnt - h[:, lo_c:hi_c]
        outs.append(jnp.dot(pooled.astype(BF16), pw_ref[g], preferred_element_type=F32))
    mix = jnp.concatenate(outs, axis=1) * ps_ref[...]
    _epilogue(x, mix, m, lng_ref[...], lnb_ref[...], wrt_ref[...], br_ref[...], x1_ref, h2_ref, route_ref)


def _pool_layer(x, mods, pool_w, pool_scale, lng, lnb, wrt, br):
    halo_blocks = TILE // POOL_HALO
    last_halo = T_ALL // POOL_HALO - 1
    return pl.pallas_call(
        _pool_kernel,
        grid=(N_TILES,),
        in_specs=[
            pl.BlockSpec((POOL_HALO, D_MODEL), lambda t: (jnp.maximum(t * halo_blocks - 1, 0), 0)),
            pl.BlockSpec((TILE, D_MODEL), lambda t: (t, 0)),
            pl.BlockSpec((POOL_HALO, D_MODEL), lambda t: (jnp.minimum((t + 1) * halo_blocks, last_halo), 0)),
            _mods_spec(0),
            pl.BlockSpec((len(POOL_SIZES), POOL_GROUP_DIM, POOL_GROUP_DIM), lambda t: (0, 0, 0)),
            pl.BlockSpec((1, D_MODEL), lambda t: (0, 0)),
        ] + _EPI_IN_SPECS,
        out_specs=_EPI_OUT_SPECS,
        out_shape=_EPI_OUT_SHAPE,
        compiler_params=_params("arbitrary"),
        name="pool_mixer",
    )(x, x, x, mods, pool_w, pool_scale, lng, lnb, wrt, br)


def _route_plan(route):
    eidx = route[:, 0:2, :].astype(jnp.int32).transpose(1, 0, 2).reshape(2 * T_ALL)
    wsel = route[:, 2:4, :].transpose(1, 0, 2).reshape(2, T_ALL)
    onehot = (eidx[:, None] == jnp.arange(N_EXPERTS, dtype=jnp.int32)[None, :]).astype(jnp.int32)
    counts = jnp.sum(onehot, axis=0)
    rank = jnp.sum((jnp.cumsum(onehot, axis=0) - onehot) * onehot, axis=1)
    padded = (counts + MOE_TILE - 1) // MOE_TILE * MOE_TILE
    ends = jnp.cumsum(padded)
    starts = ends - padded
    pos = jnp.sum(onehot * starts[None, :], axis=1) + rank
    tile_start = jnp.arange(MOE_TILES, dtype=jnp.int32) * MOE_TILE
    tile_expert = jnp.minimum(jnp.sum((ends[None, :] <= tile_start[:, None]).astype(jnp.int32), axis=1),
                              N_EXPERTS - 1)
    tile_valid = jnp.clip(counts[tile_expert] - (tile_start - starts[tile_expert]), 0, MOE_TILE)
    tile_valid = jnp.where(tile_start < ends[-1], tile_valid, 0)
    used_tiles = ends[-1] // MOE_TILE
    tile_block = jnp.minimum(jnp.arange(MOE_TILES, dtype=jnp.int32), used_tiles - 1)
    return pos.astype(jnp.int32), wsel, tile_expert.astype(jnp.int32), tile_valid.astype(jnp.int32), \
        tile_block.astype(jnp.int32)


def _expert_kernel(te_ref, nv_ref, tb_ref, x_ref, wg_ref, wu_ref, wd_ref, y_ref, wgu_s, wd_s):
    i = pl.program_id(0)
    prev = te_ref[jnp.maximum(i - 1, 0)]
    changed = jnp.logical_or(i == 0, te_ref[i] != prev)

    @pl.when(changed)
    def _():
        wgu_s[:, :D_FF] = wg_ref[0].astype(BF16)
        wgu_s[:, D_FF:] = wu_ref[0].astype(BF16)
        wd_s[...] = wd_ref[0].astype(BF16)

    nv = nv_ref[i]

    @pl.when(nv > 0)
    def _():
        rows = lax.broadcasted_iota(jnp.int32, (MOE_TILE, 1), 0)
        xb = jnp.where(rows < nv, x_ref[...], 0.0).astype(BF16)
        gu = jnp.dot(xb, wgu_s[...], preferred_element_type=F32)
        gate, up = gu[:, :D_FF], gu[:, D_FF:]
        he = (gate * jax.nn.sigmoid(gate) * up).astype(BF16)
        y_ref[...] = jnp.dot(he, wd_s[...], preferred_element_type=F32)


def _experts(xs, tile_expert, tile_valid, tile_block, w_gate, w_up, w_down):
    grid_spec = pltpu.PrefetchScalarGridSpec(
        num_scalar_prefetch=3,
        grid=(MOE_TILES,),
        in_specs=[
            pl.BlockSpec((MOE_TILE, D_MODEL), lambda i, te, nv, tb: (tb[i], 0)),
            pl.BlockSpec((1, D_MODEL, D_FF), lambda i, te, nv, tb: (te[i], 0, 0)),
            pl.BlockSpec((1, D_MODEL, D_FF), lambda i, te, nv, tb: (te[i], 0, 0)),
            pl.BlockSpec((1, D_FF, D_MODEL), lambda i, te, nv, tb: (te[i], 0, 0)),
        ],
        out_specs=pl.BlockSpec((MOE_TILE, D_MODEL), lambda i, te, nv, tb: (tb[i], 0)),
        scratch_shapes=[pltpu.VMEM((D_MODEL, 2 * D_FF), BF16), pltpu.VMEM((D_FF, D_MODEL), BF16)],
    )
    return pl.pallas_call(
        _expert_kernel,
        grid_spec=grid_spec,
        out_shape=jax.ShapeDtypeStruct((MOE_ROWS, D_MODEL), F32),
        compiler_params=_params("arbitrary"),
        name="experts",
    )(tile_expert, tile_valid, tile_block, xs, w_gate, w_up, w_down)


def _dispatch(h2, pos):
    src = jnp.concatenate([h2, h2], axis=0)
    return jnp.zeros((MOE_ROWS, D_MODEL), F32).at[pos].set(src)


def _gather_pairs(y, pos):
    return jnp.take(y, pos, axis=0).reshape(2, T_ALL, D_MODEL)


def _combine_kernel(x_ref, yg_ref, w1_ref, w2_ref, mods_ref, lng_ref, lnb_ref, o_ref):
    m = mods_ref[0, 0]
    g2 = m[5:6, :]
    moe = w1_ref[...] * yg_ref[0] + w2_ref[...] * yg_ref[1]
    o_ref[...] = _post_norm(x_ref[...], g2 * moe, lng_ref[...], lnb_ref[...])


def _combine(x1, yg, wsel, mods, layer, lng, lnb):
    return pl.pallas_call(
        _combine_kernel,
        grid=(T_ALL // LN_TILE,),
        in_specs=[
            pl.BlockSpec((LN_TILE, D_MODEL), lambda t: (t, 0)),
            pl.BlockSpec((2, LN_TILE, D_MODEL), lambda t: (0, t, 0)),
            pl.BlockSpec((LN_TILE, 1), lambda t: (t, 0)),
            pl.BlockSpec((LN_TILE, 1), lambda t: (t, 0)),
            pl.BlockSpec((1, 1, 6, D_MODEL), lambda t: (layer, _cond_row(t, LN_TILE), 0, 0)),
            pl.BlockSpec((1, D_MODEL), lambda t: (0, 0)),
            pl.BlockSpec((1, D_MODEL), lambda t: (0, 0)),
        ],
        out_specs=pl.BlockSpec((LN_TILE, D_MODEL), lambda t: (t, 0)),
        out_shape=jax.ShapeDtypeStruct((T_ALL, D_MODEL), F32),
        compiler_params=_params("arbitrary"),
        name="moe_combine",
    )(x1, yg, wsel[0].reshape(T_ALL, 1), wsel[1].reshape(T_ALL, 1), mods, lng, lnb)


def _moe(x1, h2, route, mods, layer, w_gate, w_up, w_down, lng, lnb):
    pos, wsel, tile_expert, tile_valid, tile_block = _route_plan(route)
    xs = _dispatch(h2, pos)
    y = _experts(xs, tile_expert, tile_valid, tile_block, w_gate, w_up, w_down)
    yg = _gather_pairs(y, pos)
    return _combine(x1, yg, wsel, mods, layer, lng, lnb)


def _qkv_kernel(x_ref, mods_ref, w_ref, qkv_ref, nk_ref, nv_ref):
    t = pl.program_id(0)
    m = mods_ref[0, 0]
    sh1, sc1 = m[0:1, :], m[1:2, :]
    h = (x_ref[...] * (1.0 + sc1) + sh1).astype(BF16)
    r = jnp.dot(h, w_ref[...], preferred_element_type=F32)
    qkv_ref[:, :D_MODEL] = (r[:, :D_MODEL] * (HEAD_DIM ** -0.5)).astype(BF16)
    qkv_ref[:, D_MODEL:] = r[:, D_MODEL:].astype(BF16)

    @pl.when(t < CTX_TILES)
    def _():
        for out_ref, base in ((nk_ref, D_MODEL), (nv_ref, 2 * D_MODEL)):
            for p in range(HEAD_PAIRS):
                pair = r[:, base + p * LANES: base + (p + 1) * LANES]
                out_ref[0, 0, 2 * p] = pair[:, :HEAD_DIM]
                out_ref[0, 0, 2 * p + 1] = pltpu.roll(pair, HEAD_DIM, axis=1)[:, :HEAD_DIM]


def _qkv(x, mods, w_qkv):
    cache_spec = pl.BlockSpec((1, 1, N_HEADS, SEQ, HEAD_DIM),
                              lambda t: (jnp.minimum(t, CTX_TILES - 1), 0, 0, 0, 0))
    cache_shape = jax.ShapeDtypeStruct((BATCH, 1, N_HEADS, SEQ, HEAD_DIM), F32)
    return pl.pallas_call(
        _qkv_kernel,
        grid=(N_TILES,),
        in_specs=[
            pl.BlockSpec((TILE, D_MODEL), lambda t: (t, 0)),
            _mods_spec(1),
            pl.BlockSpec((D_MODEL, 3 * D_MODEL), lambda t: (0, 0)),
        ],
        out_specs=[pl.BlockSpec((TILE, 3 * D_MODEL), lambda t: (t, 0)), cache_spec, cache_spec],
        out_shape=[jax.ShapeDtypeStruct((T_ALL, 3 * D_MODEL), BF16), cache_shape, cache_shape],
        compiler_params=_params("arbitrary"),
        name="qkv",
    )(x, mods, w_qkv)


def _dot_nt(a, b):
    return lax.dot_general(a, b, (((1,), (1,)), ((), ())), preferred_element_type=F32)


def _head_masks():
    lane = lax.broadcasted_iota(jnp.int32, (1, LANES), 1)
    return lane < HEAD_DIM, lane >= HEAD_DIM


def _ctx_attn_kernel(q_ref, k_ref, v_ref, o_ref):
    left, right = _head_masks()
    for p in range(HEAD_PAIRS):
        cols = slice(p * LANES, (p + 1) * LANES)
        q2, k2, v2 = q_ref[:, cols], k_ref[:, cols], v_ref[:, cols]
        halves = []
        for mask in (left, right):
            qh = jnp.where(mask, q2, jnp.zeros_like(q2))
            s = _dot_nt(qh, k2)
            e = jnp.exp(s - jnp.max(s, axis=-1, keepdims=True))
            o2 = jnp.dot(e.astype(BF16), v2, preferred_element_type=F32)
            halves.append(o2 / jnp.sum(e, axis=-1, keepdims=True))
        o_ref[:, cols] = jnp.where(left, halves[0], halves[1]).astype(BF16)


def _ctx_attention(qkv):
    return pl.pallas_call(
        _ctx_attn_kernel,
        grid=(BATCH,),
        in_specs=[pl.BlockSpec((SEQ, D_MODEL), lambda b, j=j: (b, j)) for j in range(3)],
        out_specs=pl.BlockSpec((SEQ, D_MODEL), lambda b: (b, 0)),
        out_shape=jax.ShapeDtypeStruct((T_CTX, D_MODEL), BF16),
        compiler_params=_params("arbitrary"),
        name="ctx_attention",
    )(qkv, qkv, qkv)


_LAT_Q_BLOCK_ROWS = 4
_LAT_KEY_ROWS = ((0, 8), (0, 12), (4, 16), (8, 16))


def _lat_attn_kernel(q_ref, k_ref, v_ref, ck_ref, cv_ref, eb_ref, o_ref, bias_s):
    left, right = _head_masks()

    @pl.when(pl.program_id(1) == 0)
    def _():
        neg = jnp.full((GRID_W, LANES), NEG_INF, F32)
        for hh in range(2):
            for r in range(GRID_ROWS):
                r0 = min(max(r - WIN_H // 2, 0), GRID_ROWS - WIN_H)
                for j in range(GRID_ROWS // 2):
                    parts = []
                    for kr in (2 * j, 2 * j + 1):
                        parts.append(eb_ref[hh, kr - r + WIN_H - 1] if r0 <= kr < r0 + WIN_H else None)
                    if parts[0] is None and parts[1] is None:
                        val = neg
                    else:
                        val = jnp.where(left, neg if parts[0] is None else parts[0],
                                        neg if parts[1] is None else parts[1])
                    bias_s[hh, r * GRID_W:(r + 1) * GRID_W, j * LANES:(j + 1) * LANES] = val

    ck = ck_ref[0]
    cv = cv_ref[0]
    qrows = _LAT_Q_BLOCK_ROWS * GRID_W
    for qb, (kr0, kr1) in enumerate(_LAT_KEY_ROWS):
        qs = slice(qb * qrows, (qb + 1) * qrows)
        ks = slice(kr0 * GRID_W, kr1 * GRID_W)
        q2, kk, vv = q_ref[qs, :], k_ref[ks, :], v_ref[ks, :]
        halves = []
        for hh, mask in enumerate((left, right)):
            qh = jnp.where(mask, q2, jnp.zeros_like(q2))
            s_lat = _dot_nt(qh, kk) + bias_s[hh, qs, ks]
            s_ctx = _dot_nt(qh, ck)
            mx = jnp.maximum(jnp.max(s_lat, axis=-1, keepdims=True), jnp.max(s_ctx, axis=-1, keepdims=True))
            e_lat = jnp.exp(s_lat - mx)
            e_ctx = jnp.exp(s_ctx - mx)
            den = jnp.sum(e_lat, axis=-1, keepdims=True) + jnp.sum(e_ctx, axis=-1, keepdims=True)
            o2 = (jnp.dot(e_lat.astype(BF16), vv, preferred_element_type=F32)
                  + jnp.dot(e_ctx.astype(BF16), cv, preferred_element_type=F32))
            halves.append(o2 / den)
        o_ref[qs, :] = jnp.where(left, halves[0], halves[1]).astype(BF16)


def _lat_attention(qkv, ck, cv, ebias):
    row0 = T_CTX // DEC_SEQ
    return pl.pallas_call(
        _lat_attn_kernel,
        grid=(HEAD_PAIRS, DEC_BATCH),
        in_specs=[
            pl.BlockSpec((DEC_SEQ, LANES), lambda p, b: (row0 + b, p)),
            pl.BlockSpec((DEC_SEQ, LANES), lambda p, b: (row0 + b, HEAD_PAIRS + p)),
            pl.BlockSpec((DEC_SEQ, LANES), lambda p, b: (row0 + b, 2 * HEAD_PAIRS + p)),
            pl.BlockSpec((1, PAST_LEN, LANES), lambda p, b: (b, 0, p)),
            pl.BlockSpec((1, PAST_LEN, LANES), lambda p, b: (b, 0, p)),
            pl.BlockSpec((2, 2 * WIN_H - 1, GRID_W, LANES), lambda p, b: (p, 0, 0, 0)),
        ],
        out_specs=pl.BlockSpec((DEC_SEQ, LANES), lambda p, b: (b, p)),
        out_shape=jax.ShapeDtypeStruct((T_LAT, D_MODEL), BF16),
        scratch_shapes=[pltpu.VMEM((2, DEC_SEQ, DEC_SEQ), F32)],
        compiler_params=_params("arbitrary", "arbitrary"),
        name="lat_attention",
    )(qkv, qkv, qkv, ck, cv, ebias)


def _expanded_bias(rel_bias):
    col = jnp.arange(GRID_W)
    col_start = jnp.clip(col - WIN_W // 2, 0, GRID_W - WIN_W)
    col_ok = (col[None, :] >= col_start[:, None]) & (col[None, :] < col_start[:, None] + WIN_W)
    dx = jnp.clip(col[None, :] - col[:, None] + (WIN_W - 1), 0, 2 * WIN_W - 2)
    eb = jnp.where(col_ok[None, None], rel_bias.astype(F32)[:, :, dx], NEG_INF)
    return jnp.concatenate([eb, eb], axis=-1)


def _wo_kernel(o_ref, x_ref, mods_ref, w_ref, lng_ref, lnb_ref, wrt_ref, br_ref, x1_ref, h2_ref, route_ref):
    mix = jnp.dot(o_ref[...], w_ref[...], preferred_element_type=F32)
    _epilogue(x_ref[...], mix, mods_ref[0, 0], lng_ref[...], lnb_ref[...], wrt_ref[...], br_ref[...],
              x1_ref, h2_ref, route_ref)


def _attn_out(o, x, mods, w_o, lng, lnb, wrt, br):
    return pl.pallas_call(
        _wo_kernel,
        grid=(N_TILES,),
        in_specs=[
            pl.BlockSpec((TILE, D_MODEL), lambda t: (t, 0)),
            pl.BlockSpec((TILE, D_MODEL), lambda t: (t, 0)),
            _mods_spec(1),
            pl.BlockSpec((D_MODEL, D_MODEL), lambda t: (0, 0)),
        ] + _EPI_IN_SPECS,
        out_specs=_EPI_OUT_SPECS,
        out_shape=_EPI_OUT_SHAPE,
        compiler_params=_params("arbitrary"),
        name="attn_out",
    )(o, x, mods, w_o, lng, lnb, wrt, br)


def kernel(x_prompt, x_sample, cache_k, cache_v, c, c_ctx, w_ada, b_ada, ln1_g, ln1_b, ln2_g, ln2_b,
           pool_w, pool_scale, w_qkv, w_o, rel_bias, w_router, b_router, w_gate, w_up, w_down):
    x = jnp.concatenate([x_prompt.reshape(T_CTX, D_MODEL), x_sample.reshape(T_LAT, D_MODEL)], axis=0)
    cond = jnp.zeros((N_COND, D_MODEL), F32).at[0].set(c_ctx).at[1:1 + DEC_BATCH].set(c)
    mods = _ada(cond, w_ada, b_ada).reshape(2, N_COND, 6, D_MODEL)
    wrt = w_router.T
    br = b_router.reshape(N_EXPERTS, 1)

    x, h2, route = _pool_layer(x, mods, pool_w[0].astype(BF16), pool_scale[0:1], ln1_g[0:1], ln1_b[0:1], wrt, br)
    x = _moe(x, h2, route, mods, 0, w_gate[0], w_up[0], w_down[0], ln2_g[0:1], ln2_b[0:1])

    qkv, new_k, new_v = _qkv(x, mods, w_qkv[0].astype(BF16))
    o_ctx = _ctx_attention(qkv)
    ck = cache_k[:, 0].transpose(0, 2, 1, 3).reshape(DEC_BATCH, PAST_LEN, D_MODEL).astype(BF16)
    cv = cache_v[:, 0].transpose(0, 2, 1, 3).reshape(DEC_BATCH, PAST_LEN, D_MODEL).astype(BF16)
    o_lat = _lat_attention(qkv, ck, cv, _expanded_bias(rel_bias[0]))
    o = jnp.concatenate([o_ctx, o_lat], axis=0)
    x, h2, route = _attn_out(o, x, mods, w_o[0].astype(BF16), ln1_g[1:2], ln1_b[1:2], wrt, br)
    x = _moe(x, h2, route, mods, 1, w_gate[1], w_up[1], w_down[1], ln2_g[1:2], ln2_b[1:2])

    y_prompt = x[:T_CTX].reshape(BATCH, SEQ, D_MODEL)
    y_sample = x[T_CTX:].reshape(DEC_BATCH, DEC_SEQ, D_MODEL)
    return y_prompt, y_sample, new_k, new_v
```

```python
import functools

import jax
import jax.numpy as jnp
from jax import lax
from jax.experimental import pallas as pl
from jax.experimental.pallas import tpu as pltpu
from jax.experimental.pallas import tpu_sc as plsc

F32 = jnp.float32
BF16 = jnp.bfloat16

D_MODEL = 1024
BATCH = 16
SEQ = 256
DEC_BATCH = 8
DEC_SEQ = 1024
PAST_LEN = 512
GRID_W = 64
GRID_ROWS = DEC_SEQ // GRID_W
POOL_SIZES = (2, 4, 8, 16)
POOL_GROUP_DIM = D_MODEL // len(POOL_SIZES)
POOL_HALO = 8
N_HEADS = 16
HEAD_DIM = 64
WIN_H = 8
WIN_W = 16
N_EXPERTS = 16
EXPERTS_PER_GROUP = 4
N_EXPERT_GROUPS = 4
D_FF = 512
ALPHA = (2.0 * 2) ** 0.25
LN_EPS = 1e-5
NEG_INF = -1e30

T_CTX = BATCH * SEQ
T_LAT = DEC_BATCH * DEC_SEQ
T_ALL = T_CTX + T_LAT
N_COND = 16
TILE = 256
N_TILES = T_ALL // TILE
CTX_TILES = T_CTX // TILE
TILES_PER_LAT_SEQ = DEC_SEQ // TILE
MOE_TILE = 256
MOE_ROWS = 2 * T_ALL + N_EXPERTS * MOE_TILE
MOE_TILES = MOE_ROWS // MOE_TILE
LN_TILE = 512
SC_WINDOW = 32
HEAD_PAIRS = N_HEADS // 2
LANES = 128
VMEM_LIMIT = 56 * 1024 * 1024


def _cond_row(t, tile):
    ctx_tiles = T_CTX // tile
    per_seq = DEC_SEQ // tile
    return jnp.maximum(t - ctx_tiles + per_seq, 0) // per_seq


def _params(*sem):
    return pltpu.CompilerParams(dimension_semantics=sem, vmem_limit_bytes=VMEM_LIMIT)


def _ada_kernel(cond_ref, w_ref, b_ref, o_ref):
    cnd = cond_ref[...]
    act = cnd * jax.nn.sigmoid(cnd)
    o_ref[0] = jnp.dot(act, w_ref[0], precision=lax.Precision.HIGHEST,
                       preferred_element_type=F32) + b_ref[0]


def _ada(cond, w_ada, b_ada):
    depth, d, n = w_ada.shape
    bn = 1536
    return pl.pallas_call(
        _ada_kernel,
        grid=(depth, n // bn),
        in_specs=[
            pl.BlockSpec((N_COND, d), lambda i, j: (0, 0)),
            pl.BlockSpec((1, d, bn), lambda i, j: (i, 0, j)),
            pl.BlockSpec((1, 1, bn), lambda i, j: (i, 0, j)),
        ],
        out_specs=pl.BlockSpec((1, N_COND, bn), lambda i, j: (i, 0, j)),
        out_shape=jax.ShapeDtypeStruct((depth, N_COND, n), F32),
        compiler_params=_params("arbitrary", "arbitrary"),
        name="ada",
    )(cond, w_ada, b_ada.reshape(depth, 1, n))


def _post_norm(x, upd, g, b):
    y = ALPHA * x + upd
    mu = jnp.mean(y, axis=-1, keepdims=True)
    yc = y - mu
    var = jnp.mean(yc * yc, axis=-1, keepdims=True)
    return yc * lax.rsqrt(var + LN_EPS) * g + b


def _route(h2, wrt, br, route_ref):
    logits = lax.dot_general(wrt, h2, (((1,), (1,)), ((), ())), precision=lax.Precision.HIGHEST,
                             preferred_element_type=F32)
    aff = jax.nn.sigmoid(logits)
    sel = aff + br
    sel_rows = [sel[e:e + 1, :] for e in range(N_EXPERTS)]
    aff_rows = [aff[e:e + 1, :] for e in range(N_EXPERTS)]

    def group_score(g):
        r = sel_rows[g * EXPERTS_PER_GROUP:(g + 1) * EXPERTS_PER_GROUP]
        best = None
        for i in range(EXPERTS_PER_GROUP):
            for j in range(i + 1, EXPERTS_PER_GROUP):
                pair = r[i] + r[j]
                best = pair if best is None else jnp.maximum(best, pair)
        return best

    best = group_score(0)
    gidx = jnp.zeros_like(best, dtype=jnp.int32)
    for g in range(1, N_EXPERT_GROUPS):
        sc = group_score(g)
        better = sc > best
        gidx = jnp.where(better, g, gidx)
        best = jnp.where(better, sc, best)

    def pick_group(rows, j):
        out = rows[j]
        for g in range(1, N_EXPERT_GROUPS):
            out = jnp.where(gidx == g, rows[g * EXPERTS_PER_GROUP + j], out)
        return out

    cand = [pick_group(sel_rows, j) for j in range(EXPERTS_PER_GROUP)]
    cand_aff = [pick_group(aff_rows, j) for j in range(EXPERTS_PER_GROUP)]

    def argmax_first(vals):
        bv, bi, ba = vals[0], jnp.zeros_like(gidx), cand_aff[0]
        for j in range(1, EXPERTS_PER_GROUP):
            better = vals[j] > bv
            bv = jnp.where(better, vals[j], bv)
            bi = jnp.where(better, j, bi)
            ba = jnp.where(better, cand_aff[j], ba)
        return bi, ba

    i1, a1 = argmax_first(cand)
    rest = [jnp.where(i1 == j, -jnp.inf, cand[j]) for j in range(EXPERTS_PER_GROUP)]
    i2, a2 = argmax_first(rest)
    denom = a1 + a2
    base = gidx * EXPERTS_PER_GROUP
    route_ref[0, 0:1, :] = (base + i1).astype(F32)
    route_ref[0, 1:2, :] = (base + i2).astype(F32)
    route_ref[0, 2:3, :] = a1 / denom
    route_ref[0, 3:4, :] = a2 / denom
    route_ref[0, 4:8, :] = jnp.zeros((4, TILE), F32)


def _epilogue(x, mix, m, lng, lnb, wrt, br, x1_ref, h2_ref, route_ref):
    g1, sh2, sc2 = m[2:3, :], m[3:4, :], m[4:5, :]
    x1 = _post_norm(x, g1 * mix, lng, lnb)
    h2 = x1 * (1.0 + sc2) + sh2
    x1_ref[...] = x1
    h2_ref[...] = h2
    _route(h2, wrt, br, route_ref)


_EPI_IN_SPECS = [
    pl.BlockSpec((1, D_MODEL), lambda t: (0, 0)),
    pl.BlockSpec((1, D_MODEL), lambda t: (0, 0)),
    pl.BlockSpec((N_EXPERTS, D_MODEL), lambda t: (0, 0)),
    pl.BlockSpec((N_EXPERTS, 1), lambda t: (0, 0)),
]
_EPI_OUT_SPECS = [
    pl.BlockSpec((TILE, D_MODEL), lambda t: (t, 0)),
    pl.BlockSpec((TILE, D_MODEL), lambda t: (t, 0)),
    pl.BlockSpec((1, 8, TILE), lambda t: (t, 0, 0)),
]
_EPI_OUT_SHAPE = [
    jax.ShapeDtypeStruct((T_ALL, D_MODEL), F32),
    jax.ShapeDtypeStruct((T_ALL, D_MODEL), F32),
    jax.ShapeDtypeStruct((N_TILES, 8, TILE), F32),
]


def _mods_spec(layer):
    return pl.BlockSpec((1, 1, 6, D_MODEL), lambda t: (layer, _cond_row(t, TILE), 0, 0))


def _pool_kernel(xp_ref, x_ref, xn_ref, mods_ref, pw_ref, ps_ref, lng_ref, lnb_ref, wrt_ref, br_ref,
                 x1_ref, h2_ref, route_ref):
    t = pl.program_id(0)
    m = mods_ref[0, 0]
    sh1, sc1 = m[0:1, :], m[1:2, :]
    in_lat = t >= CTX_TILES
    sub = (t - CTX_TILES) % TILES_PER_LAT_SEQ
    is_first = jnp.logical_or(jnp.logical_not(in_lat), sub == 0)
    is_last = jnp.logical_or(jnp.logical_not(in_lat), sub == TILES_PER_LAT_SEQ - 1)
    seq_len = jnp.where(in_lat, DEC_SEQ, SEQ)
    pos0 = jnp.where(in_lat, sub * TILE, 0)

    x = x_ref[...]
    h = x * (1.0 + sc1) + sh1
    hp = jnp.where(is_first, 0.0, xp_ref[...] * (1.0 + sc1) + sh1)
    hn = jnp.where(is_last, 0.0, xn_ref[...] * (1.0 + sc1) + sh1)
    hext = jnp.concatenate([hp, h, hn], axis=0)
    ext = TILE + 2 * POOL_HALO
    pos = pos0 + lax.broadcasted_iota(jnp.int32, (TILE, 1), 0)

    outs = []
    for g, w in enumerate(POOL_SIZES):
        lo_c, hi_c = g * POOL_GROUP_DIM, (g + 1) * POOL_GROUP_DIM
        a = hext[:, lo_c:hi_c]
        k = 1
        while k < w:
            a = a + pltpu.roll(a, ext - k, axis=0)
            k *= 2
        off = POOL_HALO - w // 2
        win = pltpu.roll(a, ext - off, axis=0)[:TILE] if off else a[:TILE]
        lo = jnp.maximum(pos - w // 2, 0)
        hi = jnp.minimum(pos - w // 2 + w, seq_len)
        cnt = (hi - lo).astype(F32)
        pooled = win / cnt - h[:, lo_c:hi_c]
        outs.append(jnp.dot(pooled.astype(BF16), pw_ref[g], preferred_element_type=F32))
    mix = jnp.concatenate(outs, axis=1) * ps_ref[...]
    _epilogue(x, mix, m, lng_ref[...], lnb_ref[...], wrt_ref[...], br_ref[...], x1_ref, h2_ref, route_ref)


def _pool_layer(x, mods, pool_w, pool_scale, lng, lnb, wrt, br):
    halo_blocks = TILE // POOL_HALO
    last_halo = T_ALL // POOL_HALO - 1
    return pl.pallas_call(
        _pool_kernel,
        grid=(N_TILES,),
        in_specs=[
            pl.BlockSpec((POOL_HALO, D_MODEL), lambda t: (jnp.maximum(t * halo_blocks - 1, 0), 0)),
            pl.BlockSpec((TILE, D_MODEL), lambda t: (t, 0)),
            pl.BlockSpec((POOL_HALO, D_MODEL), lambda t: (jnp.minimum((t + 1) * halo_blocks, last_halo), 0)),
            _mods_spec(0),
            pl.BlockSpec((len(POOL_SIZES), POOL_GROUP_DIM, POOL_GROUP_DIM), lambda t: (0, 0, 0)),
            pl.BlockSpec((1, D_MODEL), lambda t: (0, 0)),
        ] + _EPI_IN_SPECS,
        out_specs=_EPI_OUT_SPECS,
        out_shape=_EPI_OUT_SHAPE,
        compiler_params=_params("arbitrary"),
        name="pool_mixer",
    )(x, x, x, mods, pool_w, pool_scale, lng, lnb, wrt, br)


def _route_plan(route):
    eidx = route[:, 0:2, :].astype(jnp.int32).transpose(1, 0, 2).reshape(2 * T_ALL)
    wsel = route[:, 2:4, :].transpose(1, 0, 2).reshape(2, T_ALL)
    onehot = (eidx[:, None] == jnp.arange(N_EXPERTS, dtype=jnp.int32)[None, :]).astype(jnp.int32)
    counts = jnp.sum(onehot, axis=0)
    rank = jnp.sum((jnp.cumsum(onehot, axis=0) - onehot) * onehot, axis=1)
    padded = (counts + MOE_TILE - 1) // MOE_TILE * MOE_TILE
    ends = jnp.cumsum(padded)
    starts = ends - padded
    pos = jnp.sum(onehot * starts[None, :], axis=1) + rank
    tile_start = jnp.arange(MOE_TILES, dtype=jnp.int32) * MOE_TILE
    tile_expert = jnp.minimum(jnp.sum((ends[None, :] <= tile_start[:, None]).astype(jnp.int32), axis=1),
                              N_EXPERTS - 1)
    tile_valid = jnp.clip(counts[tile_expert] - (tile_start - starts[tile_expert]), 0, MOE_TILE)
    tile_valid = jnp.where(tile_start < ends[-1], tile_valid, 0)
    used_tiles = ends[-1] // MOE_TILE
    tile_block = jnp.minimum(jnp.arange(MOE_TILES, dtype=jnp.int32), used_tiles - 1)
    return pos.astype(jnp.int32), wsel, tile_expert.astype(jnp.int32), tile_valid.astype(jnp.int32), \
        tile_block.astype(jnp.int32)


def _expert_kernel(te_ref, nv_ref, tb_ref, x_ref, wg_ref, wu_ref, wd_ref, y_ref, wgu_s, wd_s):
    i = pl.program_id(0)
    prev = te_ref[jnp.maximum(i - 1, 0)]
    changed = jnp.logical_or(i == 0, te_ref[i] != prev)

    @pl.when(changed)
    def _():
        wgu_s[:, :D_FF] = wg_ref[0].astype(BF16)
        wgu_s[:, D_FF:] = wu_ref[0].astype(BF16)
        wd_s[...] = wd_ref[0].astype(BF16)

    nv = nv_ref[i]

    @pl.when(nv > 0)
    def _():
        rows = lax.broadcasted_iota(jnp.int32, (MOE_TILE, 1), 0)
        xb = jnp.where(rows < nv, x_ref[...], 0.0).astype(BF16)
        gu = jnp.dot(xb, wgu_s[...], preferred_element_type=F32)
        gate, up = gu[:, :D_FF], gu[:, D_FF:]
        he = (gate * jax.nn.sigmoid(gate) * up).astype(BF16)
        y_ref[...] = jnp.dot(he, wd_s[...], preferred_element_type=F32)


def _experts(xs, tile_expert, tile_valid, tile_block, w_gate, w_up, w_down):
    grid_spec = pltpu.PrefetchScalarGridSpec(
        num_scalar_prefetch=3,
        grid=(MOE_TILES,),
        in_specs=[
            pl.BlockSpec((MOE_TILE, D_MODEL), lambda i, te, nv, tb: (tb[i], 0)),
            pl.BlockSpec((1, D_MODEL, D_FF), lambda i, te, nv, tb: (te[i], 0, 0)),
            pl.BlockSpec((1, D_MODEL, D_FF), lambda i, te, nv, tb: (te[i], 0, 0)),
            pl.BlockSpec((1, D_FF, D_MODEL), lambda i, te, nv, tb: (te[i], 0, 0)),
        ],
        out_specs=pl.BlockSpec((MOE_TILE, D_MODEL), lambda i, te, nv, tb: (tb[i], 0)),
        scratch_shapes=[pltpu.VMEM((D_MODEL, 2 * D_FF), BF16), pltpu.VMEM((D_FF, D_MODEL), BF16)],
    )
    return pl.pallas_call(
        _expert_kernel,
        grid_spec=grid_spec,
        out_shape=jax.ShapeDtypeStruct((MOE_ROWS, D_MODEL), F32),
        compiler_params=_params("arbitrary"),
        name="experts",
    )(tile_expert, tile_valid, tile_block, xs, w_gate, w_up, w_down)


def _sc_mesh():
    return plsc.VectorSubcoreMesh(core_axis_name="core", subcore_axis_name="subcore")


SC_WORKERS = 32
SC_ROWS_PER_WORKER = 2 * T_ALL // SC_WORKERS
SC_CHUNKS = SC_ROWS_PER_WORKER // SC_WINDOW


def _sc_worker_base():
    wid = lax.axis_index("subcore") * 2 + lax.axis_index("core")
    return wid * SC_ROWS_PER_WORKER


def _dispatch(h2, pos):
    @functools.partial(
        pl.kernel, out_type=jax.ShapeDtypeStruct((MOE_ROWS, D_MODEL), F32), mesh=_sc_mesh(),
        scratch_types=[pltpu.VMEM((SC_WINDOW,), jnp.int32), pltpu.VMEM((SC_WINDOW, D_MODEL), F32),
                       pltpu.SemaphoreType.DMA],
        name="moe_dispatch")
    def scatter(x_hbm, i_hbm, o_hbm, idx_v, rows_v, sem):
        base = _sc_worker_base()
        src_base = base % T_ALL

        @pl.loop(0, SC_CHUNKS)
        def _(j):
            pltpu.sync_copy(i_hbm.at[pl.ds(base + j * SC_WINDOW, SC_WINDOW)], idx_v)
            pltpu.sync_copy(x_hbm.at[pl.ds(src_base + j * SC_WINDOW, SC_WINDOW)], rows_v)
            pltpu.async_copy(rows_v, o_hbm.at[idx_v], sem).wait()

    return scatter(h2, pos)


def _gather_pairs(y, pos):
    @functools.partial(
        pl.kernel, out_type=jax.ShapeDtypeStruct((2 * T_ALL, D_MODEL), F32), mesh=_sc_mesh(),
        scratch_types=[pltpu.VMEM((SC_WINDOW,), jnp.int32), pltpu.VMEM((SC_WINDOW, D_MODEL), F32),
                       pltpu.SemaphoreType.DMA],
        name="moe_gather")
    def gather(y_hbm, i_hbm, o_hbm, idx_v, rows_v, sem):
        base = _sc_worker_base()

        @pl.loop(0, SC_CHUNKS)
        def _(j):
            off = base + j * SC_WINDOW
            pltpu.sync_copy(i_hbm.at[pl.ds(off, SC_WINDOW)], idx_v)
            pltpu.async_copy(y_hbm.at[idx_v], rows_v, sem).wait()
            pltpu.sync_copy(rows_v, o_hbm.at[pl.ds(off, SC_WINDOW)])

    return gather(y, pos).reshape(2, T_ALL, D_MODEL)


def _combine_kernel(x_ref, yg_ref, w1_ref, w2_ref, mods_ref, lng_ref, lnb_ref, o_ref):
    m = mods_ref[0, 0]
    g2 = m[5:6, :]
    moe = w1_ref[...] * yg_ref[0] + w2_ref[...] * yg_ref[1]
    o_ref[...] = _post_norm(x_ref[...], g2 * moe, lng_ref[...], lnb_ref[...])


def _combine(x1, yg, wsel, mods, layer, lng, lnb):
    return pl.pallas_call(
        _combine_kernel,
        grid=(T_ALL // LN_TILE,),
        in_specs=[
            pl.BlockSpec((LN_TILE, D_MODEL), lambda t: (t, 0)),
            pl.BlockSpec((2, LN_TILE, D_MODEL), lambda t: (0, t, 0)),
            pl.BlockSpec((LN_TILE, 1), lambda t: (t, 0)),
            pl.BlockSpec((LN_TILE, 1), lambda t: (t, 0)),
            pl.BlockSpec((1, 1, 6, D_MODEL), lambda t: (layer, _cond_row(t, LN_TILE), 0, 0)),
            pl.BlockSpec((1, D_MODEL), lambda t: (0, 0)),
            pl.BlockSpec((1, D_MODEL), lambda t: (0, 0)),
        ],
        out_specs=pl.BlockSpec((LN_TILE, D_MODEL), lambda t: (t, 0)),
        out_shape=jax.ShapeDtypeStruct((T_ALL, D_MODEL), F32),
        compiler_params=_params("arbitrary"),
        name="moe_combine",
    )(x1, yg, wsel[0].reshape(T_ALL, 1), wsel[1].reshape(T_ALL, 1), mods, lng, lnb)


def _moe(x1, h2, route, mods, layer, w_gate, w_up, w_down, lng, lnb):
    pos, wsel, tile_expert, tile_valid, tile_block = _route_plan(route)
    xs = _dispatch(h2, pos)
    y = _experts(xs, tile_expert, tile_valid, tile_block, w_gate, w_up, w_down)
    yg = _gather_pairs(y, pos)
    return _combine(x1, yg, wsel, mods, layer, lng, lnb)


def _qkv_kernel(x_ref, mods_ref, w_ref, qkv_ref, nk_ref, nv_ref):
    t = pl.program_id(0)
    m = mods_ref[0, 0]
    sh1, sc1 = m[0:1, :], m[1:2, :]
    h = (x_ref[...] * (1.0 + sc1) + sh1).astype(BF16)
    r = jnp.dot(h, w_ref[...], preferred_element_type=F32)
    qkv_ref[:, :D_MODEL] = (r[:, :D_MODEL] * (HEAD_DIM ** -0.5)).astype(BF16)
    qkv_ref[:, D_MODEL:] = r[:, D_MODEL:].astype(BF16)

    @pl.when(t < CTX_TILES)
    def _():
        for out_ref, base in ((nk_ref, D_MODEL), (nv_ref, 2 * D_MODEL)):
            for p in range(HEAD_PAIRS):
                pair = r[:, base + p * LANES: base + (p + 1) * LANES]
                out_ref[0, 0, 2 * p] = pair[:, :HEAD_DIM]
                out_ref[0, 0, 2 * p + 1] = pltpu.roll(pair, HEAD_DIM, axis=1)[:, :HEAD_DIM]


def _qkv(x, mods, w_qkv):
    cache_spec = pl.BlockSpec((1, 1, N_HEADS, SEQ, HEAD_DIM),
                              lambda t: (jnp.minimum(t, CTX_TILES - 1), 0, 0, 0, 0))
    cache_shape = jax.ShapeDtypeStruct((BATCH, 1, N_HEADS, SEQ, HEAD_DIM), F32)
    return pl.pallas_call(
        _qkv_kernel,
        grid=(N_TILES,),
        in_specs=[
            pl.BlockSpec((TILE, D_MODEL), lambda t: (t, 0)),
            _mods_spec(1),
            pl.BlockSpec((D_MODEL, 3 * D_MODEL), lambda t: (0, 0)),
        ],
        out_specs=[pl.BlockSpec((TILE, 3 * D_MODEL), lambda t: (t, 0)), cache_spec, cache_spec],
        out_shape=[jax.ShapeDtypeStruct((T_ALL, 3 * D_MODEL), BF16), cache_shape, cache_shape],
        compiler_params=_params("arbitrary"),
        name="qkv",
    )(x, mods, w_qkv)


def _dot_nt(a, b):
    return lax.dot_general(a, b, (((1,), (1,)), ((), ())), preferred_element_type=F32)


def _head_masks():
    lane = lax.broadcasted_iota(jnp.int32, (1, LANES), 1)
    return lane < HEAD_DIM, lane >= HEAD_DIM


def _ctx_attn_kernel(q_ref, k_ref, v_ref, o_ref):
    left, right = _head_masks()
    for p in range(HEAD_PAIRS):
        cols = slice(p * LANES, (p + 1) * LANES)
        q2, k2, v2 = q_ref[:, cols], k_ref[:, cols], v_ref[:, cols]
        halves = []
        for mask in (left, right):
            qh = jnp.where(mask, q2, jnp.zeros_like(q2))
            s = _dot_nt(qh, k2)
            e = jnp.exp(s - jnp.max(s, axis=-1, keepdims=True))
            o2 = jnp.dot(e.astype(BF16), v2, preferred_element_type=F32)
            halves.append(o2 / jnp.sum(e, axis=-1, keepdims=True))
        o_ref[:, cols] = jnp.where(left, halves[0], halves[1]).astype(BF16)


def _ctx_attention(qkv):
    return pl.pallas_call(
        _ctx_attn_kernel,
        grid=(BATCH,),
        in_specs=[pl.BlockSpec((SEQ, D_MODEL), lambda b, j=j: (b, j)) for j in range(3)],
        out_specs=pl.BlockSpec((SEQ, D_MODEL), lambda b: (b, 0)),
        out_shape=jax.ShapeDtypeStruct((T_CTX, D_MODEL), BF16),
        compiler_params=_params("arbitrary"),
        name="ctx_attention",
    )(qkv, qkv, qkv)


_LAT_Q_BLOCK_ROWS = 4
_LAT_KEY_ROWS = ((0, 8), (0, 12), (4, 16), (8, 16))


def _lat_attn_kernel(q_ref, k_ref, v_ref, ck_ref, cv_ref, eb_ref, o_ref, bias_s):
    left, right = _head_masks()

    @pl.when(pl.program_id(1) == 0)
    def _():
        neg = jnp.full((GRID_W, LANES), NEG_INF, F32)
        for hh in range(2):
            for r in range(GRID_ROWS):
                r0 = min(max(r - WIN_H // 2, 0), GRID_ROWS - WIN_H)
                for j in range(GRID_ROWS // 2):
                    parts = []
                    for kr in (2 * j, 2 * j + 1):
                        parts.append(eb_ref[hh, kr - r + WIN_H - 1] if r0 <= kr < r0 + WIN_H else None)
                    if parts[0] is None and parts[1] is None:
                        val = neg
                    else:
                        val = jnp.where(left, neg if parts[0] is None else parts[0],
                                        neg if parts[1] is None else parts[1])
                    bias_s[hh, r * GRID_W:(r + 1) * GRID_W, j * LANES:(j + 1) * LANES] = val

    ck = ck_ref[0]
    cv = cv_ref[0]
    qrows = _LAT_Q_BLOCK_ROWS * GRID_W
    for qb, (kr0, kr1) in enumerate(_LAT_KEY_ROWS):
        qs = slice(qb * qrows, (qb + 1) * qrows)
        ks = slice(kr0 * GRID_W, kr1 * GRID_W)
        q2, kk, vv = q_ref[qs, :], k_ref[ks, :], v_ref[ks, :]
        halves = []
        for hh, mask in enumerate((left, right)):
            qh = jnp.where(mask, q2, jnp.zeros_like(q2))
            s_lat = _dot_nt(qh, kk) + bias_s[hh, qs, ks]
            s_ctx = _dot_nt(qh, ck)
            mx = jnp.maximum(jnp.max(s_lat, axis=-1, keepdims=True), jnp.max(s_ctx, axis=-1, keepdims=True))
            e_lat = jnp.exp(s_lat - mx)
            e_ctx = jnp.exp(s_ctx - mx)
            den = jnp.sum(e_lat, axis=-1, keepdims=True) + jnp.sum(e_ctx, axis=-1, keepdims=True)
            o2 = (jnp.dot(e_lat.astype(BF16), vv, preferred_element_type=F32)
                  + jnp.dot(e_ctx.astype(BF16), cv, preferred_element_type=F32))
            halves.append(o2 / den)
        o_ref[qs, :] = jnp.where(left, halves[0], halves[1]).astype(BF16)


def _lat_attention(qkv, ck, cv, ebias):
    row0 = T_CTX // DEC_SEQ
    return pl.pallas_call(
        _lat_attn_kernel,
        grid=(HEAD_PAIRS, DEC_BATCH),
        in_specs=[
            pl.BlockSpec((DEC_SEQ, LANES), lambda p, b: (row0 + b, p)),
            pl.BlockSpec((DEC_SEQ, LANES), lambda p, b: (row0 + b, HEAD_PAIRS + p)),
            pl.BlockSpec((DEC_SEQ, LANES), lambda p, b: (row0 + b, 2 * HEAD_PAIRS + p)),
            pl.BlockSpec((1, PAST_LEN, LANES), lambda p, b: (b, 0, p)),
            pl.BlockSpec((1, PAST_LEN, LANES), lambda p, b: (b, 0, p)),
            pl.BlockSpec((2, 2 * WIN_H - 1, GRID_W, LANES), lambda p, b: (p, 0, 0, 0)),
        ],
        out_specs=pl.BlockSpec((DEC_SEQ, LANES), lambda p, b: (b, p)),
        out_shape=jax.ShapeDtypeStruct((T_LAT, D_MODEL), BF16),
        scratch_shapes=[pltpu.VMEM((2, DEC_SEQ, DEC_SEQ), F32)],
        compiler_params=_params("arbitrary", "arbitrary"),
        name="lat_attention",
    )(qkv, qkv, qkv, ck, cv, ebias)


def _expanded_bias(rel_bias):
    col = jnp.arange(GRID_W)
    col_start = jnp.clip(col - WIN_W // 2, 0, GRID_W - WIN_W)
    col_ok = (col[None, :] >= col_start[:, None]) & (col[None, :] < col_start[:, None] + WIN_W)
    dx = jnp.clip(col[None, :] - col[:, None] + (WIN_W - 1), 0, 2 * WIN_W - 2)
    eb = jnp.where(col_ok[None, None], rel_bias.astype(F32)[:, :, dx], NEG_INF)
    return jnp.concatenate([eb, eb], axis=-1)


def _wo_kernel(o_ref, x_ref, mods_ref, w_ref, lng_ref, lnb_ref, wrt_ref, br_ref, x1_ref, h2_ref, route_ref):
    mix = jnp.dot(o_ref[...], w_ref[...], preferred_element_type=F32)
    _epilogue(x_ref[...], mix, mods_ref[0, 0], lng_ref[...], lnb_ref[...], wrt_ref[...], br_ref[...],
              x1_ref, h2_ref, route_ref)


def _attn_out(o, x, mods, w_o, lng, lnb, wrt, br):
    return pl.pallas_call(
        _wo_kernel,
        grid=(N_TILES,),
        in_specs=[
            pl.BlockSpec((TILE, D_MODEL), lambda t: (t, 0)),
            pl.BlockSpec((TILE, D_MODEL), lambda t: (t, 0)),
            _mods_spec(1),
            pl.BlockSpec((D_MODEL, D_MODEL), lambda t: (0, 0)),
        ] + _EPI_IN_SPECS,
        out_specs=_EPI_OUT_SPECS,
        out_shape=_EPI_OUT_SHAPE,
        compiler_params=_params("arbitrary"),
        name="attn_out",
    )(o, x, mods, w_o, lng, lnb, wrt, br)


def kernel(x_prompt, x_sample, cache_k, cache_v, c, c_ctx, w_ada, b_ada, ln1_g, ln1_b, ln2_g, ln2_b,
           pool_w, pool_scale, w_qkv, w_o, rel_bias, w_router, b_router, w_gate, w_up, w_down):
    x = jnp.concatenate([x_prompt.reshape(T_CTX, D_MODEL), x_sample.reshape(T_LAT, D_MODEL)], axis=0)
    cond = jnp.zeros((N_COND, D_MODEL), F32).at[0].set(c_ctx).at[1:1 + DEC_BATCH].set(c)
    mods = _ada(cond, w_ada, b_ada).reshape(2, N_COND, 6, D_MODEL)
    wrt = w_router.T
    br = b_router.reshape(N_EXPERTS, 1)

    x, h2, route = _pool_layer(x, mods, pool_w[0].astype(BF16), pool_scale[0:1], ln1_g[0:1], ln1_b[0:1], wrt, br)
    x = _moe(x, h2, route, mods, 0, w_gate[0], w_up[0], w_down[0], ln2_g[0:1], ln2_b[0:1])

    qkv, new_k, new_v = _qkv(x, mods, w_qkv[0].astype(BF16))
    o_ctx = _ctx_attention(qkv)
    ck = cache_k[:, 0].transpose(0, 2, 1, 3).reshape(DEC_BATCH, PAST_LEN, D_MODEL).astype(BF16)
    cv = cache_v[:, 0].transpose(0, 2, 1, 3).reshape(DEC_BATCH, PAST_LEN, D_MODEL).astype(BF16)
    o_lat = _lat_attention(qkv, ck, cv, _expanded_bias(rel_bias[0]))
    o = jnp.concatenate([o_ctx, o_lat], axis=0)
    x, h2, route = _attn_out(o, x, mods, w_o[0].astype(BF16), ln1_g[1:2], ln1_b[1:2], wrt, br)
    x = _moe(x, h2, route, mods, 1, w_gate[1], w_up[1], w_down[1], ln2_g[1:2], ln2_b[1:2])

    y_prompt = x[:T_CTX].reshape(BATCH, SEQ, D_MODEL)
    y_sample = x[T_CTX:].reshape(DEC_BATCH, DEC_SEQ, D_MODEL)
    return y_prompt, y_sample, new_k, new_v
```

```python
import functools

import jax
import jax.numpy as jnp
from jax import lax
from jax.experimental import pallas as pl
from jax.experimental.pallas import tpu as pltpu
from jax.experimental.pallas import tpu_sc as plsc

F32 = jnp.float32
BF16 = jnp.bfloat16

D_MODEL = 1024
BATCH = 16
SEQ = 256
DEC_BATCH = 8
DEC_SEQ = 1024
PAST_LEN = 512
GRID_W = 64
GRID_ROWS = DEC_SEQ // GRID_W
POOL_SIZES = (2, 4, 8, 16)
POOL_GROUP_DIM = D_MODEL // len(POOL_SIZES)
POOL_HALO = 8
N_HEADS = 16
HEAD_DIM = 64
WIN_H = 8
WIN_W = 16
N_EXPERTS = 16
EXPERTS_PER_GROUP = 4
N_EXPERT_GROUPS = 4
D_FF = 512
ALPHA = (2.0 * 2) ** 0.25
LN_EPS = 1e-5
NEG_INF = -1e30

T_CTX = BATCH * SEQ
T_LAT = DEC_BATCH * DEC_SEQ
T_ALL = T_CTX + T_LAT
N_COND = 16
TILE = 256
N_TILES = T_ALL // TILE
CTX_TILES = T_CTX // TILE
TILES_PER_LAT_SEQ = DEC_SEQ // TILE
MOE_TILE = 256
MOE_ROWS = 2 * T_ALL + N_EXPERTS * MOE_TILE
MOE_TILES = MOE_ROWS // MOE_TILE
LN_TILE = 512
SC_WINDOW = 32
HEAD_PAIRS = N_HEADS // 2
LANES = 128
VMEM_LIMIT = 56 * 1024 * 1024


def _cond_row(t, tile):
    ctx_tiles = T_CTX // tile
    per_seq = DEC_SEQ // tile
    return jnp.maximum(t - ctx_tiles + per_seq, 0) // per_seq


def _params(*sem):
    return pltpu.CompilerParams(dimension_semantics=sem, vmem_limit_bytes=VMEM_LIMIT)


def _ada_kernel(cond_ref, w_ref, b_ref, o_ref):
    cnd = cond_ref[...]
    act = cnd * jax.nn.sigmoid(cnd)
    o_ref[0] = jnp.dot(act, w_ref[0], precision=lax.Precision.HIGHEST,
                       preferred_element_type=F32) + b_ref[0]


def _ada(cond, w_ada, b_ada):
    depth, d, n = w_ada.shape
    bn = 1536
    return pl.pallas_call(
        _ada_kernel,
        grid=(depth, n // bn),
        in_specs=[
            pl.BlockSpec((N_COND, d), lambda i, j: (0, 0)),
            pl.BlockSpec((1, d, bn), lambda i, j: (i, 0, j)),
            pl.BlockSpec((1, 1, bn), lambda i, j: (i, 0, j)),
        ],
        out_specs=pl.BlockSpec((1, N_COND, bn), lambda i, j: (i, 0, j)),
        out_shape=jax.ShapeDtypeStruct((depth, N_COND, n), F32),
        compiler_params=_params("arbitrary", "arbitrary"),
        name="ada",
    )(cond, w_ada, b_ada.reshape(depth, 1, n))


def _post_norm(x, upd, g, b):
    y = ALPHA * x + upd
    mu = jnp.mean(y, axis=-1, keepdims=True)
    yc = y - mu
    var = jnp.mean(yc * yc, axis=-1, keepdims=True)
    return yc * lax.rsqrt(var + LN_EPS) * g + b


def _route(h2, wrt, br, route_ref):
    logits = lax.dot_general(wrt, h2, (((1,), (1,)), ((), ())), precision=lax.Precision.HIGHEST,
                             preferred_element_type=F32)
    aff = jax.nn.sigmoid(logits)
    sel = aff + br
    sel_rows = [sel[e:e + 1, :] for e in range(N_EXPERTS)]
    aff_rows = [aff[e:e + 1, :] for e in range(N_EXPERTS)]

    def group_score(g):
        r = sel_rows[g * EXPERTS_PER_GROUP:(g + 1) * EXPERTS_PER_GROUP]
        best = None
        for i in range(EXPERTS_PER_GROUP):
            for j in range(i + 1, EXPERTS_PER_GROUP):
                pair = r[i] + r[j]
                best = pair if best is None else jnp.maximum(best, pair)
        return best

    best = group_score(0)
    gidx = jnp.zeros_like(best, dtype=jnp.int32)
    for g in range(1, N_EXPERT_GROUPS):
        sc = group_score(g)
        better = sc > best
        gidx = jnp.where(better, g, gidx)
        best = jnp.where(better, sc, best)

    def pick_group(rows, j):
        out = rows[j]
        for g in range(1, N_EXPERT_GROUPS):
            out = jnp.where(gidx == g, rows[g * EXPERTS_PER_GROUP + j], out)
        return out

    cand = [pick_group(sel_rows, j) for j in range(EXPERTS_PER_GROUP)]
    cand_aff = [pick_group(aff_rows, j) for j in range(EXPERTS_PER_GROUP)]

    def argmax_first(vals):
        bv, bi, ba = vals[0], jnp.zeros_like(gidx), cand_aff[0]
        for j in range(1, EXPERTS_PER_GROUP):
            better = vals[j] > bv
            bv = jnp.where(better, vals[j], bv)
            bi = jnp.where(better, j, bi)
            ba = jnp.where(better, cand_aff[j], ba)
        return bi, ba

    i1, a1 = argmax_first(cand)
    rest = [jnp.where(i1 == j, -jnp.inf, cand[j]) for j in range(EXPERTS_PER_GROUP)]
    i2, a2 = argmax_first(rest)
    denom = a1 + a2
    base = gidx * EXPERTS_PER_GROUP
    route_ref[0, 0:1, :] = (base + i1).astype(F32)
    route_ref[0, 1:2, :] = (base + i2).astype(F32)
    route_ref[0, 2:3, :] = a1 / denom
    route_ref[0, 3:4, :] = a2 / denom
    route_ref[0, 4:8, :] = jnp.zeros((4, TILE), F32)


def _epilogue(x, mix, m, lng, lnb, wrt, br, x1_ref, h2_ref, route_ref):
    g1, sh2, sc2 = m[2:3, :], m[3:4, :], m[4:5, :]
    x1 = _post_norm(x, g1 * mix, lng, lnb)
    h2 = x1 * (1.0 + sc2) + sh2
    x1_ref[...] = x1
    h2_ref[...] = h2
    _route(h2, wrt, br, route_ref)


_EPI_IN_SPECS = [
    pl.BlockSpec((1, D_MODEL), lambda t: (0, 0)),
    pl.BlockSpec((1, D_MODEL), lambda t: (0, 0)),
    pl.BlockSpec((N_EXPERTS, D_MODEL), lambda t: (0, 0)),
    pl.BlockSpec((N_EXPERTS, 1), lambda t: (0, 0)),
]
_EPI_OUT_SPECS = [
    pl.BlockSpec((TILE, D_MODEL), lambda t: (t, 0)),
    pl.BlockSpec((TILE, D_MODEL), lambda t: (t, 0)),
    pl.BlockSpec((1, 8, TILE), lambda t: (t, 0, 0)),
]
_EPI_OUT_SHAPE = [
    jax.ShapeDtypeStruct((T_ALL, D_MODEL), F32),
    jax.ShapeDtypeStruct((T_ALL, D_MODEL), F32),
    jax.ShapeDtypeStruct((N_TILES, 8, TILE), F32),
]


def _mods_spec(layer):
    return pl.BlockSpec((1, 1, 6, D_MODEL), lambda t: (layer, _cond_row(t, TILE), 0, 0))


def _pool_kernel(xa_ref, xb_ref, xp_ref, xn_ref, mods_ref, pw_ref, ps_ref, lng_ref, lnb_ref, wrt_ref, br_ref,
                 x1_ref, h2_ref, route_ref):
    t = pl.program_id(0)
    m = mods_ref[0, 0]
    sh1, sc1 = m[0:1, :], m[1:2, :]
    in_lat = t >= CTX_TILES
    sub = (t - CTX_TILES) % TILES_PER_LAT_SEQ
    is_first = jnp.logical_or(jnp.logical_not(in_lat), sub == 0)
    is_last = jnp.logical_or(jnp.logical_not(in_lat), sub == TILES_PER_LAT_SEQ - 1)
    seq_len = jnp.where(in_lat, DEC_SEQ, SEQ)
    pos0 = jnp.where(in_lat, sub * TILE, 0)

    x = jnp.where(in_lat, xb_ref[...], xa_ref[...])
    h = x * (1.0 + sc1) + sh1
    hp = jnp.where(is_first, 0.0, xp_ref[...] * (1.0 + sc1) + sh1)
    hn = jnp.where(is_last, 0.0, xn_ref[...] * (1.0 + sc1) + sh1)
    hext = jnp.concatenate([hp, h, hn], axis=0)
    ext = TILE + 2 * POOL_HALO
    pos = pos0 + lax.broadcasted_iota(jnp.int32, (TILE, 1), 0)

    outs = []
    for g, w in enumerate(POOL_SIZES):
        lo_c, hi_c = g * POOL_GROUP_DIM, (g + 1) * POOL_GROUP_DIM
        a = hext[:, lo_c:hi_c]
        k = 1
        while k < w:
            a = a + pltpu.roll(a, ext - k, axis=0)
            k *= 2
        off = POOL_HALO - w // 2
        win = pltpu.roll(a, ext - off, axis=0)[:TILE] if off else a[:TILE]
        lo = jnp.maximum(pos - w // 2, 0)
        hi = jnp.minimum(pos - w // 2 + w, seq_len)
        cnt = (hi - lo).astype(F32)
        pooled = win / cnt - h[:, lo_c:hi_c]
        outs.append(jnp.dot(pooled.astype(BF16), pw_ref[g], preferred_element_type=F32))
    mix = jnp.concatenate(outs, axis=1) * ps_ref[...]
    _epilogue(x, mix, m, lng_ref[...], lnb_ref[...], wrt_ref[...], br_ref[...], x1_ref, h2_ref, route_ref)


def _pool_layer(x_ctx, x_lat, mods, pool_w, pool_scale, lng, lnb, wrt, br):
    halo_blocks = TILE // POOL_HALO
    last_halo = T_LAT // POOL_HALO - 1
    return pl.pallas_call(
        _pool_kernel,
        grid=(N_TILES,),
        in_specs=[
            pl.BlockSpec((TILE, D_MODEL), lambda t: (jnp.minimum(t, CTX_TILES - 1), 0)),
            pl.BlockSpec((TILE, D_MODEL), lambda t: (jnp.maximum(t - CTX_TILES, 0), 0)),
            pl.BlockSpec((POOL_HALO, D_MODEL), lambda t: (jnp.maximum((t - CTX_TILES) * halo_blocks - 1, 0), 0)),
            pl.BlockSpec((POOL_HALO, D_MODEL),
                         lambda t: (jnp.clip((t - CTX_TILES + 1) * halo_blocks, 0, last_halo), 0)),
            _mods_spec(0),
            pl.BlockSpec((len(POOL_SIZES), POOL_GROUP_DIM, POOL_GROUP_DIM), lambda t: (0, 0, 0)),
            pl.BlockSpec((1, D_MODEL), lambda t: (0, 0)),
        ] + _EPI_IN_SPECS,
        out_specs=_EPI_OUT_SPECS,
        out_shape=_EPI_OUT_SHAPE,
        compiler_params=_params("arbitrary"),
        name="pool_mixer",
    )(x_ctx, x_lat, x_lat, x_lat, mods, pool_w, pool_scale, lng, lnb, wrt, br)


def _route_plan(route):
    eidx = route[:, 0:2, :].astype(jnp.int32).transpose(1, 0, 2).reshape(2 * T_ALL)
    wsel = route[:, 2:4, :].transpose(1, 0, 2).reshape(2, T_ALL)
    onehot = (eidx[:, None] == jnp.arange(N_EXPERTS, dtype=jnp.int32)[None, :]).astype(jnp.int32)
    counts = jnp.sum(onehot, axis=0)
    rank = jnp.sum((jnp.cumsum(onehot, axis=0) - onehot) * onehot, axis=1)
    padded = (counts + MOE_TILE - 1) // MOE_TILE * MOE_TILE
    ends = jnp.cumsum(padded)
    starts = ends - padded
    pos = jnp.sum(onehot * starts[None, :], axis=1) + rank
    tile_start = jnp.arange(MOE_TILES, dtype=jnp.int32) * MOE_TILE
    tile_expert = jnp.minimum(jnp.sum((ends[None, :] <= tile_start[:, None]).astype(jnp.int32), axis=1),
                              N_EXPERTS - 1)
    tile_valid = jnp.clip(counts[tile_expert] - (tile_start - starts[tile_expert]), 0, MOE_TILE)
    tile_valid = jnp.where(tile_start < ends[-1], tile_valid, 0)
    used_tiles = ends[-1] // MOE_TILE
    tile_block = jnp.minimum(jnp.arange(MOE_TILES, dtype=jnp.int32), used_tiles - 1)
    return pos.astype(jnp.int32), wsel, tile_expert.astype(jnp.int32), tile_valid.astype(jnp.int32), \
        tile_block.astype(jnp.int32)


def _expert_kernel(te_ref, nv_ref, tb_ref, x_ref, wg_ref, wu_ref, wd_ref, y_ref, wgu_s, wd_s):
    i = pl.program_id(0)
    prev = te_ref[jnp.maximum(i - 1, 0)]
    changed = jnp.logical_or(i == 0, te_ref[i] != prev)

    @pl.when(changed)
    def _():
        wgu_s[:, :D_FF] = wg_ref[0, 0].astype(BF16)
        wgu_s[:, D_FF:] = wu_ref[0, 0].astype(BF16)
        wd_s[...] = wd_ref[0, 0].astype(BF16)

    nv = nv_ref[i]

    @pl.when(nv > 0)
    def _():
        rows = lax.broadcasted_iota(jnp.int32, (MOE_TILE, 1), 0)
        xb = jnp.where(rows < nv, x_ref[...], 0.0).astype(BF16)
        gu = jnp.dot(xb, wgu_s[...], preferred_element_type=F32)
        gate, up = gu[:, :D_FF], gu[:, D_FF:]
        he = (gate * jax.nn.sigmoid(gate) * up).astype(BF16)
        y_ref[...] = jnp.dot(he, wd_s[...], preferred_element_type=F32)


def _experts(xs, tile_expert, tile_valid, tile_block, layer, w_gate, w_up, w_down):
    grid_spec = pltpu.PrefetchScalarGridSpec(
        num_scalar_prefetch=3,
        grid=(MOE_TILES,),
        in_specs=[
            pl.BlockSpec((MOE_TILE, D_MODEL), lambda i, te, nv, tb: (tb[i], 0)),
            pl.BlockSpec((1, 1, D_MODEL, D_FF), lambda i, te, nv, tb: (layer, te[i], 0, 0)),
            pl.BlockSpec((1, 1, D_MODEL, D_FF), lambda i, te, nv, tb: (layer, te[i], 0, 0)),
            pl.BlockSpec((1, 1, D_FF, D_MODEL), lambda i, te, nv, tb: (layer, te[i], 0, 0)),
        ],
        out_specs=pl.BlockSpec((MOE_TILE, D_MODEL), lambda i, te, nv, tb: (tb[i], 0)),
        scratch_shapes=[pltpu.VMEM((D_MODEL, 2 * D_FF), BF16), pltpu.VMEM((D_FF, D_MODEL), BF16)],
    )
    return pl.pallas_call(
        _expert_kernel,
        grid_spec=grid_spec,
        out_shape=jax.ShapeDtypeStruct((MOE_ROWS, D_MODEL), F32),
        compiler_params=_params("arbitrary"),
        name="experts",
    )(tile_expert, tile_valid, tile_block, xs, w_gate, w_up, w_down)


def _sc_mesh():
    return plsc.VectorSubcoreMesh(core_axis_name="core", subcore_axis_name="subcore")


SC_WORKERS = 32
SC_ROWS_PER_WORKER = 2 * T_ALL // SC_WORKERS
SC_CHUNKS = SC_ROWS_PER_WORKER // SC_WINDOW


def _sc_worker_base():
    wid = lax.axis_index("subcore") * 2 + lax.axis_index("core")
    return wid * SC_ROWS_PER_WORKER


def _dispatch(h2, pos):
    @functools.partial(
        pl.kernel, out_type=jax.ShapeDtypeStruct((MOE_ROWS, D_MODEL), F32), mesh=_sc_mesh(),
        scratch_types=[pltpu.VMEM((SC_WINDOW,), jnp.int32), pltpu.VMEM((SC_WINDOW, D_MODEL), F32),
                       pltpu.SemaphoreType.DMA],
        name="moe_dispatch")
    def scatter(x_hbm, i_hbm, o_hbm, idx_v, rows_v, sem):
        base = _sc_worker_base()
        src_base = base % T_ALL

        @pl.loop(0, SC_CHUNKS)
        def _(j):
            pltpu.sync_copy(i_hbm.at[pl.ds(base + j * SC_WINDOW, SC_WINDOW)], idx_v)
            pltpu.sync_copy(x_hbm.at[pl.ds(src_base + j * SC_WINDOW, SC_WINDOW)], rows_v)
            pltpu.async_copy(rows_v, o_hbm.at[idx_v], sem).wait()

    return scatter(h2, pos)


def _gather_pairs(y, pos):
    @functools.partial(
        pl.kernel, out_type=jax.ShapeDtypeStruct((2 * T_ALL, D_MODEL), F32), mesh=_sc_mesh(),
        scratch_types=[pltpu.VMEM((SC_WINDOW,), jnp.int32), pltpu.VMEM((SC_WINDOW, D_MODEL), F32),
                       pltpu.SemaphoreType.DMA],
        name="moe_gather")
    def gather(y_hbm, i_hbm, o_hbm, idx_v, rows_v, sem):
        base = _sc_worker_base()

        @pl.loop(0, SC_CHUNKS)
        def _(j):
            off = base + j * SC_WINDOW
            pltpu.sync_copy(i_hbm.at[pl.ds(off, SC_WINDOW)], idx_v)
            pltpu.async_copy(y_hbm.at[idx_v], rows_v, sem).wait()
            pltpu.sync_copy(rows_v, o_hbm.at[pl.ds(off, SC_WINDOW)])

    return gather(y, pos).reshape(2, T_ALL, D_MODEL)


def _combine_kernel(x_ref, yg_ref, w1_ref, w2_ref, mods_ref, lng_ref, lnb_ref, *o_refs):
    m = mods_ref[0, 0]
    g2 = m[5:6, :]
    moe = w1_ref[...] * yg_ref[0] + w2_ref[...] * yg_ref[1]
    res = _post_norm(x_ref[...], g2 * moe, lng_ref[...], lnb_ref[...])
    if len(o_refs) == 1:
        o_refs[0][...] = res
    else:
        in_ctx = pl.program_id(0) < T_CTX // LN_TILE

        @pl.when(in_ctx)
        def _():
            o_refs[0][...] = res

        @pl.when(jnp.logical_not(in_ctx))
        def _():
            o_refs[1][...] = res


def _combine(x1, yg, wsel, mods, layer, lng, lnb, split):
    ctx_tiles = T_CTX // LN_TILE
    if split:
        out_specs = [pl.BlockSpec((LN_TILE, D_MODEL), lambda t: (jnp.minimum(t, ctx_tiles - 1), 0)),
                     pl.BlockSpec((LN_TILE, D_MODEL), lambda t: (jnp.maximum(t - ctx_tiles, 0), 0))]
        out_shape = [jax.ShapeDtypeStruct((T_CTX, D_MODEL), F32), jax.ShapeDtypeStruct((T_LAT, D_MODEL), F32)]
    else:
        out_specs = pl.BlockSpec((LN_TILE, D_MODEL), lambda t: (t, 0))
        out_shape = jax.ShapeDtypeStruct((T_ALL, D_MODEL), F32)
    return pl.pallas_call(
        _combine_kernel,
        grid=(T_ALL // LN_TILE,),
        in_specs=[
            pl.BlockSpec((LN_TILE, D_MODEL), lambda t: (t, 0)),
            pl.BlockSpec((2, LN_TILE, D_MODEL), lambda t: (0, t, 0)),
            pl.BlockSpec((LN_TILE, 1), lambda t: (t, 0)),
            pl.BlockSpec((LN_TILE, 1), lambda t: (t, 0)),
            pl.BlockSpec((1, 1, 6, D_MODEL), lambda t: (layer, _cond_row(t, LN_TILE), 0, 0)),
            pl.BlockSpec((1, D_MODEL), lambda t: (0, 0)),
            pl.BlockSpec((1, D_MODEL), lambda t: (0, 0)),
        ],
        out_specs=out_specs,
        out_shape=out_shape,
        compiler_params=_params("arbitrary"),
        name="moe_combine",
    )(x1, yg, wsel[0].reshape(T_ALL, 1), wsel[1].reshape(T_ALL, 1), mods, lng, lnb)


def _moe(x1, h2, route, mods, layer, w_gate, w_up, w_down, lng, lnb, split=False):
    pos, wsel, tile_expert, tile_valid, tile_block = _route_plan(route)
    xs = _dispatch(h2, pos)
    y = _experts(xs, tile_expert, tile_valid, tile_block, layer, w_gate, w_up, w_down)
    yg = _gather_pairs(y, pos)
    return _combine(x1, yg, wsel, mods, layer, lng, lnb, split)


def _qkv_kernel(x_ref, mods_ref, w_ref, qkv_ref, nk_ref, nv_ref):
    t = pl.program_id(0)
    m = mods_ref[0, 0]
    sh1, sc1 = m[0:1, :], m[1:2, :]
    h = (x_ref[...] * (1.0 + sc1) + sh1).astype(BF16)
    r = jnp.dot(h, w_ref[...], preferred_element_type=F32)
    qkv_ref[:, :D_MODEL] = (r[:, :D_MODEL] * (HEAD_DIM ** -0.5)).astype(BF16)
    qkv_ref[:, D_MODEL:] = r[:, D_MODEL:].astype(BF16)

    @pl.when(t < CTX_TILES)
    def _():
        for out_ref, base in ((nk_ref, D_MODEL), (nv_ref, 2 * D_MODEL)):
            for p in range(HEAD_PAIRS):
                pair = r[:, base + p * LANES: base + (p + 1) * LANES]
                out_ref[0, 0, 2 * p] = pair[:, :HEAD_DIM]
                out_ref[0, 0, 2 * p + 1] = pltpu.roll(pair, HEAD_DIM, axis=1)[:, :HEAD_DIM]


def _qkv(x, mods, w_qkv):
    cache_spec = pl.BlockSpec((1, 1, N_HEADS, SEQ, HEAD_DIM),
                              lambda t: (jnp.minimum(t, CTX_TILES - 1), 0, 0, 0, 0))
    cache_shape = jax.ShapeDtypeStruct((BATCH, 1, N_HEADS, SEQ, HEAD_DIM), F32)
    return pl.pallas_call(
        _qkv_kernel,
        grid=(N_TILES,),
        in_specs=[
            pl.BlockSpec((TILE, D_MODEL), lambda t: (t, 0)),
            _mods_spec(1),
            pl.BlockSpec((D_MODEL, 3 * D_MODEL), lambda t: (0, 0)),
        ],
        out_specs=[pl.BlockSpec((TILE, 3 * D_MODEL), lambda t: (t, 0)), cache_spec, cache_spec],
        out_shape=[jax.ShapeDtypeStruct((T_ALL, 3 * D_MODEL), BF16), cache_shape, cache_shape],
        compiler_params=_params("arbitrary"),
        name="qkv",
    )(x, mods, w_qkv)


def _dot_nt(a, b):
    return lax.dot_general(a, b, (((1,), (1,)), ((), ())), preferred_element_type=F32)


def _head_masks():
    lane = lax.broadcasted_iota(jnp.int32, (1, LANES), 1)
    return lane < HEAD_DIM, lane >= HEAD_DIM


def _ctx_attn_kernel(q_ref, k_ref, v_ref, o_ref):
    left, right = _head_masks()
    for p in range(HEAD_PAIRS):
        cols = slice(p * LANES, (p + 1) * LANES)
        q2, k2, v2 = q_ref[:, cols], k_ref[:, cols], v_ref[:, cols]
        halves = []
        for mask in (left, right):
            qh = jnp.where(mask, q2, jnp.zeros_like(q2))
            s = _dot_nt(qh, k2)
            e = jnp.exp(s - jnp.max(s, axis=-1, keepdims=True))
            o2 = jnp.dot(e.astype(BF16), v2, preferred_element_type=F32)
            halves.append(o2 / jnp.sum(e, axis=-1, keepdims=True))
        o_ref[:, cols] = jnp.where(left, halves[0], halves[1]).astype(BF16)


def _ctx_attention(qkv):
    return pl.pallas_call(
        _ctx_attn_kernel,
        grid=(BATCH,),
        in_specs=[pl.BlockSpec((SEQ, D_MODEL), lambda b, j=j: (b, j)) for j in range(3)],
        out_specs=pl.BlockSpec((SEQ, D_MODEL), lambda b: (b, 0)),
        out_shape=jax.ShapeDtypeStruct((T_ALL, D_MODEL), BF16),
        compiler_params=_params("arbitrary"),
        name="ctx_attention",
    )(qkv, qkv, qkv)


_LAT_Q_BLOCK_ROWS = 4
_LAT_KEY_ROWS = ((0, 8), (0, 12), (4, 16), (8, 16))


def _lat_attn_kernel(q_ref, k_ref, v_ref, ck_ref, cv_ref, eb_ref, o_ctx_ref, o_ref, bias_s):
    del o_ctx_ref
    left, right = _head_masks()

    @pl.when(pl.program_id(1) == 0)
    def _():
        neg = jnp.full((GRID_W, LANES), NEG_INF, F32)
        for hh in range(2):
            for r in range(GRID_ROWS):
                r0 = min(max(r - WIN_H // 2, 0), GRID_ROWS - WIN_H)
                for j in range(GRID_ROWS // 2):
                    parts = []
                    for kr in (2 * j, 2 * j + 1):
                        parts.append(eb_ref[hh, kr - r + WIN_H - 1] if r0 <= kr < r0 + WIN_H else None)
                    if parts[0] is None and parts[1] is None:
                        val = neg
                    else:
                        val = jnp.where(left, neg if parts[0] is None else parts[0],
                                        neg if parts[1] is None else parts[1])
                    bias_s[hh, r * GRID_W:(r + 1) * GRID_W, j * LANES:(j + 1) * LANES] = val

    ck = jnp.concatenate([ck_ref[0, 0, 0], ck_ref[0, 0, 1]], axis=-1).astype(BF16)
    cv = jnp.concatenate([cv_ref[0, 0, 0], cv_ref[0, 0, 1]], axis=-1).astype(BF16)
    qrows = _LAT_Q_BLOCK_ROWS * GRID_W
    for qb, (kr0, kr1) in enumerate(_LAT_KEY_ROWS):
        qs = slice(qb * qrows, (qb + 1) * qrows)
        ks = slice(kr0 * GRID_W, kr1 * GRID_W)
        q2, kk, vv = q_ref[qs, :], k_ref[ks, :], v_ref[ks, :]
        halves = []
        for hh, mask in enumerate((left, right)):
            qh = jnp.where(mask, q2, jnp.zeros_like(q2))
            s_lat = _dot_nt(qh, kk) + bias_s[hh, qs, ks]
            s_ctx = _dot_nt(qh, ck)
            mx = jnp.maximum(jnp.max(s_lat, axis=-1, keepdims=True), jnp.max(s_ctx, axis=-1, keepdims=True))
            e_lat = jnp.exp(s_lat - mx)
            e_ctx = jnp.exp(s_ctx - mx)
            den = jnp.sum(e_lat, axis=-1, keepdims=True) + jnp.sum(e_ctx, axis=-1, keepdims=True)
            o2 = (jnp.dot(e_lat.astype(BF16), vv, preferred_element_type=F32)
                  + jnp.dot(e_ctx.astype(BF16), cv, preferred_element_type=F32))
            halves.append(o2 / den)
        o_ref[qs, :] = jnp.where(left, halves[0], halves[1]).astype(BF16)


def _lat_attention(qkv, cache_k, cache_v, ebias, o_buf):
    row0 = T_CTX // DEC_SEQ
    cache_spec = pl.BlockSpec((1, 1, 2, PAST_LEN, HEAD_DIM), lambda p, b: (b, 0, p, 0, 0))
    return pl.pallas_call(
        _lat_attn_kernel,
        grid=(HEAD_PAIRS, DEC_BATCH),
        in_specs=[
            pl.BlockSpec((DEC_SEQ, LANES), lambda p, b: (row0 + b, p)),
            pl.BlockSpec((DEC_SEQ, LANES), lambda p, b: (row0 + b, HEAD_PAIRS + p)),
            pl.BlockSpec((DEC_SEQ, LANES), lambda p, b: (row0 + b, 2 * HEAD_PAIRS + p)),
            cache_spec,
            cache_spec,
            pl.BlockSpec((2, 2 * WIN_H - 1, GRID_W, LANES), lambda p, b: (p, 0, 0, 0)),
            pl.BlockSpec(memory_space=pl.ANY),
        ],
        out_specs=pl.BlockSpec((DEC_SEQ, LANES), lambda p, b: (row0 + b, p)),
        out_shape=jax.ShapeDtypeStruct((T_ALL, D_MODEL), BF16),
        scratch_shapes=[pltpu.VMEM((2, DEC_SEQ, DEC_SEQ), F32)],
        input_output_aliases={6: 0},
        compiler_params=_params("arbitrary", "arbitrary"),
        name="lat_attention",
    )(qkv, qkv, qkv, cache_k, cache_v, ebias, o_buf)


def _expanded_bias(rel_bias):
    col = jnp.arange(GRID_W)
    col_start = jnp.clip(col - WIN_W // 2, 0, GRID_W - WIN_W)
    col_ok = (col[None, :] >= col_start[:, None]) & (col[None, :] < col_start[:, None] + WIN_W)
    dx = jnp.clip(col[None, :] - col[:, None] + (WIN_W - 1), 0, 2 * WIN_W - 2)
    onehot = (dx[None] == jnp.arange(2 * WIN_W - 1)[:, None, None]).astype(F32)
    eb = jnp.einsum("hyj,jqk->hyqk", rel_bias.astype(F32), onehot, precision=lax.Precision.HIGHEST)
    eb = jnp.where(col_ok[None, None], eb, NEG_INF)
    return jnp.concatenate([eb, eb], axis=-1)


def _wo_kernel(o_ref, x_ref, mods_ref, w_ref, lng_ref, lnb_ref, wrt_ref, br_ref, x1_ref, h2_ref, route_ref):
    mix = jnp.dot(o_ref[...], w_ref[...], preferred_element_type=F32)
    _epilogue(x_ref[...], mix, mods_ref[0, 0], lng_ref[...], lnb_ref[...], wrt_ref[...], br_ref[...],
              x1_ref, h2_ref, route_ref)


def _attn_out(o, x, mods, w_o, lng, lnb, wrt, br):
    return pl.pallas_call(
        _wo_kernel,
        grid=(N_TILES,),
        in_specs=[
            pl.BlockSpec((TILE, D_MODEL), lambda t: (t, 0)),
            pl.BlockSpec((TILE, D_MODEL), lambda t: (t, 0)),
            _mods_spec(1),
            pl.BlockSpec((D_MODEL, D_MODEL), lambda t: (0, 0)),
        ] + _EPI_IN_SPECS,
        out_specs=_EPI_OUT_SPECS,
        out_shape=_EPI_OUT_SHAPE,
        compiler_params=_params("arbitrary"),
        name="attn_out",
    )(o, x, mods, w_o, lng, lnb, wrt, br)


def kernel(x_prompt, x_sample, cache_k, cache_v, c, c_ctx, w_ada, b_ada, ln1_g, ln1_b, ln2_g, ln2_b,
           pool_w, pool_scale, w_qkv, w_o, rel_bias, w_router, b_router, w_gate, w_up, w_down):
    cond = jnp.zeros((N_COND, D_MODEL), F32).at[0].set(c_ctx).at[1:1 + DEC_BATCH].set(c)
    mods = _ada(cond, w_ada, b_ada).reshape(2, N_COND, 6, D_MODEL)
    wrt = w_router.T
    br = b_router.reshape(N_EXPERTS, 1)

    x, h2, route = _pool_layer(x_prompt.reshape(T_CTX, D_MODEL), x_sample.reshape(T_LAT, D_MODEL), mods,
                               pool_w[0].astype(BF16), pool_scale[0:1], ln1_g[0:1], ln1_b[0:1], wrt, br)
    x = _moe(x, h2, route, mods, 0, w_gate, w_up, w_down, ln2_g[0:1], ln2_b[0:1])

    qkv, new_k, new_v = _qkv(x, mods, w_qkv[0].astype(BF16))
    o = _lat_attention(qkv, cache_k, cache_v, _expanded_bias(rel_bias[0]), _ctx_attention(qkv))
    x, h2, route = _attn_out(o, x, mods, w_o[0].astype(BF16), ln1_g[1:2], ln1_b[1:2], wrt, br)
    y_ctx, y_lat = _moe(x, h2, route, mods, 1, w_gate, w_up, w_down, ln2_g[1:2], ln2_b[1:2], split=True)
    return (y_ctx.reshape(BATCH, SEQ, D_MODEL), y_lat.reshape(DEC_BATCH, DEC_SEQ, D_MODEL), new_k, new_v)
```

```python
import functools

import jax
import jax.numpy as jnp
from jax import lax
from jax.experimental import pallas as pl
from jax.experimental.pallas import tpu as pltpu
from jax.experimental.pallas import tpu_sc as plsc

F32 = jnp.float32
BF16 = jnp.bfloat16

D_MODEL = 1024
BATCH = 16
SEQ = 256
DEC_BATCH = 8
DEC_SEQ = 1024
PAST_LEN = 512
GRID_W = 64
GRID_ROWS = DEC_SEQ // GRID_W
POOL_SIZES = (2, 4, 8, 16)
POOL_GROUP_DIM = D_MODEL // len(POOL_SIZES)
POOL_HALO = 8
N_HEADS = 16
HEAD_DIM = 64
WIN_H = 8
WIN_W = 16
N_EXPERTS = 16
EXPERTS_PER_GROUP = 4
N_EXPERT_GROUPS = 4
D_FF = 512
ALPHA = (2.0 * 2) ** 0.25
LN_EPS = 1e-5
NEG_INF = -1e30

T_CTX = BATCH * SEQ
T_LAT = DEC_BATCH * DEC_SEQ
T_ALL = T_CTX + T_LAT
N_COND = 16
TILE = 256
N_TILES = T_ALL // TILE
CTX_TILES = T_CTX // TILE
TILES_PER_LAT_SEQ = DEC_SEQ // TILE
MOE_TILE = 512
MOE_ROWS = 2 * T_ALL + N_EXPERTS * MOE_TILE
MOE_TILES = MOE_ROWS // MOE_TILE
LN_TILE = 512
SC_WINDOW = 64
D_PACK = D_MODEL // 2
ROUTE_ROWS = T_ALL // 128
HEAD_PAIRS = N_HEADS // 2
LANES = 128
VMEM_LIMIT = 56 * 1024 * 1024


def _cond_row(t, tile):
    ctx_tiles = T_CTX // tile
    per_seq = DEC_SEQ // tile
    return jnp.maximum(t - ctx_tiles + per_seq, 0) // per_seq


def _params(*sem):
    return pltpu.CompilerParams(dimension_semantics=sem, vmem_limit_bytes=VMEM_LIMIT)


def _pack_halves(v):
    half = v.shape[1] // 2
    hi = lax.bitcast_convert_type(v[:, :half].astype(F32), jnp.uint32)
    lo = lax.bitcast_convert_type(v[:, half:].astype(F32), jnp.uint32)
    return hi | (lo >> 16)


def _unpack_halves(p):
    left = lax.bitcast_convert_type(p & jnp.uint32(0xFFFF0000), F32)
    right = lax.bitcast_convert_type(p << 16, F32)
    return left, right


def _ada_kernel(cond_ref, w_ref, b_ref, o_ref):
    cnd = cond_ref[...]
    act = cnd * jax.nn.sigmoid(cnd)
    o_ref[0] = jnp.dot(act, w_ref[0], precision=lax.Precision.HIGHEST,
                       preferred_element_type=F32) + b_ref[0]


def _ada(cond, w_ada, b_ada):
    depth, d, n = w_ada.shape
    bn = 1536
    return pl.pallas_call(
        _ada_kernel,
        grid=(depth, n // bn),
        in_specs=[
            pl.BlockSpec((N_COND, d), lambda i, j: (0, 0)),
            pl.BlockSpec((1, d, bn), lambda i, j: (i, 0, j)),
            pl.BlockSpec((1, 1, bn), lambda i, j: (i, 0, j)),
        ],
        out_specs=pl.BlockSpec((1, N_COND, bn), lambda i, j: (i, 0, j)),
        out_shape=jax.ShapeDtypeStruct((depth, N_COND, n), F32),
        compiler_params=_params("arbitrary", "arbitrary"),
        name="ada",
    )(cond, w_ada, b_ada.reshape(depth, 1, n))


def _post_norm(x, upd, g, b):
    y = ALPHA * x + upd
    mu = jnp.mean(y, axis=-1, keepdims=True)
    yc = y - mu
    var = jnp.mean(yc * yc, axis=-1, keepdims=True)
    return yc * lax.rsqrt(var + LN_EPS) * g + b


def _router_logits(h2, h_hi, w2_ref):
    h_lo = (h2 - h_hi.astype(F32)).astype(BF16)
    w2 = w2_ref[...]
    p = _dot_nt(w2, h_hi)
    q = _dot_nt(w2[:N_EXPERTS], h_lo)
    return p[:N_EXPERTS] + p[N_EXPERTS:] + q


def _epilogue(x, mix, m, lng, lnb, w2_ref, x1_ref, h2p_ref, logit_ref):
    g1, sh2, sc2 = m[2:3, :], m[3:4, :], m[4:5, :]
    x1 = _post_norm(x, g1 * mix, lng, lnb)
    h2 = x1 * (1.0 + sc2) + sh2
    h_hi = h2.astype(BF16)
    x1_ref[...] = x1
    h2p_ref[...] = _pack_halves(h_hi)
    logit_ref[...] = _router_logits(h2, h_hi, w2_ref)


_EPI_IN_SPECS = [
    pl.BlockSpec((1, D_MODEL), lambda t: (0, 0)),
    pl.BlockSpec((1, D_MODEL), lambda t: (0, 0)),
    pl.BlockSpec((2 * N_EXPERTS, D_MODEL), lambda t: (0, 0)),
]
_EPI_OUT_SPECS = [
    pl.BlockSpec((TILE, D_MODEL), lambda t: (t, 0)),
    pl.BlockSpec((TILE, D_PACK), lambda t: (t, 0)),
    pl.BlockSpec((N_EXPERTS, TILE), lambda t: (0, t)),
]
_EPI_OUT_SHAPE = [
    jax.ShapeDtypeStruct((T_ALL, D_MODEL), F32),
    jax.ShapeDtypeStruct((T_ALL, D_PACK), jnp.uint32),
    jax.ShapeDtypeStruct((N_EXPERTS, T_ALL), F32),
]


def _route_kernel(logit_ref, br_ref, e_ref, w_ref):
    aff = jax.nn.sigmoid(logit_ref[...])
    sel = aff + br_ref[...]
    sel_rows = [sel[e] for e in range(N_EXPERTS)]
    aff_rows = [aff[e] for e in range(N_EXPERTS)]

    def group_score(g):
        r = sel_rows[g * EXPERTS_PER_GROUP:(g + 1) * EXPERTS_PER_GROUP]
        best = None
        for i in range(EXPERTS_PER_GROUP):
            for j in range(i + 1, EXPERTS_PER_GROUP):
                pair = r[i] + r[j]
                best = pair if best is None else jnp.maximum(best, pair)
        return best

    best = group_score(0)
    gidx = jnp.zeros_like(best, dtype=jnp.int32)
    for g in range(1, N_EXPERT_GROUPS):
        sc = group_score(g)
        better = sc > best
        gidx = jnp.where(better, g, gidx)
        best = jnp.where(better, sc, best)

    def pick_group(rows, j):
        out = rows[j]
        for g in range(1, N_EXPERT_GROUPS):
            out = jnp.where(gidx == g, rows[g * EXPERTS_PER_GROUP + j], out)
        return out

    cand = [pick_group(sel_rows, j) for j in range(EXPERTS_PER_GROUP)]
    cand_aff = [pick_group(aff_rows, j) for j in range(EXPERTS_PER_GROUP)]

    def argmax_first(vals):
        bv, bi, ba = vals[0], jnp.zeros_like(gidx), cand_aff[0]
        for j in range(1, EXPERTS_PER_GROUP):
            better = vals[j] > bv
            bv = jnp.where(better, vals[j], bv)
            bi = jnp.where(better, j, bi)
            ba = jnp.where(better, cand_aff[j], ba)
        return bi, ba

    i1, a1 = argmax_first(cand)
    rest = [jnp.where(i1 == j, -jnp.inf, cand[j]) for j in range(EXPERTS_PER_GROUP)]
    i2, a2 = argmax_first(rest)
    denom = a1 + a2
    base = gidx * EXPERTS_PER_GROUP
    e_ref[0] = base + i1
    e_ref[1] = base + i2
    w_ref[0] = a1 / denom
    w_ref[1] = a2 / denom


def _route(logits, b_router):
    return pl.pallas_call(
        _route_kernel,
        out_shape=[jax.ShapeDtypeStruct((2, ROUTE_ROWS, 128), jnp.int32),
                   jax.ShapeDtypeStruct((2, ROUTE_ROWS, 128), F32)],
        compiler_params=pltpu.CompilerParams(vmem_limit_bytes=VMEM_LIMIT),
        name="route",
    )(logits.reshape(N_EXPERTS, ROUTE_ROWS, 128), b_router.reshape(N_EXPERTS, 1, 1))


def _mods_spec(layer):
    return pl.BlockSpec((1, 1, 6, D_MODEL), lambda t: (layer, _cond_row(t, TILE), 0, 0))


def _pool_kernel(xa_ref, xb_ref, xp_ref, xn_ref, mods_ref, pw_ref, ps_ref, lng_ref, lnb_ref, w2_ref,
                 x1_ref, h2p_ref, logit_ref):
    t = pl.program_id(0)
    m = mods_ref[0, 0]
    sh1, sc1 = m[0:1, :], m[1:2, :]
    in_lat = t >= CTX_TILES
    sub = (t - CTX_TILES) % TILES_PER_LAT_SEQ
    is_first = jnp.logical_or(jnp.logical_not(in_lat), sub == 0)
    is_last = jnp.logical_or(jnp.logical_not(in_lat), sub == TILES_PER_LAT_SEQ - 1)
    seq_len = jnp.where(in_lat, DEC_SEQ, SEQ)
    pos0 = jnp.where(in_lat, sub * TILE, 0)

    x = jnp.where(in_lat, xb_ref[...], xa_ref[...])
    h = x * (1.0 + sc1) + sh1
    hp = jnp.where(is_first, 0.0, xp_ref[...] * (1.0 + sc1) + sh1)
    hn = jnp.where(is_last, 0.0, xn_ref[...] * (1.0 + sc1) + sh1)
    hext = jnp.concatenate([hp, h, hn], axis=0)
    ext = TILE + 2 * POOL_HALO
    pos = pos0 + lax.broadcasted_iota(jnp.int32, (TILE, 1), 0)

    outs = []
    for g, w in enumerate(POOL_SIZES):
        lo_c, hi_c = g * POOL_GROUP_DIM, (g + 1) * POOL_GROUP_DIM
        a = hext[:, lo_c:hi_c]
        k = 1
        while k < w:
            a = a + pltpu.roll(a, ext - k, axis=0)
            k *= 2
        off = POOL_HALO - w // 2
        win = pltpu.roll(a, ext - off, axis=0)[:TILE] if off else a[:TILE]
        lo = jnp.maximum(pos - w // 2, 0)
        hi = jnp.minimum(pos - w // 2 + w, seq_len)
        cnt = (hi - lo).astype(F32)
        pooled = win / cnt - h[:, lo_c:hi_c]
        outs.append(jnp.dot(pooled.astype(BF16), pw_ref[g], preferred_element_type=F32))
    mix = jnp.concatenate(outs, axis=1) * ps_ref[...]
    _epilogue(x, mix, m, lng_ref[...], lnb_ref[...], w2_ref, x1_ref, h2p_ref, logit_ref)


def _pool_layer(x_ctx, x_lat, mods, pool_w, pool_scale, lng, lnb, w2):
    halo_blocks = TILE // POOL_HALO
    last_halo = T_LAT // POOL_HALO - 1
    return pl.pallas_call(
        _pool_kernel,
        grid=(N_TILES,),
        in_specs=[
            pl.BlockSpec((TILE, D_MODEL), lambda t: (jnp.minimum(t, CTX_TILES - 1), 0)),
            pl.BlockSpec((TILE, D_MODEL), lambda t: (jnp.maximum(t - CTX_TILES, 0), 0)),
            pl.BlockSpec((POOL_HALO, D_MODEL), lambda t: (jnp.maximum((t - CTX_TILES) * halo_blocks - 1, 0), 0)),
            pl.BlockSpec((POOL_HALO, D_MODEL),
                         lambda t: (jnp.clip((t - CTX_TILES + 1) * halo_blocks, 0, last_halo), 0)),
            _mods_spec(0),
            pl.BlockSpec((len(POOL_SIZES), POOL_GROUP_DIM, POOL_GROUP_DIM), lambda t: (0, 0, 0)),
            pl.BlockSpec((1, D_MODEL), lambda t: (0, 0)),
        ] + _EPI_IN_SPECS,
        out_specs=_EPI_OUT_SPECS,
        out_shape=_EPI_OUT_SHAPE,
        compiler_params=_params("arbitrary"),
        name="pool_mixer",
    )(x_ctx, x_lat, x_lat, x_lat, mods, pool_w, pool_scale, lng, lnb, w2)


def _route_plan(eidx):
    eidx = eidx.reshape(2 * T_ALL)
    experts = jnp.arange(N_EXPERTS, dtype=jnp.int32)
    onehot = (eidx[:, None] == experts[None, :]).astype(jnp.int32)
    counts = jnp.sum(onehot, axis=0)
    rank = jnp.sum((jnp.cumsum(onehot, axis=0) - onehot) * onehot, axis=1)
    padded = (counts + MOE_TILE - 1) // MOE_TILE * MOE_TILE
    ends = jnp.cumsum(padded)
    starts = ends - padded
    pos = jnp.sum(onehot * starts[None, :], axis=1) + rank
    tile_start = jnp.arange(MOE_TILES, dtype=jnp.int32)[:, None] * MOE_TILE
    in_region = ((tile_start >= starts[None, :]) & (tile_start < ends[None, :])).astype(jnp.int32)
    used = jnp.sum(in_region, axis=1) > 0
    last_expert = jnp.max(jnp.where(counts > 0, experts, 0))
    tile_expert = jnp.where(used, jnp.sum(in_region * experts[None, :], axis=1), last_expert)
    tile_valid = jnp.sum(in_region * jnp.clip(counts[None, :] - (tile_start - starts[None, :]), 0, MOE_TILE), axis=1)
    tile_block = jnp.minimum(jnp.arange(MOE_TILES, dtype=jnp.int32), ends[-1] // MOE_TILE - 1)
    return pos.astype(jnp.int32), tile_expert.astype(jnp.int32), tile_valid.astype(jnp.int32), \
        tile_block.astype(jnp.int32)


def _expert_kernel(te_ref, nv_ref, tb_ref, x_ref, wg_ref, wu_ref, wd_ref, y_ref, wgu_s, wd_s):
    i = pl.program_id(0)
    prev = te_ref[jnp.maximum(i - 1, 0)]
    changed = jnp.logical_or(i == 0, te_ref[i] != prev)

    @pl.when(changed)
    def _():
        wgu_s[:, :D_FF] = wg_ref[0, 0].astype(BF16)
        wgu_s[:, D_FF:] = wu_ref[0, 0].astype(BF16)
        wd_s[...] = wd_ref[0, 0].astype(BF16)

    nv = nv_ref[i]

    @pl.when(nv > 0)
    def _():
        rows = lax.broadcasted_iota(jnp.int32, (MOE_TILE, 1), 0)
        xp = jnp.where(rows < nv, x_ref[...], jnp.uint32(0))
        left, right = _unpack_halves(xp)
        xb = jnp.concatenate([left.astype(BF16), right.astype(BF16)], axis=1)
        gu = jnp.dot(xb, wgu_s[...], preferred_element_type=F32)
        gate, up = gu[:, :D_FF], gu[:, D_FF:]
        he = (gate * jax.nn.sigmoid(gate) * up).astype(BF16)
        y_ref[...] = _pack_halves(jnp.dot(he, wd_s[...], preferred_element_type=F32).astype(BF16))


def _experts(xs, tile_expert, tile_valid, tile_block, layer, w_gate, w_up, w_down):
    grid_spec = pltpu.PrefetchScalarGridSpec(
        num_scalar_prefetch=3,
        grid=(MOE_TILES,),
        in_specs=[
            pl.BlockSpec((MOE_TILE, D_PACK), lambda i, te, nv, tb: (tb[i], 0)),
            pl.BlockSpec((1, 1, D_MODEL, D_FF), lambda i, te, nv, tb: (layer, te[i], 0, 0)),
            pl.BlockSpec((1, 1, D_MODEL, D_FF), lambda i, te, nv, tb: (layer, te[i], 0, 0)),
            pl.BlockSpec((1, 1, D_FF, D_MODEL), lambda i, te, nv, tb: (layer, te[i], 0, 0)),
        ],
        out_specs=pl.BlockSpec((MOE_TILE, D_PACK), lambda i, te, nv, tb: (tb[i], 0)),
        scratch_shapes=[pltpu.VMEM((D_MODEL, 2 * D_FF), BF16), pltpu.VMEM((D_FF, D_MODEL), BF16)],
    )
    return pl.pallas_call(
        _expert_kernel,
        grid_spec=grid_spec,
        out_shape=jax.ShapeDtypeStruct((MOE_ROWS, D_PACK), jnp.uint32),
        compiler_params=_params("arbitrary"),
        name="experts",
    )(tile_expert, tile_valid, tile_block, xs, w_gate, w_up, w_down)


def _sc_mesh():
    return plsc.VectorSubcoreMesh(core_axis_name="core", subcore_axis_name="subcore")


SC_WORKERS = 32
SC_ROWS_PER_WORKER = 2 * T_ALL // SC_WORKERS
SC_CHUNKS = SC_ROWS_PER_WORKER // SC_WINDOW


def _sc_worker_base():
    wid = lax.axis_index("subcore") * 2 + lax.axis_index("core")
    return wid * SC_ROWS_PER_WORKER


def _dispatch(h2, pos):
    @functools.partial(
        pl.kernel, out_type=jax.ShapeDtypeStruct((MOE_ROWS, D_PACK), jnp.uint32), mesh=_sc_mesh(),
        scratch_types=[pltpu.VMEM((SC_WINDOW,), jnp.int32), pltpu.VMEM((SC_WINDOW, D_PACK), jnp.uint32),
                       pltpu.SemaphoreType.DMA],
        name="moe_dispatch")
    def scatter(x_hbm, i_hbm, o_hbm, idx_v, rows_v, sem):
        base = _sc_worker_base()
        src_base = base % T_ALL

        @pl.loop(0, SC_CHUNKS)
        def _(j):
            pltpu.sync_copy(i_hbm.at[pl.ds(base + j * SC_WINDOW, SC_WINDOW)], idx_v)
            pltpu.sync_copy(x_hbm.at[pl.ds(src_base + j * SC_WINDOW, SC_WINDOW)], rows_v)
            pltpu.async_copy(rows_v, o_hbm.at[idx_v], sem).wait()

    return scatter(h2, pos)


def _gather_pairs(y, pos):
    @functools.partial(
        pl.kernel, out_type=jax.ShapeDtypeStruct((2 * T_ALL, D_PACK), jnp.uint32), mesh=_sc_mesh(),
        scratch_types=[pltpu.VMEM((SC_WINDOW,), jnp.int32), pltpu.VMEM((SC_WINDOW, D_PACK), jnp.uint32),
                       pltpu.SemaphoreType.DMA],
        name="moe_gather")
    def gather(y_hbm, i_hbm, o_hbm, idx_v, rows_v, sem):
        base = _sc_worker_base()

        @pl.loop(0, SC_CHUNKS)
        def _(j):
            off = base + j * SC_WINDOW
            pltpu.sync_copy(i_hbm.at[pl.ds(off, SC_WINDOW)], idx_v)
            pltpu.async_copy(y_hbm.at[idx_v], rows_v, sem).wait()
            pltpu.sync_copy(rows_v, o_hbm.at[pl.ds(off, SC_WINDOW)])

    return gather(y, pos).reshape(2, T_ALL, D_PACK)


def _combine_kernel(x_ref, yg_ref, w1_ref, w2_ref, mods_ref, lng_ref, lnb_ref, *o_refs):
    m = mods_ref[0, 0]
    g2 = m[5:6, :]
    y1 = jnp.concatenate(_unpack_halves(yg_ref[0]), axis=1)
    y2 = jnp.concatenate(_unpack_halves(yg_ref[1]), axis=1)
    moe = w1_ref[...] * y1 + w2_ref[...] * y2
    res = _post_norm(x_ref[...], g2 * moe, lng_ref[...], lnb_ref[...])
    if len(o_refs) == 1:
        o_refs[0][...] = res
    else:
        in_ctx = pl.program_id(0) < T_CTX // LN_TILE

        @pl.when(in_ctx)
        def _():
            o_refs[0][...] = res

        @pl.when(jnp.logical_not(in_ctx))
        def _():
            o_refs[1][...] = res


def _combine(x1, yg, wsel, mods, layer, lng, lnb, split):
    ctx_tiles = T_CTX // LN_TILE
    if split:
        out_specs = [pl.BlockSpec((LN_TILE, D_MODEL), lambda t: (jnp.minimum(t, ctx_tiles - 1), 0)),
                     pl.BlockSpec((LN_TILE, D_MODEL), lambda t: (jnp.maximum(t - ctx_tiles, 0), 0))]
        out_shape = [jax.ShapeDtypeStruct((T_CTX, D_MODEL), F32), jax.ShapeDtypeStruct((T_LAT, D_MODEL), F32)]
    else:
        out_specs = pl.BlockSpec((LN_TILE, D_MODEL), lambda t: (t, 0))
        out_shape = jax.ShapeDtypeStruct((T_ALL, D_MODEL), F32)
    return pl.pallas_call(
        _combine_kernel,
        grid=(T_ALL // LN_TILE,),
        in_specs=[
            pl.BlockSpec((LN_TILE, D_MODEL), lambda t: (t, 0)),
            pl.BlockSpec((2, LN_TILE, D_PACK), lambda t: (0, t, 0)),
            pl.BlockSpec((LN_TILE, 1), lambda t: (t, 0)),
            pl.BlockSpec((LN_TILE, 1), lambda t: (t, 0)),
            pl.BlockSpec((1, 1, 6, D_MODEL), lambda t: (layer, _cond_row(t, LN_TILE), 0, 0)),
            pl.BlockSpec((1, D_MODEL), lambda t: (0, 0)),
            pl.BlockSpec((1, D_MODEL), lambda t: (0, 0)),
        ],
        out_specs=out_specs,
        out_shape=out_shape,
        compiler_params=_params("arbitrary"),
        name="moe_combine",
    )(x1, yg, wsel[0].reshape(T_ALL, 1), wsel[1].reshape(T_ALL, 1), mods, lng, lnb)


def _moe(x1, h2p, logits, b_router, mods, layer, w_gate, w_up, w_down, lng, lnb, split=False):
    eidx, wsel = _route(logits, b_router)
    wsel = wsel.reshape(2, T_ALL)
    pos, tile_expert, tile_valid, tile_block = _route_plan(eidx)
    xs = _dispatch(h2p, pos)
    y = _experts(xs, tile_expert, tile_valid, tile_block, layer, w_gate, w_up, w_down)
    yg = _gather_pairs(y, pos)
    return _combine(x1, yg, wsel, mods, layer, lng, lnb, split)


def _qkv_kernel(x_ref, mods_ref, w_ref, qkv_ref, nk_ref, nv_ref):
    t = pl.program_id(0)
    m = mods_ref[0, 0]
    sh1, sc1 = m[0:1, :], m[1:2, :]
    h = (x_ref[...] * (1.0 + sc1) + sh1).astype(BF16)
    r = jnp.dot(h, w_ref[...], preferred_element_type=F32)
    qkv_ref[:, :D_MODEL] = (r[:, :D_MODEL] * (HEAD_DIM ** -0.5)).astype(BF16)
    qkv_ref[:, D_MODEL:] = r[:, D_MODEL:].astype(BF16)

    @pl.when(t < CTX_TILES)
    def _():
        for out_ref, base in ((nk_ref, D_MODEL), (nv_ref, 2 * D_MODEL)):
            for p in range(HEAD_PAIRS):
                pair = r[:, base + p * LANES: base + (p + 1) * LANES]
                out_ref[0, 0, 2 * p] = pair[:, :HEAD_DIM]
                out_ref[0, 0, 2 * p + 1] = pltpu.roll(pair, HEAD_DIM, axis=1)[:, :HEAD_DIM]


def _qkv(x, mods, w_qkv):
    cache_spec = pl.BlockSpec((1, 1, N_HEADS, SEQ, HEAD_DIM),
                              lambda t: (jnp.minimum(t, CTX_TILES - 1), 0, 0, 0, 0))
    cache_shape = jax.ShapeDtypeStruct((BATCH, 1, N_HEADS, SEQ, HEAD_DIM), F32)
    return pl.pallas_call(
        _qkv_kernel,
        grid=(N_TILES,),
        in_specs=[
            pl.BlockSpec((TILE, D_MODEL), lambda t: (t, 0)),
            _mods_spec(1),
            pl.BlockSpec((D_MODEL, 3 * D_MODEL), lambda t: (0, 0)),
        ],
        out_specs=[pl.BlockSpec((TILE, 3 * D_MODEL), lambda t: (t, 0)), cache_spec, cache_spec],
        out_shape=[jax.ShapeDtypeStruct((T_ALL, 3 * D_MODEL), BF16), cache_shape, cache_shape],
        compiler_params=_params("arbitrary"),
        name="qkv",
    )(x, mods, w_qkv)


def _dot_nt(a, b):
    return lax.dot_general(a, b, (((1,), (1,)), ((), ())), preferred_element_type=F32)


def _head_masks():
    lane = lax.broadcasted_iota(jnp.int32, (1, LANES), 1)
    return lane < HEAD_DIM, lane >= HEAD_DIM


def _ctx_attn_kernel(q_ref, k_ref, v_ref, o_ref):
    left, right = _head_masks()
    for p in range(HEAD_PAIRS):
        cols = slice(p * LANES, (p + 1) * LANES)
        q2, k2, v2 = q_ref[:, cols], k_ref[:, cols], v_ref[:, cols]
        halves = []
        for mask in (left, right):
            qh = jnp.where(mask, q2, jnp.zeros_like(q2))
            s = _dot_nt(qh, k2)
            e = jnp.exp(s - jnp.max(s, axis=-1, keepdims=True))
            o2 = jnp.dot(e.astype(BF16), v2, preferred_element_type=F32)
            halves.append(o2 / jnp.sum(e, axis=-1, keepdims=True))
        o_ref[:, cols] = jnp.where(left, halves[0], halves[1]).astype(BF16)


def _ctx_attention(qkv):
    return pl.pallas_call(
        _ctx_attn_kernel,
        grid=(BATCH,),
        in_specs=[pl.BlockSpec((SEQ, D_MODEL), lambda b, j=j: (b, j)) for j in range(3)],
        out_specs=pl.BlockSpec((SEQ, D_MODEL), lambda b: (b, 0)),
        out_shape=jax.ShapeDtypeStruct((T_ALL, D_MODEL), BF16),
        compiler_params=_params("arbitrary"),
        name="ctx_attention",
    )(qkv, qkv, qkv)


_LAT_Q_BLOCK_ROWS = 4
_LAT_KEY_ROWS = ((0, 8), (0, 12), (4, 16), (8, 16))


def _lat_attn_kernel(q_ref, k_ref, v_ref, ck_ref, cv_ref, eb_ref, o_ctx_ref, o_ref, bias_s):
    del o_ctx_ref
    left, right = _head_masks()

    @pl.when(pl.program_id(1) == 0)
    def _():
        neg = jnp.full((GRID_W, LANES), NEG_INF, F32)
        for hh in range(2):
            for r in range(GRID_ROWS):
                r0 = min(max(r - WIN_H // 2, 0), GRID_ROWS - WIN_H)
                for j in range(GRID_ROWS // 2):
                    parts = []
                    for kr in (2 * j, 2 * j + 1):
                        parts.append(eb_ref[hh, kr - r + WIN_H - 1] if r0 <= kr < r0 + WIN_H else None)
                    if parts[0] is None and parts[1] is None:
                        val = neg
                    else:
                        val = jnp.where(left, neg if parts[0] is None else parts[0],
                                        neg if parts[1] is None else parts[1])
                    bias_s[hh, r * GRID_W:(r + 1) * GRID_W, j * LANES:(j + 1) * LANES] = val

    ck = jnp.concatenate([ck_ref[0, 0, 0], ck_ref[0, 0, 1]], axis=0).astype(BF16)
    cv = jnp.concatenate([cv_ref[0, 0, 0], cv_ref[0, 0, 1]], axis=0).astype(BF16)
    qrows = _LAT_Q_BLOCK_ROWS * GRID_W
    for qb, (kr0, kr1) in enumerate(_LAT_KEY_ROWS):
        qs = slice(qb * qrows, (qb + 1) * qrows)
        ks = slice(kr0 * GRID_W, kr1 * GRID_W)
        q2, kk, vv = q_ref[qs, :], k_ref[ks, :], v_ref[ks, :]
        halves = []
        for hh, mask in enumerate((left, right)):
            qh = jnp.where(mask, q2, jnp.zeros_like(q2))
            s_lat = _dot_nt(qh, kk) + bias_s[hh, qs, ks]
            s_ctx = jnp.dot(qh, ck, preferred_element_type=F32)
            mx = jnp.maximum(jnp.max(s_lat, axis=-1, keepdims=True), jnp.max(s_ctx, axis=-1, keepdims=True))
            e_lat = jnp.exp(s_lat - mx)
            e_ctx = jnp.exp(s_ctx - mx)
            den = jnp.sum(e_lat, axis=-1, keepdims=True) + jnp.sum(e_ctx, axis=-1, keepdims=True)
            o2 = (jnp.dot(e_lat.astype(BF16), vv, preferred_element_type=F32)
                  + _dot_nt(e_ctx.astype(BF16), cv))
            halves.append(o2 / den)
        o_ref[qs, :] = jnp.where(left, halves[0], halves[1]).astype(BF16)


def _lat_attention(qkv, cache_kt, cache_vt, ebias, o_buf):
    row0 = T_CTX // DEC_SEQ
    cache_spec = pl.BlockSpec((1, 1, 2, HEAD_DIM, PAST_LEN), lambda p, b: (b, 0, p, 0, 0))
    return pl.pallas_call(
        _lat_attn_kernel,
        grid=(HEAD_PAIRS, DEC_BATCH),
        in_specs=[
            pl.BlockSpec((DEC_SEQ, LANES), lambda p, b: (row0 + b, p)),
            pl.BlockSpec((DEC_SEQ, LANES), lambda p, b: (row0 + b, HEAD_PAIRS + p)),
            pl.BlockSpec((DEC_SEQ, LANES), lambda p, b: (row0 + b, 2 * HEAD_PAIRS + p)),
            cache_spec,
            cache_spec,
            pl.BlockSpec((2, 2 * WIN_H - 1, GRID_W, LANES), lambda p, b: (p, 0, 0, 0)),
            pl.BlockSpec(memory_space=pl.ANY),
        ],
        out_specs=pl.BlockSpec((DEC_SEQ, LANES), lambda p, b: (row0 + b, p)),
        out_shape=jax.ShapeDtypeStruct((T_ALL, D_MODEL), BF16),
        scratch_shapes=[pltpu.VMEM((2, DEC_SEQ, DEC_SEQ), F32)],
        input_output_aliases={6: 0},
        compiler_params=_params("arbitrary", "arbitrary"),
        name="lat_attention",
    )(qkv, qkv, qkv, cache_kt, cache_vt, ebias, o_buf)


def _expanded_bias(rel_bias):
    col = jnp.arange(GRID_W)
    col_start = jnp.clip(col - WIN_W // 2, 0, GRID_W - WIN_W)
    col_ok = (col[None, :] >= col_start[:, None]) & (col[None, :] < col_start[:, None] + WIN_W)
    dx = jnp.clip(col[None, :] - col[:, None] + (WIN_W - 1), 0, 2 * WIN_W - 2)
    onehot = (dx[None] == jnp.arange(2 * WIN_W - 1)[:, None, None]).astype(F32)
    eb = jnp.einsum("hyj,jqk->hyqk", rel_bias.astype(F32), onehot, precision=lax.Precision.HIGHEST)
    eb = jnp.where(col_ok[None, None], eb, NEG_INF)
    return jnp.concatenate([eb, eb], axis=-1)


def _wo_kernel(o_ref, x_ref, mods_ref, w_ref, lng_ref, lnb_ref, w2_ref, x1_ref, h2p_ref, logit_ref):
    mix = jnp.dot(o_ref[...], w_ref[...], preferred_element_type=F32)
    _epilogue(x_ref[...], mix, mods_ref[0, 0], lng_ref[...], lnb_ref[...], w2_ref, x1_ref, h2p_ref, logit_ref)


def _attn_out(o, x, mods, w_o, lng, lnb, w2):
    return pl.pallas_call(
        _wo_kernel,
        grid=(N_TILES,),
        in_specs=[
            pl.BlockSpec((TILE, D_MODEL), lambda t: (t, 0)),
            pl.BlockSpec((TILE, D_MODEL), lambda t: (t, 0)),
            _mods_spec(1),
            pl.BlockSpec((D_MODEL, D_MODEL), lambda t: (0, 0)),
        ] + _EPI_IN_SPECS,
        out_specs=_EPI_OUT_SPECS,
        out_shape=_EPI_OUT_SHAPE,
        compiler_params=_params("arbitrary"),
        name="attn_out",
    )(o, x, mods, w_o, lng, lnb, w2)


def kernel(x_prompt, x_sample, cache_k, cache_v, c, c_ctx, w_ada, b_ada, ln1_g, ln1_b, ln2_g, ln2_b,
           pool_w, pool_scale, w_qkv, w_o, rel_bias, w_router, b_router, w_gate, w_up, w_down):
    cond = jnp.zeros((N_COND, D_MODEL), F32).at[0].set(c_ctx).at[1:1 + DEC_BATCH].set(c)
    mods = _ada(cond, w_ada, b_ada).reshape(2, N_COND, 6, D_MODEL)
    wrt = w_router.T
    wrt_hi = wrt.astype(BF16)
    w2 = jnp.concatenate([wrt_hi, (wrt - wrt_hi.astype(F32)).astype(BF16)], axis=0)

    x, h2p, logits = _pool_layer(x_prompt.reshape(T_CTX, D_MODEL), x_sample.reshape(T_LAT, D_MODEL), mods,
                                 pool_w[0].astype(BF16), pool_scale[0:1], ln1_g[0:1], ln1_b[0:1], w2)
    x = _moe(x, h2p, logits, b_router, mods, 0, w_gate, w_up, w_down, ln2_g[0:1], ln2_b[0:1])

    qkv, new_k, new_v = _qkv(x, mods, w_qkv[0].astype(BF16))
    o = _lat_attention(qkv, cache_k.transpose(0, 1, 2, 4, 3), cache_v.transpose(0, 1, 2, 4, 3),
                       _expanded_bias(rel_bias[0]), _ctx_attention(qkv))
    x, h2p, logits = _attn_out(o, x, mods, w_o[0].astype(BF16), ln1_g[1:2], ln1_b[1:2], w2)
    y_ctx, y_lat = _moe(x, h2p, logits, b_router, mods, 1, w_gate, w_up, w_down, ln2_g[1:2], ln2_b[1:2],
                        split=True)
    return (y_ctx.reshape(BATCH, SEQ, D_MODEL), y_lat.reshape(DEC_BATCH, DEC_SEQ, D_MODEL), new_k, new_v)
```

```python
import functools

import jax
import jax.numpy as jnp
from jax import lax
from jax.experimental import pallas as pl
from jax.experimental.pallas import tpu as pltpu
from jax.experimental.pallas import tpu_sc as plsc

F32 = jnp.float32
BF16 = jnp.bfloat16

D_MODEL = 1024
BATCH = 16
SEQ = 256
DEC_BATCH = 8
DEC_SEQ = 1024
PAST_LEN = 512
GRID_W = 64
GRID_ROWS = DEC_SEQ // GRID_W
POOL_SIZES = (2, 4, 8, 16)
POOL_GROUP_DIM = D_MODEL // len(POOL_SIZES)
POOL_HALO = 8
N_HEADS = 16
HEAD_DIM = 64
WIN_H = 8
WIN_W = 16
N_EXPERTS = 16
EXPERTS_PER_GROUP = 4
N_EXPERT_GROUPS = 4
D_FF = 512
ALPHA = (2.0 * 2) ** 0.25
LN_EPS = 1e-5
NEG_INF = -1e30

T_CTX = BATCH * SEQ
T_LAT = DEC_BATCH * DEC_SEQ
T_ALL = T_CTX + T_LAT
N_COND = 16
TILE = 256
N_TILES = T_ALL // TILE
CTX_TILES = T_CTX // TILE
TILES_PER_LAT_SEQ = DEC_SEQ // TILE
MOE_TILE = 512
MOE_SUB = 256
SEGMENTS = ((0, T_CTX), (T_CTX, T_LAT))
LN_TILE = 512
SC_WINDOW = 64
D_PACK = D_MODEL // 2
ROUTE_ROWS = T_ALL // 128
HEAD_PAIRS = N_HEADS // 2
LANES = 128
VMEM_LIMIT = 56 * 1024 * 1024


def _cond_row(t, tile):
    ctx_tiles = T_CTX // tile
    per_seq = DEC_SEQ // tile
    return jnp.maximum(t - ctx_tiles + per_seq, 0) // per_seq


def _params(*sem):
    return pltpu.CompilerParams(dimension_semantics=sem, vmem_limit_bytes=VMEM_LIMIT)


def _pack_halves(v):
    half = v.shape[1] // 2
    hi = lax.bitcast_convert_type(v[:, :half].astype(F32), jnp.uint32)
    lo = lax.bitcast_convert_type(v[:, half:].astype(F32), jnp.uint32)
    return hi | (lo >> 16)


def _unpack_halves(p):
    left = lax.bitcast_convert_type(p & jnp.uint32(0xFFFF0000), F32)
    right = lax.bitcast_convert_type(p << 16, F32)
    return left, right


def _ada_kernel(cond_ref, w_ref, b_ref, o_ref):
    cnd = cond_ref[...]
    act = cnd * jax.nn.sigmoid(cnd)
    a_hi = act.astype(BF16)
    a_lo = (act - a_hi.astype(F32)).astype(BF16)
    w = w_ref[0]
    w_hi = w.astype(BF16)
    w_lo = (w - w_hi.astype(F32)).astype(BF16)
    a2 = jnp.concatenate([a_hi, a_lo], axis=0)
    p = jnp.dot(a2, w_hi, preferred_element_type=F32)
    q = jnp.dot(a_hi, w_lo, preferred_element_type=F32)
    o_ref[0] = p[:N_COND] + p[N_COND:] + q + b_ref[0]


def _ada(cond, w_ada, b_ada):
    depth, d, n = w_ada.shape
    bn = 1536
    return pl.pallas_call(
        _ada_kernel,
        grid=(depth, n // bn),
        in_specs=[
            pl.BlockSpec((N_COND, d), lambda i, j: (0, 0)),
            pl.BlockSpec((1, d, bn), lambda i, j: (i, 0, j)),
            pl.BlockSpec((1, 1, bn), lambda i, j: (i, 0, j)),
        ],
        out_specs=pl.BlockSpec((1, N_COND, bn), lambda i, j: (i, 0, j)),
        out_shape=jax.ShapeDtypeStruct((depth, N_COND, n), F32),
        compiler_params=_params("arbitrary", "arbitrary"),
        name="ada",
    )(cond, w_ada, b_ada.reshape(depth, 1, n))


def _post_norm(x, upd, g, b):
    y = ALPHA * x + upd
    mu = jnp.mean(y, axis=-1, keepdims=True)
    yc = y - mu
    var = jnp.mean(yc * yc, axis=-1, keepdims=True)
    return yc * lax.rsqrt(var + LN_EPS) * g + b


def _router_logits(h2, h_hi, w2_ref):
    h_lo = (h2 - h_hi.astype(F32)).astype(BF16)
    w2 = w2_ref[...]
    p = _dot_nt(w2, h_hi)
    q = _dot_nt(w2[:N_EXPERTS], h_lo)
    return p[:N_EXPERTS] + p[N_EXPERTS:] + q


def _epilogue(x, mix, m, lng, lnb, w2_ref, x1_ref, h2p_ref, logit_ref):
    g1, sh2, sc2 = m[2:3, :], m[3:4, :], m[4:5, :]
    x1 = _post_norm(x, g1 * mix, lng, lnb)
    h2 = x1 * (1.0 + sc2) + sh2
    h_hi = h2.astype(BF16)
    x1_ref[...] = x1
    h2p_ref[...] = _pack_halves(h_hi)
    logit_ref[...] = _router_logits(h2, h_hi, w2_ref)


_EPI_IN_SPECS = [
    pl.BlockSpec((1, D_MODEL), lambda t: (0, 0)),
    pl.BlockSpec((1, D_MODEL), lambda t: (0, 0)),
    pl.BlockSpec((2 * N_EXPERTS, D_MODEL), lambda t: (0, 0)),
]
_EPI_OUT_SPECS = [
    pl.BlockSpec((TILE, D_MODEL), lambda t: (t, 0)),
    pl.BlockSpec((TILE, D_PACK), lambda t: (t, 0)),
    pl.BlockSpec((N_EXPERTS, TILE), lambda t: (0, t)),
]
_EPI_OUT_SHAPE = [
    jax.ShapeDtypeStruct((T_ALL, D_MODEL), F32),
    jax.ShapeDtypeStruct((T_ALL, D_PACK), jnp.uint32),
    jax.ShapeDtypeStruct((N_EXPERTS, T_ALL), F32),
]


def _route_kernel(logit_ref, br_ref, e_ref, w_ref):
    aff = jax.nn.sigmoid(logit_ref[...])
    sel = aff + br_ref[...]
    sel_rows = [sel[e] for e in range(N_EXPERTS)]
    aff_rows = [aff[e] for e in range(N_EXPERTS)]

    def group_score(g):
        r = sel_rows[g * EXPERTS_PER_GROUP:(g + 1) * EXPERTS_PER_GROUP]
        best = None
        for i in range(EXPERTS_PER_GROUP):
            for j in range(i + 1, EXPERTS_PER_GROUP):
                pair = r[i] + r[j]
                best = pair if best is None else jnp.maximum(best, pair)
        return best

    best = group_score(0)
    gidx = jnp.zeros_like(best, dtype=jnp.int32)
    for g in range(1, N_EXPERT_GROUPS):
        sc = group_score(g)
        better = sc > best
        gidx = jnp.where(better, g, gidx)
        best = jnp.where(better, sc, best)

    def pick_group(rows, j):
        out = rows[j]
        for g in range(1, N_EXPERT_GROUPS):
            out = jnp.where(gidx == g, rows[g * EXPERTS_PER_GROUP + j], out)
        return out

    cand = [pick_group(sel_rows, j) for j in range(EXPERTS_PER_GROUP)]
    cand_aff = [pick_group(aff_rows, j) for j in range(EXPERTS_PER_GROUP)]

    def argmax_first(vals):
        bv, bi, ba = vals[0], jnp.zeros_like(gidx), cand_aff[0]
        for j in range(1, EXPERTS_PER_GROUP):
            better = vals[j] > bv
            bv = jnp.where(better, vals[j], bv)
            bi = jnp.where(better, j, bi)
            ba = jnp.where(better, cand_aff[j], ba)
        return bi, ba

    i1, a1 = argmax_first(cand)
    rest = [jnp.where(i1 == j, -jnp.inf, cand[j]) for j in range(EXPERTS_PER_GROUP)]
    i2, a2 = argmax_first(rest)
    denom = a1 + a2
    base = gidx * EXPERTS_PER_GROUP
    e_ref[0] = base + i1
    e_ref[1] = base + i2
    w_ref[0] = a1 / denom
    w_ref[1] = a2 / denom


def _route(logits, b_router):
    return pl.pallas_call(
        _route_kernel,
        out_shape=[jax.ShapeDtypeStruct((2, ROUTE_ROWS, 128), jnp.int32),
                   jax.ShapeDtypeStruct((2, ROUTE_ROWS, 128), F32)],
        compiler_params=pltpu.CompilerParams(vmem_limit_bytes=VMEM_LIMIT),
        name="route",
    )(logits.reshape(N_EXPERTS, ROUTE_ROWS, 128), b_router.reshape(N_EXPERTS, 1, 1))


def _mods_spec(layer):
    return pl.BlockSpec((1, 1, 6, D_MODEL), lambda t: (layer, _cond_row(t, TILE), 0, 0))


def _pool_kernel(xa_ref, xb_ref, xp_ref, xn_ref, mods_ref, pw_ref, ps_ref, lng_ref, lnb_ref, w2_ref,
                 x1_ref, h2p_ref, logit_ref):
    t = pl.program_id(0)
    m = mods_ref[0, 0]
    sh1, sc1 = m[0:1, :], m[1:2, :]
    in_lat = t >= CTX_TILES
    sub = (t - CTX_TILES) % TILES_PER_LAT_SEQ
    is_first = jnp.logical_or(jnp.logical_not(in_lat), sub == 0)
    is_last = jnp.logical_or(jnp.logical_not(in_lat), sub == TILES_PER_LAT_SEQ - 1)
    seq_len = jnp.where(in_lat, DEC_SEQ, SEQ)
    pos0 = jnp.where(in_lat, sub * TILE, 0)

    x = jnp.where(in_lat, xb_ref[...], xa_ref[...])
    h = x * (1.0 + sc1) + sh1
    hp = jnp.where(is_first, 0.0, xp_ref[...] * (1.0 + sc1) + sh1)
    hn = jnp.where(is_last, 0.0, xn_ref[...] * (1.0 + sc1) + sh1)
    hext = jnp.concatenate([hp, h, hn], axis=0)
    ext = TILE + 2 * POOL_HALO
    pos = pos0 + lax.broadcasted_iota(jnp.int32, (TILE, 1), 0)

    outs = []
    for g, w in enumerate(POOL_SIZES):
        lo_c, hi_c = g * POOL_GROUP_DIM, (g + 1) * POOL_GROUP_DIM
        a = hext[:, lo_c:hi_c]
        k = 1
        while k < w:
            a = a + pltpu.roll(a, ext - k, axis=0)
            k *= 2
        off = POOL_HALO - w // 2
        win = pltpu.roll(a, ext - off, axis=0)[:TILE] if off else a[:TILE]
        lo = jnp.maximum(pos - w // 2, 0)
        hi = jnp.minimum(pos - w // 2 + w, seq_len)
        cnt = (hi - lo).astype(F32)
        pooled = win / cnt - h[:, lo_c:hi_c]
        outs.append(jnp.dot(pooled.astype(BF16), pw_ref[g], preferred_element_type=F32))
    mix = jnp.concatenate(outs, axis=1) * ps_ref[...]
    _epilogue(x, mix, m, lng_ref[...], lnb_ref[...], w2_ref, x1_ref, h2p_ref, logit_ref)


def _pool_layer(x_ctx, x_lat, mods, pool_w, pool_scale, lng, lnb, w2):
    halo_blocks = TILE // POOL_HALO
    last_halo = T_LAT // POOL_HALO - 1
    return pl.pallas_call(
        _pool_kernel,
        grid=(N_TILES,),
        in_specs=[
            pl.BlockSpec((TILE, D_MODEL), lambda t: (jnp.minimum(t, CTX_TILES - 1), 0)),
            pl.BlockSpec((TILE, D_MODEL), lambda t: (jnp.maximum(t - CTX_TILES, 0), 0)),
            pl.BlockSpec((POOL_HALO, D_MODEL), lambda t: (jnp.maximum((t - CTX_TILES) * halo_blocks - 1, 0), 0)),
            pl.BlockSpec((POOL_HALO, D_MODEL),
                         lambda t: (jnp.clip((t - CTX_TILES + 1) * halo_blocks, 0, last_halo), 0)),
            _mods_spec(0),
            pl.BlockSpec((len(POOL_SIZES), POOL_GROUP_DIM, POOL_GROUP_DIM), lambda t: (0, 0, 0)),
            pl.BlockSpec((1, D_MODEL), lambda t: (0, 0)),
        ] + _EPI_IN_SPECS,
        out_specs=_EPI_OUT_SPECS,
        out_shape=_EPI_OUT_SHAPE,
        compiler_params=_params("arbitrary"),
        name="pool_mixer",
    )(x_ctx, x_lat, x_lat, x_lat, mods, pool_w, pool_scale, lng, lnb, w2)


def _moe_rows(n):
    return 2 * n + N_EXPERTS * MOE_TILE


def _route_plan(eidx):
    n = eidx.shape[1]
    tiles = _moe_rows(n) // MOE_TILE
    eidx = eidx.reshape(2 * n)
    experts = jnp.arange(N_EXPERTS, dtype=jnp.int32)
    onehot = (eidx[:, None] == experts[None, :]).astype(jnp.int32)
    counts = jnp.sum(onehot, axis=0)
    rank = jnp.sum((jnp.cumsum(onehot, axis=0) - onehot) * onehot, axis=1)
    padded = (counts + MOE_TILE - 1) // MOE_TILE * MOE_TILE
    ends = jnp.cumsum(padded)
    starts = ends - padded
    pos = jnp.sum(onehot * starts[None, :], axis=1) + rank
    tile_start = jnp.arange(tiles, dtype=jnp.int32)[:, None] * MOE_TILE
    in_region = ((tile_start >= starts[None, :]) & (tile_start < ends[None, :])).astype(jnp.int32)
    used = jnp.sum(in_region, axis=1) > 0
    last_expert = jnp.max(jnp.where(counts > 0, experts, 0))
    tile_expert = jnp.where(used, jnp.sum(in_region * experts[None, :], axis=1), last_expert)
    tile_valid = jnp.sum(in_region * jnp.clip(counts[None, :] - (tile_start - starts[None, :]), 0, MOE_TILE), axis=1)
    tile_block = jnp.minimum(jnp.arange(tiles, dtype=jnp.int32), ends[-1] // MOE_TILE - 1)
    return pos.astype(jnp.int32), tile_expert.astype(jnp.int32), tile_valid.astype(jnp.int32), \
        tile_block.astype(jnp.int32)


def _expert_kernel(te_ref, nv_ref, tb_ref, x_ref, wg_ref, wu_ref, wd_ref, y_ref, wgu_s, wd_s):
    i = pl.program_id(0)
    prev = te_ref[jnp.maximum(i - 1, 0)]
    changed = jnp.logical_or(i == 0, te_ref[i] != prev)

    @pl.when(changed)
    def _():
        wgu_s[:, :D_FF] = wg_ref[0, 0].astype(BF16)
        wgu_s[:, D_FF:] = wu_ref[0, 0].astype(BF16)
        wd_s[...] = wd_ref[0, 0].astype(BF16)

    nv = nv_ref[i]
    for r0 in range(0, MOE_TILE, MOE_SUB):
        @pl.when(nv > r0)
        def _(r0=r0):
            rs = slice(r0, r0 + MOE_SUB)
            rows = r0 + lax.broadcasted_iota(jnp.int32, (MOE_SUB, 1), 0)
            xp = jnp.where(rows < nv, x_ref[rs, :], jnp.uint32(0))
            left, right = _unpack_halves(xp)
            xb = jnp.concatenate([left.astype(BF16), right.astype(BF16)], axis=1)
            gu = jnp.dot(xb, wgu_s[...], preferred_element_type=F32)
            gate, up = gu[:, :D_FF], gu[:, D_FF:]
            he = (gate * jax.nn.sigmoid(gate) * up).astype(BF16)
            y_ref[rs, :] = _pack_halves(jnp.dot(he, wd_s[...], preferred_element_type=F32).astype(BF16))


def _experts(xs, tile_expert, tile_valid, tile_block, layer, w_gate, w_up, w_down):
    grid_spec = pltpu.PrefetchScalarGridSpec(
        num_scalar_prefetch=3,
        grid=(xs.shape[0] // MOE_TILE,),
        in_specs=[
            pl.BlockSpec((MOE_TILE, D_PACK), lambda i, te, nv, tb: (tb[i], 0)),
            pl.BlockSpec((1, 1, D_MODEL, D_FF), lambda i, te, nv, tb: (layer, te[i], 0, 0)),
            pl.BlockSpec((1, 1, D_MODEL, D_FF), lambda i, te, nv, tb: (layer, te[i], 0, 0)),
            pl.BlockSpec((1, 1, D_FF, D_MODEL), lambda i, te, nv, tb: (layer, te[i], 0, 0)),
        ],
        out_specs=pl.BlockSpec((MOE_TILE, D_PACK), lambda i, te, nv, tb: (tb[i], 0)),
        scratch_shapes=[pltpu.VMEM((D_MODEL, 2 * D_FF), BF16), pltpu.VMEM((D_FF, D_MODEL), BF16)],
    )
    return pl.pallas_call(
        _expert_kernel,
        grid_spec=grid_spec,
        out_shape=jax.ShapeDtypeStruct(xs.shape, jnp.uint32),
        compiler_params=_params("arbitrary"),
        name="experts",
    )(tile_expert, tile_valid, tile_block, xs, w_gate, w_up, w_down)


SC_WORKERS = 32


def _sc_mesh():
    return plsc.VectorSubcoreMesh(core_axis_name="core", subcore_axis_name="subcore")


def _sc_worker():
    return lax.axis_index("subcore") * 2 + lax.axis_index("core")


def _sc_scratch():
    return [pltpu.VMEM((SC_WINDOW,), jnp.int32), pltpu.VMEM((SC_WINDOW, D_PACK), jnp.uint32),
            pltpu.SemaphoreType.DMA]


def _dispatch(h2, pos, t0, n):
    per_worker = 2 * n // SC_WORKERS

    @functools.partial(pl.kernel, out_type=jax.ShapeDtypeStruct((_moe_rows(n), D_PACK), jnp.uint32),
                       mesh=_sc_mesh(), scratch_types=_sc_scratch(), name="moe_dispatch")
    def scatter(x_hbm, i_hbm, o_hbm, idx_v, rows_v, sem):
        base = _sc_worker() * per_worker
        src_base = t0 + base % n

        @pl.loop(0, per_worker // SC_WINDOW)
        def _(j):
            pltpu.sync_copy(i_hbm.at[pl.ds(base + j * SC_WINDOW, SC_WINDOW)], idx_v)
            pltpu.sync_copy(x_hbm.at[pl.ds(src_base + j * SC_WINDOW, SC_WINDOW)], rows_v)
            pltpu.async_copy(rows_v, o_hbm.at[idx_v], sem).wait()

    return scatter(h2, pos)


def _gather_pairs(y, pos, n):
    per_worker = 2 * n // SC_WORKERS

    @functools.partial(pl.kernel, out_type=jax.ShapeDtypeStruct((2 * n, D_PACK), jnp.uint32),
                       mesh=_sc_mesh(), scratch_types=_sc_scratch(), name="moe_gather")
    def gather(y_hbm, i_hbm, o_hbm, idx_v, rows_v, sem):
        base = _sc_worker() * per_worker

        @pl.loop(0, per_worker // SC_WINDOW)
        def _(j):
            off = base + j * SC_WINDOW
            pltpu.sync_copy(i_hbm.at[pl.ds(off, SC_WINDOW)], idx_v)
            pltpu.async_copy(y_hbm.at[idx_v], rows_v, sem).wait()
            pltpu.sync_copy(rows_v, o_hbm.at[pl.ds(off, SC_WINDOW)])

    return gather(y, pos).reshape(2, n, D_PACK)


def _combine_kernel(x_ref, yg_ref, w1_ref, w2_ref, mods_ref, lng_ref, lnb_ref, *rest):
    o_ref = rest[-1]
    m = mods_ref[0, 0]
    g2 = m[5:6, :]
    y1 = jnp.concatenate(_unpack_halves(yg_ref[0]), axis=1)
    y2 = jnp.concatenate(_unpack_halves(yg_ref[1]), axis=1)
    moe = w1_ref[...] * y1 + w2_ref[...] * y2
    o_ref[...] = _post_norm(x_ref[...], g2 * moe, lng_ref[...], lnb_ref[...])


def _combine(x1, yg, wsel, mods, layer, lng, lnb, t0, n, whole, into):
    off = t0 // LN_TILE
    out_off = off if whole else 0
    in_specs = [
        pl.BlockSpec((LN_TILE, D_MODEL), lambda t: (off + t, 0)),
        pl.BlockSpec((2, LN_TILE, D_PACK), lambda t: (0, t, 0)),
        pl.BlockSpec((LN_TILE, 1), lambda t: (t, 0)),
        pl.BlockSpec((LN_TILE, 1), lambda t: (t, 0)),
        pl.BlockSpec((1, 1, 6, D_MODEL), lambda t: (layer, _cond_row(off + t, LN_TILE), 0, 0)),
        pl.BlockSpec((1, D_MODEL), lambda t: (0, 0)),
        pl.BlockSpec((1, D_MODEL), lambda t: (0, 0)),
    ]
    args = [x1, yg, wsel[0].reshape(n, 1), wsel[1].reshape(n, 1), mods, lng, lnb]
    aliases = {}
    if into is not None:
        in_specs.append(pl.BlockSpec(memory_space=pl.ANY))
        aliases = {len(args): 0}
        args.append(into)
    return pl.pallas_call(
        _combine_kernel,
        grid=(n // LN_TILE,),
        in_specs=in_specs,
        out_specs=pl.BlockSpec((LN_TILE, D_MODEL), lambda t: (out_off + t, 0)),
        out_shape=jax.ShapeDtypeStruct((T_ALL if whole else n, D_MODEL), F32),
        input_output_aliases=aliases,
        compiler_params=_params("arbitrary"),
        name="moe_combine",
    )(*args)


def _moe(x1, h2p, logits, b_router, mods, layer, w_gate, w_up, w_down, lng, lnb, whole):
    eidx, wsel = _route(logits, b_router)
    eidx, wsel = eidx.reshape(2, T_ALL), wsel.reshape(2, T_ALL)
    plans = [_route_plan(eidx[:, t0:t0 + n]) for t0, n in SEGMENTS]
    xs = [_dispatch(h2p, plan[0], t0, n) for plan, (t0, n) in zip(plans, SEGMENTS)]
    ys = [_experts(x, *plan[1:], layer, w_gate, w_up, w_down) for x, plan in zip(xs, plans)]
    ygs = [_gather_pairs(y, plan[0], n) for y, plan, (t0, n) in zip(ys, plans, SEGMENTS)]
    outs, buf = [], None
    for yg, (t0, n) in zip(ygs, SEGMENTS):
        buf = _combine(x1, yg, wsel[:, t0:t0 + n], mods, layer, lng, lnb, t0, n, whole, buf if whole else None)
        outs.append(buf)
    return buf if whole else outs


def _qkv_kernel(x_ref, mods_ref, w_ref, qkv_ref, nk_ref, nv_ref, w_s):
    t = pl.program_id(0)

    @pl.when(t == 0)
    def _():
        w_s[...] = w_ref[0].astype(BF16)

    m = mods_ref[0, 0]
    sh1, sc1 = m[0:1, :], m[1:2, :]
    h = (x_ref[...] * (1.0 + sc1) + sh1).astype(BF16)
    r = jnp.dot(h, w_s[...], preferred_element_type=F32)
    qkv_ref[:, :D_MODEL] = (r[:, :D_MODEL] * (HEAD_DIM ** -0.5)).astype(BF16)
    qkv_ref[:, D_MODEL:] = r[:, D_MODEL:].astype(BF16)

    @pl.when(t < CTX_TILES)
    def _():
        for out_ref, base in ((nk_ref, D_MODEL), (nv_ref, 2 * D_MODEL)):
            for p in range(HEAD_PAIRS):
                pair = r[:, base + p * LANES: base + (p + 1) * LANES]
                out_ref[0, 0, 2 * p] = pair[:, :HEAD_DIM]
                out_ref[0, 0, 2 * p + 1] = pltpu.roll(pair, HEAD_DIM, axis=1)[:, :HEAD_DIM]


def _qkv(x, mods, w_qkv):
    cache_spec = pl.BlockSpec((1, 1, N_HEADS, SEQ, HEAD_DIM),
                              lambda t: (jnp.minimum(t, CTX_TILES - 1), 0, 0, 0, 0))
    cache_shape = jax.ShapeDtypeStruct((BATCH, 1, N_HEADS, SEQ, HEAD_DIM), F32)
    return pl.pallas_call(
        _qkv_kernel,
        grid=(N_TILES,),
        in_specs=[
            pl.BlockSpec((TILE, D_MODEL), lambda t: (t, 0)),
            _mods_spec(1),
            pl.BlockSpec((1, D_MODEL, 3 * D_MODEL), lambda t: (0, 0, 0)),
        ],
        out_specs=[pl.BlockSpec((TILE, 3 * D_MODEL), lambda t: (t, 0)), cache_spec, cache_spec],
        out_shape=[jax.ShapeDtypeStruct((T_ALL, 3 * D_MODEL), BF16), cache_shape, cache_shape],
        scratch_shapes=[pltpu.VMEM((D_MODEL, 3 * D_MODEL), BF16)],
        compiler_params=_params("arbitrary"),
        name="qkv",
    )(x, mods, w_qkv)


def _dot_nt(a, b):
    return lax.dot_general(a, b, (((1,), (1,)), ((), ())), preferred_element_type=F32)


def _head_masks():
    lane = lax.broadcasted_iota(jnp.int32, (1, LANES), 1)
    return lane < HEAD_DIM, lane >= HEAD_DIM


def _ctx_attn_kernel(q_ref, k_ref, v_ref, o_ref):
    left, right = _head_masks()
    for p in range(HEAD_PAIRS):
        cols = slice(p * LANES, (p + 1) * LANES)
        q2, k2, v2 = q_ref[:, cols], k_ref[:, cols], v_ref[:, cols]
        halves = []
        for mask in (left, right):
            qh = jnp.where(mask, q2, jnp.zeros_like(q2))
            s = _dot_nt(qh, k2)
            e = jnp.exp(s - jnp.max(s, axis=-1, keepdims=True))
            o2 = jnp.dot(e.astype(BF16), v2, preferred_element_type=F32)
            halves.append(o2 / jnp.sum(e, axis=-1, keepdims=True))
        o_ref[:, cols] = jnp.where(left, halves[0], halves[1]).astype(BF16)


def _ctx_attention(qkv):
    return pl.pallas_call(
        _ctx_attn_kernel,
        grid=(BATCH,),
        in_specs=[pl.BlockSpec((SEQ, D_MODEL), lambda b, j=j: (b, j)) for j in range(3)],
        out_specs=pl.BlockSpec((SEQ, D_MODEL), lambda b: (b, 0)),
        out_shape=jax.ShapeDtypeStruct((T_ALL, D_MODEL), BF16),
        compiler_params=_params("arbitrary"),
        name="ctx_attention",
    )(qkv, qkv, qkv)


_LAT_Q_BLOCK_ROWS = 4
_LAT_KEY_ROWS = ((0, 8), (0, 12), (4, 16), (8, 16))


def _lat_attn_kernel(q_ref, k_ref, v_ref, ck_ref, cv_ref, eb_ref, o_ctx_ref, o_ref, bias_s):
    del o_ctx_ref
    left, right = _head_masks()

    @pl.when(pl.program_id(1) == 0)
    def _():
        neg = jnp.full((GRID_W, LANES), NEG_INF, F32)
        for hh in range(2):
            for r in range(GRID_ROWS):
                r0 = min(max(r - WIN_H // 2, 0), GRID_ROWS - WIN_H)
                for j in range(GRID_ROWS // 2):
                    parts = []
                    for kr in (2 * j, 2 * j + 1):
                        parts.append(eb_ref[hh, kr - r + WIN_H - 1] if r0 <= kr < r0 + WIN_H else None)
                    if parts[0] is None and parts[1] is None:
                        val = neg
                    else:
                        val = jnp.where(left, neg if parts[0] is None else parts[0],
                                        neg if parts[1] is None else parts[1])
                    bias_s[hh, r * GRID_W:(r + 1) * GRID_W, j * LANES:(j + 1) * LANES] = val

    ck = jnp.concatenate([ck_ref[0, 0, 0], ck_ref[0, 0, 1]], axis=0).astype(BF16)
    cv = jnp.concatenate([cv_ref[0, 0, 0], cv_ref[0, 0, 1]], axis=0).astype(BF16)
    qrows = _LAT_Q_BLOCK_ROWS * GRID_W
    for qb, (kr0, kr1) in enumerate(_LAT_KEY_ROWS):
        qs = slice(qb * qrows, (qb + 1) * qrows)
        ks = slice(kr0 * GRID_W, kr1 * GRID_W)
        q2, kk, vv = q_ref[qs, :], k_ref[ks, :], v_ref[ks, :]
        halves = []
        for hh, mask in enumerate((left, right)):
            qh = jnp.where(mask, q2, jnp.zeros_like(q2))
            s_lat = _dot_nt(qh, kk) + bias_s[hh, qs, ks]
            s_ctx = jnp.dot(qh, ck, preferred_element_type=F32)
            mx = jnp.maximum(jnp.max(s_lat, axis=-1, keepdims=True), jnp.max(s_ctx, axis=-1, keepdims=True))
            e_lat = jnp.exp(s_lat - mx)
            e_ctx = jnp.exp(s_ctx - mx)
            den = jnp.sum(e_lat, axis=-1, keepdims=True) + jnp.sum(e_ctx, axis=-1, keepdims=True)
            o2 = (jnp.dot(e_lat.astype(BF16), vv, preferred_element_type=F32)
                  + _dot_nt(e_ctx.astype(BF16), cv))
            halves.append(o2 / den)
        o_ref[qs, :] = jnp.where(left, halves[0], halves[1]).astype(BF16)


def _lat_attention(qkv, cache_kt, cache_vt, ebias, o_buf):
    row0 = T_CTX // DEC_SEQ
    cache_spec = pl.BlockSpec((1, 1, 2, HEAD_DIM, PAST_LEN), lambda p, b: (b, 0, p, 0, 0))
    return pl.pallas_call(
        _lat_attn_kernel,
        grid=(HEAD_PAIRS, DEC_BATCH),
        in_specs=[
            pl.BlockSpec((DEC_SEQ, LANES), lambda p, b: (row0 + b, p)),
            pl.BlockSpec((DEC_SEQ, LANES), lambda p, b: (row0 + b, HEAD_PAIRS + p)),
            pl.BlockSpec((DEC_SEQ, LANES), lambda p, b: (row0 + b, 2 * HEAD_PAIRS + p)),
            cache_spec,
            cache_spec,
            pl.BlockSpec((2, 2 * WIN_H - 1, GRID_W, LANES), lambda p, b: (p, 0, 0, 0)),
            pl.BlockSpec(memory_space=pl.ANY),
        ],
        out_specs=pl.BlockSpec((DEC_SEQ, LANES), lambda p, b: (row0 + b, p)),
        out_shape=jax.ShapeDtypeStruct((T_ALL, D_MODEL), BF16),
        scratch_shapes=[pltpu.VMEM((2, DEC_SEQ, DEC_SEQ), F32)],
        input_output_aliases={6: 0},
        compiler_params=_params("arbitrary", "arbitrary"),
        name="lat_attention",
    )(qkv, qkv, qkv, cache_kt, cache_vt, ebias, o_buf)


def _expanded_bias(rel_bias):
    col = jnp.arange(GRID_W)
    col_start = jnp.clip(col - WIN_W // 2, 0, GRID_W - WIN_W)
    col_ok = (col[None, :] >= col_start[:, None]) & (col[None, :] < col_start[:, None] + WIN_W)
    dx = jnp.clip(col[None, :] - col[:, None] + (WIN_W - 1), 0, 2 * WIN_W - 2)
    onehot = (dx[None] == jnp.arange(2 * WIN_W - 1)[:, None, None]).astype(F32)
    eb = jnp.einsum("hyj,jqk->hyqk", rel_bias.astype(F32), onehot, precision=lax.Precision.HIGHEST)
    eb = jnp.where(col_ok[None, None], eb, NEG_INF)
    return jnp.concatenate([eb, eb], axis=-1)


def _wo_kernel(o_ref, x_ref, mods_ref, w_ref, lng_ref, lnb_ref, w2_ref, x1_ref, h2p_ref, logit_ref, w_s):
    @pl.when(pl.program_id(0) == 0)
    def _():
        w_s[...] = w_ref[0].astype(BF16)

    mix = jnp.dot(o_ref[...], w_s[...], preferred_element_type=F32)
    _epilogue(x_ref[...], mix, mods_ref[0, 0], lng_ref[...], lnb_ref[...], w2_ref, x1_ref, h2p_ref, logit_ref)


def _attn_out(o, x, mods, w_o, lng, lnb, w2):
    return pl.pallas_call(
        _wo_kernel,
        grid=(N_TILES,),
        in_specs=[
            pl.BlockSpec((TILE, D_MODEL), lambda t: (t, 0)),
            pl.BlockSpec((TILE, D_MODEL), lambda t: (t, 0)),
            _mods_spec(1),
            pl.BlockSpec((1, D_MODEL, D_MODEL), lambda t: (0, 0, 0)),
        ] + _EPI_IN_SPECS,
        out_specs=_EPI_OUT_SPECS,
        out_shape=_EPI_OUT_SHAPE,
        scratch_shapes=[pltpu.VMEM((D_MODEL, D_MODEL), BF16)],
        compiler_params=_params("arbitrary"),
        name="attn_out",
    )(o, x, mods, w_o, lng, lnb, w2)


def kernel(x_prompt, x_sample, cache_k, cache_v, c, c_ctx, w_ada, b_ada, ln1_g, ln1_b, ln2_g, ln2_b,
           pool_w, pool_scale, w_qkv, w_o, rel_bias, w_router, b_router, w_gate, w_up, w_down):
    cond = jnp.zeros((N_COND, D_MODEL), F32).at[0].set(c_ctx).at[1:1 + DEC_BATCH].set(c)
    mods = _ada(cond, w_ada, b_ada).reshape(2, N_COND, 6, D_MODEL)
    wrt = w_router.T
    wrt_hi = wrt.astype(BF16)
    w2 = jnp.concatenate([wrt_hi, (wrt - wrt_hi.astype(F32)).astype(BF16)], axis=0)

    x, h2p, logits = _pool_layer(x_prompt.reshape(T_CTX, D_MODEL), x_sample.reshape(T_LAT, D_MODEL), mods,
                                 pool_w[0].astype(BF16), pool_scale[0:1], ln1_g[0:1], ln1_b[0:1], w2)
    x = _moe(x, h2p, logits, b_router, mods, 0, w_gate, w_up, w_down, ln2_g[0:1], ln2_b[0:1], whole=True)

    qkv, new_k, new_v = _qkv(x, mods, w_qkv)
    o = _lat_attention(qkv, cache_k.transpose(0, 1, 2, 4, 3), cache_v.transpose(0, 1, 2, 4, 3),
                       _expanded_bias(rel_bias[0]), _ctx_attention(qkv))
    x, h2p, logits = _attn_out(o, x, mods, w_o, ln1_g[1:2], ln1_b[1:2], w2)
    y_ctx, y_lat = _moe(x, h2p, logits, b_router, mods, 1, w_gate, w_up, w_down, ln2_g[1:2], ln2_b[1:2],
                        whole=False)
    return (y_ctx.reshape(BATCH, SEQ, D_MODEL), y_lat.reshape(DEC_BATCH, DEC_SEQ, D_MODEL), new_k, new_v)
```

```python
import functools

import jax
import jax.numpy as jnp
from jax import lax
from jax.experimental import pallas as pl
from jax.experimental.pallas import tpu as pltpu
from jax.experimental.pallas import tpu_sc as plsc

F32 = jnp.float32
BF16 = jnp.bfloat16

D_MODEL = 1024
BATCH = 16
SEQ = 256
DEC_BATCH = 8
DEC_SEQ = 1024
PAST_LEN = 512
GRID_W = 64
GRID_ROWS = DEC_SEQ // GRID_W
POOL_SIZES = (2, 4, 8, 16)
POOL_GROUP_DIM = D_MODEL // len(POOL_SIZES)
POOL_HALO = 8
N_HEADS = 16
HEAD_DIM = 64
WIN_H = 8
WIN_W = 16
N_EXPERTS = 16
EXPERTS_PER_GROUP = 4
N_EXPERT_GROUPS = 4
D_FF = 512
ALPHA = (2.0 * 2) ** 0.25
LN_EPS = 1e-5
NEG_INF = -1e30

T_CTX = BATCH * SEQ
T_LAT = DEC_BATCH * DEC_SEQ
T_ALL = T_CTX + T_LAT
N_COND = 16
TILE = 256
N_TILES = T_ALL // TILE
CTX_TILES = T_CTX // TILE
TILES_PER_LAT_SEQ = DEC_SEQ // TILE
MOE_TILE = 1024
MOE_SUB = 256
SEGMENTS = ((0, T_CTX), (T_CTX, T_LAT))
LN_TILE = 512
QKV_TILE = 512
WO_TILE = 512
SC_WINDOW = 64
D_PACK = D_MODEL // 2
ROUTE_ROWS = T_ALL // 128
HEAD_PAIRS = N_HEADS // 2
LANES = 128
VMEM_LIMIT = 56 * 1024 * 1024


def _cond_row(t, tile):
    ctx_tiles = T_CTX // tile
    per_seq = DEC_SEQ // tile
    return jnp.maximum(t - ctx_tiles + per_seq, 0) // per_seq


def _params(*sem):
    return pltpu.CompilerParams(dimension_semantics=sem, vmem_limit_bytes=VMEM_LIMIT)


def _pack_halves(v):
    half = v.shape[1] // 2
    hi = lax.bitcast_convert_type(v[:, :half].astype(F32), jnp.uint32)
    lo = lax.bitcast_convert_type(v[:, half:].astype(F32), jnp.uint32)
    return hi | (lo >> 16)


def _unpack_halves(p):
    left = lax.bitcast_convert_type(p & jnp.uint32(0xFFFF0000), F32)
    right = lax.bitcast_convert_type(p << 16, F32)
    return left, right


def _ada_kernel(cond_ref, w_ref, b_ref, o_ref):
    cnd = cond_ref[...]
    act = cnd * jax.nn.sigmoid(cnd)
    a_hi = act.astype(BF16)
    a_lo = (act - a_hi.astype(F32)).astype(BF16)
    w = w_ref[0]
    w_hi = w.astype(BF16)
    w_lo = (w - w_hi.astype(F32)).astype(BF16)
    a2 = jnp.concatenate([a_hi, a_lo], axis=0)
    p = jnp.dot(a2, w_hi, preferred_element_type=F32)
    q = jnp.dot(a_hi, w_lo, preferred_element_type=F32)
    o_ref[0] = p[:N_COND] + p[N_COND:] + q + b_ref[0]


def _ada(cond, w_ada, b_ada):
    depth, d, n = w_ada.shape
    bn = 1536
    return pl.pallas_call(
        _ada_kernel,
        grid=(depth, n // bn),
        in_specs=[
            pl.BlockSpec((N_COND, d), lambda i, j: (0, 0)),
            pl.BlockSpec((1, d, bn), lambda i, j: (i, 0, j)),
            pl.BlockSpec((1, 1, bn), lambda i, j: (i, 0, j)),
        ],
        out_specs=pl.BlockSpec((1, N_COND, bn), lambda i, j: (i, 0, j)),
        out_shape=jax.ShapeDtypeStruct((depth, N_COND, n), F32),
        compiler_params=_params("arbitrary", "arbitrary"),
        name="ada",
    )(cond, w_ada, b_ada.reshape(depth, 1, n))


def _post_norm(x, upd, g, b):
    y = ALPHA * x + upd
    mu = jnp.mean(y, axis=-1, keepdims=True)
    yc = y - mu
    var = jnp.mean(yc * yc, axis=-1, keepdims=True)
    return yc * lax.rsqrt(var + LN_EPS) * g + b


def _router_logits(h2, h_hi, w2_ref):
    h_lo = (h2 - h_hi.astype(F32)).astype(BF16)
    w2 = w2_ref[...]
    p = _dot_nt(w2, h_hi)
    q = _dot_nt(w2[:N_EXPERTS], h_lo)
    return p[:N_EXPERTS] + p[N_EXPERTS:] + q


def _epilogue(x, mix, m, lng, lnb, w2_ref, x1_ref, h2p_ref, logit_ref):
    g1, sh2, sc2 = m[2:3, :], m[3:4, :], m[4:5, :]
    x1 = _post_norm(x, g1 * mix, lng, lnb)
    h2 = x1 * (1.0 + sc2) + sh2
    h_hi = h2.astype(BF16)
    x1_ref[...] = x1
    h2p_ref[...] = _pack_halves(h_hi)
    logit_ref[...] = _router_logits(h2, h_hi, w2_ref)


_EPI_IN_SPECS = [
    pl.BlockSpec((1, D_MODEL), lambda t: (0, 0)),
    pl.BlockSpec((1, D_MODEL), lambda t: (0, 0)),
    pl.BlockSpec((2 * N_EXPERTS, D_MODEL), lambda t: (0, 0)),
]
_EPI_OUT_SPECS = [
    pl.BlockSpec((TILE, D_MODEL), lambda t: (t, 0)),
    pl.BlockSpec((TILE, D_PACK), lambda t: (t, 0)),
    pl.BlockSpec((N_EXPERTS, TILE), lambda t: (0, t)),
]
_EPI_OUT_SHAPE = [
    jax.ShapeDtypeStruct((T_ALL, D_MODEL), F32),
    jax.ShapeDtypeStruct((T_ALL, D_PACK), jnp.uint32),
    jax.ShapeDtypeStruct((N_EXPERTS, T_ALL), F32),
]


def _route_kernel(logit_ref, br_ref, e_ref, w_ref):
    aff = jax.nn.sigmoid(logit_ref[...])
    sel = aff + br_ref[...]
    sel_rows = [sel[e] for e in range(N_EXPERTS)]
    aff_rows = [aff[e] for e in range(N_EXPERTS)]

    def group_score(g):
        r = sel_rows[g * EXPERTS_PER_GROUP:(g + 1) * EXPERTS_PER_GROUP]
        best = None
        for i in range(EXPERTS_PER_GROUP):
            for j in range(i + 1, EXPERTS_PER_GROUP):
                pair = r[i] + r[j]
                best = pair if best is None else jnp.maximum(best, pair)
        return best

    best = group_score(0)
    gidx = jnp.zeros_like(best, dtype=jnp.int32)
    for g in range(1, N_EXPERT_GROUPS):
        sc = group_score(g)
        better = sc > best
        gidx = jnp.where(better, g, gidx)
        best = jnp.where(better, sc, best)

    def pick_group(rows, j):
        out = rows[j]
        for g in range(1, N_EXPERT_GROUPS):
            out = jnp.where(gidx == g, rows[g * EXPERTS_PER_GROUP + j], out)
        return out

    cand = [pick_group(sel_rows, j) for j in range(EXPERTS_PER_GROUP)]
    cand_aff = [pick_group(aff_rows, j) for j in range(EXPERTS_PER_GROUP)]

    def argmax_first(vals):
        bv, bi, ba = vals[0], jnp.zeros_like(gidx), cand_aff[0]
        for j in range(1, EXPERTS_PER_GROUP):
            better = vals[j] > bv
            bv = jnp.where(better, vals[j], bv)
            bi = jnp.where(better, j, bi)
            ba = jnp.where(better, cand_aff[j], ba)
        return bi, ba

    i1, a1 = argmax_first(cand)
    rest = [jnp.where(i1 == j, -jnp.inf, cand[j]) for j in range(EXPERTS_PER_GROUP)]
    i2, a2 = argmax_first(rest)
    denom = a1 + a2
    base = gidx * EXPERTS_PER_GROUP
    e_ref[0] = base + i1
    e_ref[1] = base + i2
    w_ref[0] = a1 / denom
    w_ref[1] = a2 / denom


def _route(logits, b_router):
    return pl.pallas_call(
        _route_kernel,
        out_shape=[jax.ShapeDtypeStruct((2, ROUTE_ROWS, 128), jnp.int32),
                   jax.ShapeDtypeStruct((2, ROUTE_ROWS, 128), F32)],
        compiler_params=pltpu.CompilerParams(vmem_limit_bytes=VMEM_LIMIT),
        name="route",
    )(logits.reshape(N_EXPERTS, ROUTE_ROWS, 128), b_router.reshape(N_EXPERTS, 1, 1))


def _mods_spec(layer):
    return pl.BlockSpec((1, 1, 6, D_MODEL), lambda t: (layer, _cond_row(t, TILE), 0, 0))


def _pool_kernel(xa_ref, xb_ref, xp_ref, xn_ref, mods_ref, pw_ref, ps_ref, lng_ref, lnb_ref, w2_ref,
                 x1_ref, h2p_ref, logit_ref):
    t = pl.program_id(0)
    m = mods_ref[0, 0]
    sh1, sc1 = m[0:1, :], m[1:2, :]
    in_lat = t >= CTX_TILES
    sub = (t - CTX_TILES) % TILES_PER_LAT_SEQ
    is_first = jnp.logical_or(jnp.logical_not(in_lat), sub == 0)
    is_last = jnp.logical_or(jnp.logical_not(in_lat), sub == TILES_PER_LAT_SEQ - 1)
    seq_len = jnp.where(in_lat, DEC_SEQ, SEQ)
    pos0 = jnp.where(in_lat, sub * TILE, 0)

    x = jnp.where(in_lat, xb_ref[...], xa_ref[...])
    h = x * (1.0 + sc1) + sh1
    hp = jnp.where(is_first, 0.0, xp_ref[...] * (1.0 + sc1) + sh1)
    hn = jnp.where(is_last, 0.0, xn_ref[...] * (1.0 + sc1) + sh1)
    hext = jnp.concatenate([hp, h, hn], axis=0)
    ext = TILE + 2 * POOL_HALO
    pos = pos0 + lax.broadcasted_iota(jnp.int32, (TILE, 1), 0)

    outs = []
    for g, w in enumerate(POOL_SIZES):
        lo_c, hi_c = g * POOL_GROUP_DIM, (g + 1) * POOL_GROUP_DIM
        a = hext[:, lo_c:hi_c]
        k = 1
        while k < w:
            a = a + pltpu.roll(a, ext - k, axis=0)
            k *= 2
        off = POOL_HALO - w // 2
        win = pltpu.roll(a, ext - off, axis=0)[:TILE] if off else a[:TILE]
        lo = jnp.maximum(pos - w // 2, 0)
        hi = jnp.minimum(pos - w // 2 + w, seq_len)
        cnt = (hi - lo).astype(F32)
        pooled = win / cnt - h[:, lo_c:hi_c]
        outs.append(jnp.dot(pooled.astype(BF16), pw_ref[g], preferred_element_type=F32))
    mix = jnp.concatenate(outs, axis=1) * ps_ref[...]
    _epilogue(x, mix, m, lng_ref[...], lnb_ref[...], w2_ref, x1_ref, h2p_ref, logit_ref)


def _pool_layer(x_ctx, x_lat, mods, pool_w, pool_scale, lng, lnb, w2):
    halo_blocks = TILE // POOL_HALO
    last_halo = T_LAT // POOL_HALO - 1
    return pl.pallas_call(
        _pool_kernel,
        grid=(N_TILES,),
        in_specs=[
            pl.BlockSpec((TILE, D_MODEL), lambda t: (jnp.minimum(t, CTX_TILES - 1), 0)),
            pl.BlockSpec((TILE, D_MODEL), lambda t: (jnp.maximum(t - CTX_TILES, 0), 0)),
            pl.BlockSpec((POOL_HALO, D_MODEL), lambda t: (jnp.maximum((t - CTX_TILES) * halo_blocks - 1, 0), 0)),
            pl.BlockSpec((POOL_HALO, D_MODEL),
                         lambda t: (jnp.clip((t - CTX_TILES + 1) * halo_blocks, 0, last_halo), 0)),
            _mods_spec(0),
            pl.BlockSpec((len(POOL_SIZES), POOL_GROUP_DIM, POOL_GROUP_DIM), lambda t: (0, 0, 0)),
            pl.BlockSpec((1, D_MODEL), lambda t: (0, 0)),
        ] + _EPI_IN_SPECS,
        out_specs=_EPI_OUT_SPECS,
        out_shape=_EPI_OUT_SHAPE,
        compiler_params=_params("arbitrary"),
        name="pool_mixer",
    )(x_ctx, x_lat, x_lat, x_lat, mods, pool_w, pool_scale, lng, lnb, w2)


def _moe_rows(n):
    return 2 * n + N_EXPERTS * MOE_TILE


def _route_plan(eidx):
    n = eidx.shape[1]
    tiles = _moe_rows(n) // MOE_TILE
    eidx = eidx.reshape(2 * n)
    experts = jnp.arange(N_EXPERTS, dtype=jnp.int32)
    onehot = (eidx[:, None] == experts[None, :]).astype(jnp.int32)
    counts = jnp.sum(onehot, axis=0)
    rank = jnp.sum((jnp.cumsum(onehot, axis=0) - onehot) * onehot, axis=1)
    padded = (counts + MOE_TILE - 1) // MOE_TILE * MOE_TILE
    ends = jnp.cumsum(padded)
    starts = ends - padded
    pos = jnp.sum(onehot * starts[None, :], axis=1) + rank
    tile_start = jnp.arange(tiles, dtype=jnp.int32)[:, None] * MOE_TILE
    in_region = ((tile_start >= starts[None, :]) & (tile_start < ends[None, :])).astype(jnp.int32)
    used = jnp.sum(in_region, axis=1) > 0
    last_expert = jnp.max(jnp.where(counts > 0, experts, 0))
    tile_expert = jnp.where(used, jnp.sum(in_region * experts[None, :], axis=1), last_expert)
    tile_valid = jnp.sum(in_region * jnp.clip(counts[None, :] - (tile_start - starts[None, :]), 0, MOE_TILE), axis=1)
    tile_block = jnp.minimum(jnp.arange(tiles, dtype=jnp.int32), ends[-1] // MOE_TILE - 1)
    return pos.astype(jnp.int32), tile_expert.astype(jnp.int32), tile_valid.astype(jnp.int32), \
        tile_block.astype(jnp.int32)


def _expert_kernel(te_ref, nv_ref, tb_ref, x_ref, wg_ref, wu_ref, wd_ref, y_ref, wgu_s, wd_s):
    i = pl.program_id(0)
    prev = te_ref[jnp.maximum(i - 1, 0)]
    changed = jnp.logical_or(i == 0, te_ref[i] != prev)

    @pl.when(changed)
    def _():
        wgu_s[:, :D_FF] = wg_ref[0, 0].astype(BF16)
        wgu_s[:, D_FF:] = wu_ref[0, 0].astype(BF16)
        wd_s[...] = wd_ref[0, 0].astype(BF16)

    nv = nv_ref[i]

    def gate_up(r0):
        rows = r0 + lax.broadcasted_iota(jnp.int32, (MOE_SUB, 1), 0)
        xp = jnp.where(rows < nv, x_ref[r0:r0 + MOE_SUB, :], jnp.uint32(0))
        left, right = _unpack_halves(xp)
        xb = jnp.concatenate([left.astype(BF16), right.astype(BF16)], axis=1)
        return jnp.dot(xb, wgu_s[...], preferred_element_type=F32)

    def down(r0, gu):
        gate, up = gu[:, :D_FF], gu[:, D_FF:]
        he = (gate * jax.nn.sigmoid(gate) * up).astype(BF16)
        y_ref[r0:r0 + MOE_SUB, :] = _pack_halves(jnp.dot(he, wd_s[...], preferred_element_type=F32).astype(BF16))

    def run(n_sub):
        gu = gate_up(0)
        for j in range(n_sub):
            nxt = gate_up((j + 1) * MOE_SUB) if j + 1 < n_sub else None
            down(j * MOE_SUB, gu)
            gu = nxt

    n_subs = MOE_TILE // MOE_SUB
    for n_sub in range(1, n_subs + 1):
        lo = (n_sub - 1) * MOE_SUB
        in_range = nv > lo if n_sub == n_subs else jnp.logical_and(nv > lo, nv <= lo + MOE_SUB)
        pl.when(in_range)(functools.partial(run, n_sub))


def _experts(xs, tile_expert, tile_valid, tile_block, layer, w_gate, w_up, w_down):
    grid_spec = pltpu.PrefetchScalarGridSpec(
        num_scalar_prefetch=3,
        grid=(xs.shape[0] // MOE_TILE,),
        in_specs=[
            pl.BlockSpec((MOE_TILE, D_PACK), lambda i, te, nv, tb: (tb[i], 0)),
            pl.BlockSpec((1, 1, D_MODEL, D_FF), lambda i, te, nv, tb: (layer, te[i], 0, 0)),
            pl.BlockSpec((1, 1, D_MODEL, D_FF), lambda i, te, nv, tb: (layer, te[i], 0, 0)),
            pl.BlockSpec((1, 1, D_FF, D_MODEL), lambda i, te, nv, tb: (layer, te[i], 0, 0)),
        ],
        out_specs=pl.BlockSpec((MOE_TILE, D_PACK), lambda i, te, nv, tb: (tb[i], 0)),
        scratch_shapes=[pltpu.VMEM((D_MODEL, 2 * D_FF), BF16), pltpu.VMEM((D_FF, D_MODEL), BF16)],
    )
    return pl.pallas_call(
        _expert_kernel,
        grid_spec=grid_spec,
        out_shape=jax.ShapeDtypeStruct(xs.shape, jnp.uint32),
        compiler_params=_params("arbitrary"),
        name="experts",
    )(tile_expert, tile_valid, tile_block, xs, w_gate, w_up, w_down)


SC_WORKERS = 32


def _sc_mesh():
    return plsc.VectorSubcoreMesh(core_axis_name="core", subcore_axis_name="subcore")


def _sc_worker():
    return lax.axis_index("subcore") * 2 + lax.axis_index("core")


def _sc_scratch():
    return [pltpu.VMEM((SC_WINDOW,), jnp.int32), pltpu.VMEM((SC_WINDOW, D_PACK), jnp.uint32),
            pltpu.SemaphoreType.DMA]


def _dispatch(h2, pos):
    n = h2.shape[0]
    per_worker = 2 * n // SC_WORKERS

    @functools.partial(pl.kernel, out_type=jax.ShapeDtypeStruct((_moe_rows(n), D_PACK), jnp.uint32),
                       mesh=_sc_mesh(), scratch_types=_sc_scratch(), name="moe_dispatch")
    def scatter(x_hbm, i_hbm, o_hbm, idx_v, rows_v, sem):
        base = _sc_worker() * per_worker
        src_base = base % n

        @pl.loop(0, per_worker // SC_WINDOW)
        def _(j):
            pltpu.sync_copy(i_hbm.at[pl.ds(base + j * SC_WINDOW, SC_WINDOW)], idx_v)
            pltpu.sync_copy(x_hbm.at[pl.ds(src_base + j * SC_WINDOW, SC_WINDOW)], rows_v)
            pltpu.async_copy(rows_v, o_hbm.at[idx_v], sem).wait()

    return scatter(h2, pos)


def _gather_pairs(y, pos):
    n = pos.shape[0] // 2
    per_worker = 2 * n // SC_WORKERS

    @functools.partial(pl.kernel, out_type=jax.ShapeDtypeStruct((2 * n, D_PACK), jnp.uint32),
                       mesh=_sc_mesh(), scratch_types=_sc_scratch(), name="moe_gather")
    def gather(y_hbm, i_hbm, o_hbm, idx_v, rows_v, sem):
        base = _sc_worker() * per_worker

        @pl.loop(0, per_worker // SC_WINDOW)
        def _(j):
            off = base + j * SC_WINDOW
            pltpu.sync_copy(i_hbm.at[pl.ds(off, SC_WINDOW)], idx_v)
            pltpu.async_copy(y_hbm.at[idx_v], rows_v, sem).wait()
            pltpu.sync_copy(rows_v, o_hbm.at[pl.ds(off, SC_WINDOW)])

    return gather(y, pos).reshape(2, n, D_PACK)


def _combine_kernel(x_ref, yg_ref, w1_ref, w2_ref, mods_ref, lng_ref, lnb_ref, o_ref):
    m = mods_ref[0, 0]
    g2 = m[5:6, :]
    y1 = jnp.concatenate(_unpack_halves(yg_ref[0]), axis=1)
    y2 = jnp.concatenate(_unpack_halves(yg_ref[1]), axis=1)
    moe = w1_ref[...] * y1 + w2_ref[...] * y2
    o_ref[...] = _post_norm(x_ref[...], g2 * moe, lng_ref[...], lnb_ref[...])


def _combine(x1, yg, wsel, mods, layer, lng, lnb, t0, n):
    off = t0 // LN_TILE
    return pl.pallas_call(
        _combine_kernel,
        grid=(n // LN_TILE,),
        in_specs=[
            pl.BlockSpec((LN_TILE, D_MODEL), lambda t: (off + t, 0)),
            pl.BlockSpec((2, LN_TILE, D_PACK), lambda t: (0, off + t, 0)),
            pl.BlockSpec((LN_TILE, 1), lambda t: (off + t, 0)),
            pl.BlockSpec((LN_TILE, 1), lambda t: (off + t, 0)),
            pl.BlockSpec((1, 1, 6, D_MODEL), lambda t: (layer, _cond_row(off + t, LN_TILE), 0, 0)),
            pl.BlockSpec((1, D_MODEL), lambda t: (0, 0)),
            pl.BlockSpec((1, D_MODEL), lambda t: (0, 0)),
        ],
        out_specs=pl.BlockSpec((LN_TILE, D_MODEL), lambda t: (t, 0)),
        out_shape=jax.ShapeDtypeStruct((n, D_MODEL), F32),
        compiler_params=_params("arbitrary"),
        name="moe_combine",
    )(x1, yg, wsel[0].reshape(T_ALL, 1), wsel[1].reshape(T_ALL, 1), mods, lng, lnb)


def _moe(x1, h2p, logits, b_router, mods, layer, w_gate, w_up, w_down, lng, lnb, segments):
    eidx, wsel = _route(logits, b_router)
    eidx, wsel = eidx.reshape(2, T_ALL), wsel.reshape(2, T_ALL)
    pos, tile_expert, tile_valid, tile_block = _route_plan(eidx)
    xs = _dispatch(h2p, pos)
    y = _experts(xs, tile_expert, tile_valid, tile_block, layer, w_gate, w_up, w_down)
    yg = _gather_pairs(y, pos)
    return [_combine(x1, yg, wsel, mods, layer, lng, lnb, t0, n) for t0, n in segments]


def _qkv_kernel(x_ref, mods_ref, w_ref, qkv_ref, nk_ref, nv_ref, w_s):
    t = pl.program_id(0)

    @pl.when(t == 0)
    def _():
        w_s[...] = w_ref[0].astype(BF16)

    m = mods_ref[0, 0]
    sh1, sc1 = m[0:1, :], m[1:2, :]
    h = (x_ref[...] * (1.0 + sc1) + sh1).astype(BF16)
    r = jnp.dot(h, w_s[...], preferred_element_type=F32)
    qkv_ref[:, :D_MODEL] = (r[:, :D_MODEL] * (HEAD_DIM ** -0.5)).astype(BF16)
    qkv_ref[:, D_MODEL:] = r[:, D_MODEL:].astype(BF16)

    @pl.when(t < T_CTX // QKV_TILE)
    def _():
        for out_ref, base in ((nk_ref, D_MODEL), (nv_ref, 2 * D_MODEL)):
            for b in range(QKV_TILE // SEQ):
                for p in range(HEAD_PAIRS):
                    pair = r[b * SEQ:(b + 1) * SEQ, base + p * LANES: base + (p + 1) * LANES].T
                    out_ref[b, 0, 2 * p] = pair[:HEAD_DIM]
                    out_ref[b, 0, 2 * p + 1] = pair[HEAD_DIM:]


def _qkv(x, mods, w_qkv):
    seqs = QKV_TILE // SEQ
    cache_spec = pl.BlockSpec((seqs, 1, N_HEADS, HEAD_DIM, SEQ),
                              lambda t: (jnp.minimum(t, T_CTX // QKV_TILE - 1), 0, 0, 0, 0))
    cache_shape = jax.ShapeDtypeStruct((BATCH, 1, N_HEADS, HEAD_DIM, SEQ), F32)
    return pl.pallas_call(
        _qkv_kernel,
        grid=(T_ALL // QKV_TILE,),
        in_specs=[
            pl.BlockSpec((QKV_TILE, D_MODEL), lambda t: (t, 0)),
            pl.BlockSpec((1, 1, 6, D_MODEL), lambda t: (1, _cond_row(t, QKV_TILE), 0, 0)),
            pl.BlockSpec((1, D_MODEL, 3 * D_MODEL), lambda t: (0, 0, 0)),
        ],
        out_specs=[pl.BlockSpec((QKV_TILE, 3 * D_MODEL), lambda t: (t, 0)), cache_spec, cache_spec],
        out_shape=[jax.ShapeDtypeStruct((T_ALL, 3 * D_MODEL), BF16), cache_shape, cache_shape],
        scratch_shapes=[pltpu.VMEM((D_MODEL, 3 * D_MODEL), BF16)],
        compiler_params=_params("arbitrary"),
        name="qkv",
    )(x, mods, w_qkv)


def _dot_nt(a, b):
    return lax.dot_general(a, b, (((1,), (1,)), ((), ())), preferred_element_type=F32)


def _head_masks():
    lane = lax.broadcasted_iota(jnp.int32, (1, LANES), 1)
    return lane < HEAD_DIM, lane >= HEAD_DIM


def _ctx_attn_kernel(q_ref, k_ref, v_ref, o_ref):
    left, right = _head_masks()
    for p in range(HEAD_PAIRS):
        cols = slice(p * LANES, (p + 1) * LANES)
        q2, k2, v2 = q_ref[:, cols], k_ref[:, cols], v_ref[:, cols]
        halves = []
        for mask in (left, right):
            qh = jnp.where(mask, q2, jnp.zeros_like(q2))
            s = _dot_nt(qh, k2)
            e = jnp.exp(s - jnp.max(s, axis=-1, keepdims=True))
            o2 = jnp.dot(e.astype(BF16), v2, preferred_element_type=F32)
            halves.append(o2 / jnp.sum(e, axis=-1, keepdims=True))
        o_ref[:, cols] = jnp.where(left, halves[0], halves[1]).astype(BF16)


def _ctx_attention(qkv):
    return pl.pallas_call(
        _ctx_attn_kernel,
        grid=(BATCH,),
        in_specs=[pl.BlockSpec((SEQ, D_MODEL), lambda b, j=j: (b, j)) for j in range(3)],
        out_specs=pl.BlockSpec((SEQ, D_MODEL), lambda b: (b, 0)),
        out_shape=jax.ShapeDtypeStruct((T_ALL, D_MODEL), BF16),
        compiler_params=_params("arbitrary"),
        name="ctx_attention",
    )(qkv, qkv, qkv)


_LAT_Q_BLOCK_ROWS = 4
_LAT_KEY_ROWS = ((0, 8), (0, 12), (4, 16), (8, 16))


def _softmax_rows(s_ref, p_ref, l_ref):
    sc = s_ref[...]
    e = jnp.exp(sc - jnp.max(sc, axis=-1, keepdims=True))
    l_ref[...] = jnp.sum(e, axis=-1, keepdims=True)
    p_ref[...] = e.astype(BF16)


def _lat_attn_kernel(q_ref, k_ref, v_ref, ck_ref, cv_ref, eb_ref, o_ctx_ref, o_ref, bias_s, s_scr, p_scr, l_scr):
    del o_ctx_ref
    left, right = _head_masks()

    @pl.when(pl.program_id(1) == 0)
    def _():
        neg = jnp.full((GRID_W, LANES), NEG_INF, F32)
        for hh in range(2):
            for r in range(GRID_ROWS):
                r0 = min(max(r - WIN_H // 2, 0), GRID_ROWS - WIN_H)
                for j in range(GRID_ROWS // 2):
                    parts = []
                    for kr in (2 * j, 2 * j + 1):
                        parts.append(eb_ref[hh, kr - r + WIN_H - 1] if r0 <= kr < r0 + WIN_H else None)
                    if parts[0] is None and parts[1] is None:
                        val = neg
                    else:
                        val = jnp.where(left, neg if parts[0] is None else parts[0],
                                        neg if parts[1] is None else parts[1])
                    bias_s[hh, r * GRID_W:(r + 1) * GRID_W, j * LANES:(j + 1) * LANES] = val

    ck = jnp.concatenate([ck_ref[0, 0, 0], ck_ref[0, 0, 1]], axis=0).astype(BF16)
    cv = jnp.concatenate([cv_ref[0, 0, 0], cv_ref[0, 0, 1]], axis=0).astype(BF16)
    qrows = _LAT_Q_BLOCK_ROWS * GRID_W
    units = [(qb, hh) for qb in range(len(_LAT_KEY_ROWS)) for hh in range(2)]

    def refs(u):
        nk = (_LAT_KEY_ROWS[units[u][0]][1] - _LAT_KEY_ROWS[units[u][0]][0]) * GRID_W
        width = nk + PAST_LEN
        return nk, s_scr.at[u % 2, :, :width], p_scr.at[u % 2, :, :width], l_scr.at[u % 2]

    def scores(u):
        qb, hh = units[u]
        kr0, kr1 = _LAT_KEY_ROWS[qb]
        qs, ks = slice(qb * qrows, (qb + 1) * qrows), slice(kr0 * GRID_W, kr1 * GRID_W)
        nk, s_ref, _, _ = refs(u)
        q2 = q_ref[qs, :]
        qh = jnp.where(right if hh else left, q2, jnp.zeros_like(q2))
        s_ref[:, :nk] = _dot_nt(qh, k_ref[ks, :]) + bias_s[hh, qs, ks]
        s_ref[:, nk:] = jnp.dot(qh, ck, preferred_element_type=F32)

    def weighted_values(u):
        qb, _ = units[u]
        kr0, kr1 = _LAT_KEY_ROWS[qb]
        nk, _, p_ref, l_ref = refs(u)
        o2 = (jnp.dot(p_ref[:, :nk], v_ref[kr0 * GRID_W:kr1 * GRID_W, :], preferred_element_type=F32)
              + _dot_nt(p_ref[:, nk:], cv))
        return o2 / l_ref[...]

    scores(0)
    halves = []
    for u in range(len(units)):
        if u + 1 < len(units):
            scores(u + 1)
        _, s_ref, p_ref, l_ref = refs(u)
        _softmax_rows(s_ref, p_ref, l_ref)
        halves.append(weighted_values(u))
        if len(halves) == 2:
            qb = units[u][0]
            o_ref[qb * qrows:(qb + 1) * qrows, :] = jnp.where(left, halves[0], halves[1]).astype(BF16)
            halves = []


def _lat_attention(qkv, cache_kt, cache_vt, ebias, o_buf):
    row0 = T_CTX // DEC_SEQ
    qrows = _LAT_Q_BLOCK_ROWS * GRID_W
    max_keys = max(k1 - k0 for k0, k1 in _LAT_KEY_ROWS) * GRID_W + PAST_LEN
    cache_spec = pl.BlockSpec((1, 1, 2, HEAD_DIM, PAST_LEN), lambda p, b: (b, 0, p, 0, 0))
    return pl.pallas_call(
        _lat_attn_kernel,
        grid=(HEAD_PAIRS, DEC_BATCH),
        in_specs=[
            pl.BlockSpec((DEC_SEQ, LANES), lambda p, b: (row0 + b, p)),
            pl.BlockSpec((DEC_SEQ, LANES), lambda p, b: (row0 + b, HEAD_PAIRS + p)),
            pl.BlockSpec((DEC_SEQ, LANES), lambda p, b: (row0 + b, 2 * HEAD_PAIRS + p)),
            cache_spec,
            cache_spec,
            pl.BlockSpec((2, 2 * WIN_H - 1, GRID_W, LANES), lambda p, b: (p, 0, 0, 0)),
            pl.BlockSpec(memory_space=pl.ANY),
        ],
        out_specs=pl.BlockSpec((DEC_SEQ, LANES), lambda p, b: (row0 + b, p)),
        out_shape=jax.ShapeDtypeStruct((T_ALL, D_MODEL), BF16),
        scratch_shapes=[pltpu.VMEM((2, DEC_SEQ, DEC_SEQ), F32),
                        pltpu.VMEM((2, qrows, max_keys), F32),
                        pltpu.VMEM((2, qrows, max_keys), BF16),
                        pltpu.VMEM((2, qrows, 1), F32)],
        input_output_aliases={6: 0},
        compiler_params=_params("arbitrary", "arbitrary"),
        name="lat_attention",
    )(qkv, qkv, qkv, cache_kt, cache_vt, ebias, o_buf)


def _expanded_bias(rel_bias):
    col = jnp.arange(GRID_W)
    col_start = jnp.clip(col - WIN_W // 2, 0, GRID_W - WIN_W)
    col_ok = (col[None, :] >= col_start[:, None]) & (col[None, :] < col_start[:, None] + WIN_W)
    dx = jnp.clip(col[None, :] - col[:, None] + (WIN_W - 1), 0, 2 * WIN_W - 2)
    onehot = (dx[None] == jnp.arange(2 * WIN_W - 1)[:, None, None]).astype(F32)
    eb = jnp.einsum("hyj,jqk->hyqk", rel_bias.astype(F32), onehot, precision=lax.Precision.HIGHEST)
    eb = jnp.where(col_ok[None, None], eb, NEG_INF)
    return jnp.concatenate([eb, eb], axis=-1)


def _wo_kernel(o_ref, x_ref, mods_ref, w_ref, lng_ref, lnb_ref, w2_ref, x1_ref, h2p_ref, logit_ref, w_s):
    @pl.when(pl.program_id(0) == 0)
    def _():
        w_s[...] = w_ref[0].astype(BF16)

    for r0 in range(0, WO_TILE, TILE):
        rs = slice(r0, r0 + TILE)
        mix = jnp.dot(o_ref[rs, :], w_s[...], preferred_element_type=F32)
        _epilogue(x_ref[rs, :], mix, mods_ref[0, 0], lng_ref[...], lnb_ref[...], w2_ref,
                  x1_ref.at[rs, :], h2p_ref.at[rs, :], logit_ref.at[:, rs])


def _attn_out(o, x, mods, w_o, lng, lnb, w2):
    return pl.pallas_call(
        _wo_kernel,
        grid=(T_ALL // WO_TILE,),
        in_specs=[
            pl.BlockSpec((WO_TILE, D_MODEL), lambda t: (t, 0)),
            pl.BlockSpec((WO_TILE, D_MODEL), lambda t: (t, 0)),
            pl.BlockSpec((1, 1, 6, D_MODEL), lambda t: (1, _cond_row(t, WO_TILE), 0, 0)),
            pl.BlockSpec((1, D_MODEL, D_MODEL), lambda t: (0, 0, 0)),
        ] + _EPI_IN_SPECS,
        out_specs=[
            pl.BlockSpec((WO_TILE, D_MODEL), lambda t: (t, 0)),
            pl.BlockSpec((WO_TILE, D_PACK), lambda t: (t, 0)),
            pl.BlockSpec((N_EXPERTS, WO_TILE), lambda t: (0, t)),
        ],
        out_shape=_EPI_OUT_SHAPE,
        scratch_shapes=[pltpu.VMEM((D_MODEL, D_MODEL), BF16)],
        compiler_params=_params("arbitrary"),
        name="attn_out",
    )(o, x, mods, w_o, lng, lnb, w2)


def kernel(x_prompt, x_sample, cache_k, cache_v, c, c_ctx, w_ada, b_ada, ln1_g, ln1_b, ln2_g, ln2_b,
           pool_w, pool_scale, w_qkv, w_o, rel_bias, w_router, b_router, w_gate, w_up, w_down):
    cond = jnp.zeros((N_COND, D_MODEL), F32).at[0].set(c_ctx).at[1:1 + DEC_BATCH].set(c)
    mods = _ada(cond, w_ada, b_ada).reshape(2, N_COND, 6, D_MODEL)
    wrt = w_router.T
    wrt_hi = wrt.astype(BF16)
    w2 = jnp.concatenate([wrt_hi, (wrt - wrt_hi.astype(F32)).astype(BF16)], axis=0)

    x, h2p, logits = _pool_layer(x_prompt.reshape(T_CTX, D_MODEL), x_sample.reshape(T_LAT, D_MODEL), mods,
                                 pool_w[0].astype(BF16), pool_scale[0:1], ln1_g[0:1], ln1_b[0:1], w2)
    x, = _moe(x, h2p, logits, b_router, mods, 0, w_gate, w_up, w_down, ln2_g[0:1], ln2_b[0:1], ((0, T_ALL),))

    qkv, new_kt, new_vt = _qkv(x, mods, w_qkv)
    o = _lat_attention(qkv, cache_k.transpose(0, 1, 2, 4, 3), cache_v.transpose(0, 1, 2, 4, 3),
                       _expanded_bias(rel_bias[0]), _ctx_attention(qkv))
    x, h2p, logits = _attn_out(o, x, mods, w_o, ln1_g[1:2], ln1_b[1:2], w2)
    y_ctx, y_lat = _moe(x, h2p, logits, b_router, mods, 1, w_gate, w_up, w_down, ln2_g[1:2], ln2_b[1:2],
                        SEGMENTS)
    return (y_ctx.reshape(BATCH, SEQ, D_MODEL), y_lat.reshape(DEC_BATCH, DEC_SEQ, D_MODEL),
            new_kt.transpose(0, 1, 2, 4, 3), new_vt.transpose(0, 1, 2, 4, 3))
```

```python
import functools

import jax
import jax.numpy as jnp
from jax import lax
from jax.experimental import pallas as pl
from jax.experimental.pallas import tpu as pltpu
from jax.experimental.pallas import tpu_sc as plsc

F32 = jnp.float32
BF16 = jnp.bfloat16

D_MODEL = 1024
BATCH = 16
SEQ = 256
DEC_BATCH = 8
DEC_SEQ = 1024
PAST_LEN = 512
GRID_W = 64
GRID_ROWS = DEC_SEQ // GRID_W
POOL_SIZES = (2, 4, 8, 16)
POOL_GROUP_DIM = D_MODEL // len(POOL_SIZES)
POOL_HALO = 8
N_HEADS = 16
HEAD_DIM = 64
WIN_H = 8
WIN_W = 16
N_EXPERTS = 16
EXPERTS_PER_GROUP = 4
N_EXPERT_GROUPS = 4
D_FF = 512
ALPHA = (2.0 * 2) ** 0.25
LN_EPS = 1e-5
NEG_INF = -1e30

T_CTX = BATCH * SEQ
T_LAT = DEC_BATCH * DEC_SEQ
T_ALL = T_CTX + T_LAT
N_COND = 16
TILE = 256
N_TILES = T_ALL // TILE
CTX_TILES = T_CTX // TILE
TILES_PER_LAT_SEQ = DEC_SEQ // TILE
MOE_TILE = 1024
MOE_SUB = 256
SEGMENTS = ((0, T_CTX), (T_CTX, T_LAT))
LN_TILE = 512
QKV_TILE = 512
WO_TILE = 512
SC_WINDOW = 64
D_PACK = D_MODEL // 2
ROUTE_ROWS = T_ALL // 128
HEAD_PAIRS = N_HEADS // 2
LANES = 128
VMEM_LIMIT = 56 * 1024 * 1024


def _cond_row(t, tile):
    ctx_tiles = T_CTX // tile
    per_seq = DEC_SEQ // tile
    return jnp.maximum(t - ctx_tiles + per_seq, 0) // per_seq


def _params(*sem):
    return pltpu.CompilerParams(dimension_semantics=sem, vmem_limit_bytes=VMEM_LIMIT)


def _pack_halves(v):
    half = v.shape[1] // 2
    hi = lax.bitcast_convert_type(v[:, :half].astype(F32), jnp.uint32)
    lo = lax.bitcast_convert_type(v[:, half:].astype(F32), jnp.uint32)
    return hi | (lo >> 16)


def _unpack_halves(p):
    left = lax.bitcast_convert_type(p & jnp.uint32(0xFFFF0000), F32)
    right = lax.bitcast_convert_type(p << 16, F32)
    return left, right


def _ada_kernel(cond_ref, w_ref, b_ref, o_ref):
    cnd = cond_ref[...]
    act = cnd * jax.nn.sigmoid(cnd)
    a_hi = act.astype(BF16)
    a_lo = (act - a_hi.astype(F32)).astype(BF16)
    w = w_ref[0]
    w_hi = w.astype(BF16)
    w_lo = (w - w_hi.astype(F32)).astype(BF16)
    a2 = jnp.concatenate([a_hi, a_lo], axis=0)
    p = jnp.dot(a2, w_hi, preferred_element_type=F32)
    q = jnp.dot(a_hi, w_lo, preferred_element_type=F32)
    o_ref[0] = p[:N_COND] + p[N_COND:] + q + b_ref[0]


def _ada(cond, w_ada, b_ada):
    depth, d, n = w_ada.shape
    bn = 1536
    return pl.pallas_call(
        _ada_kernel,
        grid=(depth, n // bn),
        in_specs=[
            pl.BlockSpec((N_COND, d), lambda i, j: (0, 0)),
            pl.BlockSpec((1, d, bn), lambda i, j: (i, 0, j)),
            pl.BlockSpec((1, 1, bn), lambda i, j: (i, 0, j)),
        ],
        out_specs=pl.BlockSpec((1, N_COND, bn), lambda i, j: (i, 0, j)),
        out_shape=jax.ShapeDtypeStruct((depth, N_COND, n), F32),
        compiler_params=_params("arbitrary", "arbitrary"),
        name="ada",
    )(cond, w_ada, b_ada.reshape(depth, 1, n))


def _post_norm(x, upd, g, b):
    y = ALPHA * x + upd
    mu = jnp.mean(y, axis=-1, keepdims=True)
    yc = y - mu
    var = jnp.mean(yc * yc, axis=-1, keepdims=True)
    return yc * lax.rsqrt(var + LN_EPS) * g + b


def _router_logits(h2, h_hi, w2_ref):
    h_lo = (h2 - h_hi.astype(F32)).astype(BF16)
    w2 = w2_ref[...]
    p = _dot_nt(w2, h_hi)
    q = _dot_nt(w2[:N_EXPERTS], h_lo)
    return p[:N_EXPERTS] + p[N_EXPERTS:] + q


def _epilogue(x, mix, m, lng, lnb, w2_ref, x1_ref, h2p_ref, logit_ref):
    g1, sh2, sc2 = m[2:3, :], m[3:4, :], m[4:5, :]
    x1 = _post_norm(x, g1 * mix, lng, lnb)
    h2 = x1 * (1.0 + sc2) + sh2
    h_hi = h2.astype(BF16)
    x1_ref[...] = x1
    h2p_ref[...] = _pack_halves(h_hi)
    logit_ref[...] = _router_logits(h2, h_hi, w2_ref)


_EPI_IN_SPECS = [
    pl.BlockSpec((1, D_MODEL), lambda t: (0, 0)),
    pl.BlockSpec((1, D_MODEL), lambda t: (0, 0)),
    pl.BlockSpec((2 * N_EXPERTS, D_MODEL), lambda t: (0, 0)),
]
_EPI_OUT_SPECS = [
    pl.BlockSpec((TILE, D_MODEL), lambda t: (t, 0)),
    pl.BlockSpec((TILE, D_PACK), lambda t: (t, 0)),
    pl.BlockSpec((N_EXPERTS, TILE), lambda t: (0, t)),
]
_EPI_OUT_SHAPE = [
    jax.ShapeDtypeStruct((T_ALL, D_MODEL), F32),
    jax.ShapeDtypeStruct((T_ALL, D_PACK), jnp.uint32),
    jax.ShapeDtypeStruct((N_EXPERTS, T_ALL), F32),
]


def _route_kernel(logit_ref, br_ref, e_ref, w_ref):
    aff = jax.nn.sigmoid(logit_ref[...])
    sel = aff + br_ref[...]
    sel_rows = [sel[e] for e in range(N_EXPERTS)]
    aff_rows = [aff[e] for e in range(N_EXPERTS)]

    def group_score(g):
        r = sel_rows[g * EXPERTS_PER_GROUP:(g + 1) * EXPERTS_PER_GROUP]
        best = None
        for i in range(EXPERTS_PER_GROUP):
            for j in range(i + 1, EXPERTS_PER_GROUP):
                pair = r[i] + r[j]
                best = pair if best is None else jnp.maximum(best, pair)
        return best

    best = group_score(0)
    gidx = jnp.zeros_like(best, dtype=jnp.int32)
    for g in range(1, N_EXPERT_GROUPS):
        sc = group_score(g)
        better = sc > best
        gidx = jnp.where(better, g, gidx)
        best = jnp.where(better, sc, best)

    def pick_group(rows, j):
        out = rows[j]
        for g in range(1, N_EXPERT_GROUPS):
            out = jnp.where(gidx == g, rows[g * EXPERTS_PER_GROUP + j], out)
        return out

    cand = [pick_group(sel_rows, j) for j in range(EXPERTS_PER_GROUP)]
    cand_aff = [pick_group(aff_rows, j) for j in range(EXPERTS_PER_GROUP)]

    def argmax_first(vals):
        bv, bi, ba = vals[0], jnp.zeros_like(gidx), cand_aff[0]
        for j in range(1, EXPERTS_PER_GROUP):
            better = vals[j] > bv
            bv = jnp.where(better, vals[j], bv)
            bi = jnp.where(better, j, bi)
            ba = jnp.where(better, cand_aff[j], ba)
        return bi, ba

    i1, a1 = argmax_first(cand)
    rest = [jnp.where(i1 == j, -jnp.inf, cand[j]) for j in range(EXPERTS_PER_GROUP)]
    i2, a2 = argmax_first(rest)
    denom = a1 + a2
    base = gidx * EXPERTS_PER_GROUP
    e_ref[0] = base + i1
    e_ref[1] = base + i2
    w_ref[0] = a1 / denom
    w_ref[1] = a2 / denom


def _route(logits, b_router):
    return pl.pallas_call(
        _route_kernel,
        out_shape=[jax.ShapeDtypeStruct((2, ROUTE_ROWS, 128), jnp.int32),
                   jax.ShapeDtypeStruct((2, ROUTE_ROWS, 128), F32)],
        compiler_params=pltpu.CompilerParams(vmem_limit_bytes=VMEM_LIMIT),
        name="route",
    )(logits.reshape(N_EXPERTS, ROUTE_ROWS, 128), b_router.reshape(N_EXPERTS, 1, 1))


def _mods_spec(layer):
    return pl.BlockSpec((1, 1, 6, D_MODEL), lambda t: (layer, _cond_row(t, TILE), 0, 0))


def _pool_kernel(xa_ref, xb_ref, xp_ref, xn_ref, mods_ref, pw_ref, ps_ref, lng_ref, lnb_ref, w2_ref,
                 x1_ref, h2p_ref, logit_ref):
    t = pl.program_id(0)
    m = mods_ref[0, 0]
    sh1, sc1 = m[0:1, :], m[1:2, :]
    in_lat = t >= CTX_TILES
    sub = (t - CTX_TILES) % TILES_PER_LAT_SEQ
    is_first = jnp.logical_or(jnp.logical_not(in_lat), sub == 0)
    is_last = jnp.logical_or(jnp.logical_not(in_lat), sub == TILES_PER_LAT_SEQ - 1)
    seq_len = jnp.where(in_lat, DEC_SEQ, SEQ)
    pos0 = jnp.where(in_lat, sub * TILE, 0)

    x = jnp.where(in_lat, xb_ref[...], xa_ref[...])
    h = x * (1.0 + sc1) + sh1
    hp = jnp.where(is_first, 0.0, xp_ref[...] * (1.0 + sc1) + sh1)
    hn = jnp.where(is_last, 0.0, xn_ref[...] * (1.0 + sc1) + sh1)
    hext = jnp.concatenate([hp, h, hn], axis=0)
    ext = TILE + 2 * POOL_HALO
    pos = pos0 + lax.broadcasted_iota(jnp.int32, (TILE, 1), 0)

    outs = []
    for g, w in enumerate(POOL_SIZES):
        lo_c, hi_c = g * POOL_GROUP_DIM, (g + 1) * POOL_GROUP_DIM
        a = hext[:, lo_c:hi_c]
        k = 1
        while k < w:
            a = a + pltpu.roll(a, ext - k, axis=0)
            k *= 2
        off = POOL_HALO - w // 2
        win = pltpu.roll(a, ext - off, axis=0)[:TILE] if off else a[:TILE]
        lo = jnp.maximum(pos - w // 2, 0)
        hi = jnp.minimum(pos - w // 2 + w, seq_len)
        cnt = (hi - lo).astype(F32)
        pooled = win / cnt - h[:, lo_c:hi_c]
        outs.append(jnp.dot(pooled.astype(BF16), pw_ref[g], preferred_element_type=F32))
    mix = jnp.concatenate(outs, axis=1) * ps_ref[...]
    _epilogue(x, mix, m, lng_ref[...], lnb_ref[...], w2_ref, x1_ref, h2p_ref, logit_ref)


def _pool_layer(x_ctx, x_lat, mods, pool_w, pool_scale, lng, lnb, w2):
    halo_blocks = TILE // POOL_HALO
    last_halo = T_LAT // POOL_HALO - 1
    return pl.pallas_call(
        _pool_kernel,
        grid=(N_TILES,),
        in_specs=[
            pl.BlockSpec((TILE, D_MODEL), lambda t: (jnp.minimum(t, CTX_TILES - 1), 0)),
            pl.BlockSpec((TILE, D_MODEL), lambda t: (jnp.maximum(t - CTX_TILES, 0), 0)),
            pl.BlockSpec((POOL_HALO, D_MODEL), lambda t: (jnp.maximum((t - CTX_TILES) * halo_blocks - 1, 0), 0)),
            pl.BlockSpec((POOL_HALO, D_MODEL),
                         lambda t: (jnp.clip((t - CTX_TILES + 1) * halo_blocks, 0, last_halo), 0)),
            _mods_spec(0),
            pl.BlockSpec((len(POOL_SIZES), POOL_GROUP_DIM, POOL_GROUP_DIM), lambda t: (0, 0, 0)),
            pl.BlockSpec((1, D_MODEL), lambda t: (0, 0)),
        ] + _EPI_IN_SPECS,
        out_specs=_EPI_OUT_SPECS,
        out_shape=_EPI_OUT_SHAPE,
        compiler_params=_params("arbitrary"),
        name="pool_mixer",
    )(x_ctx, x_lat, x_lat, x_lat, mods, pool_w, pool_scale, lng, lnb, w2)


def _moe_rows(n):
    return 2 * n + N_EXPERTS * MOE_TILE


def _route_plan(eidx):
    n = eidx.shape[1]
    tiles = _moe_rows(n) // MOE_TILE
    eidx = eidx.reshape(2 * n)
    experts = jnp.arange(N_EXPERTS, dtype=jnp.int32)
    onehot = (eidx[:, None] == experts[None, :]).astype(jnp.int32)
    counts = jnp.sum(onehot, axis=0)
    rank = jnp.sum((jnp.cumsum(onehot, axis=0) - onehot) * onehot, axis=1)
    padded = (counts + MOE_TILE - 1) // MOE_TILE * MOE_TILE
    ends = jnp.cumsum(padded)
    starts = ends - padded
    pos = jnp.sum(onehot * starts[None, :], axis=1) + rank
    tile_start = jnp.arange(tiles, dtype=jnp.int32)[:, None] * MOE_TILE
    in_region = ((tile_start >= starts[None, :]) & (tile_start < ends[None, :])).astype(jnp.int32)
    used = jnp.sum(in_region, axis=1) > 0
    last_expert = jnp.max(jnp.where(counts > 0, experts, 0))
    tile_expert = jnp.where(used, jnp.sum(in_region * experts[None, :], axis=1), last_expert)
    tile_valid = jnp.sum(in_region * jnp.clip(counts[None, :] - (tile_start - starts[None, :]), 0, MOE_TILE), axis=1)
    tile_block = jnp.minimum(jnp.arange(tiles, dtype=jnp.int32), ends[-1] // MOE_TILE - 1)
    return pos.astype(jnp.int32), tile_expert.astype(jnp.int32), tile_valid.astype(jnp.int32), \
        tile_block.astype(jnp.int32)


def _expert_kernel(te_ref, nv_ref, tb_ref, x_ref, wg_ref, wu_ref, wd_ref, y_ref, wgu_s, wd_s):
    i = pl.program_id(0)
    prev = te_ref[jnp.maximum(i - 1, 0)]
    changed = jnp.logical_or(i == 0, te_ref[i] != prev)

    @pl.when(changed)
    def _():
        wgu_s[:, :D_FF] = wg_ref[0, 0].astype(BF16)
        wgu_s[:, D_FF:] = wu_ref[0, 0].astype(BF16)
        wd_s[...] = wd_ref[0, 0].astype(BF16)

    nv = nv_ref[i]

    def gate_up(r0):
        rows = r0 + lax.broadcasted_iota(jnp.int32, (MOE_SUB, 1), 0)
        xp = jnp.where(rows < nv, x_ref[r0:r0 + MOE_SUB, :], jnp.uint32(0))
        left, right = _unpack_halves(xp)
        xb = jnp.concatenate([left.astype(BF16), right.astype(BF16)], axis=1)
        return jnp.dot(xb, wgu_s[...], preferred_element_type=F32)

    def down(r0, gu):
        gate, up = gu[:, :D_FF], gu[:, D_FF:]
        he = (gate * jax.nn.sigmoid(gate) * up).astype(BF16)
        y_ref[r0:r0 + MOE_SUB, :] = _pack_halves(jnp.dot(he, wd_s[...], preferred_element_type=F32).astype(BF16))

    def run(n_sub):
        gu = gate_up(0)
        for j in range(n_sub):
            nxt = gate_up((j + 1) * MOE_SUB) if j + 1 < n_sub else None
            down(j * MOE_SUB, gu)
            gu = nxt

    n_subs = MOE_TILE // MOE_SUB
    for n_sub in range(1, n_subs + 1):
        lo = (n_sub - 1) * MOE_SUB
        in_range = nv > lo if n_sub == n_subs else jnp.logical_and(nv > lo, nv <= lo + MOE_SUB)
        pl.when(in_range)(functools.partial(run, n_sub))


def _experts(xs, tile_expert, tile_valid, tile_block, layer, w_gate, w_up, w_down):
    grid_spec = pltpu.PrefetchScalarGridSpec(
        num_scalar_prefetch=3,
        grid=(xs.shape[0] // MOE_TILE,),
        in_specs=[
            pl.BlockSpec((MOE_TILE, D_PACK), lambda i, te, nv, tb: (tb[i], 0)),
            pl.BlockSpec((1, 1, D_MODEL, D_FF), lambda i, te, nv, tb: (layer, te[i], 0, 0)),
            pl.BlockSpec((1, 1, D_MODEL, D_FF), lambda i, te, nv, tb: (layer, te[i], 0, 0)),
            pl.BlockSpec((1, 1, D_FF, D_MODEL), lambda i, te, nv, tb: (layer, te[i], 0, 0)),
        ],
        out_specs=pl.BlockSpec((MOE_TILE, D_PACK), lambda i, te, nv, tb: (tb[i], 0)),
        scratch_shapes=[pltpu.VMEM((D_MODEL, 2 * D_FF), BF16), pltpu.VMEM((D_FF, D_MODEL), BF16)],
    )
    return pl.pallas_call(
        _expert_kernel,
        grid_spec=grid_spec,
        out_shape=jax.ShapeDtypeStruct(xs.shape, jnp.uint32),
        compiler_params=_params("arbitrary"),
        name="experts",
    )(tile_expert, tile_valid, tile_block, xs, w_gate, w_up, w_down)


SC_WORKERS = 32


def _sc_mesh():
    return plsc.VectorSubcoreMesh(core_axis_name="core", subcore_axis_name="subcore")


def _sc_worker():
    return lax.axis_index("subcore") * 2 + lax.axis_index("core")


def _sc_scratch():
    return [pltpu.VMEM((SC_WINDOW,), jnp.int32), pltpu.VMEM((SC_WINDOW, D_PACK), jnp.uint32),
            pltpu.SemaphoreType.DMA]


def _dispatch(h2, pos):
    n = h2.shape[0]
    per_worker = n // SC_WORKERS
    scratch = _sc_scratch() + [pltpu.VMEM((SC_WINDOW,), jnp.int32), pltpu.SemaphoreType.DMA]

    @functools.partial(pl.kernel, out_type=jax.ShapeDtypeStruct((_moe_rows(n), D_PACK), jnp.uint32),
                       mesh=_sc_mesh(), scratch_types=scratch, name="moe_dispatch")
    def scatter(x_hbm, i_hbm, o_hbm, idx0_v, rows_v, sem0, idx1_v, sem1):
        base = _sc_worker() * per_worker

        @pl.loop(0, per_worker // SC_WINDOW)
        def _(j):
            t0 = base + j * SC_WINDOW
            pltpu.sync_copy(i_hbm.at[pl.ds(t0, SC_WINDOW)], idx0_v)
            pltpu.sync_copy(i_hbm.at[pl.ds(n + t0, SC_WINDOW)], idx1_v)
            pltpu.sync_copy(x_hbm.at[pl.ds(t0, SC_WINDOW)], rows_v)
            first = pltpu.async_copy(rows_v, o_hbm.at[idx0_v], sem0)
            second = pltpu.async_copy(rows_v, o_hbm.at[idx1_v], sem1)
            first.wait()
            second.wait()

    return scatter(h2, pos)


def _gather_pairs(y, pos):
    n = pos.shape[0] // 2
    per_worker = 2 * n // SC_WORKERS

    @functools.partial(pl.kernel, out_type=jax.ShapeDtypeStruct((2 * n, D_PACK), jnp.uint32),
                       mesh=_sc_mesh(), scratch_types=_sc_scratch(), name="moe_gather")
    def gather(y_hbm, i_hbm, o_hbm, idx_v, rows_v, sem):
        base = _sc_worker() * per_worker

        @pl.loop(0, per_worker // SC_WINDOW)
        def _(j):
            off = base + j * SC_WINDOW
            pltpu.sync_copy(i_hbm.at[pl.ds(off, SC_WINDOW)], idx_v)
            pltpu.async_copy(y_hbm.at[idx_v], rows_v, sem).wait()
            pltpu.sync_copy(rows_v, o_hbm.at[pl.ds(off, SC_WINDOW)])

    return gather(y, pos).reshape(2, n, D_PACK)


def _moe_post_norm(x_ref, yg_ref, w1_ref, w2_ref, mods_ref, lng_ref, lnb_ref):
    g2 = mods_ref[0, 0][5:6, :]
    y1 = jnp.concatenate(_unpack_halves(yg_ref[0]), axis=1)
    y2 = jnp.concatenate(_unpack_halves(yg_ref[1]), axis=1)
    moe = w1_ref[...] * y1 + w2_ref[...] * y2
    return _post_norm(x_ref[...], g2 * moe, lng_ref[...], lnb_ref[...])


def _combine_kernel(x_ref, yg_ref, w1_ref, w2_ref, mods_ref, lng_ref, lnb_ref, o_ref):
    o_ref[...] = _moe_post_norm(x_ref, yg_ref, w1_ref, w2_ref, mods_ref, lng_ref, lnb_ref)


def _combine(x1, yg, wsel, mods, layer, lng, lnb, t0, n):
    off = t0 // LN_TILE
    return pl.pallas_call(
        _combine_kernel,
        grid=(n // LN_TILE,),
        in_specs=[
            pl.BlockSpec((LN_TILE, D_MODEL), lambda t: (off + t, 0)),
            pl.BlockSpec((2, LN_TILE, D_PACK), lambda t: (0, off + t, 0)),
            pl.BlockSpec((LN_TILE, 1), lambda t: (off + t, 0)),
            pl.BlockSpec((LN_TILE, 1), lambda t: (off + t, 0)),
            pl.BlockSpec((1, 1, 6, D_MODEL), lambda t: (layer, _cond_row(off + t, LN_TILE), 0, 0)),
            pl.BlockSpec((1, D_MODEL), lambda t: (0, 0)),
            pl.BlockSpec((1, D_MODEL), lambda t: (0, 0)),
        ],
        out_specs=pl.BlockSpec((LN_TILE, D_MODEL), lambda t: (t, 0)),
        out_shape=jax.ShapeDtypeStruct((n, D_MODEL), F32),
        compiler_params=_params("arbitrary"),
        name="moe_combine",
    )(x1, yg, wsel[0].reshape(T_ALL, 1), wsel[1].reshape(T_ALL, 1), mods, lng, lnb)


def _moe_rows_of_tokens(h2p, logits, b_router, layer, w_gate, w_up, w_down):
    eidx, wsel = _route(logits, b_router)
    eidx, wsel = eidx.reshape(2, T_ALL), wsel.reshape(2, T_ALL)
    pos, tile_expert, tile_valid, tile_block = _route_plan(eidx)
    xs = _dispatch(h2p, pos)
    y = _experts(xs, tile_expert, tile_valid, tile_block, layer, w_gate, w_up, w_down)
    return _gather_pairs(y, pos), wsel


def _qkv_kernel(x1_ref, yg_ref, w1_ref, w2_ref, mods0_ref, lng_ref, lnb_ref, mods_ref, w_ref,
                x_ref, qkv_ref, nk_ref, nv_ref, w_s):
    t = pl.program_id(0)

    @pl.when(t == 0)
    def _():
        w_s[...] = w_ref[0].astype(BF16)

    m = mods_ref[0, 0]
    sh1, sc1 = m[0:1, :], m[1:2, :]

    def norm(rows):
        x = _moe_post_norm(x1_ref.at[rows], yg_ref.at[:, rows], w1_ref.at[rows], w2_ref.at[rows],
                           mods0_ref, lng_ref, lnb_ref)
        x_ref[rows, :] = x
        return (x * (1.0 + sc1) + sh1).astype(BF16)

    def project(rows, h):
        r = jnp.dot(h, w_s[...], preferred_element_type=F32)
        qkv_ref[rows, :D_MODEL] = (r[:, :D_MODEL] * (HEAD_DIM ** -0.5)).astype(BF16)
        qkv_ref[rows, D_MODEL:] = r[:, D_MODEL:].astype(BF16)
        return r

    is_ctx = t < T_CTX // QKV_TILE
    rows = slice(0, QKV_TILE)

    @pl.when(is_ctx)
    def _():
        r = project(rows, norm(rows))
        for out_ref, base in ((nk_ref, D_MODEL), (nv_ref, 2 * D_MODEL)):
            for b in range(QKV_TILE // SEQ):
                for p in range(HEAD_PAIRS):
                    pair = r[b * SEQ:(b + 1) * SEQ, base + p * LANES: base + (p + 1) * LANES].T
                    out_ref[b, 0, 2 * p] = pair[:HEAD_DIM]
                    out_ref[b, 0, 2 * p + 1] = pair[HEAD_DIM:]

    @pl.when(jnp.logical_not(is_ctx))
    def _():
        project(rows, norm(rows))


def _qkv(x1, yg, wsel, mods, lng, lnb, w_qkv):
    seqs = QKV_TILE // SEQ
    row_spec = pl.BlockSpec((QKV_TILE, D_MODEL), lambda t: (t, 0))
    col_spec = pl.BlockSpec((QKV_TILE, 1), lambda t: (t, 0))
    vec_spec = pl.BlockSpec((1, D_MODEL), lambda t: (0, 0))

    def mods_spec(layer):
        return pl.BlockSpec((1, 1, 6, D_MODEL), lambda t: (layer, _cond_row(t, QKV_TILE), 0, 0))

    cache_spec = pl.BlockSpec((seqs, 1, N_HEADS, HEAD_DIM, SEQ),
                              lambda t: (jnp.minimum(t, T_CTX // QKV_TILE - 1), 0, 0, 0, 0))
    cache_shape = jax.ShapeDtypeStruct((BATCH, 1, N_HEADS, HEAD_DIM, SEQ), F32)
    return pl.pallas_call(
        _qkv_kernel,
        grid=(T_ALL // QKV_TILE,),
        in_specs=[
            row_spec,
            pl.BlockSpec((2, QKV_TILE, D_PACK), lambda t: (0, t, 0)),
            col_spec,
            col_spec,
            mods_spec(0),
            vec_spec,
            vec_spec,
            mods_spec(1),
            pl.BlockSpec((1, D_MODEL, 3 * D_MODEL), lambda t: (0, 0, 0)),
        ],
        out_specs=[row_spec, pl.BlockSpec((QKV_TILE, 3 * D_MODEL), lambda t: (t, 0)), cache_spec, cache_spec],
        out_shape=[jax.ShapeDtypeStruct((T_ALL, D_MODEL), F32),
                   jax.ShapeDtypeStruct((T_ALL, 3 * D_MODEL), BF16), cache_shape, cache_shape],
        scratch_shapes=[pltpu.VMEM((D_MODEL, 3 * D_MODEL), BF16)],
        compiler_params=_params("arbitrary"),
        name="qkv",
    )(x1, yg, wsel[0].reshape(T_ALL, 1), wsel[1].reshape(T_ALL, 1), mods, lng, lnb, mods, w_qkv)


def _dot_nt(a, b):
    return lax.dot_general(a, b, (((1,), (1,)), ((), ())), preferred_element_type=F32)


def _head_masks():
    lane = lax.broadcasted_iota(jnp.int32, (1, LANES), 1)
    return lane < HEAD_DIM, lane >= HEAD_DIM


def _ctx_attn_kernel(q_ref, k_ref, v_ref, o_ref):
    left, right = _head_masks()
    for p in range(HEAD_PAIRS):
        cols = slice(p * LANES, (p + 1) * LANES)
        q2, k2, v2 = q_ref[:, cols], k_ref[:, cols], v_ref[:, cols]
        halves = []
        for mask in (left, right):
            qh = jnp.where(mask, q2, jnp.zeros_like(q2))
            s = _dot_nt(qh, k2)
            e = jnp.exp(s - jnp.max(s, axis=-1, keepdims=True))
            o2 = jnp.dot(e.astype(BF16), v2, preferred_element_type=F32)
            halves.append(o2 / jnp.sum(e, axis=-1, keepdims=True))
        o_ref[:, cols] = jnp.where(left, halves[0], halves[1]).astype(BF16)


def _ctx_attention(qkv):
    return pl.pallas_call(
        _ctx_attn_kernel,
        grid=(BATCH,),
        in_specs=[pl.BlockSpec((SEQ, D_MODEL), lambda b, j=j: (b, j)) for j in range(3)],
        out_specs=pl.BlockSpec((SEQ, D_MODEL), lambda b: (b, 0)),
        out_shape=jax.ShapeDtypeStruct((T_ALL, D_MODEL), BF16),
        compiler_params=_params("arbitrary"),
        name="ctx_attention",
    )(qkv, qkv, qkv)


_LAT_Q_BLOCK_ROWS = 4
_LAT_KEY_ROWS = ((0, 8), (0, 12), (4, 16), (8, 16))


def _softmax_rows(s_ref, p_ref, l_ref):
    sc = s_ref[...]
    e = jnp.exp(sc - jnp.max(sc, axis=-1, keepdims=True))
    l_ref[...] = jnp.sum(e, axis=-1, keepdims=True)
    p_ref[...] = e.astype(BF16)


def _lat_attn_kernel(q_ref, k_ref, v_ref, ck_ref, cv_ref, eb_ref, o_ctx_ref, o_ref, bias_s, s_scr, p_scr, l_scr):
    del o_ctx_ref
    left, right = _head_masks()

    @pl.when(pl.program_id(1) == 0)
    def _():
        neg = jnp.full((GRID_W, LANES), NEG_INF, F32)
        for hh in range(2):
            for r in range(GRID_ROWS):
                r0 = min(max(r - WIN_H // 2, 0), GRID_ROWS - WIN_H)
                for j in range(GRID_ROWS // 2):
                    parts = []
                    for kr in (2 * j, 2 * j + 1):
                        parts.append(eb_ref[hh, kr - r + WIN_H - 1] if r0 <= kr < r0 + WIN_H else None)
                    if parts[0] is None and parts[1] is None:
                        val = neg
                    else:
                        val = jnp.where(left, neg if parts[0] is None else parts[0],
                                        neg if parts[1] is None else parts[1])
                    bias_s[hh, r * GRID_W:(r + 1) * GRID_W, j * LANES:(j + 1) * LANES] = val

    ck = jnp.concatenate([ck_ref[0, 0, 0], ck_ref[0, 0, 1]], axis=0).astype(BF16)
    cv = jnp.concatenate([cv_ref[0, 0, 0], cv_ref[0, 0, 1]], axis=0).astype(BF16)
    qrows = _LAT_Q_BLOCK_ROWS * GRID_W
    units = [(qb, hh) for qb in range(len(_LAT_KEY_ROWS)) for hh in range(2)]

    def refs(u):
        nk = (_LAT_KEY_ROWS[units[u][0]][1] - _LAT_KEY_ROWS[units[u][0]][0]) * GRID_W
        width = nk + PAST_LEN
        return nk, s_scr.at[u % 2, :, :width], p_scr.at[u % 2, :, :width], l_scr.at[u % 2]

    def scores(u):
        qb, hh = units[u]
        kr0, kr1 = _LAT_KEY_ROWS[qb]
        qs, ks = slice(qb * qrows, (qb + 1) * qrows), slice(kr0 * GRID_W, kr1 * GRID_W)
        nk, s_ref, _, _ = refs(u)
        q2 = q_ref[qs, :]
        qh = jnp.where(right if hh else left, q2, jnp.zeros_like(q2))
        s_ref[:, :nk] = _dot_nt(qh, k_ref[ks, :]) + bias_s[hh, qs, ks]
        s_ref[:, nk:] = jnp.dot(qh, ck, preferred_element_type=F32)

    def weighted_values(u):
        qb, _ = units[u]
        kr0, kr1 = _LAT_KEY_ROWS[qb]
        nk, _, p_ref, l_ref = refs(u)
        o2 = (jnp.dot(p_ref[:, :nk], v_ref[kr0 * GRID_W:kr1 * GRID_W, :], preferred_element_type=F32)
              + _dot_nt(p_ref[:, nk:], cv))
        return o2 / l_ref[...]

    scores(0)
    halves = []
    for u in range(len(units)):
        if u + 1 < len(units):
            scores(u + 1)
        _, s_ref, p_ref, l_ref = refs(u)
        _softmax_rows(s_ref, p_ref, l_ref)
        halves.append(weighted_values(u))
        if len(halves) == 2:
            qb = units[u][0]
            o_ref[qb * qrows:(qb + 1) * qrows, :] = jnp.where(left, halves[0], halves[1]).astype(BF16)
            halves = []


def _lat_attention(qkv, cache_kt, cache_vt, ebias, o_buf):
    row0 = T_CTX // DEC_SEQ
    qrows = _LAT_Q_BLOCK_ROWS * GRID_W
    max_keys = max(k1 - k0 for k0, k1 in _LAT_KEY_ROWS) * GRID_W + PAST_LEN
    cache_spec = pl.BlockSpec((1, 1, 2, HEAD_DIM, PAST_LEN), lambda p, b: (b, 0, p, 0, 0))
    return pl.pallas_call(
        _lat_attn_kernel,
        grid=(HEAD_PAIRS, DEC_BATCH),
        in_specs=[
            pl.BlockSpec((DEC_SEQ, LANES), lambda p, b: (row0 + b, p)),
            pl.BlockSpec((DEC_SEQ, LANES), lambda p, b: (row0 + b, HEAD_PAIRS + p)),
            pl.BlockSpec((DEC_SEQ, LANES), lambda p, b: (row0 + b, 2 * HEAD_PAIRS + p)),
            cache_spec,
            cache_spec,
            pl.BlockSpec((2, 2 * WIN_H - 1, GRID_W, LANES), lambda p, b: (p, 0, 0, 0)),
            pl.BlockSpec(memory_space=pl.ANY),
        ],
        out_specs=pl.BlockSpec((DEC_SEQ, LANES), lambda p, b: (row0 + b, p)),
        out_shape=jax.ShapeDtypeStruct((T_ALL, D_MODEL), BF16),
        scratch_shapes=[pltpu.VMEM((2, DEC_SEQ, DEC_SEQ), F32),
                        pltpu.VMEM((2, qrows, max_keys), F32),
                        pltpu.VMEM((2, qrows, max_keys), BF16),
                        pltpu.VMEM((2, qrows, 1), F32)],
        input_output_aliases={6: 0},
        compiler_params=_params("arbitrary", "arbitrary"),
        name="lat_attention",
    )(qkv, qkv, qkv, cache_kt, cache_vt, ebias, o_buf)


def _expanded_bias(rel_bias):
    col = jnp.arange(GRID_W)
    col_start = jnp.clip(col - WIN_W // 2, 0, GRID_W - WIN_W)
    col_ok = (col[None, :] >= col_start[:, None]) & (col[None, :] < col_start[:, None] + WIN_W)
    dx = jnp.clip(col[None, :] - col[:, None] + (WIN_W - 1), 0, 2 * WIN_W - 2)
    onehot = (dx[None] == jnp.arange(2 * WIN_W - 1)[:, None, None]).astype(F32)
    eb = jnp.einsum("hyj,jqk->hyqk", rel_bias.astype(F32), onehot, precision=lax.Precision.HIGHEST)
    eb = jnp.where(col_ok[None, None], eb, NEG_INF)
    return jnp.concatenate([eb, eb], axis=-1)


def _wo_kernel(o_ref, x_ref, mods_ref, w_ref, lng_ref, lnb_ref, w2_ref, x1_ref, h2p_ref, logit_ref, w_s):
    @pl.when(pl.program_id(0) == 0)
    def _():
        w_s[...] = w_ref[0].astype(BF16)

    for r0 in range(0, WO_TILE, TILE):
        rs = slice(r0, r0 + TILE)
        mix = jnp.dot(o_ref[rs, :], w_s[...], preferred_element_type=F32)
        _epilogue(x_ref[rs, :], mix, mods_ref[0, 0], lng_ref[...], lnb_ref[...], w2_ref,
                  x1_ref.at[rs, :], h2p_ref.at[rs, :], logit_ref.at[:, rs])


def _attn_out(o, x, mods, w_o, lng, lnb, w2):
    return pl.pallas_call(
        _wo_kernel,
        grid=(T_ALL // WO_TILE,),
        in_specs=[
            pl.BlockSpec((WO_TILE, D_MODEL), lambda t: (t, 0)),
            pl.BlockSpec((WO_TILE, D_MODEL), lambda t: (t, 0)),
            pl.BlockSpec((1, 1, 6, D_MODEL), lambda t: (1, _cond_row(t, WO_TILE), 0, 0)),
            pl.BlockSpec((1, D_MODEL, D_MODEL), lambda t: (0, 0, 0)),
        ] + _EPI_IN_SPECS,
        out_specs=[
            pl.BlockSpec((WO_TILE, D_MODEL), lambda t: (t, 0)),
            pl.BlockSpec((WO_TILE, D_PACK), lambda t: (t, 0)),
            pl.BlockSpec((N_EXPERTS, WO_TILE), lambda t: (0, t)),
        ],
        out_shape=_EPI_OUT_SHAPE,
        scratch_shapes=[pltpu.VMEM((D_MODEL, D_MODEL), BF16)],
        compiler_params=_params("arbitrary"),
        name="attn_out",
    )(o, x, mods, w_o, lng, lnb, w2)


def kernel(x_prompt, x_sample, cache_k, cache_v, c, c_ctx, w_ada, b_ada, ln1_g, ln1_b, ln2_g, ln2_b,
           pool_w, pool_scale, w_qkv, w_o, rel_bias, w_router, b_router, w_gate, w_up, w_down):
    cond = jnp.zeros((N_COND, D_MODEL), F32).at[0].set(c_ctx).at[1:1 + DEC_BATCH].set(c)
    mods = _ada(cond, w_ada, b_ada).reshape(2, N_COND, 6, D_MODEL)
    wrt = w_router.T
    wrt_hi = wrt.astype(BF16)
    w2 = jnp.concatenate([wrt_hi, (wrt - wrt_hi.astype(F32)).astype(BF16)], axis=0)

    x, h2p, logits = _pool_layer(x_prompt.reshape(T_CTX, D_MODEL), x_sample.reshape(T_LAT, D_MODEL), mods,
                                 pool_w[0].astype(BF16), pool_scale[0:1], ln1_g[0:1], ln1_b[0:1], w2)
    yg, wsel = _moe_rows_of_tokens(h2p, logits, b_router, 0, w_gate, w_up, w_down)

    x, qkv, new_kt, new_vt = _qkv(x, yg, wsel, mods, ln2_g[0:1], ln2_b[0:1], w_qkv)
    o = _lat_attention(qkv, cache_k.transpose(0, 1, 2, 4, 3), cache_v.transpose(0, 1, 2, 4, 3),
                       _expanded_bias(rel_bias[0]), _ctx_attention(qkv))
    x, h2p, logits = _attn_out(o, x, mods, w_o, ln1_g[1:2], ln1_b[1:2], w2)
    yg, wsel = _moe_rows_of_tokens(h2p, logits, b_router, 1, w_gate, w_up, w_down)
    y_ctx, y_lat = [_combine(x, yg, wsel, mods, 1, ln2_g[1:2], ln2_b[1:2], t0, n) for t0, n in SEGMENTS]
    return (y_ctx.reshape(BATCH, SEQ, D_MODEL), y_lat.reshape(DEC_BATCH, DEC_SEQ, D_MODEL),
            new_kt.transpose(0, 1, 2, 4, 3), new_vt.transpose(0, 1, 2, 4, 3))
```

```python
import functools

import jax
import jax.numpy as jnp
from jax import lax
from jax.experimental import pallas as pl
from jax.experimental.pallas import tpu as pltpu
from jax.experimental.pallas import tpu_sc as plsc

F32 = jnp.float32
BF16 = jnp.bfloat16

D_MODEL = 1024
BATCH = 16
SEQ = 256
DEC_BATCH = 8
DEC_SEQ = 1024
PAST_LEN = 512
GRID_W = 64
GRID_ROWS = DEC_SEQ // GRID_W
POOL_SIZES = (2, 4, 8, 16)
POOL_GROUP_DIM = D_MODEL // len(POOL_SIZES)
POOL_HALO = 8
N_HEADS = 16
HEAD_DIM = 64
WIN_H = 8
WIN_W = 16
N_EXPERTS = 16
EXPERTS_PER_GROUP = 4
N_EXPERT_GROUPS = 4
D_FF = 512
ALPHA = (2.0 * 2) ** 0.25
LN_EPS = 1e-5
NEG_INF = -1e30

T_CTX = BATCH * SEQ
T_LAT = DEC_BATCH * DEC_SEQ
T_ALL = T_CTX + T_LAT
N_COND = 16
TILE = 256
N_TILES = T_ALL // TILE
CTX_TILES = T_CTX // TILE
TILES_PER_LAT_SEQ = DEC_SEQ // TILE
MOE_TILE = 1024
MOE_SUB = 256
SEGMENTS = ((0, T_CTX), (T_CTX, T_LAT))
LN_TILE = 512
QKV_TILE = 512
WO_TILE = 512
SC_WINDOW = 64
D_PACK = D_MODEL // 2
ROUTE_ROWS = T_ALL // 128
HEAD_PAIRS = N_HEADS // 2
LANES = 128
VMEM_LIMIT = 56 * 1024 * 1024


def _cond_row(t, tile):
    ctx_tiles = T_CTX // tile
    per_seq = DEC_SEQ // tile
    return jnp.maximum(t - ctx_tiles + per_seq, 0) // per_seq


def _params(*sem):
    return pltpu.CompilerParams(dimension_semantics=sem, vmem_limit_bytes=VMEM_LIMIT)


def _pack_halves(v):
    half = v.shape[1] // 2
    hi = lax.bitcast_convert_type(v[:, :half].astype(F32), jnp.uint32)
    lo = lax.bitcast_convert_type(v[:, half:].astype(F32), jnp.uint32)
    return hi | (lo >> 16)


def _unpack_halves(p):
    left = lax.bitcast_convert_type(p & jnp.uint32(0xFFFF0000), F32)
    right = lax.bitcast_convert_type(p << 16, F32)
    return left, right


def _ada_kernel(cond_ref, w_ref, b_ref, o_ref):
    cnd = cond_ref[...]
    act = cnd * jax.nn.sigmoid(cnd)
    a_hi = act.astype(BF16)
    a_lo = (act - a_hi.astype(F32)).astype(BF16)
    w = w_ref[0]
    w_hi = w.astype(BF16)
    w_lo = (w - w_hi.astype(F32)).astype(BF16)
    a2 = jnp.concatenate([a_hi, a_lo], axis=0)
    p = jnp.dot(a2, w_hi, preferred_element_type=F32)
    q = jnp.dot(a_hi, w_lo, preferred_element_type=F32)
    o_ref[0] = p[:N_COND] + p[N_COND:] + q + b_ref[0]


def _ada(cond, w_ada, b_ada):
    depth, d, n = w_ada.shape
    bn = 1536
    return pl.pallas_call(
        _ada_kernel,
        grid=(depth, n // bn),
        in_specs=[
            pl.BlockSpec((N_COND, d), lambda i, j: (0, 0)),
            pl.BlockSpec((1, d, bn), lambda i, j: (i, 0, j)),
            pl.BlockSpec((1, 1, bn), lambda i, j: (i, 0, j)),
        ],
        out_specs=pl.BlockSpec((1, N_COND, bn), lambda i, j: (i, 0, j)),
        out_shape=jax.ShapeDtypeStruct((depth, N_COND, n), F32),
        compiler_params=_params("arbitrary", "arbitrary"),
        name="ada",
    )(cond, w_ada, b_ada.reshape(depth, 1, n))


def _post_norm(x, upd, g, b):
    y = ALPHA * x + upd
    mu = jnp.mean(y, axis=-1, keepdims=True)
    yc = y - mu
    var = jnp.mean(yc * yc, axis=-1, keepdims=True)
    return yc * lax.rsqrt(var + LN_EPS) * g + b


def _router_logits(h2, h_hi, w2_ref):
    h_lo = (h2 - h_hi.astype(F32)).astype(BF16)
    w2 = w2_ref[...]
    p = _dot_nt(w2, h_hi)
    q = _dot_nt(w2[:N_EXPERTS], h_lo)
    return p[:N_EXPERTS] + p[N_EXPERTS:] + q


def _epilogue(x, mix, m, lng, lnb, w2_ref, x1_ref, h2p_ref, logit_ref):
    g1, sh2, sc2 = m[2:3, :], m[3:4, :], m[4:5, :]
    x1 = _post_norm(x, g1 * mix, lng, lnb)
    h2 = x1 * (1.0 + sc2) + sh2
    h_hi = h2.astype(BF16)
    x1_ref[...] = x1
    h2p_ref[...] = _pack_halves(h_hi)
    logit_ref[...] = _router_logits(h2, h_hi, w2_ref)


_EPI_IN_SPECS = [
    pl.BlockSpec((1, D_MODEL), lambda t: (0, 0)),
    pl.BlockSpec((1, D_MODEL), lambda t: (0, 0)),
    pl.BlockSpec((2 * N_EXPERTS, D_MODEL), lambda t: (0, 0)),
]
_EPI_OUT_SPECS = [
    pl.BlockSpec((TILE, D_MODEL), lambda t: (t, 0)),
    pl.BlockSpec((TILE, D_PACK), lambda t: (t, 0)),
    pl.BlockSpec((N_EXPERTS, TILE), lambda t: (0, t)),
]
_EPI_OUT_SHAPE = [
    jax.ShapeDtypeStruct((T_ALL, D_MODEL), F32),
    jax.ShapeDtypeStruct((T_ALL, D_PACK), jnp.uint32),
    jax.ShapeDtypeStruct((N_EXPERTS, T_ALL), F32),
]


def _route_kernel(logit_ref, br_ref, e_ref, w_ref):
    aff = jax.nn.sigmoid(logit_ref[...])
    sel = aff + br_ref[...]
    sel_rows = [sel[e] for e in range(N_EXPERTS)]
    aff_rows = [aff[e] for e in range(N_EXPERTS)]

    def group_score(g):
        r = sel_rows[g * EXPERTS_PER_GROUP:(g + 1) * EXPERTS_PER_GROUP]
        best = None
        for i in range(EXPERTS_PER_GROUP):
            for j in range(i + 1, EXPERTS_PER_GROUP):
                pair = r[i] + r[j]
                best = pair if best is None else jnp.maximum(best, pair)
        return best

    best = group_score(0)
    gidx = jnp.zeros_like(best, dtype=jnp.int32)
    for g in range(1, N_EXPERT_GROUPS):
        sc = group_score(g)
        better = sc > best
        gidx = jnp.where(better, g, gidx)
        best = jnp.where(better, sc, best)

    def pick_group(rows, j):
        out = rows[j]
        for g in range(1, N_EXPERT_GROUPS):
            out = jnp.where(gidx == g, rows[g * EXPERTS_PER_GROUP + j], out)
        return out

    cand = [pick_group(sel_rows, j) for j in range(EXPERTS_PER_GROUP)]
    cand_aff = [pick_group(aff_rows, j) for j in range(EXPERTS_PER_GROUP)]

    def argmax_first(vals):
        bv, bi, ba = vals[0], jnp.zeros_like(gidx), cand_aff[0]
        for j in range(1, EXPERTS_PER_GROUP):
            better = vals[j] > bv
            bv = jnp.where(better, vals[j], bv)
            bi = jnp.where(better, j, bi)
            ba = jnp.where(better, cand_aff[j], ba)
        return bi, ba

    i1, a1 = argmax_first(cand)
    rest = [jnp.where(i1 == j, -jnp.inf, cand[j]) for j in range(EXPERTS_PER_GROUP)]
    i2, a2 = argmax_first(rest)
    denom = a1 + a2
    base = gidx * EXPERTS_PER_GROUP
    e_ref[0] = base + i1
    e_ref[1] = base + i2
    w_ref[0] = a1 / denom
    w_ref[1] = a2 / denom


def _route(logits, b_router):
    return pl.pallas_call(
        _route_kernel,
        out_shape=[jax.ShapeDtypeStruct((2, ROUTE_ROWS, 128), jnp.int32),
                   jax.ShapeDtypeStruct((2, ROUTE_ROWS, 128), F32)],
        compiler_params=pltpu.CompilerParams(vmem_limit_bytes=VMEM_LIMIT),
        name="route",
    )(logits.reshape(N_EXPERTS, ROUTE_ROWS, 128), b_router.reshape(N_EXPERTS, 1, 1))


def _mods_spec(layer):
    return pl.BlockSpec((1, 1, 6, D_MODEL), lambda t: (layer, _cond_row(t, TILE), 0, 0))


def _pool_kernel(xa_ref, xb_ref, xp_ref, xn_ref, mods_ref, pw_ref, ps_ref, lng_ref, lnb_ref, w2_ref,
                 x1_ref, h2p_ref, logit_ref):
    t = pl.program_id(0)
    m = mods_ref[0, 0]
    sh1, sc1 = m[0:1, :], m[1:2, :]
    in_lat = t >= CTX_TILES
    sub = (t - CTX_TILES) % TILES_PER_LAT_SEQ
    is_first = jnp.logical_or(jnp.logical_not(in_lat), sub == 0)
    is_last = jnp.logical_or(jnp.logical_not(in_lat), sub == TILES_PER_LAT_SEQ - 1)
    seq_len = jnp.where(in_lat, DEC_SEQ, SEQ)
    pos0 = jnp.where(in_lat, sub * TILE, 0)

    x = jnp.where(in_lat, xb_ref[...], xa_ref[...])
    h = x * (1.0 + sc1) + sh1
    hp = jnp.where(is_first, 0.0, xp_ref[...] * (1.0 + sc1) + sh1)
    hn = jnp.where(is_last, 0.0, xn_ref[...] * (1.0 + sc1) + sh1)
    hext = jnp.concatenate([hp, h, hn], axis=0)
    ext = TILE + 2 * POOL_HALO
    pos = pos0 + lax.broadcasted_iota(jnp.int32, (TILE, 1), 0)

    outs = []
    for g, w in enumerate(POOL_SIZES):
        lo_c, hi_c = g * POOL_GROUP_DIM, (g + 1) * POOL_GROUP_DIM
        a = hext[:, lo_c:hi_c]
        k = 1
        while k < w:
            a = a + pltpu.roll(a, ext - k, axis=0)
            k *= 2
        off = POOL_HALO - w // 2
        win = pltpu.roll(a, ext - off, axis=0)[:TILE] if off else a[:TILE]
        lo = jnp.maximum(pos - w // 2, 0)
        hi = jnp.minimum(pos - w // 2 + w, seq_len)
        cnt = (hi - lo).astype(F32)
        pooled = win / cnt - h[:, lo_c:hi_c]
        outs.append(jnp.dot(pooled.astype(BF16), pw_ref[g], preferred_element_type=F32))
    mix = jnp.concatenate(outs, axis=1) * ps_ref[...]
    _epilogue(x, mix, m, lng_ref[...], lnb_ref[...], w2_ref, x1_ref, h2p_ref, logit_ref)


def _pool_layer(x_ctx, x_lat, mods, pool_w, pool_scale, lng, lnb, w2):
    halo_blocks = TILE // POOL_HALO
    last_halo = T_LAT // POOL_HALO - 1
    return pl.pallas_call(
        _pool_kernel,
        grid=(N_TILES,),
        in_specs=[
            pl.BlockSpec((TILE, D_MODEL), lambda t: (jnp.minimum(t, CTX_TILES - 1), 0)),
            pl.BlockSpec((TILE, D_MODEL), lambda t: (jnp.maximum(t - CTX_TILES, 0), 0)),
            pl.BlockSpec((POOL_HALO, D_MODEL), lambda t: (jnp.maximum((t - CTX_TILES) * halo_blocks - 1, 0), 0)),
            pl.BlockSpec((POOL_HALO, D_MODEL),
                         lambda t: (jnp.clip((t - CTX_TILES + 1) * halo_blocks, 0, last_halo), 0)),
            _mods_spec(0),
            pl.BlockSpec((len(POOL_SIZES), POOL_GROUP_DIM, POOL_GROUP_DIM), lambda t: (0, 0, 0)),
            pl.BlockSpec((1, D_MODEL), lambda t: (0, 0)),
        ] + _EPI_IN_SPECS,
        out_specs=_EPI_OUT_SPECS,
        out_shape=_EPI_OUT_SHAPE,
        compiler_params=_params("arbitrary"),
        name="pool_mixer",
    )(x_ctx, x_lat, x_lat, x_lat, mods, pool_w, pool_scale, lng, lnb, w2)


def _moe_rows(n):
    return 2 * n + N_EXPERTS * MOE_TILE


def _route_plan(eidx):
    n = eidx.shape[1]
    tiles = _moe_rows(n) // MOE_TILE
    eidx = eidx.reshape(2 * n)
    experts = jnp.arange(N_EXPERTS, dtype=jnp.int32)
    onehot = (eidx[:, None] == experts[None, :]).astype(jnp.int32)
    counts = jnp.sum(onehot, axis=0)
    rank = jnp.sum((jnp.cumsum(onehot, axis=0) - onehot) * onehot, axis=1)
    padded = (counts + MOE_TILE - 1) // MOE_TILE * MOE_TILE
    ends = jnp.cumsum(padded)
    starts = ends - padded
    pos = jnp.sum(onehot * starts[None, :], axis=1) + rank
    tile_start = jnp.arange(tiles, dtype=jnp.int32)[:, None] * MOE_TILE
    in_region = ((tile_start >= starts[None, :]) & (tile_start < ends[None, :])).astype(jnp.int32)
    used = jnp.sum(in_region, axis=1) > 0
    last_expert = jnp.max(jnp.where(counts > 0, experts, 0))
    tile_expert = jnp.where(used, jnp.sum(in_region * experts[None, :], axis=1), last_expert)
    tile_valid = jnp.sum(in_region * jnp.clip(counts[None, :] - (tile_start - starts[None, :]), 0, MOE_TILE), axis=1)
    tile_block = jnp.minimum(jnp.arange(tiles, dtype=jnp.int32), ends[-1] // MOE_TILE - 1)
    return pos.astype(jnp.int32), tile_expert.astype(jnp.int32), tile_valid.astype(jnp.int32), \
        tile_block.astype(jnp.int32)


def _expert_kernel(te_ref, nv_ref, tb_ref, x_ref, wg_ref, wu_ref, wd_ref, y_ref, wgu_s, wd_s):
    i = pl.program_id(0)
    prev = te_ref[jnp.maximum(i - 1, 0)]
    changed = jnp.logical_or(i == 0, te_ref[i] != prev)

    @pl.when(changed)
    def _():
        wgu_s[:, :D_FF] = wg_ref[0, 0].astype(BF16)
        wgu_s[:, D_FF:] = wu_ref[0, 0].astype(BF16)
        wd_s[...] = wd_ref[0, 0].astype(BF16)

    nv = nv_ref[i]

    def gate_up(r0):
        rows = r0 + lax.broadcasted_iota(jnp.int32, (MOE_SUB, 1), 0)
        xp = jnp.where(rows < nv, x_ref[r0:r0 + MOE_SUB, :], jnp.uint32(0))
        left, right = _unpack_halves(xp)
        xb = jnp.concatenate([left.astype(BF16), right.astype(BF16)], axis=1)
        return jnp.dot(xb, wgu_s[...], preferred_element_type=F32)

    def down(r0, gu):
        gate, up = gu[:, :D_FF], gu[:, D_FF:]
        he = (gate * jax.nn.sigmoid(gate) * up).astype(BF16)
        y_ref[r0:r0 + MOE_SUB, :] = _pack_halves(jnp.dot(he, wd_s[...], preferred_element_type=F32).astype(BF16))

    def run(n_sub):
        gu = gate_up(0)
        for j in range(n_sub):
            nxt = gate_up((j + 1) * MOE_SUB) if j + 1 < n_sub else None
            down(j * MOE_SUB, gu)
            gu = nxt

    n_subs = MOE_TILE // MOE_SUB
    for n_sub in range(1, n_subs + 1):
        lo = (n_sub - 1) * MOE_SUB
        in_range = nv > lo if n_sub == n_subs else jnp.logical_and(nv > lo, nv <= lo + MOE_SUB)
        pl.when(in_range)(functools.partial(run, n_sub))


def _experts(xs, tile_expert, tile_valid, tile_block, layer, w_gate, w_up, w_down):
    grid_spec = pltpu.PrefetchScalarGridSpec(
        num_scalar_prefetch=3,
        grid=(xs.shape[0] // MOE_TILE,),
        in_specs=[
            pl.BlockSpec((MOE_TILE, D_PACK), lambda i, te, nv, tb: (tb[i], 0)),
            pl.BlockSpec((1, 1, D_MODEL, D_FF), lambda i, te, nv, tb: (layer, te[i], 0, 0)),
            pl.BlockSpec((1, 1, D_MODEL, D_FF), lambda i, te, nv, tb: (layer, te[i], 0, 0)),
            pl.BlockSpec((1, 1, D_FF, D_MODEL), lambda i, te, nv, tb: (layer, te[i], 0, 0)),
        ],
        out_specs=pl.BlockSpec((MOE_TILE, D_PACK), lambda i, te, nv, tb: (tb[i], 0)),
        scratch_shapes=[pltpu.VMEM((D_MODEL, 2 * D_FF), BF16), pltpu.VMEM((D_FF, D_MODEL), BF16)],
    )
    return pl.pallas_call(
        _expert_kernel,
        grid_spec=grid_spec,
        out_shape=jax.ShapeDtypeStruct(xs.shape, jnp.uint32),
        compiler_params=_params("arbitrary"),
        name="experts",
    )(tile_expert, tile_valid, tile_block, xs, w_gate, w_up, w_down)


SC_WORKERS = 32


def _sc_mesh():
    return plsc.VectorSubcoreMesh(core_axis_name="core", subcore_axis_name="subcore")


def _sc_worker():
    return lax.axis_index("subcore") * 2 + lax.axis_index("core")


def _sc_scratch():
    return [pltpu.VMEM((SC_WINDOW,), jnp.int32), pltpu.VMEM((SC_WINDOW, D_PACK), jnp.uint32),
            pltpu.SemaphoreType.DMA]


def _dispatch(h2, pos):
    n = h2.shape[0]
    per_worker = n // SC_WORKERS
    scratch = _sc_scratch() + [pltpu.VMEM((SC_WINDOW,), jnp.int32), pltpu.SemaphoreType.DMA]

    @functools.partial(pl.kernel, out_type=jax.ShapeDtypeStruct((_moe_rows(n), D_PACK), jnp.uint32),
                       mesh=_sc_mesh(), scratch_types=scratch, name="moe_dispatch")
    def scatter(x_hbm, i_hbm, o_hbm, idx0_v, rows_v, sem0, idx1_v, sem1):
        base = _sc_worker() * per_worker

        @pl.loop(0, per_worker // SC_WINDOW)
        def _(j):
            t0 = base + j * SC_WINDOW
            pltpu.sync_copy(i_hbm.at[pl.ds(t0, SC_WINDOW)], idx0_v)
            pltpu.sync_copy(i_hbm.at[pl.ds(n + t0, SC_WINDOW)], idx1_v)
            pltpu.sync_copy(x_hbm.at[pl.ds(t0, SC_WINDOW)], rows_v)
            first = pltpu.async_copy(rows_v, o_hbm.at[idx0_v], sem0)
            second = pltpu.async_copy(rows_v, o_hbm.at[idx1_v], sem1)
            first.wait()
            second.wait()

    return scatter(h2, pos)


def _gather_pairs(y, pos, t0, n):
    n_all = pos.shape[0] // 2
    per_worker = 2 * n // SC_WORKERS

    @functools.partial(pl.kernel, out_type=jax.ShapeDtypeStruct((2 * n, D_PACK), jnp.uint32),
                       mesh=_sc_mesh(), scratch_types=_sc_scratch(), name="moe_gather")
    def gather(y_hbm, i_hbm, o_hbm, idx_v, rows_v, sem):
        base = _sc_worker() * per_worker
        src = (base // n) * n_all + t0 + base % n

        @pl.loop(0, per_worker // SC_WINDOW)
        def _(j):
            pltpu.sync_copy(i_hbm.at[pl.ds(src + j * SC_WINDOW, SC_WINDOW)], idx_v)
            pltpu.async_copy(y_hbm.at[idx_v], rows_v, sem).wait()
            pltpu.sync_copy(rows_v, o_hbm.at[pl.ds(base + j * SC_WINDOW, SC_WINDOW)])

    return gather(y, pos).reshape(2, n, D_PACK)


def _moe_post_norm(x_ref, yg_ref, w1_ref, w2_ref, mods_ref, lng_ref, lnb_ref):
    g2 = mods_ref[0, 0][5:6, :]
    y1 = jnp.concatenate(_unpack_halves(yg_ref[0]), axis=1)
    y2 = jnp.concatenate(_unpack_halves(yg_ref[1]), axis=1)
    moe = w1_ref[...] * y1 + w2_ref[...] * y2
    return _post_norm(x_ref[...], g2 * moe, lng_ref[...], lnb_ref[...])


def _combine_kernel(x_ref, yg_ref, w1_ref, w2_ref, mods_ref, lng_ref, lnb_ref, o_ref):
    o_ref[...] = _moe_post_norm(x_ref, yg_ref, w1_ref, w2_ref, mods_ref, lng_ref, lnb_ref)


def _combine(x1, yg, wsel, mods, layer, lng, lnb, t0, n):
    off = t0 // LN_TILE
    return pl.pallas_call(
        _combine_kernel,
        grid=(n // LN_TILE,),
        in_specs=[
            pl.BlockSpec((LN_TILE, D_MODEL), lambda t: (off + t, 0)),
            pl.BlockSpec((2, LN_TILE, D_PACK), lambda t: (0, t, 0)),
            pl.BlockSpec((LN_TILE, 1), lambda t: (off + t, 0)),
            pl.BlockSpec((LN_TILE, 1), lambda t: (off + t, 0)),
            pl.BlockSpec((1, 1, 6, D_MODEL), lambda t: (layer, _cond_row(off + t, LN_TILE), 0, 0)),
            pl.BlockSpec((1, D_MODEL), lambda t: (0, 0)),
            pl.BlockSpec((1, D_MODEL), lambda t: (0, 0)),
        ],
        out_specs=pl.BlockSpec((LN_TILE, D_MODEL), lambda t: (t, 0)),
        out_shape=jax.ShapeDtypeStruct((n, D_MODEL), F32),
        compiler_params=_params("arbitrary"),
        name="moe_combine",
    )(x1, yg, wsel[0].reshape(T_ALL, 1), wsel[1].reshape(T_ALL, 1), mods, lng, lnb)


def _moe_rows_of_tokens(h2p, logits, b_router, layer, w_gate, w_up, w_down):
    eidx, wsel = _route(logits, b_router)
    eidx, wsel = eidx.reshape(2, T_ALL), wsel.reshape(2, T_ALL)
    pos, tile_expert, tile_valid, tile_block = _route_plan(eidx)
    xs = _dispatch(h2p, pos)
    y = _experts(xs, tile_expert, tile_valid, tile_block, layer, w_gate, w_up, w_down)
    return [_gather_pairs(y, pos, t0, n) for t0, n in SEGMENTS], wsel


def _qkv_kernel(is_ctx, x1_ref, yg_ref, w1_ref, w2_ref, mods0_ref, lng_ref, lnb_ref, mods_ref, w_ref, *rest):
    if is_ctx:
        x_ref, qkv_ref, nk_ref, nv_ref, w_s = rest
    else:
        x_ref, qkv_ref, w_s = rest[2:]

    @pl.when(pl.program_id(0) == 0)
    def _():
        w_s[...] = w_ref[0].astype(BF16)

    m = mods_ref[0, 0]
    sh1, sc1 = m[0:1, :], m[1:2, :]

    def norm(rows):
        x = _moe_post_norm(x1_ref.at[rows], yg_ref.at[:, rows], w1_ref.at[rows], w2_ref.at[rows],
                           mods0_ref, lng_ref, lnb_ref)
        x_ref[rows, :] = x
        return (x * (1.0 + sc1) + sh1).astype(BF16)

    def project(rows, h):
        r = jnp.dot(h, w_s[...], preferred_element_type=F32)
        qkv_ref[rows, :D_MODEL] = (r[:, :D_MODEL] * (HEAD_DIM ** -0.5)).astype(BF16)
        qkv_ref[rows, D_MODEL:] = r[:, D_MODEL:].astype(BF16)
        return r

    rows = slice(0, QKV_TILE)
    r = project(rows, norm(rows))
    if is_ctx:
        for out_ref, base in ((nk_ref, D_MODEL), (nv_ref, 2 * D_MODEL)):
            for b in range(QKV_TILE // SEQ):
                for p in range(HEAD_PAIRS):
                    pair = r[b * SEQ:(b + 1) * SEQ, base + p * LANES: base + (p + 1) * LANES].T
                    out_ref[b, 0, 2 * p] = pair[:HEAD_DIM]
                    out_ref[b, 0, 2 * p + 1] = pair[HEAD_DIM:]


def _qkv(x1, yg, wsel, mods, lng, lnb, w_qkv, t0, n, into):
    is_ctx = into is None
    off = t0 // QKV_TILE
    seqs = QKV_TILE // SEQ
    row_spec = pl.BlockSpec((QKV_TILE, D_MODEL), lambda t: (off + t, 0))
    col_spec = pl.BlockSpec((QKV_TILE, 1), lambda t: (off + t, 0))
    vec_spec = pl.BlockSpec((1, D_MODEL), lambda t: (0, 0))

    def mods_spec(layer):
        return pl.BlockSpec((1, 1, 6, D_MODEL), lambda t: (layer, _cond_row(off + t, QKV_TILE), 0, 0))

    in_specs = [
        row_spec,
        pl.BlockSpec((2, QKV_TILE, D_PACK), lambda t: (0, t, 0)),
        col_spec,
        col_spec,
        mods_spec(0),
        vec_spec,
        vec_spec,
        mods_spec(1),
        pl.BlockSpec((1, D_MODEL, 3 * D_MODEL), lambda t: (0, 0, 0)),
    ]
    args = [x1, yg, wsel[0].reshape(T_ALL, 1), wsel[1].reshape(T_ALL, 1), mods, lng, lnb, mods, w_qkv]
    out_specs = [row_spec, pl.BlockSpec((QKV_TILE, 3 * D_MODEL), lambda t: (off + t, 0))]
    out_shape = [jax.ShapeDtypeStruct((T_ALL, D_MODEL), F32), jax.ShapeDtypeStruct((T_ALL, 3 * D_MODEL), BF16)]
    aliases = {}
    if is_ctx:
        cache_spec = pl.BlockSpec((seqs, 1, N_HEADS, HEAD_DIM, SEQ), lambda t: (t, 0, 0, 0, 0))
        cache_shape = jax.ShapeDtypeStruct((BATCH, 1, N_HEADS, HEAD_DIM, SEQ), F32)
        out_specs += [cache_spec, cache_spec]
        out_shape += [cache_shape, cache_shape]
    else:
        aliases = {len(args): 0, len(args) + 1: 1}
        in_specs += [pl.BlockSpec(memory_space=pl.ANY), pl.BlockSpec(memory_space=pl.ANY)]
        args += list(into)
    return pl.pallas_call(
        functools.partial(_qkv_kernel, is_ctx),
        grid=(n // QKV_TILE,),
        in_specs=in_specs,
        out_specs=out_specs,
        out_shape=out_shape,
        scratch_shapes=[pltpu.VMEM((D_MODEL, 3 * D_MODEL), BF16)],
        input_output_aliases=aliases,
        compiler_params=_params("arbitrary"),
        name="qkv",
    )(*args)


def _dot_nt(a, b):
    return lax.dot_general(a, b, (((1,), (1,)), ((), ())), preferred_element_type=F32)


def _head_masks():
    lane = lax.broadcasted_iota(jnp.int32, (1, LANES), 1)
    return lane < HEAD_DIM, lane >= HEAD_DIM


def _ctx_attn_kernel(q_ref, k_ref, v_ref, o_ref):
    left, right = _head_masks()
    for p in range(HEAD_PAIRS):
        cols = slice(p * LANES, (p + 1) * LANES)
        q2, k2, v2 = q_ref[:, cols], k_ref[:, cols], v_ref[:, cols]
        halves = []
        for mask in (left, right):
            qh = jnp.where(mask, q2, jnp.zeros_like(q2))
            s = _dot_nt(qh, k2)
            e = jnp.exp(s - jnp.max(s, axis=-1, keepdims=True))
            o2 = jnp.dot(e.astype(BF16), v2, preferred_element_type=F32)
            halves.append(o2 / jnp.sum(e, axis=-1, keepdims=True))
        o_ref[:, cols] = jnp.where(left, halves[0], halves[1]).astype(BF16)


def _ctx_attention(qkv):
    return pl.pallas_call(
        _ctx_attn_kernel,
        grid=(BATCH,),
        in_specs=[pl.BlockSpec((SEQ, D_MODEL), lambda b, j=j: (b, j)) for j in range(3)],
        out_specs=pl.BlockSpec((SEQ, D_MODEL), lambda b: (b, 0)),
        out_shape=jax.ShapeDtypeStruct((T_ALL, D_MODEL), BF16),
        compiler_params=_params("arbitrary"),
        name="ctx_attention",
    )(qkv, qkv, qkv)


_LAT_Q_BLOCK_ROWS = 4
_LAT_KEY_ROWS = ((0, 8), (0, 12), (4, 16), (8, 16))


def _softmax_rows(s_ref, p_ref, l_ref):
    sc = s_ref[...]
    e = jnp.exp(sc - jnp.max(sc, axis=-1, keepdims=True))
    l_ref[...] = jnp.sum(e, axis=-1, keepdims=True)
    p_ref[...] = e.astype(BF16)


def _lat_attn_kernel(q_ref, k_ref, v_ref, ck_ref, cv_ref, eb_ref, o_ctx_ref, o_ref, bias_s, s_scr, p_scr, l_scr):
    del o_ctx_ref
    left, right = _head_masks()

    @pl.when(pl.program_id(1) == 0)
    def _():
        neg = jnp.full((GRID_W, LANES), NEG_INF, F32)
        for hh in range(2):
            for r in range(GRID_ROWS):
                r0 = min(max(r - WIN_H // 2, 0), GRID_ROWS - WIN_H)
                for j in range(GRID_ROWS // 2):
                    parts = []
                    for kr in (2 * j, 2 * j + 1):
                        parts.append(eb_ref[hh, kr - r + WIN_H - 1] if r0 <= kr < r0 + WIN_H else None)
                    if parts[0] is None and parts[1] is None:
                        val = neg
                    else:
                        val = jnp.where(left, neg if parts[0] is None else parts[0],
                                        neg if parts[1] is None else parts[1])
                    bias_s[hh, r * GRID_W:(r + 1) * GRID_W, j * LANES:(j + 1) * LANES] = val

    ck = jnp.concatenate([ck_ref[0, 0, 0], ck_ref[0, 0, 1]], axis=0).astype(BF16)
    cv = jnp.concatenate([cv_ref[0, 0, 0], cv_ref[0, 0, 1]], axis=0).astype(BF16)
    qrows = _LAT_Q_BLOCK_ROWS * GRID_W
    units = [(qb, hh) for qb in range(len(_LAT_KEY_ROWS)) for hh in range(2)]

    def refs(u):
        nk = (_LAT_KEY_ROWS[units[u][0]][1] - _LAT_KEY_ROWS[units[u][0]][0]) * GRID_W
        width = nk + PAST_LEN
        return nk, s_scr.at[u % 2, :, :width], p_scr.at[u % 2, :, :width], l_scr.at[u % 2]

    def scores(u):
        qb, hh = units[u]
        kr0, kr1 = _LAT_KEY_ROWS[qb]
        qs, ks = slice(qb * qrows, (qb + 1) * qrows), slice(kr0 * GRID_W, kr1 * GRID_W)
        nk, s_ref, _, _ = refs(u)
        q2 = q_ref[qs, :]
        qh = jnp.where(right if hh else left, q2, jnp.zeros_like(q2))
        s_ref[:, :nk] = _dot_nt(qh, k_ref[ks, :]) + bias_s[hh, qs, ks]
        s_ref[:, nk:] = jnp.dot(qh, ck, preferred_element_type=F32)

    def weighted_values(u):
        qb, _ = units[u]
        kr0, kr1 = _LAT_KEY_ROWS[qb]
        nk, _, p_ref, l_ref = refs(u)
        o2 = (jnp.dot(p_ref[:, :nk], v_ref[kr0 * GRID_W:kr1 * GRID_W, :], preferred_element_type=F32)
              + _dot_nt(p_ref[:, nk:], cv))
        return o2 / l_ref[...]

    scores(0)
    halves = []
    for u in range(len(units)):
        if u + 1 < len(units):
            scores(u + 1)
        _, s_ref, p_ref, l_ref = refs(u)
        _softmax_rows(s_ref, p_ref, l_ref)
        halves.append(weighted_values(u))
        if len(halves) == 2:
            qb = units[u][0]
            o_ref[qb * qrows:(qb + 1) * qrows, :] = jnp.where(left, halves[0], halves[1]).astype(BF16)
            halves = []


def _lat_attention(qkv, cache_kt, cache_vt, ebias, o_buf):
    row0 = T_CTX // DEC_SEQ
    qrows = _LAT_Q_BLOCK_ROWS * GRID_W
    max_keys = max(k1 - k0 for k0, k1 in _LAT_KEY_ROWS) * GRID_W + PAST_LEN
    cache_spec = pl.BlockSpec((1, 1, 2, HEAD_DIM, PAST_LEN), lambda p, b: (b, 0, p, 0, 0))
    return pl.pallas_call(
        _lat_attn_kernel,
        grid=(HEAD_PAIRS, DEC_BATCH),
        in_specs=[
            pl.BlockSpec((DEC_SEQ, LANES), lambda p, b: (row0 + b, p)),
            pl.BlockSpec((DEC_SEQ, LANES), lambda p, b: (row0 + b, HEAD_PAIRS + p)),
            pl.BlockSpec((DEC_SEQ, LANES), lambda p, b: (row0 + b, 2 * HEAD_PAIRS + p)),
            cache_spec,
            cache_spec,
            pl.BlockSpec((2, 2 * WIN_H - 1, GRID_W, LANES), lambda p, b: (p, 0, 0, 0)),
            pl.BlockSpec(memory_space=pl.ANY),
        ],
        out_specs=pl.BlockSpec((DEC_SEQ, LANES), lambda p, b: (row0 + b, p)),
        out_shape=jax.ShapeDtypeStruct((T_ALL, D_MODEL), BF16),
        scratch_shapes=[pltpu.VMEM((2, DEC_SEQ, DEC_SEQ), F32),
                        pltpu.VMEM((2, qrows, max_keys), F32),
                        pltpu.VMEM((2, qrows, max_keys), BF16),
                        pltpu.VMEM((2, qrows, 1), F32)],
        input_output_aliases={6: 0},
        compiler_params=_params("arbitrary", "arbitrary"),
        name="lat_attention",
    )(qkv, qkv, qkv, cache_kt, cache_vt, ebias, o_buf)


def _expanded_bias(rel_bias):
    col = jnp.arange(GRID_W)
    col_start = jnp.clip(col - WIN_W // 2, 0, GRID_W - WIN_W)
    col_ok = (col[None, :] >= col_start[:, None]) & (col[None, :] < col_start[:, None] + WIN_W)
    dx = jnp.clip(col[None, :] - col[:, None] + (WIN_W - 1), 0, 2 * WIN_W - 2)
    onehot = (dx[None] == jnp.arange(2 * WIN_W - 1)[:, None, None]).astype(F32)
    eb = jnp.einsum("hyj,jqk->hyqk", rel_bias.astype(F32), onehot, precision=lax.Precision.HIGHEST)
    eb = jnp.where(col_ok[None, None], eb, NEG_INF)
    return jnp.concatenate([eb, eb], axis=-1)


def _wo_kernel(o_ref, x_ref, mods_ref, w_ref, lng_ref, lnb_ref, w2_ref, x1_ref, h2p_ref, logit_ref, w_s):
    @pl.when(pl.program_id(0) == 0)
    def _():
        w_s[...] = w_ref[0].astype(BF16)

    for r0 in range(0, WO_TILE, TILE):
        rs = slice(r0, r0 + TILE)
        mix = jnp.dot(o_ref[rs, :], w_s[...], preferred_element_type=F32)
        _epilogue(x_ref[rs, :], mix, mods_ref[0, 0], lng_ref[...], lnb_ref[...], w2_ref,
                  x1_ref.at[rs, :], h2p_ref.at[rs, :], logit_ref.at[:, rs])


def _attn_out(o, x, mods, w_o, lng, lnb, w2):
    return pl.pallas_call(
        _wo_kernel,
        grid=(T_ALL // WO_TILE,),
        in_specs=[
            pl.BlockSpec((WO_TILE, D_MODEL), lambda t: (t, 0)),
            pl.BlockSpec((WO_TILE, D_MODEL), lambda t: (t, 0)),
            pl.BlockSpec((1, 1, 6, D_MODEL), lambda t: (1, _cond_row(t, WO_TILE), 0, 0)),
            pl.BlockSpec((1, D_MODEL, D_MODEL), lambda t: (0, 0, 0)),
        ] + _EPI_IN_SPECS,
        out_specs=[
            pl.BlockSpec((WO_TILE, D_MODEL), lambda t: (t, 0)),
            pl.BlockSpec((WO_TILE, D_PACK), lambda t: (t, 0)),
            pl.BlockSpec((N_EXPERTS, WO_TILE), lambda t: (0, t)),
        ],
        out_shape=_EPI_OUT_SHAPE,
        scratch_shapes=[pltpu.VMEM((D_MODEL, D_MODEL), BF16)],
        compiler_params=_params("arbitrary"),
        name="attn_out",
    )(o, x, mods, w_o, lng, lnb, w2)


def kernel(x_prompt, x_sample, cache_k, cache_v, c, c_ctx, w_ada, b_ada, ln1_g, ln1_b, ln2_g, ln2_b,
           pool_w, pool_scale, w_qkv, w_o, rel_bias, w_router, b_router, w_gate, w_up, w_down):
    cond = jnp.zeros((N_COND, D_MODEL), F32).at[0].set(c_ctx).at[1:1 + DEC_BATCH].set(c)
    mods = _ada(cond, w_ada, b_ada).reshape(2, N_COND, 6, D_MODEL)
    wrt = w_router.T
    wrt_hi = wrt.astype(BF16)
    w2 = jnp.concatenate([wrt_hi, (wrt - wrt_hi.astype(F32)).astype(BF16)], axis=0)

    x, h2p, logits = _pool_layer(x_prompt.reshape(T_CTX, D_MODEL), x_sample.reshape(T_LAT, D_MODEL), mods,
                                 pool_w[0].astype(BF16), pool_scale[0:1], ln1_g[0:1], ln1_b[0:1], w2)
    (yg_ctx, yg_lat), wsel = _moe_rows_of_tokens(h2p, logits, b_router, 0, w_gate, w_up, w_down)

    x_ctx, qkv_ctx, new_kt, new_vt = _qkv(x, yg_ctx, wsel, mods, ln2_g[0:1], ln2_b[0:1], w_qkv, 0, T_CTX, None)
    x, qkv = _qkv(x, yg_lat, wsel, mods, ln2_g[0:1], ln2_b[0:1], w_qkv, T_CTX, T_LAT, (x_ctx, qkv_ctx))
    o = _lat_attention(qkv, cache_k.transpose(0, 1, 2, 4, 3), cache_v.transpose(0, 1, 2, 4, 3),
                       _expanded_bias(rel_bias[0]), _ctx_attention(qkv))
    x, h2p, logits = _attn_out(o, x, mods, w_o, ln1_g[1:2], ln1_b[1:2], w2)
    ygs, wsel = _moe_rows_of_tokens(h2p, logits, b_router, 1, w_gate, w_up, w_down)
    y_ctx, y_lat = [_combine(x, yg, wsel, mods, 1, ln2_g[1:2], ln2_b[1:2], t0, n)
                    for yg, (t0, n) in zip(ygs, SEGMENTS)]
    return (y_ctx.reshape(BATCH, SEQ, D_MODEL), y_lat.reshape(DEC_BATCH, DEC_SEQ, D_MODEL),
            new_kt.transpose(0, 1, 2, 4, 3), new_vt.transpose(0, 1, 2, 4, 3))
```

```python
import functools

import jax
import jax.numpy as jnp
from jax import lax
from jax.experimental import pallas as pl
from jax.experimental.pallas import tpu as pltpu
from jax.experimental.pallas import tpu_sc as plsc

F32 = jnp.float32
BF16 = jnp.bfloat16

D_MODEL = 1024
BATCH = 16
SEQ = 256
DEC_BATCH = 8
DEC_SEQ = 1024
PAST_LEN = 512
GRID_W = 64
GRID_ROWS = DEC_SEQ // GRID_W
POOL_SIZES = (2, 4, 8, 16)
POOL_GROUP_DIM = D_MODEL // len(POOL_SIZES)
POOL_HALO = 8
N_HEADS = 16
HEAD_DIM = 64
WIN_H = 8
WIN_W = 16
N_EXPERTS = 16
EXPERTS_PER_GROUP = 4
N_EXPERT_GROUPS = 4
D_FF = 512
ALPHA = (2.0 * 2) ** 0.25
LN_EPS = 1e-5
NEG_INF = -1e30

T_CTX = BATCH * SEQ
T_LAT = DEC_BATCH * DEC_SEQ
T_ALL = T_CTX + T_LAT
N_COND = 16
TILE = 256
N_TILES = T_ALL // TILE
CTX_TILES = T_CTX // TILE
TILES_PER_LAT_SEQ = DEC_SEQ // TILE
MOE_TILE = 1024
MOE_SUB = 256
SEGMENTS = ((0, T_CTX), (T_CTX, T_LAT))
LN_TILE = 512
QKV_TILE = 512
WO_TILE = 512
SC_WINDOW = 64
D_PACK = D_MODEL // 2
ROUTE_ROWS = T_ALL // 128
HEAD_PAIRS = N_HEADS // 2
LANES = 128
VMEM_LIMIT = 56 * 1024 * 1024


def _cond_row(t, tile):
    ctx_tiles = T_CTX // tile
    per_seq = DEC_SEQ // tile
    return jnp.maximum(t - ctx_tiles + per_seq, 0) // per_seq


def _params(*sem):
    return pltpu.CompilerParams(dimension_semantics=sem, vmem_limit_bytes=VMEM_LIMIT)


def _pack_halves(v):
    half = v.shape[1] // 2
    hi = lax.bitcast_convert_type(v[:, :half].astype(F32), jnp.uint32)
    lo = lax.bitcast_convert_type(v[:, half:].astype(F32), jnp.uint32)
    return hi | (lo >> 16)


def _unpack_halves(p):
    left = lax.bitcast_convert_type(p & jnp.uint32(0xFFFF0000), F32)
    right = lax.bitcast_convert_type(p << 16, F32)
    return left, right


def _ada_kernel(cond_ref, w_ref, b_ref, o_ref):
    cnd = cond_ref[...]
    act = cnd * jax.nn.sigmoid(cnd)
    a_hi = act.astype(BF16)
    a_lo = (act - a_hi.astype(F32)).astype(BF16)
    w = w_ref[0]
    w_hi = w.astype(BF16)
    w_lo = (w - w_hi.astype(F32)).astype(BF16)
    a2 = jnp.concatenate([a_hi, a_lo], axis=0)
    p = jnp.dot(a2, w_hi, preferred_element_type=F32)
    q = jnp.dot(a_hi, w_lo, preferred_element_type=F32)
    o_ref[0] = p[:N_COND] + p[N_COND:] + q + b_ref[0]


def _ada(cond, w_ada, b_ada, layer):
    depth, d, n = w_ada.shape
    bn = 1536
    mods = pl.pallas_call(
        _ada_kernel,
        grid=(n // bn,),
        in_specs=[
            pl.BlockSpec((N_COND, d), lambda j: (0, 0)),
            pl.BlockSpec((1, d, bn), lambda j: (layer, 0, j)),
            pl.BlockSpec((1, 1, bn), lambda j: (layer, 0, j)),
        ],
        out_specs=pl.BlockSpec((1, N_COND, bn), lambda j: (0, 0, j)),
        out_shape=jax.ShapeDtypeStruct((1, N_COND, n), F32),
        compiler_params=_params("arbitrary"),
        name="ada",
    )(cond, w_ada, b_ada.reshape(depth, 1, n))
    return mods.reshape(N_COND, 6, D_MODEL)


def _post_norm(x, upd, g, b):
    y = ALPHA * x + upd
    mu = jnp.mean(y, axis=-1, keepdims=True)
    yc = y - mu
    var = jnp.mean(yc * yc, axis=-1, keepdims=True)
    return yc * lax.rsqrt(var + LN_EPS) * g + b


def _router_logits(h2, h_hi, w2_ref):
    h_lo = (h2 - h_hi.astype(F32)).astype(BF16)
    w2 = w2_ref[...]
    p = _dot_nt(w2, h_hi)
    q = _dot_nt(w2[:N_EXPERTS], h_lo)
    return p[:N_EXPERTS] + p[N_EXPERTS:] + q


def _epilogue(x, mix, m, lng, lnb, w2_ref, x1_ref, h2p_ref, logit_ref):
    g1, sh2, sc2 = m[2:3, :], m[3:4, :], m[4:5, :]
    x1 = _post_norm(x, g1 * mix, lng, lnb)
    h2 = x1 * (1.0 + sc2) + sh2
    h_hi = h2.astype(BF16)
    x1_ref[...] = x1
    h2p_ref[...] = _pack_halves(h_hi)
    logit_ref[...] = _router_logits(h2, h_hi, w2_ref)


_EPI_IN_SPECS = [
    pl.BlockSpec((1, D_MODEL), lambda t: (0, 0)),
    pl.BlockSpec((1, D_MODEL), lambda t: (0, 0)),
    pl.BlockSpec((2 * N_EXPERTS, D_MODEL), lambda t: (0, 0)),
]
_EPI_OUT_SPECS = [
    pl.BlockSpec((TILE, D_MODEL), lambda t: (t, 0)),
    pl.BlockSpec((TILE, D_PACK), lambda t: (t, 0)),
    pl.BlockSpec((N_EXPERTS, TILE), lambda t: (0, t)),
]
_EPI_OUT_SHAPE = [
    jax.ShapeDtypeStruct((T_ALL, D_MODEL), F32),
    jax.ShapeDtypeStruct((T_ALL, D_PACK), jnp.uint32),
    jax.ShapeDtypeStruct((N_EXPERTS, T_ALL), F32),
]


def _route_kernel(logit_ref, br_ref, e_ref, w_ref):
    aff = jax.nn.sigmoid(logit_ref[...])
    sel = aff + br_ref[...]
    sel_rows = [sel[e] for e in range(N_EXPERTS)]
    aff_rows = [aff[e] for e in range(N_EXPERTS)]

    def group_score(g):
        r = sel_rows[g * EXPERTS_PER_GROUP:(g + 1) * EXPERTS_PER_GROUP]
        best = None
        for i in range(EXPERTS_PER_GROUP):
            for j in range(i + 1, EXPERTS_PER_GROUP):
                pair = r[i] + r[j]
                best = pair if best is None else jnp.maximum(best, pair)
        return best

    best = group_score(0)
    gidx = jnp.zeros_like(best, dtype=jnp.int32)
    for g in range(1, N_EXPERT_GROUPS):
        sc = group_score(g)
        better = sc > best
        gidx = jnp.where(better, g, gidx)
        best = jnp.where(better, sc, best)

    def pick_group(rows, j):
        out = rows[j]
        for g in range(1, N_EXPERT_GROUPS):
            out = jnp.where(gidx == g, rows[g * EXPERTS_PER_GROUP + j], out)
        return out

    cand = [pick_group(sel_rows, j) for j in range(EXPERTS_PER_GROUP)]
    cand_aff = [pick_group(aff_rows, j) for j in range(EXPERTS_PER_GROUP)]

    def argmax_first(vals):
        bv, bi, ba = vals[0], jnp.zeros_like(gidx), cand_aff[0]
        for j in range(1, EXPERTS_PER_GROUP):
            better = vals[j] > bv
            bv = jnp.where(better, vals[j], bv)
            bi = jnp.where(better, j, bi)
            ba = jnp.where(better, cand_aff[j], ba)
        return bi, ba

    i1, a1 = argmax_first(cand)
    rest = [jnp.where(i1 == j, -jnp.inf, cand[j]) for j in range(EXPERTS_PER_GROUP)]
    i2, a2 = argmax_first(rest)
    denom = a1 + a2
    base = gidx * EXPERTS_PER_GROUP
    e_ref[0] = base + i1
    e_ref[1] = base + i2
    w_ref[0] = a1 / denom
    w_ref[1] = a2 / denom


def _route(logits, b_router):
    return pl.pallas_call(
        _route_kernel,
        out_shape=[jax.ShapeDtypeStruct((2, ROUTE_ROWS, 128), jnp.int32),
                   jax.ShapeDtypeStruct((2, ROUTE_ROWS, 128), F32)],
        compiler_params=pltpu.CompilerParams(vmem_limit_bytes=VMEM_LIMIT),
        name="route",
    )(logits.reshape(N_EXPERTS, ROUTE_ROWS, 128), b_router.reshape(N_EXPERTS, 1, 1))


def _mods_spec(tile, first_tile=0):
    return pl.BlockSpec((1, 6, D_MODEL), lambda t: (_cond_row(first_tile + t, tile), 0, 0))


def _pool_kernel(xa_ref, xb_ref, xp_ref, xn_ref, mods_ref, pw_ref, ps_ref, lng_ref, lnb_ref, w2_ref,
                 x1_ref, h2p_ref, logit_ref):
    t = pl.program_id(0)
    m = mods_ref[0]
    sh1, sc1 = m[0:1, :], m[1:2, :]
    in_lat = t >= CTX_TILES
    sub = (t - CTX_TILES) % TILES_PER_LAT_SEQ
    is_first = jnp.logical_or(jnp.logical_not(in_lat), sub == 0)
    is_last = jnp.logical_or(jnp.logical_not(in_lat), sub == TILES_PER_LAT_SEQ - 1)
    seq_len = jnp.where(in_lat, DEC_SEQ, SEQ)
    pos0 = jnp.where(in_lat, sub * TILE, 0)

    x = jnp.where(in_lat, xb_ref[...], xa_ref[...])
    h = x * (1.0 + sc1) + sh1
    hp = jnp.where(is_first, 0.0, xp_ref[...] * (1.0 + sc1) + sh1)
    hn = jnp.where(is_last, 0.0, xn_ref[...] * (1.0 + sc1) + sh1)
    hext = jnp.concatenate([hp, h, hn], axis=0)
    ext = TILE + 2 * POOL_HALO
    pos = pos0 + lax.broadcasted_iota(jnp.int32, (TILE, 1), 0)

    outs = []
    for g, w in enumerate(POOL_SIZES):
        lo_c, hi_c = g * POOL_GROUP_DIM, (g + 1) * POOL_GROUP_DIM
        a = hext[:, lo_c:hi_c]
        k = 1
        while k < w:
            a = a + pltpu.roll(a, ext - k, axis=0)
            k *= 2
        off = POOL_HALO - w // 2
        win = pltpu.roll(a, ext - off, axis=0)[:TILE] if off else a[:TILE]
        lo = jnp.maximum(pos - w // 2, 0)
        hi = jnp.minimum(pos - w // 2 + w, seq_len)
        cnt = (hi - lo).astype(F32)
        pooled = win / cnt - h[:, lo_c:hi_c]
        outs.append(jnp.dot(pooled.astype(BF16), pw_ref[g], preferred_element_type=F32))
    mix = jnp.concatenate(outs, axis=1) * ps_ref[...]
    _epilogue(x, mix, m, lng_ref[...], lnb_ref[...], w2_ref, x1_ref, h2p_ref, logit_ref)


def _pool_layer(x_ctx, x_lat, mods, pool_w, pool_scale, lng, lnb, w2):
    halo_blocks = TILE // POOL_HALO
    last_halo = T_LAT // POOL_HALO - 1
    return pl.pallas_call(
        _pool_kernel,
        grid=(N_TILES,),
        in_specs=[
            pl.BlockSpec((TILE, D_MODEL), lambda t: (jnp.minimum(t, CTX_TILES - 1), 0)),
            pl.BlockSpec((TILE, D_MODEL), lambda t: (jnp.maximum(t - CTX_TILES, 0), 0)),
            pl.BlockSpec((POOL_HALO, D_MODEL), lambda t: (jnp.maximum((t - CTX_TILES) * halo_blocks - 1, 0), 0)),
            pl.BlockSpec((POOL_HALO, D_MODEL),
                         lambda t: (jnp.clip((t - CTX_TILES + 1) * halo_blocks, 0, last_halo), 0)),
            _mods_spec(TILE),
            pl.BlockSpec((len(POOL_SIZES), POOL_GROUP_DIM, POOL_GROUP_DIM), lambda t: (0, 0, 0)),
            pl.BlockSpec((1, D_MODEL), lambda t: (0, 0)),
        ] + _EPI_IN_SPECS,
        out_specs=_EPI_OUT_SPECS,
        out_shape=_EPI_OUT_SHAPE,
        compiler_params=_params("arbitrary"),
        name="pool_mixer",
    )(x_ctx, x_lat, x_lat, x_lat, mods, pool_w, pool_scale, lng, lnb, w2)


def _moe_rows(n):
    return 2 * n + N_EXPERTS * MOE_TILE


def _route_plan(eidx):
    n = eidx.shape[1]
    tiles = _moe_rows(n) // MOE_TILE
    eidx = eidx.reshape(2 * n)
    experts = jnp.arange(N_EXPERTS, dtype=jnp.int32)
    onehot = (eidx[:, None] == experts[None, :]).astype(jnp.int32)
    counts = jnp.sum(onehot, axis=0)
    rank = jnp.sum((jnp.cumsum(onehot, axis=0) - onehot) * onehot, axis=1)
    padded = (counts + MOE_TILE - 1) // MOE_TILE * MOE_TILE
    ends = jnp.cumsum(padded)
    starts = ends - padded
    pos = jnp.sum(onehot * starts[None, :], axis=1) + rank
    tile_start = jnp.arange(tiles, dtype=jnp.int32)[:, None] * MOE_TILE
    in_region = ((tile_start >= starts[None, :]) & (tile_start < ends[None, :])).astype(jnp.int32)
    used = jnp.sum(in_region, axis=1) > 0
    last_expert = jnp.max(jnp.where(counts > 0, experts, 0))
    tile_expert = jnp.where(used, jnp.sum(in_region * experts[None, :], axis=1), last_expert)
    first_block = starts // MOE_TILE
    block_in_expert = padded[None, :] // MOE_TILE - 1 - (tile_start - starts[None, :]) // MOE_TILE
    tile_valid = jnp.sum(in_region * jnp.clip(counts[None, :] - block_in_expert * MOE_TILE, 0, MOE_TILE), axis=1)
    tile_block = jnp.where(used, jnp.sum(in_region * (first_block[None, :] + block_in_expert), axis=1),
                           jnp.sum(jnp.where(experts == last_expert, first_block, 0)))
    return pos.astype(jnp.int32), tile_expert.astype(jnp.int32), tile_valid.astype(jnp.int32), \
        tile_block.astype(jnp.int32)


def _expert_kernel(te_ref, nv_ref, tb_ref, x_ref, wg_ref, wu_ref, wd_ref, y_ref, wgu_s, wd_s):
    i = pl.program_id(0)
    prev = te_ref[jnp.maximum(i - 1, 0)]
    changed = jnp.logical_or(i == 0, te_ref[i] != prev)

    @pl.when(changed)
    def _():
        wgu_s[:, :D_FF] = wg_ref[0, 0].astype(BF16)
        wgu_s[:, D_FF:] = wu_ref[0, 0].astype(BF16)
        wd_s[...] = wd_ref[0, 0].astype(BF16)

    nv = nv_ref[i]

    def gate_up(r0):
        rows = r0 + lax.broadcasted_iota(jnp.int32, (MOE_SUB, 1), 0)
        xp = jnp.where(rows < nv, x_ref[r0:r0 + MOE_SUB, :], jnp.uint32(0))
        left, right = _unpack_halves(xp)
        xb = jnp.concatenate([left.astype(BF16), right.astype(BF16)], axis=1)
        return jnp.dot(xb, wgu_s[...], preferred_element_type=F32)

    def down(r0, gu):
        gate, up = gu[:, :D_FF], gu[:, D_FF:]
        he = (gate * jax.nn.sigmoid(gate) * up).astype(BF16)
        y_ref[r0:r0 + MOE_SUB, :] = _pack_halves(jnp.dot(he, wd_s[...], preferred_element_type=F32).astype(BF16))

    def run(n_sub):
        gu = gate_up(0)
        for j in range(n_sub):
            nxt = gate_up((j + 1) * MOE_SUB) if j + 1 < n_sub else None
            down(j * MOE_SUB, gu)
            gu = nxt

    n_subs = MOE_TILE // MOE_SUB
    for n_sub in range(1, n_subs + 1):
        lo = (n_sub - 1) * MOE_SUB
        in_range = nv > lo if n_sub == n_subs else jnp.logical_and(nv > lo, nv <= lo + MOE_SUB)
        pl.when(in_range)(functools.partial(run, n_sub))


def _experts(xs, tile_expert, tile_valid, tile_block, layer, w_gate, w_up, w_down):
    grid_spec = pltpu.PrefetchScalarGridSpec(
        num_scalar_prefetch=3,
        grid=(xs.shape[0] // MOE_TILE,),
        in_specs=[
            pl.BlockSpec((MOE_TILE, D_PACK), lambda i, te, nv, tb: (tb[i], 0)),
            pl.BlockSpec((1, 1, D_MODEL, D_FF), lambda i, te, nv, tb: (layer, te[i], 0, 0)),
            pl.BlockSpec((1, 1, D_MODEL, D_FF), lambda i, te, nv, tb: (layer, te[i], 0, 0)),
            pl.BlockSpec((1, 1, D_FF, D_MODEL), lambda i, te, nv, tb: (layer, te[i], 0, 0)),
        ],
        out_specs=pl.BlockSpec((MOE_TILE, D_PACK), lambda i, te, nv, tb: (tb[i], 0)),
        scratch_shapes=[pltpu.VMEM((D_MODEL, 2 * D_FF), BF16), pltpu.VMEM((D_FF, D_MODEL), BF16)],
    )
    return pl.pallas_call(
        _expert_kernel,
        grid_spec=grid_spec,
        out_shape=jax.ShapeDtypeStruct(xs.shape, jnp.uint32),
        compiler_params=_params("arbitrary"),
        name="experts",
    )(tile_expert, tile_valid, tile_block, xs, w_gate, w_up, w_down)


SC_WORKERS = 32


def _sc_mesh():
    return plsc.VectorSubcoreMesh(core_axis_name="core", subcore_axis_name="subcore")


def _sc_worker():
    return lax.axis_index("subcore") * 2 + lax.axis_index("core")


def _sc_scratch():
    return [pltpu.VMEM((SC_WINDOW,), jnp.int32), pltpu.VMEM((SC_WINDOW, D_PACK), jnp.uint32),
            pltpu.SemaphoreType.DMA]


def _dispatch(h2, pos):
    n = h2.shape[0]
    per_worker = n // SC_WORKERS
    scratch = _sc_scratch() + [pltpu.VMEM((SC_WINDOW,), jnp.int32), pltpu.SemaphoreType.DMA]

    @functools.partial(pl.kernel, out_type=jax.ShapeDtypeStruct((_moe_rows(n), D_PACK), jnp.uint32),
                       mesh=_sc_mesh(), scratch_types=scratch, name="moe_dispatch")
    def scatter(x_hbm, i_hbm, o_hbm, idx0_v, rows_v, sem0, idx1_v, sem1):
        base = _sc_worker() * per_worker

        @pl.loop(0, per_worker // SC_WINDOW)
        def _(j):
            t0 = base + j * SC_WINDOW
            pltpu.sync_copy(i_hbm.at[pl.ds(t0, SC_WINDOW)], idx0_v)
            pltpu.sync_copy(i_hbm.at[pl.ds(n + t0, SC_WINDOW)], idx1_v)
            pltpu.sync_copy(x_hbm.at[pl.ds(t0, SC_WINDOW)], rows_v)
            first = pltpu.async_copy(rows_v, o_hbm.at[idx0_v], sem0)
            second = pltpu.async_copy(rows_v, o_hbm.at[idx1_v], sem1)
            first.wait()
            second.wait()

    return scatter(h2, pos)


def _gather_pairs(y, pos):
    n = pos.shape[0] // 2
    per_worker = 2 * n // SC_WORKERS

    @functools.partial(pl.kernel, out_type=jax.ShapeDtypeStruct((2 * n, D_PACK), jnp.uint32),
                       mesh=_sc_mesh(), scratch_types=_sc_scratch(), name="moe_gather")
    def gather(y_hbm, i_hbm, o_hbm, idx_v, rows_v, sem):
        base = _sc_worker() * per_worker

        @pl.loop(0, per_worker // SC_WINDOW)
        def _(j):
            off = base + j * SC_WINDOW
            pltpu.sync_copy(i_hbm.at[pl.ds(off, SC_WINDOW)], idx_v)
            pltpu.async_copy(y_hbm.at[idx_v], rows_v, sem).wait()
            pltpu.sync_copy(rows_v, o_hbm.at[pl.ds(off, SC_WINDOW)])

    return gather(y, pos).reshape(2, n, D_PACK)


def _moe_post_norm(x_ref, yg_ref, w_ref, mods_ref, lng_ref, lnb_ref):
    g2 = mods_ref[0][5:6, :]
    y1 = jnp.concatenate(_unpack_halves(yg_ref[0]), axis=1)
    y2 = jnp.concatenate(_unpack_halves(yg_ref[1]), axis=1)
    w = w_ref[...]
    moe = w[:, 0:1] * y1 + w[:, 1:2] * y2
    return _post_norm(x_ref[...], g2 * moe, lng_ref[...], lnb_ref[...])


def _combine_kernel(x_ref, yg_ref, w_ref, mods_ref, lng_ref, lnb_ref, o_ref):
    o_ref[...] = _moe_post_norm(x_ref, yg_ref, w_ref, mods_ref, lng_ref, lnb_ref)


def _combine(x1, yg, wsel, mods, lng, lnb, t0, n):
    off = t0 // LN_TILE
    return pl.pallas_call(
        _combine_kernel,
        grid=(n // LN_TILE,),
        in_specs=[
            pl.BlockSpec((LN_TILE, D_MODEL), lambda t: (off + t, 0)),
            pl.BlockSpec((2, LN_TILE, D_PACK), lambda t: (0, off + t, 0)),
            pl.BlockSpec((LN_TILE, 2), lambda t: (off + t, 0)),
            _mods_spec(LN_TILE, off),
            pl.BlockSpec((1, D_MODEL), lambda t: (0, 0)),
            pl.BlockSpec((1, D_MODEL), lambda t: (0, 0)),
        ],
        out_specs=pl.BlockSpec((LN_TILE, D_MODEL), lambda t: (t, 0)),
        out_shape=jax.ShapeDtypeStruct((n, D_MODEL), F32),
        compiler_params=_params("arbitrary"),
        name="moe_combine",
    )(x1, yg, wsel, mods, lng, lnb)


def _moe_rows_of_tokens(h2p, logits, b_router, layer, w_gate, w_up, w_down):
    eidx, wsel = _route(logits, b_router)
    eidx, wsel = eidx.reshape(2, T_ALL), wsel.reshape(2, T_ALL).T
    pos, tile_expert, tile_valid, tile_block = _route_plan(eidx)
    xs = _dispatch(h2p, pos)
    y = _experts(xs, tile_expert, tile_valid, tile_block, layer, w_gate, w_up, w_down)
    return _gather_pairs(y, pos), wsel


def _qkv_kernel(x1_ref, yg_ref, wsel_ref, mods0_ref, lng_ref, lnb_ref, mods_ref, w_ref,
                x_ref, qkv_ref, nk_ref, nv_ref, w_s):
    t = pl.program_id(0)

    @pl.when(t == 0)
    def _():
        w_s[...] = w_ref[0].astype(BF16)

    m = mods_ref[0]
    sh1, sc1 = m[0:1, :], m[1:2, :]

    def norm(rows):
        x = _moe_post_norm(x1_ref.at[rows], yg_ref.at[:, rows], wsel_ref.at[rows], mods0_ref, lng_ref, lnb_ref)
        x_ref[rows, :] = x
        return (x * (1.0 + sc1) + sh1).astype(BF16)

    def project(rows, h):
        r = jnp.dot(h, w_s[...], preferred_element_type=F32)
        qkv_ref[rows, :D_MODEL] = (r[:, :D_MODEL] * (HEAD_DIM ** -0.5)).astype(BF16)
        qkv_ref[rows, D_MODEL:] = r[:, D_MODEL:].astype(BF16)
        return r

    is_ctx = t < T_CTX // QKV_TILE
    rows = slice(0, QKV_TILE)

    @pl.when(is_ctx)
    def _():
        r = project(rows, norm(rows))
        for out_ref, base in ((nk_ref, D_MODEL), (nv_ref, 2 * D_MODEL)):
            for b in range(QKV_TILE // SEQ):
                for p in range(HEAD_PAIRS):
                    pair = r[b * SEQ:(b + 1) * SEQ, base + p * LANES: base + (p + 1) * LANES].T
                    out_ref[b, 0, 2 * p] = pair[:HEAD_DIM]
                    out_ref[b, 0, 2 * p + 1] = pair[HEAD_DIM:]

    @pl.when(jnp.logical_not(is_ctx))
    def _():
        project(rows, norm(rows))


def _qkv(x1, yg, wsel, mods0, lng, lnb, mods1, w_qkv):
    seqs = QKV_TILE // SEQ
    row_spec = pl.BlockSpec((QKV_TILE, D_MODEL), lambda t: (t, 0))
    vec_spec = pl.BlockSpec((1, D_MODEL), lambda t: (0, 0))

    cache_spec = pl.BlockSpec((seqs, 1, N_HEADS, HEAD_DIM, SEQ),
                              lambda t: (jnp.minimum(t, T_CTX // QKV_TILE - 1), 0, 0, 0, 0))
    cache_shape = jax.ShapeDtypeStruct((BATCH, 1, N_HEADS, HEAD_DIM, SEQ), F32)
    return pl.pallas_call(
        _qkv_kernel,
        grid=(T_ALL // QKV_TILE,),
        in_specs=[
            row_spec,
            pl.BlockSpec((2, QKV_TILE, D_PACK), lambda t: (0, t, 0)),
            pl.BlockSpec((QKV_TILE, 2), lambda t: (t, 0)),
            _mods_spec(QKV_TILE),
            vec_spec,
            vec_spec,
            _mods_spec(QKV_TILE),
            pl.BlockSpec((1, D_MODEL, 3 * D_MODEL), lambda t: (0, 0, 0)),
        ],
        out_specs=[row_spec, pl.BlockSpec((QKV_TILE, 3 * D_MODEL), lambda t: (t, 0)), cache_spec, cache_spec],
        out_shape=[jax.ShapeDtypeStruct((T_ALL, D_MODEL), F32),
                   jax.ShapeDtypeStruct((T_ALL, 3 * D_MODEL), BF16), cache_shape, cache_shape],
        scratch_shapes=[pltpu.VMEM((D_MODEL, 3 * D_MODEL), BF16)],
        compiler_params=_params("arbitrary"),
        name="qkv",
    )(x1, yg, wsel, mods0, lng, lnb, mods1, w_qkv)


def _dot_nt(a, b):
    return lax.dot_general(a, b, (((1,), (1,)), ((), ())), preferred_element_type=F32)


def _head_masks():
    lane = lax.broadcasted_iota(jnp.int32, (1, LANES), 1)
    return lane < HEAD_DIM, lane >= HEAD_DIM


def _ctx_attn_kernel(q_ref, k_ref, v_ref, o_ref):
    left, right = _head_masks()
    for p in range(HEAD_PAIRS):
        cols = slice(p * LANES, (p + 1) * LANES)
        q2, k2, v2 = q_ref[:, cols], k_ref[:, cols], v_ref[:, cols]
        halves = []
        for mask in (left, right):
            qh = jnp.where(mask, q2, jnp.zeros_like(q2))
            s = _dot_nt(qh, k2)
            e = jnp.exp(s - jnp.max(s, axis=-1, keepdims=True))
            o2 = jnp.dot(e.astype(BF16), v2, preferred_element_type=F32)
            halves.append(o2 / jnp.sum(e, axis=-1, keepdims=True))
        o_ref[:, cols] = jnp.where(left, halves[0], halves[1]).astype(BF16)


def _ctx_attention(qkv):
    return pl.pallas_call(
        _ctx_attn_kernel,
        grid=(BATCH,),
        in_specs=[pl.BlockSpec((SEQ, D_MODEL), lambda b, j=j: (b, j)) for j in range(3)],
        out_specs=pl.BlockSpec((SEQ, D_MODEL), lambda b: (b, 0)),
        out_shape=jax.ShapeDtypeStruct((T_ALL, D_MODEL), BF16),
        compiler_params=_params("arbitrary"),
        name="ctx_attention",
    )(qkv, qkv, qkv)


_LAT_Q_BLOCK_ROWS = 4
_LAT_KEY_ROWS = ((0, 8), (0, 12), (4, 16), (8, 16))


def _softmax_rows(s_ref, p_ref, l_ref):
    sc = s_ref[...]
    e = jnp.exp(sc - jnp.max(sc, axis=-1, keepdims=True))
    l_ref[...] = jnp.sum(e, axis=-1, keepdims=True)
    p_ref[...] = e.astype(BF16)


def _lat_attn_kernel(q_ref, k_ref, v_ref, ck_ref, cv_ref, eb_ref, o_ctx_ref, o_ref, bias_s, s_scr, p_scr, l_scr):
    del o_ctx_ref
    left, right = _head_masks()

    @pl.when(pl.program_id(1) == 0)
    def _():
        neg = jnp.full((GRID_W, LANES), NEG_INF, F32)
        for hh in range(2):
            for r in range(GRID_ROWS):
                r0 = min(max(r - WIN_H // 2, 0), GRID_ROWS - WIN_H)
                for j in range(GRID_ROWS // 2):
                    parts = []
                    for kr in (2 * j, 2 * j + 1):
                        parts.append(eb_ref[hh, kr - r + WIN_H - 1] if r0 <= kr < r0 + WIN_H else None)
                    if parts[0] is None and parts[1] is None:
                        val = neg
                    else:
                        val = jnp.where(left, neg if parts[0] is None else parts[0],
                                        neg if parts[1] is None else parts[1])
                    bias_s[hh, r * GRID_W:(r + 1) * GRID_W, j * LANES:(j + 1) * LANES] = val

    ck = jnp.concatenate([ck_ref[0, 0, 0], ck_ref[0, 0, 1]], axis=0).astype(BF16)
    cv = jnp.concatenate([cv_ref[0, 0, 0], cv_ref[0, 0, 1]], axis=0).astype(BF16)
    qrows = _LAT_Q_BLOCK_ROWS * GRID_W
    units = [(qb, hh) for qb in range(len(_LAT_KEY_ROWS)) for hh in range(2)]

    def refs(u):
        nk = (_LAT_KEY_ROWS[units[u][0]][1] - _LAT_KEY_ROWS[units[u][0]][0]) * GRID_W
        width = nk + PAST_LEN
        return nk, s_scr.at[u % 2, :, :width], p_scr.at[u % 2, :, :width], l_scr.at[u % 2]

    def scores(u):
        qb, hh = units[u]
        kr0, kr1 = _LAT_KEY_ROWS[qb]
        qs, ks = slice(qb * qrows, (qb + 1) * qrows), slice(kr0 * GRID_W, kr1 * GRID_W)
        nk, s_ref, _, _ = refs(u)
        q2 = q_ref[qs, :]
        qh = jnp.where(right if hh else left, q2, jnp.zeros_like(q2))
        s_ref[:, :nk] = _dot_nt(qh, k_ref[ks, :]) + bias_s[hh, qs, ks]
        s_ref[:, nk:] = jnp.dot(qh, ck, preferred_element_type=F32)

    def weighted_values(u):
        qb, _ = units[u]
        kr0, kr1 = _LAT_KEY_ROWS[qb]
        nk, _, p_ref, l_ref = refs(u)
        o2 = (jnp.dot(p_ref[:, :nk], v_ref[kr0 * GRID_W:kr1 * GRID_W, :], preferred_element_type=F32)
              + _dot_nt(p_ref[:, nk:], cv))
        return o2 / l_ref[...]

    scores(0)
    halves = []
    for u in range(len(units)):
        if u + 1 < len(units):
            scores(u + 1)
        _, s_ref, p_ref, l_ref = refs(u)
        _softmax_rows(s_ref, p_ref, l_ref)
        halves.append(weighted_values(u))
        if len(halves) == 2:
            qb = units[u][0]
            o_ref[qb * qrows:(qb + 1) * qrows, :] = jnp.where(left, halves[0], halves[1]).astype(BF16)
            halves = []


def _lat_attention(qkv, cache_kt, cache_vt, ebias, o_buf):
    row0 = T_CTX // DEC_SEQ
    qrows = _LAT_Q_BLOCK_ROWS * GRID_W
    max_keys = max(k1 - k0 for k0, k1 in _LAT_KEY_ROWS) * GRID_W + PAST_LEN
    cache_spec = pl.BlockSpec((1, 1, 2, HEAD_DIM, PAST_LEN), lambda p, b: (b, 0, p, 0, 0))
    return pl.pallas_call(
        _lat_attn_kernel,
        grid=(HEAD_PAIRS, DEC_BATCH),
        in_specs=[
            pl.BlockSpec((DEC_SEQ, LANES), lambda p, b: (row0 + b, p)),
            pl.BlockSpec((DEC_SEQ, LANES), lambda p, b: (row0 + b, HEAD_PAIRS + p)),
            pl.BlockSpec((DEC_SEQ, LANES), lambda p, b: (row0 + b, 2 * HEAD_PAIRS + p)),
            cache_spec,
            cache_spec,
            pl.BlockSpec((2, 2 * WIN_H - 1, GRID_W, LANES), lambda p, b: (p, 0, 0, 0)),
            pl.BlockSpec(memory_space=pl.ANY),
        ],
        out_specs=pl.BlockSpec((DEC_SEQ, LANES), lambda p, b: (row0 + b, p)),
        out_shape=jax.ShapeDtypeStruct((T_ALL, D_MODEL), BF16),
        scratch_shapes=[pltpu.VMEM((2, DEC_SEQ, DEC_SEQ), F32),
                        pltpu.VMEM((2, qrows, max_keys), F32),
                        pltpu.VMEM((2, qrows, max_keys), BF16),
                        pltpu.VMEM((2, qrows, 1), F32)],
        input_output_aliases={6: 0},
        compiler_params=_params("arbitrary", "arbitrary"),
        name="lat_attention",
    )(qkv, qkv, qkv, cache_kt, cache_vt, ebias, o_buf)


def _expanded_bias(rel_bias):
    col = jnp.arange(GRID_W)
    col_start = jnp.clip(col - WIN_W // 2, 0, GRID_W - WIN_W)
    col_ok = (col[None, :] >= col_start[:, None]) & (col[None, :] < col_start[:, None] + WIN_W)
    dx = jnp.clip(col[None, :] - col[:, None] + (WIN_W - 1), 0, 2 * WIN_W - 2)
    onehot = (dx[None] == jnp.arange(2 * WIN_W - 1)[:, None, None]).astype(F32)
    eb = jnp.einsum("hyj,jqk->hyqk", rel_bias.astype(F32), onehot, precision=lax.Precision.HIGHEST)
    eb = jnp.where(col_ok[None, None], eb, NEG_INF)
    return jnp.concatenate([eb, eb], axis=-1)


def _wo_kernel(o_ref, x_ref, mods_ref, w_ref, lng_ref, lnb_ref, w2_ref, x1_ref, h2p_ref, logit_ref, w_s):
    @pl.when(pl.program_id(0) == 0)
    def _():
        w_s[...] = w_ref[0].astype(BF16)

    for r0 in range(0, WO_TILE, TILE):
        rs = slice(r0, r0 + TILE)
        mix = jnp.dot(o_ref[rs, :], w_s[...], preferred_element_type=F32)
        _epilogue(x_ref[rs, :], mix, mods_ref[0], lng_ref[...], lnb_ref[...], w2_ref,
                  x1_ref.at[rs, :], h2p_ref.at[rs, :], logit_ref.at[:, rs])


def _attn_out(o, x, mods, w_o, lng, lnb, w2):
    return pl.pallas_call(
        _wo_kernel,
        grid=(T_ALL // WO_TILE,),
        in_specs=[
            pl.BlockSpec((WO_TILE, D_MODEL), lambda t: (t, 0)),
            pl.BlockSpec((WO_TILE, D_MODEL), lambda t: (t, 0)),
            _mods_spec(WO_TILE),
            pl.BlockSpec((1, D_MODEL, D_MODEL), lambda t: (0, 0, 0)),
        ] + _EPI_IN_SPECS,
        out_specs=[
            pl.BlockSpec((WO_TILE, D_MODEL), lambda t: (t, 0)),
            pl.BlockSpec((WO_TILE, D_PACK), lambda t: (t, 0)),
            pl.BlockSpec((N_EXPERTS, WO_TILE), lambda t: (0, t)),
        ],
        out_shape=_EPI_OUT_SHAPE,
        scratch_shapes=[pltpu.VMEM((D_MODEL, D_MODEL), BF16)],
        compiler_params=_params("arbitrary"),
        name="attn_out",
    )(o, x, mods, w_o, lng, lnb, w2)


def kernel(x_prompt, x_sample, cache_k, cache_v, c, c_ctx, w_ada, b_ada, ln1_g, ln1_b, ln2_g, ln2_b,
           pool_w, pool_scale, w_qkv, w_o, rel_bias, w_router, b_router, w_gate, w_up, w_down):
    cond = jnp.zeros((N_COND, D_MODEL), F32).at[0].set(c_ctx).at[1:1 + DEC_BATCH].set(c)
    mods0, mods1 = _ada(cond, w_ada, b_ada, 0), _ada(cond, w_ada, b_ada, 1)
    wrt = w_router.T
    wrt_hi = wrt.astype(BF16)
    w2 = jnp.concatenate([wrt_hi, (wrt - wrt_hi.astype(F32)).astype(BF16)], axis=0)

    x, h2p, logits = _pool_layer(x_prompt.reshape(T_CTX, D_MODEL), x_sample.reshape(T_LAT, D_MODEL), mods0,
                                 pool_w[0].astype(BF16), pool_scale[0:1], ln1_g[0:1], ln1_b[0:1], w2)
    yg, wsel = _moe_rows_of_tokens(h2p, logits, b_router, 0, w_gate, w_up, w_down)

    x, qkv, new_kt, new_vt = _qkv(x, yg, wsel, mods0, ln2_g[0:1], ln2_b[0:1], mods1, w_qkv)
    o = _lat_attention(qkv, cache_k.transpose(0, 1, 2, 4, 3), cache_v.transpose(0, 1, 2, 4, 3),
                       _expanded_bias(rel_bias[0]), _ctx_attention(qkv))
    x, h2p, logits = _attn_out(o, x, mods1, w_o, ln1_g[1:2], ln1_b[1:2], w2)
    yg, wsel = _moe_rows_of_tokens(h2p, logits, b_router, 1, w_gate, w_up, w_down)
    y_ctx, y_lat = [_combine(x, yg, wsel, mods1, ln2_g[1:2], ln2_b[1:2], t0, n) for t0, n in SEGMENTS]
    return (y_ctx.reshape(BATCH, SEQ, D_MODEL), y_lat.reshape(DEC_BATCH, DEC_SEQ, D_MODEL),
            new_kt.transpose(0, 1, 2, 4, 3), new_vt.transpose(0, 1, 2, 4, 3))
```

```python
import functools

import jax
import jax.numpy as jnp
from jax import lax
from jax.experimental import pallas as pl
from jax.experimental.pallas import tpu as pltpu
from jax.experimental.pallas import tpu_sc as plsc

F32 = jnp.float32
BF16 = jnp.bfloat16

D_MODEL = 1024
BATCH = 16
SEQ = 256
DEC_BATCH = 8
DEC_SEQ = 1024
PAST_LEN = 512
GRID_W = 64
GRID_ROWS = DEC_SEQ // GRID_W
POOL_SIZES = (2, 4, 8, 16)
POOL_GROUP_DIM = D_MODEL // len(POOL_SIZES)
POOL_HALO = 8
N_HEADS = 16
HEAD_DIM = 64
WIN_H = 8
WIN_W = 16
N_EXPERTS = 16
EXPERTS_PER_GROUP = 4
N_EXPERT_GROUPS = 4
D_FF = 512
ALPHA = (2.0 * 2) ** 0.25
LN_EPS = 1e-5
NEG_INF = -1e30

T_CTX = BATCH * SEQ
T_LAT = DEC_BATCH * DEC_SEQ
T_ALL = T_CTX + T_LAT
N_COND = 16
TILE = 256
N_TILES = T_ALL // TILE
CTX_TILES = T_CTX // TILE
TILES_PER_LAT_SEQ = DEC_SEQ // TILE
MOE_TILE = 1024
MOE_SUB = 256
SEGMENTS = ((0, T_CTX), (T_CTX, T_LAT))
LN_TILE = 512
QKV_TILE = 512
WO_TILE = 512
SC_WINDOW = 64
D_PACK = D_MODEL // 2
ROUTE_ROWS = T_ALL // 128
HEAD_PAIRS = N_HEADS // 2
LANES = 128
VMEM_LIMIT = 56 * 1024 * 1024


def _cond_row(t, tile):
    ctx_tiles = T_CTX // tile
    per_seq = DEC_SEQ // tile
    return jnp.maximum(t - ctx_tiles + per_seq, 0) // per_seq


def _params(*sem):
    return pltpu.CompilerParams(dimension_semantics=sem, vmem_limit_bytes=VMEM_LIMIT)


def _pack_halves(v):
    half = v.shape[1] // 2
    hi = lax.bitcast_convert_type(v[:, :half].astype(F32), jnp.uint32)
    lo = lax.bitcast_convert_type(v[:, half:].astype(F32), jnp.uint32)
    return hi | (lo >> 16)


def _unpack_halves(p):
    left = lax.bitcast_convert_type(p & jnp.uint32(0xFFFF0000), F32)
    right = lax.bitcast_convert_type(p << 16, F32)
    return left, right


def _ada_kernel(cond_ref, w_ref, b_ref, o_ref):
    cnd = cond_ref[...]
    act = cnd * jax.nn.sigmoid(cnd)
    a_hi = act.astype(BF16)
    a_lo = (act - a_hi.astype(F32)).astype(BF16)
    w = w_ref[0]
    w_hi = w.astype(BF16)
    w_lo = (w - w_hi.astype(F32)).astype(BF16)
    a2 = jnp.concatenate([a_hi, a_lo], axis=0)
    p = jnp.dot(a2, w_hi, preferred_element_type=F32)
    q = jnp.dot(a_hi, w_lo, preferred_element_type=F32)
    o_ref[0] = p[:N_COND] + p[N_COND:] + q + b_ref[0]


def _ada(cond, w_ada, b_ada, layer):
    depth, d, n = w_ada.shape
    bn = 1536
    mods = pl.pallas_call(
        _ada_kernel,
        grid=(n // bn,),
        in_specs=[
            pl.BlockSpec((N_COND, d), lambda j: (0, 0)),
            pl.BlockSpec((1, d, bn), lambda j: (layer, 0, j)),
            pl.BlockSpec((1, 1, bn), lambda j: (layer, 0, j)),
        ],
        out_specs=pl.BlockSpec((1, N_COND, bn), lambda j: (0, 0, j)),
        out_shape=jax.ShapeDtypeStruct((1, N_COND, n), F32),
        compiler_params=_params("arbitrary"),
        name="ada",
    )(cond, w_ada, b_ada.reshape(depth, 1, n))
    return mods.reshape(N_COND, 6, D_MODEL)


def _post_norm(x, upd, g, b):
    y = ALPHA * x + upd
    mu = jnp.mean(y, axis=-1, keepdims=True)
    yc = y - mu
    var = jnp.mean(yc * yc, axis=-1, keepdims=True)
    return yc * lax.rsqrt(var + LN_EPS) * g + b


def _router_logits(h2, h_hi, w2_ref):
    h_lo = (h2 - h_hi.astype(F32)).astype(BF16)
    w2 = w2_ref[...]
    p = _dot_nt(w2, h_hi)
    q = _dot_nt(w2[:N_EXPERTS], h_lo)
    return p[:N_EXPERTS] + p[N_EXPERTS:] + q


def _epilogue(x, mix, m, lng, lnb, w2_ref, x1_ref, h2p_ref, logit_ref):
    g1, sh2, sc2 = m[2:3, :], m[3:4, :], m[4:5, :]
    x1 = _post_norm(x, g1 * mix, lng, lnb)
    h2 = x1 * (1.0 + sc2) + sh2
    h_hi = h2.astype(BF16)
    x1_ref[...] = x1
    h2p_ref[...] = _pack_halves(h_hi)
    logit_ref[...] = _router_logits(h2, h_hi, w2_ref)


_EPI_IN_SPECS = [
    pl.BlockSpec((1, D_MODEL), lambda t: (0, 0)),
    pl.BlockSpec((1, D_MODEL), lambda t: (0, 0)),
    pl.BlockSpec((2 * N_EXPERTS, D_MODEL), lambda t: (0, 0)),
]
_EPI_OUT_SPECS = [
    pl.BlockSpec((TILE, D_MODEL), lambda t: (t, 0)),
    pl.BlockSpec((TILE, D_PACK), lambda t: (t, 0)),
    pl.BlockSpec((N_EXPERTS, TILE), lambda t: (0, t)),
]
_EPI_OUT_SHAPE = [
    jax.ShapeDtypeStruct((T_ALL, D_MODEL), F32),
    jax.ShapeDtypeStruct((T_ALL, D_PACK), jnp.uint32),
    jax.ShapeDtypeStruct((N_EXPERTS, T_ALL), F32),
]


def _route_kernel(logit_ref, br_ref, e_ref, w_ref):
    aff = jax.nn.sigmoid(logit_ref[...])
    sel = aff + br_ref[...]
    sel_rows = [sel[e] for e in range(N_EXPERTS)]
    aff_rows = [aff[e] for e in range(N_EXPERTS)]

    def group_score(g):
        r = sel_rows[g * EXPERTS_PER_GROUP:(g + 1) * EXPERTS_PER_GROUP]
        best = None
        for i in range(EXPERTS_PER_GROUP):
            for j in range(i + 1, EXPERTS_PER_GROUP):
                pair = r[i] + r[j]
                best = pair if best is None else jnp.maximum(best, pair)
        return best

    best = group_score(0)
    gidx = jnp.zeros_like(best, dtype=jnp.int32)
    for g in range(1, N_EXPERT_GROUPS):
        sc = group_score(g)
        better = sc > best
        gidx = jnp.where(better, g, gidx)
        best = jnp.where(better, sc, best)

    def pick_group(rows, j):
        out = rows[j]
        for g in range(1, N_EXPERT_GROUPS):
            out = jnp.where(gidx == g, rows[g * EXPERTS_PER_GROUP + j], out)
        return out

    cand = [pick_group(sel_rows, j) for j in range(EXPERTS_PER_GROUP)]
    cand_aff = [pick_group(aff_rows, j) for j in range(EXPERTS_PER_GROUP)]

    def argmax_first(vals):
        bv, bi, ba = vals[0], jnp.zeros_like(gidx), cand_aff[0]
        for j in range(1, EXPERTS_PER_GROUP):
            better = vals[j] > bv
            bv = jnp.where(better, vals[j], bv)
            bi = jnp.where(better, j, bi)
            ba = jnp.where(better, cand_aff[j], ba)
        return bi, ba

    i1, a1 = argmax_first(cand)
    rest = [jnp.where(i1 == j, -jnp.inf, cand[j]) for j in range(EXPERTS_PER_GROUP)]
    i2, a2 = argmax_first(rest)
    denom = a1 + a2
    base = gidx * EXPERTS_PER_GROUP
    e_ref[0] = base + i1
    e_ref[1] = base + i2
    w_ref[0] = a1 / denom
    w_ref[1] = a2 / denom


def _route(logits, b_router):
    return pl.pallas_call(
        _route_kernel,
        out_shape=[jax.ShapeDtypeStruct((2, ROUTE_ROWS, 128), jnp.int32),
                   jax.ShapeDtypeStruct((2, ROUTE_ROWS, 128), F32)],
        compiler_params=pltpu.CompilerParams(vmem_limit_bytes=VMEM_LIMIT),
        name="route",
    )(logits.reshape(N_EXPERTS, ROUTE_ROWS, 128), b_router.reshape(N_EXPERTS, 1, 1))


def _mods_spec(tile, first_tile=0):
    return pl.BlockSpec((1, 6, D_MODEL), lambda t: (_cond_row(first_tile + t, tile), 0, 0))


def _pool_kernel(xa_ref, xb_ref, xp_ref, xn_ref, mods_ref, pw_ref, ps_ref, lng_ref, lnb_ref, w2_ref,
                 x1_ref, h2p_ref, logit_ref):
    t = pl.program_id(0)
    m = mods_ref[0]
    sh1, sc1 = m[0:1, :], m[1:2, :]
    in_lat = t >= CTX_TILES
    sub = (t - CTX_TILES) % TILES_PER_LAT_SEQ
    is_first = jnp.logical_or(jnp.logical_not(in_lat), sub == 0)
    is_last = jnp.logical_or(jnp.logical_not(in_lat), sub == TILES_PER_LAT_SEQ - 1)
    seq_len = jnp.where(in_lat, DEC_SEQ, SEQ)
    pos0 = jnp.where(in_lat, sub * TILE, 0)

    x = jnp.where(in_lat, xb_ref[...], xa_ref[...])
    h = x * (1.0 + sc1) + sh1
    hp = jnp.where(is_first, 0.0, xp_ref[...] * (1.0 + sc1) + sh1)
    hn = jnp.where(is_last, 0.0, xn_ref[...] * (1.0 + sc1) + sh1)
    hext = jnp.concatenate([hp, h, hn], axis=0)
    ext = TILE + 2 * POOL_HALO
    pos = pos0 + lax.broadcasted_iota(jnp.int32, (TILE, 1), 0)

    outs = []
    for g, w in enumerate(POOL_SIZES):
        lo_c, hi_c = g * POOL_GROUP_DIM, (g + 1) * POOL_GROUP_DIM
        a = hext[:, lo_c:hi_c]
        k = 1
        while k < w:
            a = a + pltpu.roll(a, ext - k, axis=0)
            k *= 2
        off = POOL_HALO - w // 2
        win = pltpu.roll(a, ext - off, axis=0)[:TILE] if off else a[:TILE]
        lo = jnp.maximum(pos - w // 2, 0)
        hi = jnp.minimum(pos - w // 2 + w, seq_len)
        cnt = (hi - lo).astype(F32)
        pooled = win / cnt - h[:, lo_c:hi_c]
        outs.append(jnp.dot(pooled.astype(BF16), pw_ref[g], preferred_element_type=F32))
    mix = jnp.concatenate(outs, axis=1) * ps_ref[...]
    _epilogue(x, mix, m, lng_ref[...], lnb_ref[...], w2_ref, x1_ref, h2p_ref, logit_ref)


def _pool_layer(x_ctx, x_lat, mods, pool_w, pool_scale, lng, lnb, w2):
    halo_blocks = TILE // POOL_HALO
    last_halo = T_LAT // POOL_HALO - 1
    return pl.pallas_call(
        _pool_kernel,
        grid=(N_TILES,),
        in_specs=[
            pl.BlockSpec((TILE, D_MODEL), lambda t: (jnp.minimum(t, CTX_TILES - 1), 0)),
            pl.BlockSpec((TILE, D_MODEL), lambda t: (jnp.maximum(t - CTX_TILES, 0), 0)),
            pl.BlockSpec((POOL_HALO, D_MODEL), lambda t: (jnp.maximum((t - CTX_TILES) * halo_blocks - 1, 0), 0)),
            pl.BlockSpec((POOL_HALO, D_MODEL),
                         lambda t: (jnp.clip((t - CTX_TILES + 1) * halo_blocks, 0, last_halo), 0)),
            _mods_spec(TILE),
            pl.BlockSpec((len(POOL_SIZES), POOL_GROUP_DIM, POOL_GROUP_DIM), lambda t: (0, 0, 0)),
            pl.BlockSpec((1, D_MODEL), lambda t: (0, 0)),
        ] + _EPI_IN_SPECS,
        out_specs=_EPI_OUT_SPECS,
        out_shape=_EPI_OUT_SHAPE,
        compiler_params=_params("arbitrary"),
        name="pool_mixer",
    )(x_ctx, x_lat, x_lat, x_lat, mods, pool_w, pool_scale, lng, lnb, w2)


def _moe_rows(n):
    return 2 * n + N_EXPERTS * MOE_TILE


def _route_plan(eidx):
    n = eidx.shape[1]
    tiles = _moe_rows(n) // MOE_TILE
    eidx = eidx.reshape(2 * n)
    experts = jnp.arange(N_EXPERTS, dtype=jnp.int32)
    onehot = (eidx[:, None] == experts[None, :]).astype(jnp.int32)
    counts = jnp.sum(onehot, axis=0)
    rank = jnp.sum((jnp.cumsum(onehot, axis=0) - onehot) * onehot, axis=1)
    padded = (counts + MOE_TILE - 1) // MOE_TILE * MOE_TILE
    ends = jnp.cumsum(padded)
    starts = ends - padded
    pos = jnp.sum(onehot * starts[None, :], axis=1) + rank
    tile_start = jnp.arange(tiles, dtype=jnp.int32)[:, None] * MOE_TILE
    in_region = ((tile_start >= starts[None, :]) & (tile_start < ends[None, :])).astype(jnp.int32)
    used = jnp.sum(in_region, axis=1) > 0
    last_expert = jnp.max(jnp.where(counts > 0, experts, 0))
    tile_expert = jnp.where(used, jnp.sum(in_region * experts[None, :], axis=1), last_expert)
    first_block = starts // MOE_TILE
    block_in_expert = padded[None, :] // MOE_TILE - 1 - (tile_start - starts[None, :]) // MOE_TILE
    tile_valid = jnp.sum(in_region * jnp.clip(counts[None, :] - block_in_expert * MOE_TILE, 0, MOE_TILE), axis=1)
    tile_block = jnp.where(used, jnp.sum(in_region * (first_block[None, :] + block_in_expert), axis=1),
                           jnp.sum(jnp.where(experts == last_expert, first_block, 0)))
    return pos.astype(jnp.int32), tile_expert.astype(jnp.int32), tile_valid.astype(jnp.int32), \
        tile_block.astype(jnp.int32)


def _expert_kernel(te_ref, nv_ref, tb_ref, x_ref, wg_ref, wu_ref, wd_ref, y_ref, wgu_s, wd_s):
    i = pl.program_id(0)
    prev = te_ref[jnp.maximum(i - 1, 0)]
    changed = jnp.logical_or(i == 0, te_ref[i] != prev)

    @pl.when(changed)
    def _():
        wgu_s[:, :D_FF] = wg_ref[0, 0].astype(BF16)
        wgu_s[:, D_FF:] = wu_ref[0, 0].astype(BF16)
        wd_s[...] = wd_ref[0, 0].astype(BF16)

    nv = nv_ref[i]

    def gate_up(r0):
        rows = r0 + lax.broadcasted_iota(jnp.int32, (MOE_SUB, 1), 0)
        xp = jnp.where(rows < nv, x_ref[r0:r0 + MOE_SUB, :], jnp.uint32(0))
        left, right = _unpack_halves(xp)
        xb = jnp.concatenate([left.astype(BF16), right.astype(BF16)], axis=1)
        return jnp.dot(xb, wgu_s[...], preferred_element_type=F32)

    def down(r0, gu):
        gate, up = gu[:, :D_FF], gu[:, D_FF:]
        he = (gate * jax.nn.sigmoid(gate) * up).astype(BF16)
        y_ref[r0:r0 + MOE_SUB, :] = _pack_halves(jnp.dot(he, wd_s[...], preferred_element_type=F32).astype(BF16))

    def run(n_sub):
        gu = gate_up(0)
        for j in range(n_sub):
            nxt = gate_up((j + 1) * MOE_SUB) if j + 1 < n_sub else None
            down(j * MOE_SUB, gu)
            gu = nxt

    n_subs = MOE_TILE // MOE_SUB
    for n_sub in range(1, n_subs + 1):
        lo = (n_sub - 1) * MOE_SUB
        in_range = nv > lo if n_sub == n_subs else jnp.logical_and(nv > lo, nv <= lo + MOE_SUB)
        pl.when(in_range)(functools.partial(run, n_sub))


def _experts(xs, tile_expert, tile_valid, tile_block, layer, w_gate, w_up, w_down):
    grid_spec = pltpu.PrefetchScalarGridSpec(
        num_scalar_prefetch=3,
        grid=(xs.shape[0] // MOE_TILE,),
        in_specs=[
            pl.BlockSpec((MOE_TILE, D_PACK), lambda i, te, nv, tb: (tb[i], 0)),
            pl.BlockSpec((1, 1, D_MODEL, D_FF), lambda i, te, nv, tb: (layer, te[i], 0, 0)),
            pl.BlockSpec((1, 1, D_MODEL, D_FF), lambda i, te, nv, tb: (layer, te[i], 0, 0)),
            pl.BlockSpec((1, 1, D_FF, D_MODEL), lambda i, te, nv, tb: (layer, te[i], 0, 0)),
        ],
        out_specs=pl.BlockSpec((MOE_TILE, D_PACK), lambda i, te, nv, tb: (tb[i], 0)),
        scratch_shapes=[pltpu.VMEM((D_MODEL, 2 * D_FF), BF16), pltpu.VMEM((D_FF, D_MODEL), BF16)],
    )
    return pl.pallas_call(
        _expert_kernel,
        grid_spec=grid_spec,
        out_shape=jax.ShapeDtypeStruct(xs.shape, jnp.uint32),
        compiler_params=_params("arbitrary"),
        name="experts",
    )(tile_expert, tile_valid, tile_block, xs, w_gate, w_up, w_down)


SC_WORKERS = 32


def _sc_mesh():
    return plsc.VectorSubcoreMesh(core_axis_name="core", subcore_axis_name="subcore")


def _sc_worker():
    return lax.axis_index("subcore") * 2 + lax.axis_index("core")


def _dispatch(h2, pos):
    n = h2.shape[0]
    per_worker = n // SC_WORKERS
    pairs = per_worker // (2 * SC_WINDOW)
    one_set = [pltpu.VMEM((SC_WINDOW,), jnp.int32), pltpu.VMEM((SC_WINDOW,), jnp.int32),
               pltpu.VMEM((SC_WINDOW, D_PACK), jnp.uint32),
               pltpu.SemaphoreType.DMA, pltpu.SemaphoreType.DMA, pltpu.SemaphoreType.DMA]

    @functools.partial(pl.kernel, out_type=jax.ShapeDtypeStruct((_moe_rows(n), D_PACK), jnp.uint32),
                       mesh=_sc_mesh(), scratch_types=one_set + one_set, name="moe_dispatch")
    def scatter(x_hbm, i_hbm, o_hbm, *bufs):
        base = _sc_worker() * per_worker
        sets = (bufs[:6], bufs[6:])

        def rows_in(j, s):
            _, _, rows, sem, _, _ = sets[s]
            return pltpu.make_async_copy(x_hbm.at[pl.ds(base + j * SC_WINDOW, SC_WINDOW)], rows, sem)

        def rows_out(k, s):
            idx, rows, sem = sets[s][k], sets[s][2], sets[s][4 + k]
            return pltpu.make_async_copy(rows, o_hbm.at[idx], sem)

        def fetch(j, s):
            t0 = base + j * SC_WINDOW
            pltpu.sync_copy(i_hbm.at[pl.ds(t0, SC_WINDOW)], sets[s][0])
            pltpu.sync_copy(i_hbm.at[pl.ds(n + t0, SC_WINDOW)], sets[s][1])
            rows_in(j, s).start()

        fetch(0, 0)

        @pl.loop(0, pairs)
        def _(g):
            j0, j1 = 2 * g, 2 * g + 1
            fetch(j1, 1)
            rows_in(j0, 0).wait()
            rows_out(0, 0).start()
            rows_out(1, 0).start()
            rows_in(j1, 1).wait()
            rows_out(0, 1).start()
            rows_out(1, 1).start()
            rows_out(0, 0).wait()
            rows_out(1, 0).wait()

            @pl.when(g + 1 < pairs)
            def _():
                fetch(j0 + 2, 0)

            rows_out(0, 1).wait()
            rows_out(1, 1).wait()

    return scatter(h2, pos)


def _gather_pairs(y, pos):
    n = pos.shape[0] // 2
    per_worker = 2 * n // SC_WORKERS
    pairs = per_worker // (2 * SC_WINDOW)
    scratch = [pltpu.VMEM((per_worker,), jnp.int32),
               pltpu.VMEM((SC_WINDOW, D_PACK), jnp.uint32), pltpu.VMEM((SC_WINDOW, D_PACK), jnp.uint32),
               pltpu.SemaphoreType.DMA, pltpu.SemaphoreType.DMA, pltpu.SemaphoreType.DMA, pltpu.SemaphoreType.DMA]

    @functools.partial(pl.kernel, out_type=jax.ShapeDtypeStruct((2 * n, D_PACK), jnp.uint32),
                       mesh=_sc_mesh(), scratch_types=scratch, name="moe_gather")
    def gather(y_hbm, i_hbm, o_hbm, idx_v, rows0, rows1, read0, read1, write0, write1):
        base = _sc_worker() * per_worker
        pltpu.sync_copy(i_hbm.at[pl.ds(base, per_worker)], idx_v)

        def read(j, rows, sem):
            return pltpu.make_async_copy(y_hbm.at[idx_v.at[pl.ds(j * SC_WINDOW, SC_WINDOW)]], rows, sem)

        def write(j, rows, sem):
            return pltpu.make_async_copy(rows, o_hbm.at[pl.ds(base + j * SC_WINDOW, SC_WINDOW)], sem)

        read(0, rows0, read0).start()

        @pl.loop(0, pairs)
        def _(g):
            j0, j1 = 2 * g, 2 * g + 1
            read(j1, rows1, read1).start()
            read(j0, rows0, read0).wait()
            write(j0, rows0, write0).start()
            read(j1, rows1, read1).wait()
            write(j1, rows1, write1).start()
            write(j0, rows0, write0).wait()

            @pl.when(g + 1 < pairs)
            def _():
                read(j0 + 2, rows0, read0).start()

            write(j1, rows1, write1).wait()

    return gather(y, pos).reshape(2, n, D_PACK)


def _moe_post_norm(x_ref, yg_ref, w_ref, mods_ref, lng_ref, lnb_ref):
    g2 = mods_ref[0][5:6, :]
    y1 = jnp.concatenate(_unpack_halves(yg_ref[0]), axis=1)
    y2 = jnp.concatenate(_unpack_halves(yg_ref[1]), axis=1)
    w = w_ref[...]
    moe = w[:, 0:1] * y1 + w[:, 1:2] * y2
    return _post_norm(x_ref[...], g2 * moe, lng_ref[...], lnb_ref[...])


def _combine_kernel(x_ref, yg_ref, w_ref, mods_ref, lng_ref, lnb_ref, o_ref):
    o_ref[...] = _moe_post_norm(x_ref, yg_ref, w_ref, mods_ref, lng_ref, lnb_ref)


def _combine(x1, yg, wsel, mods, lng, lnb, t0, n):
    off = t0 // LN_TILE
    return pl.pallas_call(
        _combine_kernel,
        grid=(n // LN_TILE,),
        in_specs=[
            pl.BlockSpec((LN_TILE, D_MODEL), lambda t: (off + t, 0)),
            pl.BlockSpec((2, LN_TILE, D_PACK), lambda t: (0, off + t, 0)),
            pl.BlockSpec((LN_TILE, 2), lambda t: (off + t, 0)),
            _mods_spec(LN_TILE, off),
            pl.BlockSpec((1, D_MODEL), lambda t: (0, 0)),
            pl.BlockSpec((1, D_MODEL), lambda t: (0, 0)),
        ],
        out_specs=pl.BlockSpec((LN_TILE, D_MODEL), lambda t: (t, 0)),
        out_shape=jax.ShapeDtypeStruct((n, D_MODEL), F32),
        compiler_params=_params("arbitrary"),
        name="moe_combine",
    )(x1, yg, wsel, mods, lng, lnb)


def _moe_rows_of_tokens(h2p, logits, b_router, layer, w_gate, w_up, w_down):
    eidx, wsel = _route(logits, b_router)
    eidx, wsel = eidx.reshape(2, T_ALL), wsel.reshape(2, T_ALL).T
    pos, tile_expert, tile_valid, tile_block = _route_plan(eidx)
    xs = _dispatch(h2p, pos)
    y = _experts(xs, tile_expert, tile_valid, tile_block, layer, w_gate, w_up, w_down)
    return _gather_pairs(y, pos), wsel


def _qkv_kernel(x1_ref, yg_ref, wsel_ref, mods0_ref, lng_ref, lnb_ref, mods_ref, w_ref,
                x_ref, qkv_ref, nk_ref, nv_ref, w_s):
    t = pl.program_id(0)

    @pl.when(t == 0)
    def _():
        w_s[...] = w_ref[0].astype(BF16)

    m = mods_ref[0]
    sh1, sc1 = m[0:1, :], m[1:2, :]

    def norm(rows):
        x = _moe_post_norm(x1_ref.at[rows], yg_ref.at[:, rows], wsel_ref.at[rows], mods0_ref, lng_ref, lnb_ref)
        x_ref[rows, :] = x
        return (x * (1.0 + sc1) + sh1).astype(BF16)

    def project(rows, h):
        r = jnp.dot(h, w_s[...], preferred_element_type=F32)
        qkv_ref[rows, :D_MODEL] = (r[:, :D_MODEL] * (HEAD_DIM ** -0.5)).astype(BF16)
        qkv_ref[rows, D_MODEL:] = r[:, D_MODEL:].astype(BF16)
        return r

    is_ctx = t < T_CTX // QKV_TILE
    rows = slice(0, QKV_TILE)

    @pl.when(is_ctx)
    def _():
        r = project(rows, norm(rows))
        for out_ref, base in ((nk_ref, D_MODEL), (nv_ref, 2 * D_MODEL)):
            for b in range(QKV_TILE // SEQ):
                for p in range(HEAD_PAIRS):
                    pair = r[b * SEQ:(b + 1) * SEQ, base + p * LANES: base + (p + 1) * LANES].T
                    out_ref[b, 0, 2 * p] = pair[:HEAD_DIM]
                    out_ref[b, 0, 2 * p + 1] = pair[HEAD_DIM:]

    @pl.when(jnp.logical_not(is_ctx))
    def _():
        project(rows, norm(rows))


def _qkv(x1, yg, wsel, mods0, lng, lnb, mods1, w_qkv):
    seqs = QKV_TILE // SEQ
    row_spec = pl.BlockSpec((QKV_TILE, D_MODEL), lambda t: (t, 0))
    vec_spec = pl.BlockSpec((1, D_MODEL), lambda t: (0, 0))

    cache_spec = pl.BlockSpec((seqs, 1, N_HEADS, HEAD_DIM, SEQ),
                              lambda t: (jnp.minimum(t, T_CTX // QKV_TILE - 1), 0, 0, 0, 0))
    cache_shape = jax.ShapeDtypeStruct((BATCH, 1, N_HEADS, HEAD_DIM, SEQ), F32)
    return pl.pallas_call(
        _qkv_kernel,
        grid=(T_ALL // QKV_TILE,),
        in_specs=[
            row_spec,
            pl.BlockSpec((2, QKV_TILE, D_PACK), lambda t: (0, t, 0)),
            pl.BlockSpec((QKV_TILE, 2), lambda t: (t, 0)),
            _mods_spec(QKV_TILE),
            vec_spec,
            vec_spec,
            _mods_spec(QKV_TILE),
            pl.BlockSpec((1, D_MODEL, 3 * D_MODEL), lambda t: (0, 0, 0)),
        ],
        out_specs=[row_spec, pl.BlockSpec((QKV_TILE, 3 * D_MODEL), lambda t: (t, 0)), cache_spec, cache_spec],
        out_shape=[jax.ShapeDtypeStruct((T_ALL, D_MODEL), F32),
                   jax.ShapeDtypeStruct((T_ALL, 3 * D_MODEL), BF16), cache_shape, cache_shape],
        scratch_shapes=[pltpu.VMEM((D_MODEL, 3 * D_MODEL), BF16)],
        compiler_params=_params("arbitrary"),
        name="qkv",
    )(x1, yg, wsel, mods0, lng, lnb, mods1, w_qkv)


def _dot_nt(a, b):
    return lax.dot_general(a, b, (((1,), (1,)), ((), ())), preferred_element_type=F32)


def _head_masks():
    lane = lax.broadcasted_iota(jnp.int32, (1, LANES), 1)
    return lane < HEAD_DIM, lane >= HEAD_DIM


def _ctx_attn_kernel(q_ref, k_ref, v_ref, o_ref):
    left, right = _head_masks()
    for p in range(HEAD_PAIRS):
        cols = slice(p * LANES, (p + 1) * LANES)
        q2, k2, v2 = q_ref[:, cols], k_ref[:, cols], v_ref[:, cols]
        halves = []
        for mask in (left, right):
            qh = jnp.where(mask, q2, jnp.zeros_like(q2))
            s = _dot_nt(qh, k2)
            e = jnp.exp(s - jnp.max(s, axis=-1, keepdims=True))
            o2 = jnp.dot(e.astype(BF16), v2, preferred_element_type=F32)
            halves.append(o2 / jnp.sum(e, axis=-1, keepdims=True))
        o_ref[:, cols] = jnp.where(left, halves[0], halves[1]).astype(BF16)


def _ctx_attention(qkv):
    return pl.pallas_call(
        _ctx_attn_kernel,
        grid=(BATCH,),
        in_specs=[pl.BlockSpec((SEQ, D_MODEL), lambda b, j=j: (b, j)) for j in range(3)],
        out_specs=pl.BlockSpec((SEQ, D_MODEL), lambda b: (b, 0)),
        out_shape=jax.ShapeDtypeStruct((T_ALL, D_MODEL), BF16),
        compiler_params=_params("arbitrary"),
        name="ctx_attention",
    )(qkv, qkv, qkv)


_LAT_Q_BLOCK_ROWS = 4
_LAT_KEY_ROWS = ((0, 8), (0, 12), (4, 16), (8, 16))


def _softmax_rows(s_ref, p_ref, l_ref):
    sc = s_ref[...]
    e = jnp.exp(sc - jnp.max(sc, axis=-1, keepdims=True))
    l_ref[...] = jnp.sum(e, axis=-1, keepdims=True)
    p_ref[...] = e.astype(BF16)


def _lat_attn_kernel(q_ref, k_ref, v_ref, ck_ref, cv_ref, eb_ref, o_ctx_ref, o_ref, bias_s, s_scr, p_scr, l_scr):
    del o_ctx_ref
    left, right = _head_masks()

    @pl.when(pl.program_id(1) == 0)
    def _():
        neg = jnp.full((GRID_W, LANES), NEG_INF, F32)
        for hh in range(2):
            for r in range(GRID_ROWS):
                r0 = min(max(r - WIN_H // 2, 0), GRID_ROWS - WIN_H)
                for j in range(GRID_ROWS // 2):
                    parts = []
                    for kr in (2 * j, 2 * j + 1):
                        parts.append(eb_ref[hh, kr - r + WIN_H - 1] if r0 <= kr < r0 + WIN_H else None)
                    if parts[0] is None and parts[1] is None:
                        val = neg
                    else:
                        val = jnp.where(left, neg if parts[0] is None else parts[0],
                                        neg if parts[1] is None else parts[1])
                    bias_s[hh, r * GRID_W:(r + 1) * GRID_W, j * LANES:(j + 1) * LANES] = val

    ck = jnp.concatenate([ck_ref[0, 0, 0], ck_ref[0, 0, 1]], axis=0).astype(BF16)
    cv = jnp.concatenate([cv_ref[0, 0, 0], cv_ref[0, 0, 1]], axis=0).astype(BF16)
    qrows = _LAT_Q_BLOCK_ROWS * GRID_W
    units = [(qb, hh) for qb in range(len(_LAT_KEY_ROWS)) for hh in range(2)]

    def refs(u):
        nk = (_LAT_KEY_ROWS[units[u][0]][1] - _LAT_KEY_ROWS[units[u][0]][0]) * GRID_W
        width = nk + PAST_LEN
        return nk, s_scr.at[u % 2, :, :width], p_scr.at[u % 2, :, :width], l_scr.at[u % 2]

    def scores(u):
        qb, hh = units[u]
        kr0, kr1 = _LAT_KEY_ROWS[qb]
        qs, ks = slice(qb * qrows, (qb + 1) * qrows), slice(kr0 * GRID_W, kr1 * GRID_W)
        nk, s_ref, _, _ = refs(u)
        q2 = q_ref[qs, :]
        qh = jnp.where(right if hh else left, q2, jnp.zeros_like(q2))
        s_ref[:, :nk] = _dot_nt(qh, k_ref[ks, :]) + bias_s[hh, qs, ks]
        s_ref[:, nk:] = jnp.dot(qh, ck, preferred_element_type=F32)

    def weighted_values(u):
        qb, _ = units[u]
        kr0, kr1 = _LAT_KEY_ROWS[qb]
        nk, _, p_ref, l_ref = refs(u)
        o2 = (jnp.dot(p_ref[:, :nk], v_ref[kr0 * GRID_W:kr1 * GRID_W, :], preferred_element_type=F32)
              + _dot_nt(p_ref[:, nk:], cv))
        return o2 / l_ref[...]

    scores(0)
    halves = []
    for u in range(len(units)):
        if u + 1 < len(units):
            scores(u + 1)
        _, s_ref, p_ref, l_ref = refs(u)
        _softmax_rows(s_ref, p_ref, l_ref)
        halves.append(weighted_values(u))
        if len(halves) == 2:
            qb = units[u][0]
            o_ref[qb * qrows:(qb + 1) * qrows, :] = jnp.where(left, halves[0], halves[1]).astype(BF16)
            halves = []


def _lat_attention(qkv, cache_kt, cache_vt, ebias, o_buf):
    row0 = T_CTX // DEC_SEQ
    qrows = _LAT_Q_BLOCK_ROWS * GRID_W
    max_keys = max(k1 - k0 for k0, k1 in _LAT_KEY_ROWS) * GRID_W + PAST_LEN
    cache_spec = pl.BlockSpec((1, 1, 2, HEAD_DIM, PAST_LEN), lambda p, b: (b, 0, p, 0, 0))
    return pl.pallas_call(
        _lat_attn_kernel,
        grid=(HEAD_PAIRS, DEC_BATCH),
        in_specs=[
            pl.BlockSpec((DEC_SEQ, LANES), lambda p, b: (row0 + b, p)),
            pl.BlockSpec((DEC_SEQ, LANES), lambda p, b: (row0 + b, HEAD_PAIRS + p)),
            pl.BlockSpec((DEC_SEQ, LANES), lambda p, b: (row0 + b, 2 * HEAD_PAIRS + p)),
            cache_spec,
            cache_spec,
            pl.BlockSpec((2, 2 * WIN_H - 1, GRID_W, LANES), lambda p, b: (p, 0, 0, 0)),
            pl.BlockSpec(memory_space=pl.ANY),
        ],
        out_specs=pl.BlockSpec((DEC_SEQ, LANES), lambda p, b: (row0 + b, p)),
        out_shape=jax.ShapeDtypeStruct((T_ALL, D_MODEL), BF16),
        scratch_shapes=[pltpu.VMEM((2, DEC_SEQ, DEC_SEQ), F32),
                        pltpu.VMEM((2, qrows, max_keys), F32),
                        pltpu.VMEM((2, qrows, max_keys), BF16),
                        pltpu.VMEM((2, qrows, 1), F32)],
        input_output_aliases={6: 0},
        compiler_params=_params("arbitrary", "arbitrary"),
        name="lat_attention",
    )(qkv, qkv, qkv, cache_kt, cache_vt, ebias, o_buf)


def _expanded_bias(rel_bias):
    col = jnp.arange(GRID_W)
    col_start = jnp.clip(col - WIN_W // 2, 0, GRID_W - WIN_W)
    col_ok = (col[None, :] >= col_start[:, None]) & (col[None, :] < col_start[:, None] + WIN_W)
    dx = jnp.clip(col[None, :] - col[:, None] + (WIN_W - 1), 0, 2 * WIN_W - 2)
    onehot = (dx[None] == jnp.arange(2 * WIN_W - 1)[:, None, None]).astype(F32)
    eb = jnp.einsum("hyj,jqk->hyqk", rel_bias.astype(F32), onehot, precision=lax.Precision.HIGHEST)
    eb = jnp.where(col_ok[None, None], eb, NEG_INF)
    return jnp.concatenate([eb, eb], axis=-1)


def _wo_kernel(o_ref, x_ref, mods_ref, w_ref, lng_ref, lnb_ref, w2_ref, x1_ref, h2p_ref, logit_ref, w_s):
    @pl.when(pl.program_id(0) == 0)
    def _():
        w_s[...] = w_ref[0].astype(BF16)

    for r0 in range(0, WO_TILE, TILE):
        rs = slice(r0, r0 + TILE)
        mix = jnp.dot(o_ref[rs, :], w_s[...], preferred_element_type=F32)
        _epilogue(x_ref[rs, :], mix, mods_ref[0], lng_ref[...], lnb_ref[...], w2_ref,
                  x1_ref.at[rs, :], h2p_ref.at[rs, :], logit_ref.at[:, rs])


def _attn_out(o, x, mods, w_o, lng, lnb, w2):
    return pl.pallas_call(
        _wo_kernel,
        grid=(T_ALL // WO_TILE,),
        in_specs=[
            pl.BlockSpec((WO_TILE, D_MODEL), lambda t: (t, 0)),
            pl.BlockSpec((WO_TILE, D_MODEL), lambda t: (t, 0)),
            _mods_spec(WO_TILE),
            pl.BlockSpec((1, D_MODEL, D_MODEL), lambda t: (0, 0, 0)),
        ] + _EPI_IN_SPECS,
        out_specs=[
            pl.BlockSpec((WO_TILE, D_MODEL), lambda t: (t, 0)),
            pl.BlockSpec((WO_TILE, D_PACK), lambda t: (t, 0)),
            pl.BlockSpec((N_EXPERTS, WO_TILE), lambda t: (0, t)),
        ],
        out_shape=_EPI_OUT_SHAPE,
        scratch_shapes=[pltpu.VMEM((D_MODEL, D_MODEL), BF16)],
        compiler_params=_params("arbitrary"),
        name="attn_out",
    )(o, x, mods, w_o, lng, lnb, w2)


def kernel(x_prompt, x_sample, cache_k, cache_v, c, c_ctx, w_ada, b_ada, ln1_g, ln1_b, ln2_g, ln2_b,
           pool_w, pool_scale, w_qkv, w_o, rel_bias, w_router, b_router, w_gate, w_up, w_down):
    cond = jnp.zeros((N_COND, D_MODEL), F32).at[0].set(c_ctx).at[1:1 + DEC_BATCH].set(c)
    mods0, mods1 = _ada(cond, w_ada, b_ada, 0), _ada(cond, w_ada, b_ada, 1)
    wrt = w_router.T
    wrt_hi = wrt.astype(BF16)
    w2 = jnp.concatenate([wrt_hi, (wrt - wrt_hi.astype(F32)).astype(BF16)], axis=0)

    x, h2p, logits = _pool_layer(x_prompt.reshape(T_CTX, D_MODEL), x_sample.reshape(T_LAT, D_MODEL), mods0,
                                 pool_w[0].astype(BF16), pool_scale[0:1], ln1_g[0:1], ln1_b[0:1], w2)
    yg, wsel = _moe_rows_of_tokens(h2p, logits, b_router, 0, w_gate, w_up, w_down)

    x, qkv, new_kt, new_vt = _qkv(x, yg, wsel, mods0, ln2_g[0:1], ln2_b[0:1], mods1, w_qkv)
    o = _lat_attention(qkv, cache_k.transpose(0, 1, 2, 4, 3), cache_v.transpose(0, 1, 2, 4, 3),
                       _expanded_bias(rel_bias[0]), _ctx_attention(qkv))
    x, h2p, logits = _attn_out(o, x, mods1, w_o, ln1_g[1:2], ln1_b[1:2], w2)
    yg, wsel = _moe_rows_of_tokens(h2p, logits, b_router, 1, w_gate, w_up, w_down)
    y_ctx, y_lat = [_combine(x, yg, wsel, mods1, ln2_g[1:2], ln2_b[1:2], t0, n) for t0, n in SEGMENTS]
    return (y_ctx.reshape(BATCH, SEQ, D_MODEL), y_lat.reshape(DEC_BATCH, DEC_SEQ, D_MODEL),
            new_kt.transpose(0, 1, 2, 4, 3), new_vt.transpose(0, 1, 2, 4, 3))
```

```python
import functools

import jax
import jax.numpy as jnp
from jax import lax
from jax.experimental import pallas as pl
from jax.experimental.pallas import tpu as pltpu
from jax.experimental.pallas import tpu_sc as plsc

F32 = jnp.float32
BF16 = jnp.bfloat16

D_MODEL = 1024
BATCH = 16
SEQ = 256
DEC_BATCH = 8
DEC_SEQ = 1024
PAST_LEN = 512
GRID_W = 64
GRID_ROWS = DEC_SEQ // GRID_W
POOL_SIZES = (2, 4, 8, 16)
POOL_GROUP_DIM = D_MODEL // len(POOL_SIZES)
POOL_HALO = 8
N_HEADS = 16
HEAD_DIM = 64
WIN_H = 8
WIN_W = 16
N_EXPERTS = 16
EXPERTS_PER_GROUP = 4
N_EXPERT_GROUPS = 4
D_FF = 512
ALPHA = (2.0 * 2) ** 0.25
LN_EPS = 1e-5
NEG_INF = -1e30

T_CTX = BATCH * SEQ
T_LAT = DEC_BATCH * DEC_SEQ
T_ALL = T_CTX + T_LAT
N_COND = 16
TILE = 256
N_TILES = T_ALL // TILE
CTX_TILES = T_CTX // TILE
TILES_PER_LAT_SEQ = DEC_SEQ // TILE
MOE_TILE = 1024
MOE_SUB = 256
SEGMENTS = ((0, T_CTX), (T_CTX, T_LAT))
LN_TILE = 512
QKV_TILE = 512
WO_TILE = 1024
SC_WINDOW = 64
D_PACK = D_MODEL // 2
ROUTE_ROWS = T_ALL // 128
HEAD_PAIRS = N_HEADS // 2
LANES = 128
VMEM_LIMIT = 56 * 1024 * 1024


def _cond_row(t, tile):
    ctx_tiles = T_CTX // tile
    per_seq = DEC_SEQ // tile
    return jnp.maximum(t - ctx_tiles + per_seq, 0) // per_seq


def _params(*sem):
    return pltpu.CompilerParams(dimension_semantics=sem, vmem_limit_bytes=VMEM_LIMIT)


def _pack_halves(v):
    half = v.shape[1] // 2
    hi = lax.bitcast_convert_type(v[:, :half].astype(F32), jnp.uint32)
    lo = lax.bitcast_convert_type(v[:, half:].astype(F32), jnp.uint32)
    return hi | (lo >> 16)


def _unpack_halves(p):
    left = lax.bitcast_convert_type(p & jnp.uint32(0xFFFF0000), F32)
    right = lax.bitcast_convert_type(p << 16, F32)
    return left, right


def _ada_kernel(cond_ref, w_ref, b_ref, o_ref):
    cnd = cond_ref[...]
    act = cnd * jax.nn.sigmoid(cnd)
    a_hi = act.astype(BF16)
    a_lo = (act - a_hi.astype(F32)).astype(BF16)
    w = w_ref[0]
    w_hi = w.astype(BF16)
    w_lo = (w - w_hi.astype(F32)).astype(BF16)
    a2 = jnp.concatenate([a_hi, a_lo], axis=0)
    p = jnp.dot(a2, w_hi, preferred_element_type=F32)
    q = jnp.dot(a_hi, w_lo, preferred_element_type=F32)
    o_ref[0] = p[:N_COND] + p[N_COND:] + q + b_ref[0]


def _ada(cond, w_ada, b_ada, layer):
    depth, d, n = w_ada.shape
    bn = 1536
    mods = pl.pallas_call(
        _ada_kernel,
        grid=(n // bn,),
        in_specs=[
            pl.BlockSpec((N_COND, d), lambda j: (0, 0)),
            pl.BlockSpec((1, d, bn), lambda j: (layer, 0, j)),
            pl.BlockSpec((1, 1, bn), lambda j: (layer, 0, j)),
        ],
        out_specs=pl.BlockSpec((1, N_COND, bn), lambda j: (0, 0, j)),
        out_shape=jax.ShapeDtypeStruct((1, N_COND, n), F32),
        compiler_params=_params("arbitrary"),
        name="ada",
    )(cond, w_ada, b_ada.reshape(depth, 1, n))
    return mods.reshape(N_COND, 6, D_MODEL)


def _post_norm(x, upd, g, b):
    y = ALPHA * x + upd
    mu = jnp.mean(y, axis=-1, keepdims=True)
    yc = y - mu
    var = jnp.mean(yc * yc, axis=-1, keepdims=True)
    return yc * lax.rsqrt(var + LN_EPS) * g + b


def _router_logits(h2, h_hi, w2_ref):
    h_lo = (h2 - h_hi.astype(F32)).astype(BF16)
    w2 = w2_ref[...]
    p = _dot_nt(w2, h_hi)
    q = _dot_nt(w2[:N_EXPERTS], h_lo)
    return p[:N_EXPERTS] + p[N_EXPERTS:] + q


def _epilogue(x, mix, m, lng, lnb, w2_ref, x1_ref, h2p_ref, logit_ref):
    g1, sh2, sc2 = m[2:3, :], m[3:4, :], m[4:5, :]
    x1 = _post_norm(x, g1 * mix, lng, lnb)
    h2 = x1 * (1.0 + sc2) + sh2
    h_hi = h2.astype(BF16)
    x1_ref[...] = x1
    h2p_ref[...] = _pack_halves(h_hi)
    logit_ref[...] = _router_logits(h2, h_hi, w2_ref)


_EPI_IN_SPECS = [
    pl.BlockSpec((1, D_MODEL), lambda t: (0, 0)),
    pl.BlockSpec((1, D_MODEL), lambda t: (0, 0)),
    pl.BlockSpec((2 * N_EXPERTS, D_MODEL), lambda t: (0, 0)),
]
_EPI_OUT_SPECS = [
    pl.BlockSpec((TILE, D_MODEL), lambda t: (t, 0)),
    pl.BlockSpec((TILE, D_PACK), lambda t: (t, 0)),
    pl.BlockSpec((N_EXPERTS, TILE), lambda t: (0, t)),
]
_EPI_OUT_SHAPE = [
    jax.ShapeDtypeStruct((T_ALL, D_MODEL), F32),
    jax.ShapeDtypeStruct((T_ALL, D_PACK), jnp.uint32),
    jax.ShapeDtypeStruct((N_EXPERTS, T_ALL), F32),
]


def _route_kernel(logit_ref, br_ref, e_ref, w_ref):
    aff = jax.nn.sigmoid(logit_ref[...])
    sel = aff + br_ref[...]
    sel_rows = [sel[e] for e in range(N_EXPERTS)]
    aff_rows = [aff[e] for e in range(N_EXPERTS)]

    def group_score(g):
        r = sel_rows[g * EXPERTS_PER_GROUP:(g + 1) * EXPERTS_PER_GROUP]
        best = None
        for i in range(EXPERTS_PER_GROUP):
            for j in range(i + 1, EXPERTS_PER_GROUP):
                pair = r[i] + r[j]
                best = pair if best is None else jnp.maximum(best, pair)
        return best

    best = group_score(0)
    gidx = jnp.zeros_like(best, dtype=jnp.int32)
    for g in range(1, N_EXPERT_GROUPS):
        sc = group_score(g)
        better = sc > best
        gidx = jnp.where(better, g, gidx)
        best = jnp.where(better, sc, best)

    def pick_group(rows, j):
        out = rows[j]
        for g in range(1, N_EXPERT_GROUPS):
            out = jnp.where(gidx == g, rows[g * EXPERTS_PER_GROUP + j], out)
        return out

    cand = [pick_group(sel_rows, j) for j in range(EXPERTS_PER_GROUP)]
    cand_aff = [pick_group(aff_rows, j) for j in range(EXPERTS_PER_GROUP)]

    def argmax_first(vals):
        bv, bi, ba = vals[0], jnp.zeros_like(gidx), cand_aff[0]
        for j in range(1, EXPERTS_PER_GROUP):
            better = vals[j] > bv
            bv = jnp.where(better, vals[j], bv)
            bi = jnp.where(better, j, bi)
            ba = jnp.where(better, cand_aff[j], ba)
        return bi, ba

    i1, a1 = argmax_first(cand)
    rest = [jnp.where(i1 == j, -jnp.inf, cand[j]) for j in range(EXPERTS_PER_GROUP)]
    i2, a2 = argmax_first(rest)
    denom = a1 + a2
    base = gidx * EXPERTS_PER_GROUP
    e_ref[0] = base + i1
    e_ref[1] = base + i2
    w_ref[0] = a1 / denom
    w_ref[1] = a2 / denom


def _route(logits, b_router):
    return pl.pallas_call(
        _route_kernel,
        out_shape=[jax.ShapeDtypeStruct((2, ROUTE_ROWS, 128), jnp.int32),
                   jax.ShapeDtypeStruct((2, ROUTE_ROWS, 128), F32)],
        compiler_params=pltpu.CompilerParams(vmem_limit_bytes=VMEM_LIMIT),
        name="route",
    )(logits.reshape(N_EXPERTS, ROUTE_ROWS, 128), b_router.reshape(N_EXPERTS, 1, 1))


def _mods_spec(tile, first_tile=0):
    return pl.BlockSpec((1, 6, D_MODEL), lambda t: (_cond_row(first_tile + t, tile), 0, 0))


def _pool_kernel(xa_ref, xb_ref, xp_ref, xn_ref, mods_ref, pw_ref, ps_ref, lng_ref, lnb_ref, w2_ref,
                 x1_ref, h2p_ref, logit_ref):
    t = pl.program_id(0)
    m = mods_ref[0]
    sh1, sc1 = m[0:1, :], m[1:2, :]
    in_lat = t >= CTX_TILES
    sub = (t - CTX_TILES) % TILES_PER_LAT_SEQ
    is_first = jnp.logical_or(jnp.logical_not(in_lat), sub == 0)
    is_last = jnp.logical_or(jnp.logical_not(in_lat), sub == TILES_PER_LAT_SEQ - 1)
    seq_len = jnp.where(in_lat, DEC_SEQ, SEQ)
    pos0 = jnp.where(in_lat, sub * TILE, 0)

    x = jnp.where(in_lat, xb_ref[...], xa_ref[...])
    h = x * (1.0 + sc1) + sh1
    hp = jnp.where(is_first, 0.0, xp_ref[...] * (1.0 + sc1) + sh1)
    hn = jnp.where(is_last, 0.0, xn_ref[...] * (1.0 + sc1) + sh1)
    hext = jnp.concatenate([hp, h, hn], axis=0)
    ext = TILE + 2 * POOL_HALO
    pos = pos0 + lax.broadcasted_iota(jnp.int32, (TILE, 1), 0)

    outs = []
    for g, w in enumerate(POOL_SIZES):
        lo_c, hi_c = g * POOL_GROUP_DIM, (g + 1) * POOL_GROUP_DIM
        a = hext[:, lo_c:hi_c]
        k = 1
        while k < w:
            a = a + pltpu.roll(a, ext - k, axis=0)
            k *= 2
        off = POOL_HALO - w // 2
        win = pltpu.roll(a, ext - off, axis=0)[:TILE] if off else a[:TILE]
        lo = jnp.maximum(pos - w // 2, 0)
        hi = jnp.minimum(pos - w // 2 + w, seq_len)
        cnt = (hi - lo).astype(F32)
        pooled = win / cnt - h[:, lo_c:hi_c]
        outs.append(jnp.dot(pooled.astype(BF16), pw_ref[g], preferred_element_type=F32))
    mix = jnp.concatenate(outs, axis=1) * ps_ref[...]
    _epilogue(x, mix, m, lng_ref[...], lnb_ref[...], w2_ref, x1_ref, h2p_ref, logit_ref)


def _pool_layer(x_ctx, x_lat, mods, pool_w, pool_scale, lng, lnb, w2):
    halo_blocks = TILE // POOL_HALO
    last_halo = T_LAT // POOL_HALO - 1
    return pl.pallas_call(
        _pool_kernel,
        grid=(N_TILES,),
        in_specs=[
            pl.BlockSpec((TILE, D_MODEL), lambda t: (jnp.minimum(t, CTX_TILES - 1), 0)),
            pl.BlockSpec((TILE, D_MODEL), lambda t: (jnp.maximum(t - CTX_TILES, 0), 0)),
            pl.BlockSpec((POOL_HALO, D_MODEL), lambda t: (jnp.maximum((t - CTX_TILES) * halo_blocks - 1, 0), 0)),
            pl.BlockSpec((POOL_HALO, D_MODEL),
                         lambda t: (jnp.clip((t - CTX_TILES + 1) * halo_blocks, 0, last_halo), 0)),
            _mods_spec(TILE),
            pl.BlockSpec((len(POOL_SIZES), POOL_GROUP_DIM, POOL_GROUP_DIM), lambda t: (0, 0, 0)),
            pl.BlockSpec((1, D_MODEL), lambda t: (0, 0)),
        ] + _EPI_IN_SPECS,
        out_specs=_EPI_OUT_SPECS,
        out_shape=_EPI_OUT_SHAPE,
        compiler_params=_params("arbitrary"),
        name="pool_mixer",
    )(x_ctx, x_lat, x_lat, x_lat, mods, pool_w, pool_scale, lng, lnb, w2)


def _moe_rows(n):
    return 2 * n + N_EXPERTS * MOE_TILE


def _route_plan(eidx):
    n = eidx.shape[1]
    tiles = _moe_rows(n) // MOE_TILE
    eidx = eidx.reshape(2 * n)
    experts = jnp.arange(N_EXPERTS, dtype=jnp.int32)
    onehot = (eidx[:, None] == experts[None, :]).astype(jnp.int32)
    counts = jnp.sum(onehot, axis=0)
    rank = jnp.sum((jnp.cumsum(onehot, axis=0) - onehot) * onehot, axis=1)
    padded = (counts + MOE_TILE - 1) // MOE_TILE * MOE_TILE
    ends = jnp.cumsum(padded)
    starts = ends - padded
    pos = jnp.sum(onehot * starts[None, :], axis=1) + rank
    tile_start = jnp.arange(tiles, dtype=jnp.int32)[:, None] * MOE_TILE
    in_region = ((tile_start >= starts[None, :]) & (tile_start < ends[None, :])).astype(jnp.int32)
    used = jnp.sum(in_region, axis=1) > 0
    last_expert = jnp.max(jnp.where(counts > 0, experts, 0))
    tile_expert = jnp.where(used, jnp.sum(in_region * experts[None, :], axis=1), last_expert)
    first_block = starts // MOE_TILE
    block_in_expert = padded[None, :] // MOE_TILE - 1 - (tile_start - starts[None, :]) // MOE_TILE
    tile_valid = jnp.sum(in_region * jnp.clip(counts[None, :] - block_in_expert * MOE_TILE, 0, MOE_TILE), axis=1)
    tile_block = jnp.where(used, jnp.sum(in_region * (first_block[None, :] + block_in_expert), axis=1),
                           jnp.sum(jnp.where(experts == last_expert, first_block, 0)))
    return pos.astype(jnp.int32), tile_expert.astype(jnp.int32), tile_valid.astype(jnp.int32), \
        tile_block.astype(jnp.int32)


def _expert_kernel(te_ref, nv_ref, tb_ref, x_ref, wg_ref, wu_ref, wd_ref, y_ref, wgu_s, wd_s):
    i = pl.program_id(0)
    prev = te_ref[jnp.maximum(i - 1, 0)]
    changed = jnp.logical_or(i == 0, te_ref[i] != prev)

    @pl.when(changed)
    def _():
        wgu_s[:, :D_FF] = wg_ref[0, 0].astype(BF16)
        wgu_s[:, D_FF:] = wu_ref[0, 0].astype(BF16)
        wd_s[...] = wd_ref[0, 0].astype(BF16)

    nv = nv_ref[i]

    def gate_up(r0):
        rows = r0 + lax.broadcasted_iota(jnp.int32, (MOE_SUB, 1), 0)
        xp = jnp.where(rows < nv, x_ref[r0:r0 + MOE_SUB, :], jnp.uint32(0))
        left, right = _unpack_halves(xp)
        xb = jnp.concatenate([left.astype(BF16), right.astype(BF16)], axis=1)
        return jnp.dot(xb, wgu_s[...], preferred_element_type=F32)

    def down(r0, gu):
        gate, up = gu[:, :D_FF], gu[:, D_FF:]
        he = (gate * jax.nn.sigmoid(gate) * up).astype(BF16)
        y_ref[r0:r0 + MOE_SUB, :] = _pack_halves(jnp.dot(he, wd_s[...], preferred_element_type=F32).astype(BF16))

    def run(n_sub):
        gu = gate_up(0)
        for j in range(n_sub):
            nxt = gate_up((j + 1) * MOE_SUB) if j + 1 < n_sub else None
            down(j * MOE_SUB, gu)
            gu = nxt

    n_subs = MOE_TILE // MOE_SUB
    for n_sub in range(1, n_subs + 1):
        lo = (n_sub - 1) * MOE_SUB
        in_range = nv > lo if n_sub == n_subs else jnp.logical_and(nv > lo, nv <= lo + MOE_SUB)
        pl.when(in_range)(functools.partial(run, n_sub))


def _experts(xs, tile_expert, tile_valid, tile_block, layer, w_gate, w_up, w_down):
    grid_spec = pltpu.PrefetchScalarGridSpec(
        num_scalar_prefetch=3,
        grid=(xs.shape[0] // MOE_TILE,),
        in_specs=[
            pl.BlockSpec((MOE_TILE, D_PACK), lambda i, te, nv, tb: (tb[i], 0)),
            pl.BlockSpec((1, 1, D_MODEL, D_FF), lambda i, te, nv, tb: (layer, te[i], 0, 0)),
            pl.BlockSpec((1, 1, D_MODEL, D_FF), lambda i, te, nv, tb: (layer, te[i], 0, 0)),
            pl.BlockSpec((1, 1, D_FF, D_MODEL), lambda i, te, nv, tb: (layer, te[i], 0, 0)),
        ],
        out_specs=pl.BlockSpec((MOE_TILE, D_PACK), lambda i, te, nv, tb: (tb[i], 0)),
        scratch_shapes=[pltpu.VMEM((D_MODEL, 2 * D_FF), BF16), pltpu.VMEM((D_FF, D_MODEL), BF16)],
    )
    return pl.pallas_call(
        _expert_kernel,
        grid_spec=grid_spec,
        out_shape=jax.ShapeDtypeStruct(xs.shape, jnp.uint32),
        compiler_params=_params("arbitrary"),
        name="experts",
    )(tile_expert, tile_valid, tile_block, xs, w_gate, w_up, w_down)


SC_WORKERS = 32


def _sc_mesh():
    return plsc.VectorSubcoreMesh(core_axis_name="core", subcore_axis_name="subcore")


def _sc_worker():
    return lax.axis_index("subcore") * 2 + lax.axis_index("core")


def _dispatch(h2, pos):
    n = h2.shape[0]
    per_worker = n // SC_WORKERS
    pairs = per_worker // (2 * SC_WINDOW)
    one_set = [pltpu.VMEM((SC_WINDOW,), jnp.int32), pltpu.VMEM((SC_WINDOW,), jnp.int32),
               pltpu.VMEM((SC_WINDOW, D_PACK), jnp.uint32),
               pltpu.SemaphoreType.DMA, pltpu.SemaphoreType.DMA, pltpu.SemaphoreType.DMA]

    @functools.partial(pl.kernel, out_type=jax.ShapeDtypeStruct((_moe_rows(n), D_PACK), jnp.uint32),
                       mesh=_sc_mesh(), scratch_types=one_set + one_set, name="moe_dispatch")
    def scatter(x_hbm, i_hbm, o_hbm, *bufs):
        base = _sc_worker() * per_worker
        sets = (bufs[:6], bufs[6:])

        def rows_in(j, s):
            _, _, rows, sem, _, _ = sets[s]
            return pltpu.make_async_copy(x_hbm.at[pl.ds(base + j * SC_WINDOW, SC_WINDOW)], rows, sem)

        def rows_out(k, s):
            idx, rows, sem = sets[s][k], sets[s][2], sets[s][4 + k]
            return pltpu.make_async_copy(rows, o_hbm.at[idx], sem)

        def fetch(j, s):
            t0 = base + j * SC_WINDOW
            pltpu.sync_copy(i_hbm.at[pl.ds(t0, SC_WINDOW)], sets[s][0])
            pltpu.sync_copy(i_hbm.at[pl.ds(n + t0, SC_WINDOW)], sets[s][1])
            rows_in(j, s).start()

        fetch(0, 0)

        @pl.loop(0, pairs)
        def _(g):
            j0, j1 = 2 * g, 2 * g + 1
            fetch(j1, 1)
            rows_in(j0, 0).wait()
            rows_out(0, 0).start()
            rows_out(1, 0).start()
            rows_in(j1, 1).wait()
            rows_out(0, 1).start()
            rows_out(1, 1).start()
            rows_out(0, 0).wait()
            rows_out(1, 0).wait()

            @pl.when(g + 1 < pairs)
            def _():
                fetch(j0 + 2, 0)

            rows_out(0, 1).wait()
            rows_out(1, 1).wait()

    return scatter(h2, pos)


def _gather_pairs(y, pos):
    n = pos.shape[0] // 2
    per_worker = 2 * n // SC_WORKERS
    pairs = per_worker // (2 * SC_WINDOW)
    scratch = [pltpu.VMEM((per_worker,), jnp.int32),
               pltpu.VMEM((SC_WINDOW, D_PACK), jnp.uint32), pltpu.VMEM((SC_WINDOW, D_PACK), jnp.uint32),
               pltpu.SemaphoreType.DMA, pltpu.SemaphoreType.DMA, pltpu.SemaphoreType.DMA, pltpu.SemaphoreType.DMA]

    @functools.partial(pl.kernel, out_type=jax.ShapeDtypeStruct((2 * n, D_PACK), jnp.uint32),
                       mesh=_sc_mesh(), scratch_types=scratch, name="moe_gather")
    def gather(y_hbm, i_hbm, o_hbm, idx_v, rows0, rows1, read0, read1, write0, write1):
        base = _sc_worker() * per_worker
        pltpu.sync_copy(i_hbm.at[pl.ds(base, per_worker)], idx_v)

        def read(j, rows, sem):
            return pltpu.make_async_copy(y_hbm.at[idx_v.at[pl.ds(j * SC_WINDOW, SC_WINDOW)]], rows, sem)

        def write(j, rows, sem):
            return pltpu.make_async_copy(rows, o_hbm.at[pl.ds(base + j * SC_WINDOW, SC_WINDOW)], sem)

        read(0, rows0, read0).start()

        @pl.loop(0, pairs)
        def _(g):
            j0, j1 = 2 * g, 2 * g + 1
            read(j1, rows1, read1).start()
            read(j0, rows0, read0).wait()
            write(j0, rows0, write0).start()
            read(j1, rows1, read1).wait()
            write(j1, rows1, write1).start()
            write(j0, rows0, write0).wait()

            @pl.when(g + 1 < pairs)
            def _():
                read(j0 + 2, rows0, read0).start()

            write(j1, rows1, write1).wait()

    return gather(y, pos).reshape(2, n, D_PACK)


def _moe_post_norm(x_ref, yg_ref, w_ref, mods_ref, lng_ref, lnb_ref):
    g2 = mods_ref[0][5:6, :]
    y1 = jnp.concatenate(_unpack_halves(yg_ref[0]), axis=1)
    y2 = jnp.concatenate(_unpack_halves(yg_ref[1]), axis=1)
    w = w_ref[...]
    moe = w[:, 0:1] * y1 + w[:, 1:2] * y2
    return _post_norm(x_ref[...], g2 * moe, lng_ref[...], lnb_ref[...])


def _combine_kernel(x_ref, yg_ref, w_ref, mods_ref, lng_ref, lnb_ref, o_ref):
    o_ref[...] = _moe_post_norm(x_ref, yg_ref, w_ref, mods_ref, lng_ref, lnb_ref)


def _combine(x1, yg, wsel, mods, lng, lnb, t0, n):
    off = t0 // LN_TILE
    return pl.pallas_call(
        _combine_kernel,
        grid=(n // LN_TILE,),
        in_specs=[
            pl.BlockSpec((LN_TILE, D_MODEL), lambda t: (off + t, 0)),
            pl.BlockSpec((2, LN_TILE, D_PACK), lambda t: (0, off + t, 0)),
            pl.BlockSpec((LN_TILE, 2), lambda t: (off + t, 0)),
            _mods_spec(LN_TILE, off),
            pl.BlockSpec((1, D_MODEL), lambda t: (0, 0)),
            pl.BlockSpec((1, D_MODEL), lambda t: (0, 0)),
        ],
        out_specs=pl.BlockSpec((LN_TILE, D_MODEL), lambda t: (t, 0)),
        out_shape=jax.ShapeDtypeStruct((n, D_MODEL), F32),
        compiler_params=_params("arbitrary"),
        name="moe_combine",
    )(x1, yg, wsel, mods, lng, lnb)


def _moe_rows_of_tokens(h2p, logits, b_router, layer, w_gate, w_up, w_down):
    eidx, wsel = _route(logits, b_router)
    eidx, wsel = eidx.reshape(2, T_ALL), wsel.reshape(2, T_ALL).T
    pos, tile_expert, tile_valid, tile_block = _route_plan(eidx)
    xs = _dispatch(h2p, pos)
    y = _experts(xs, tile_expert, tile_valid, tile_block, layer, w_gate, w_up, w_down)
    return _gather_pairs(y, pos), wsel


def _qkv_kernel(x1_ref, yg_ref, wsel_ref, mods0_ref, lng_ref, lnb_ref, mods_ref, w_ref,
                x_ref, qkv_ref, nk_ref, nv_ref, w_s):
    t = pl.program_id(0)

    @pl.when(t == 0)
    def _():
        w_s[...] = w_ref[0].astype(BF16)

    m = mods_ref[0]
    sh1, sc1 = m[0:1, :], m[1:2, :]

    def norm(rows):
        x = _moe_post_norm(x1_ref.at[rows], yg_ref.at[:, rows], wsel_ref.at[rows], mods0_ref, lng_ref, lnb_ref)
        x_ref[rows, :] = x
        return (x * (1.0 + sc1) + sh1).astype(BF16)

    def project(rows, h):
        r = jnp.dot(h, w_s[...], preferred_element_type=F32)
        qkv_ref[rows, :D_MODEL] = (r[:, :D_MODEL] * (HEAD_DIM ** -0.5)).astype(BF16)
        qkv_ref[rows, D_MODEL:] = r[:, D_MODEL:].astype(BF16)
        return r

    is_ctx = t < T_CTX // QKV_TILE
    rows = slice(0, QKV_TILE)

    @pl.when(is_ctx)
    def _():
        r = project(rows, norm(rows))
        for out_ref, base in ((nk_ref, D_MODEL), (nv_ref, 2 * D_MODEL)):
            for b in range(QKV_TILE // SEQ):
                for p in range(HEAD_PAIRS):
                    pair = r[b * SEQ:(b + 1) * SEQ, base + p * LANES: base + (p + 1) * LANES].T
                    out_ref[b, 0, 2 * p] = pair[:HEAD_DIM]
                    out_ref[b, 0, 2 * p + 1] = pair[HEAD_DIM:]

    @pl.when(jnp.logical_not(is_ctx))
    def _():
        project(rows, norm(rows))


def _qkv(x1, yg, wsel, mods0, lng, lnb, mods1, w_qkv):
    seqs = QKV_TILE // SEQ
    row_spec = pl.BlockSpec((QKV_TILE, D_MODEL), lambda t: (t, 0))
    vec_spec = pl.BlockSpec((1, D_MODEL), lambda t: (0, 0))

    cache_spec = pl.BlockSpec((seqs, 1, N_HEADS, HEAD_DIM, SEQ),
                              lambda t: (jnp.minimum(t, T_CTX // QKV_TILE - 1), 0, 0, 0, 0))
    cache_shape = jax.ShapeDtypeStruct((BATCH, 1, N_HEADS, HEAD_DIM, SEQ), F32)
    return pl.pallas_call(
        _qkv_kernel,
        grid=(T_ALL // QKV_TILE,),
        in_specs=[
            row_spec,
            pl.BlockSpec((2, QKV_TILE, D_PACK), lambda t: (0, t, 0)),
            pl.BlockSpec((QKV_TILE, 2), lambda t: (t, 0)),
            _mods_spec(QKV_TILE),
            vec_spec,
            vec_spec,
            _mods_spec(QKV_TILE),
            pl.BlockSpec((1, D_MODEL, 3 * D_MODEL), lambda t: (0, 0, 0)),
        ],
        out_specs=[row_spec, pl.BlockSpec((QKV_TILE, 3 * D_MODEL), lambda t: (t, 0)), cache_spec, cache_spec],
        out_shape=[jax.ShapeDtypeStruct((T_ALL, D_MODEL), F32),
                   jax.ShapeDtypeStruct((T_ALL, 3 * D_MODEL), BF16), cache_shape, cache_shape],
        scratch_shapes=[pltpu.VMEM((D_MODEL, 3 * D_MODEL), BF16)],
        compiler_params=_params("arbitrary"),
        name="qkv",
    )(x1, yg, wsel, mods0, lng, lnb, mods1, w_qkv)


def _dot_nt(a, b):
    return lax.dot_general(a, b, (((1,), (1,)), ((), ())), preferred_element_type=F32)


def _head_masks():
    lane = lax.broadcasted_iota(jnp.int32, (1, LANES), 1)
    return lane < HEAD_DIM, lane >= HEAD_DIM


def _ctx_attn_kernel(q_ref, k_ref, v_ref, o_ref):
    left, right = _head_masks()
    for p in range(HEAD_PAIRS):
        cols = slice(p * LANES, (p + 1) * LANES)
        q2, k2, v2 = q_ref[:, cols], k_ref[:, cols], v_ref[:, cols]
        halves = []
        for mask in (left, right):
            qh = jnp.where(mask, q2, jnp.zeros_like(q2))
            s = _dot_nt(qh, k2)
            e = jnp.exp(s - jnp.max(s, axis=-1, keepdims=True))
            o2 = jnp.dot(e.astype(BF16), v2, preferred_element_type=F32)
            halves.append(o2 / jnp.sum(e, axis=-1, keepdims=True))
        o_ref[:, cols] = jnp.where(left, halves[0], halves[1]).astype(BF16)


def _ctx_attention(qkv):
    return pl.pallas_call(
        _ctx_attn_kernel,
        grid=(BATCH,),
        in_specs=[pl.BlockSpec((SEQ, D_MODEL), lambda b, j=j: (b, j)) for j in range(3)],
        out_specs=pl.BlockSpec((SEQ, D_MODEL), lambda b: (b, 0)),
        out_shape=jax.ShapeDtypeStruct((T_ALL, D_MODEL), BF16),
        compiler_params=_params("arbitrary"),
        name="ctx_attention",
    )(qkv, qkv, qkv)


_LAT_Q_BLOCK_ROWS = 4
_LAT_KEY_ROWS = ((0, 8), (0, 12), (4, 16), (8, 16))


def _softmax_rows(s_ref, p_ref, l_ref):
    sc = s_ref[...]
    e = jnp.exp(sc - jnp.max(sc, axis=-1, keepdims=True))
    l_ref[...] = jnp.sum(e, axis=-1, keepdims=True)
    p_ref[...] = e.astype(BF16)


def _lat_attn_kernel(q_ref, k_ref, v_ref, ck_ref, cv_ref, eb_ref, o_ctx_ref, o_ref, bias_s, s_scr, p_scr, l_scr):
    del o_ctx_ref
    left, right = _head_masks()

    @pl.when(pl.program_id(1) == 0)
    def _():
        neg = jnp.full((GRID_W, LANES), NEG_INF, F32)
        for hh in range(2):
            for r in range(GRID_ROWS):
                r0 = min(max(r - WIN_H // 2, 0), GRID_ROWS - WIN_H)
                for j in range(GRID_ROWS // 2):
                    parts = []
                    for kr in (2 * j, 2 * j + 1):
                        parts.append(eb_ref[hh, kr - r + WIN_H - 1] if r0 <= kr < r0 + WIN_H else None)
                    if parts[0] is None and parts[1] is None:
                        val = neg
                    else:
                        val = jnp.where(left, neg if parts[0] is None else parts[0],
                                        neg if parts[1] is None else parts[1])
                    bias_s[hh, r * GRID_W:(r + 1) * GRID_W, j * LANES:(j + 1) * LANES] = val

    ck = jnp.concatenate([ck_ref[0, 0, 0], ck_ref[0, 0, 1]], axis=0).astype(BF16)
    cv = jnp.concatenate([cv_ref[0, 0, 0], cv_ref[0, 0, 1]], axis=0).astype(BF16)
    qrows = _LAT_Q_BLOCK_ROWS * GRID_W
    units = [(qb, hh) for qb in range(len(_LAT_KEY_ROWS)) for hh in range(2)]

    def refs(u):
        nk = (_LAT_KEY_ROWS[units[u][0]][1] - _LAT_KEY_ROWS[units[u][0]][0]) * GRID_W
        width = nk + PAST_LEN
        return nk, s_scr.at[u % 2, :, :width], p_scr.at[u % 2, :, :width], l_scr.at[u % 2]

    def scores(u):
        qb, hh = units[u]
        kr0, kr1 = _LAT_KEY_ROWS[qb]
        qs, ks = slice(qb * qrows, (qb + 1) * qrows), slice(kr0 * GRID_W, kr1 * GRID_W)
        nk, s_ref, _, _ = refs(u)
        q2 = q_ref[qs, :]
        qh = jnp.where(right if hh else left, q2, jnp.zeros_like(q2))
        s_ref[:, :nk] = _dot_nt(qh, k_ref[ks, :]) + bias_s[hh, qs, ks]
        s_ref[:, nk:] = jnp.dot(qh, ck, preferred_element_type=F32)

    def weighted_values(u):
        qb, _ = units[u]
        kr0, kr1 = _LAT_KEY_ROWS[qb]
        nk, _, p_ref, l_ref = refs(u)
        o2 = (jnp.dot(p_ref[:, :nk], v_ref[kr0 * GRID_W:kr1 * GRID_W, :], preferred_element_type=F32)
              + _dot_nt(p_ref[:, nk:], cv))
        return o2 / l_ref[...]

    scores(0)
    halves = []
    for u in range(len(units)):
        if u + 1 < len(units):
            scores(u + 1)
        _, s_ref, p_ref, l_ref = refs(u)
        _softmax_rows(s_ref, p_ref, l_ref)
        halves.append(weighted_values(u))
        if len(halves) == 2:
            qb = units[u][0]
            o_ref[qb * qrows:(qb + 1) * qrows, :] = jnp.where(left, halves[0], halves[1]).astype(BF16)
            halves = []


def _lat_attention(qkv, cache_kt, cache_vt, ebias, o_buf):
    row0 = T_CTX // DEC_SEQ
    qrows = _LAT_Q_BLOCK_ROWS * GRID_W
    max_keys = max(k1 - k0 for k0, k1 in _LAT_KEY_ROWS) * GRID_W + PAST_LEN
    cache_spec = pl.BlockSpec((1, 1, 2, HEAD_DIM, PAST_LEN), lambda p, b: (b, 0, p, 0, 0))
    return pl.pallas_call(
        _lat_attn_kernel,
        grid=(HEAD_PAIRS, DEC_BATCH),
        in_specs=[
            pl.BlockSpec((DEC_SEQ, LANES), lambda p, b: (row0 + b, p)),
            pl.BlockSpec((DEC_SEQ, LANES), lambda p, b: (row0 + b, HEAD_PAIRS + p)),
            pl.BlockSpec((DEC_SEQ, LANES), lambda p, b: (row0 + b, 2 * HEAD_PAIRS + p)),
            cache_spec,
            cache_spec,
            pl.BlockSpec((2, 2 * WIN_H - 1, GRID_W, LANES), lambda p, b: (p, 0, 0, 0)),
            pl.BlockSpec(memory_space=pl.ANY),
        ],
        out_specs=pl.BlockSpec((DEC_SEQ, LANES), lambda p, b: (row0 + b, p)),
        out_shape=jax.ShapeDtypeStruct((T_ALL, D_MODEL), BF16),
        scratch_shapes=[pltpu.VMEM((2, DEC_SEQ, DEC_SEQ), F32),
                        pltpu.VMEM((2, qrows, max_keys), F32),
                        pltpu.VMEM((2, qrows, max_keys), BF16),
                        pltpu.VMEM((2, qrows, 1), F32)],
        input_output_aliases={6: 0},
        compiler_params=_params("arbitrary", "arbitrary"),
        name="lat_attention",
    )(qkv, qkv, qkv, cache_kt, cache_vt, ebias, o_buf)


def _expanded_bias(rel_bias):
    col = jnp.arange(GRID_W)
    col_start = jnp.clip(col - WIN_W // 2, 0, GRID_W - WIN_W)
    col_ok = (col[None, :] >= col_start[:, None]) & (col[None, :] < col_start[:, None] + WIN_W)
    dx = jnp.clip(col[None, :] - col[:, None] + (WIN_W - 1), 0, 2 * WIN_W - 2)
    onehot = (dx[None] == jnp.arange(2 * WIN_W - 1)[:, None, None]).astype(F32)
    eb = jnp.einsum("hyj,jqk->hyqk", rel_bias.astype(F32), onehot, precision=lax.Precision.HIGHEST)
    eb = jnp.where(col_ok[None, None], eb, NEG_INF)
    return jnp.concatenate([eb, eb], axis=-1)


def _wo_kernel(o_ref, x_ref, mods_ref, w_ref, lng_ref, lnb_ref, w2_ref, x1_ref, h2p_ref, logit_ref, w_s):
    @pl.when(pl.program_id(0) == 0)
    def _():
        w_s[...] = w_ref[0].astype(BF16)

    subs = [slice(r0, r0 + TILE) for r0 in range(0, WO_TILE, TILE)]
    mix = jnp.dot(o_ref[subs[0], :], w_s[...], preferred_element_type=F32)
    for j, rs in enumerate(subs):
        nxt = jnp.dot(o_ref[subs[j + 1], :], w_s[...], preferred_element_type=F32) if j + 1 < len(subs) else None
        _epilogue(x_ref[rs, :], mix, mods_ref[0], lng_ref[...], lnb_ref[...], w2_ref,
                  x1_ref.at[rs, :], h2p_ref.at[rs, :], logit_ref.at[:, rs])
        mix = nxt


def _attn_out(o, x, mods, w_o, lng, lnb, w2):
    return pl.pallas_call(
        _wo_kernel,
        grid=(T_ALL // WO_TILE,),
        in_specs=[
            pl.BlockSpec((WO_TILE, D_MODEL), lambda t: (t, 0)),
            pl.BlockSpec((WO_TILE, D_MODEL), lambda t: (t, 0)),
            _mods_spec(WO_TILE),
            pl.BlockSpec((1, D_MODEL, D_MODEL), lambda t: (0, 0, 0)),
        ] + _EPI_IN_SPECS,
        out_specs=[
            pl.BlockSpec((WO_TILE, D_MODEL), lambda t: (t, 0)),
            pl.BlockSpec((WO_TILE, D_PACK), lambda t: (t, 0)),
            pl.BlockSpec((N_EXPERTS, WO_TILE), lambda t: (0, t)),
        ],
        out_shape=_EPI_OUT_SHAPE,
        scratch_shapes=[pltpu.VMEM((D_MODEL, D_MODEL), BF16)],
        compiler_params=_params("arbitrary"),
        name="attn_out",
    )(o, x, mods, w_o, lng, lnb, w2)


def kernel(x_prompt, x_sample, cache_k, cache_v, c, c_ctx, w_ada, b_ada, ln1_g, ln1_b, ln2_g, ln2_b,
           pool_w, pool_scale, w_qkv, w_o, rel_bias, w_router, b_router, w_gate, w_up, w_down):
    cond = jnp.zeros((N_COND, D_MODEL), F32).at[0].set(c_ctx).at[1:1 + DEC_BATCH].set(c)
    mods0, mods1 = _ada(cond, w_ada, b_ada, 0), _ada(cond, w_ada, b_ada, 1)
    wrt = w_router.T
    wrt_hi = wrt.astype(BF16)
    w2 = jnp.concatenate([wrt_hi, (wrt - wrt_hi.astype(F32)).astype(BF16)], axis=0)

    x, h2p, logits = _pool_layer(x_prompt.reshape(T_CTX, D_MODEL), x_sample.reshape(T_LAT, D_MODEL), mods0,
                                 pool_w[0].astype(BF16), pool_scale[0:1], ln1_g[0:1], ln1_b[0:1], w2)
    yg, wsel = _moe_rows_of_tokens(h2p, logits, b_router, 0, w_gate, w_up, w_down)

    x, qkv, new_kt, new_vt = _qkv(x, yg, wsel, mods0, ln2_g[0:1], ln2_b[0:1], mods1, w_qkv)
    o = _lat_attention(qkv, cache_k.transpose(0, 1, 2, 4, 3), cache_v.transpose(0, 1, 2, 4, 3),
                       _expanded_bias(rel_bias[0]), _ctx_attention(qkv))
    x, h2p, logits = _attn_out(o, x, mods1, w_o, ln1_g[1:2], ln1_b[1:2], w2)
    yg, wsel = _moe_rows_of_tokens(h2p, logits, b_router, 1, w_gate, w_up, w_down)
    y_ctx, y_lat = [_combine(x, yg, wsel, mods1, ln2_g[1:2], ln2_b[1:2], t0, n) for t0, n in SEGMENTS]
    return (y_ctx.reshape(BATCH, SEQ, D_MODEL), y_lat.reshape(DEC_BATCH, DEC_SEQ, D_MODEL),
            new_kt.transpose(0, 1, 2, 4, 3), new_vt.transpose(0, 1, 2, 4, 3))
```

```python
import functools

import jax
import jax.numpy as jnp
from jax import lax
from jax.experimental import pallas as pl
from jax.experimental.pallas import tpu as pltpu
from jax.experimental.pallas import tpu_sc as plsc

F32 = jnp.float32
BF16 = jnp.bfloat16

D_MODEL = 1024
BATCH = 16
SEQ = 256
DEC_BATCH = 8
DEC_SEQ = 1024
PAST_LEN = 512
GRID_W = 64
GRID_ROWS = DEC_SEQ // GRID_W
POOL_SIZES = (2, 4, 8, 16)
POOL_GROUP_DIM = D_MODEL // len(POOL_SIZES)
POOL_HALO = 8
N_HEADS = 16
HEAD_DIM = 64
WIN_H = 8
WIN_W = 16
N_EXPERTS = 16
EXPERTS_PER_GROUP = 4
N_EXPERT_GROUPS = 4
D_FF = 512
ALPHA = (2.0 * 2) ** 0.25
LN_EPS = 1e-5
NEG_INF = -1e30

T_CTX = BATCH * SEQ
T_LAT = DEC_BATCH * DEC_SEQ
T_ALL = T_CTX + T_LAT
N_COND = 16
TILE = 256
TILES_PER_LAT_SEQ = DEC_SEQ // TILE
MOE_TILE = 1024
MOE_SUB = 256
SEGMENTS = ((0, T_CTX), (T_CTX, T_LAT))
LN_TILE = 512
QKV_TILE = 512
POOL_TILE = 512
WO_TILE = 1024
SC_WINDOW = 64
D_PACK = D_MODEL // 2
ROUTE_ROWS = T_ALL // 128
HEAD_PAIRS = N_HEADS // 2
LANES = 128
VMEM_LIMIT = 56 * 1024 * 1024


def _cond_row(t, tile):
    ctx_tiles = T_CTX // tile
    per_seq = DEC_SEQ // tile
    return jnp.maximum(t - ctx_tiles + per_seq, 0) // per_seq


def _params(*sem):
    return pltpu.CompilerParams(dimension_semantics=sem, vmem_limit_bytes=VMEM_LIMIT)


def _pack_halves(v):
    half = v.shape[1] // 2
    hi = lax.bitcast_convert_type(v[:, :half].astype(F32), jnp.uint32)
    lo = lax.bitcast_convert_type(v[:, half:].astype(F32), jnp.uint32)
    return hi | (lo >> 16)


def _unpack_halves(p):
    left = lax.bitcast_convert_type(p & jnp.uint32(0xFFFF0000), F32)
    right = lax.bitcast_convert_type(p << 16, F32)
    return left, right


def _ada_kernel(cond_ref, w_ref, b_ref, o_ref):
    cnd = cond_ref[...]
    act = cnd * jax.nn.sigmoid(cnd)
    a_hi = act.astype(BF16)
    a_lo = (act - a_hi.astype(F32)).astype(BF16)
    w = w_ref[0]
    w_hi = w.astype(BF16)
    w_lo = (w - w_hi.astype(F32)).astype(BF16)
    a2 = jnp.concatenate([a_hi, a_lo], axis=0)
    p = jnp.dot(a2, w_hi, preferred_element_type=F32)
    q = jnp.dot(a_hi, w_lo, preferred_element_type=F32)
    o_ref[0] = p[:N_COND] + p[N_COND:] + q + b_ref[0]


def _ada(cond, w_ada, b_ada, layer):
    depth, d, n = w_ada.shape
    bn = 1536
    mods = pl.pallas_call(
        _ada_kernel,
        grid=(n // bn,),
        in_specs=[
            pl.BlockSpec((N_COND, d), lambda j: (0, 0)),
            pl.BlockSpec((1, d, bn), lambda j: (layer, 0, j)),
            pl.BlockSpec((1, 1, bn), lambda j: (layer, 0, j)),
        ],
        out_specs=pl.BlockSpec((1, N_COND, bn), lambda j: (0, 0, j)),
        out_shape=jax.ShapeDtypeStruct((1, N_COND, n), F32),
        compiler_params=_params("arbitrary"),
        name="ada",
    )(cond, w_ada, b_ada.reshape(depth, 1, n))
    return mods.reshape(N_COND, 6, D_MODEL)


def _post_norm(x, upd, g, b):
    y = ALPHA * x + upd
    mu = jnp.mean(y, axis=-1, keepdims=True)
    yc = y - mu
    var = jnp.mean(yc * yc, axis=-1, keepdims=True)
    return yc * lax.rsqrt(var + LN_EPS) * g + b


def _router_logits(h2, h_hi, w2_ref):
    h_lo = (h2 - h_hi.astype(F32)).astype(BF16)
    w2 = w2_ref[...]
    p = _dot_nt(w2, h_hi)
    q = _dot_nt(w2[:N_EXPERTS], h_lo)
    return p[:N_EXPERTS] + p[N_EXPERTS:] + q


def _epilogue(x, mix, m, lng, lnb, w2_ref, x1_ref, h2p_ref, logit_ref):
    g1, sh2, sc2 = m[2:3, :], m[3:4, :], m[4:5, :]
    x1 = _post_norm(x, g1 * mix, lng, lnb)
    h2 = x1 * (1.0 + sc2) + sh2
    h_hi = h2.astype(BF16)
    x1_ref[...] = x1
    h2p_ref[...] = _pack_halves(h_hi)
    logit_ref[...] = _router_logits(h2, h_hi, w2_ref)


_EPI_IN_SPECS = [
    pl.BlockSpec((1, D_MODEL), lambda t: (0, 0)),
    pl.BlockSpec((1, D_MODEL), lambda t: (0, 0)),
    pl.BlockSpec((2 * N_EXPERTS, D_MODEL), lambda t: (0, 0)),
]
_EPI_OUT_SHAPE = [
    jax.ShapeDtypeStruct((T_ALL, D_MODEL), F32),
    jax.ShapeDtypeStruct((T_ALL, D_PACK), jnp.uint32),
    jax.ShapeDtypeStruct((N_EXPERTS, T_ALL), F32),
]


def _route_kernel(logit_ref, br_ref, e_ref, w_ref):
    aff = jax.nn.sigmoid(logit_ref[...])
    sel = aff + br_ref[...]
    sel_rows = [sel[e] for e in range(N_EXPERTS)]
    aff_rows = [aff[e] for e in range(N_EXPERTS)]

    def group_score(g):
        r = sel_rows[g * EXPERTS_PER_GROUP:(g + 1) * EXPERTS_PER_GROUP]
        best = None
        for i in range(EXPERTS_PER_GROUP):
            for j in range(i + 1, EXPERTS_PER_GROUP):
                pair = r[i] + r[j]
                best = pair if best is None else jnp.maximum(best, pair)
        return best

    best = group_score(0)
    gidx = jnp.zeros_like(best, dtype=jnp.int32)
    for g in range(1, N_EXPERT_GROUPS):
        sc = group_score(g)
        better = sc > best
        gidx = jnp.where(better, g, gidx)
        best = jnp.where(better, sc, best)

    def pick_group(rows, j):
        out = rows[j]
        for g in range(1, N_EXPERT_GROUPS):
            out = jnp.where(gidx == g, rows[g * EXPERTS_PER_GROUP + j], out)
        return out

    cand = [pick_group(sel_rows, j) for j in range(EXPERTS_PER_GROUP)]
    cand_aff = [pick_group(aff_rows, j) for j in range(EXPERTS_PER_GROUP)]

    def argmax_first(vals):
        bv, bi, ba = vals[0], jnp.zeros_like(gidx), cand_aff[0]
        for j in range(1, EXPERTS_PER_GROUP):
            better = vals[j] > bv
            bv = jnp.where(better, vals[j], bv)
            bi = jnp.where(better, j, bi)
            ba = jnp.where(better, cand_aff[j], ba)
        return bi, ba

    i1, a1 = argmax_first(cand)
    rest = [jnp.where(i1 == j, -jnp.inf, cand[j]) for j in range(EXPERTS_PER_GROUP)]
    i2, a2 = argmax_first(rest)
    denom = a1 + a2
    base = gidx * EXPERTS_PER_GROUP
    e_ref[0] = base + i1
    e_ref[1] = base + i2
    w_ref[0] = a1 / denom
    w_ref[1] = a2 / denom


def _route(logits, b_router):
    return pl.pallas_call(
        _route_kernel,
        out_shape=[jax.ShapeDtypeStruct((2, ROUTE_ROWS, 128), jnp.int32),
                   jax.ShapeDtypeStruct((2, ROUTE_ROWS, 128), F32)],
        compiler_params=pltpu.CompilerParams(vmem_limit_bytes=VMEM_LIMIT),
        name="route",
    )(logits.reshape(N_EXPERTS, ROUTE_ROWS, 128), b_router.reshape(N_EXPERTS, 1, 1))


def _mods_spec(tile, first_tile=0):
    return pl.BlockSpec((1, 6, D_MODEL), lambda t: (_cond_row(first_tile + t, tile), 0, 0))


def _pool_kernel(xa_ref, xb_ref, xp_ref, xn_ref, mods_ref, pw_ref, ps_ref, lng_ref, lnb_ref, w2_ref,
                 x1_ref, h2p_ref, logit_ref):
    t = pl.program_id(0)
    m = mods_ref[0]
    sh1, sc1 = m[0:1, :], m[1:2, :]
    subs = POOL_TILE // TILE
    in_lat = t >= T_CTX // POOL_TILE
    tile_in_seq = (t - T_CTX // POOL_TILE) % (DEC_SEQ // POOL_TILE)
    seq_len = jnp.where(in_lat, DEC_SEQ, SEQ)

    x = jnp.where(in_lat, xb_ref[...], xa_ref[...])
    h = x * (1.0 + sc1) + sh1
    halo_before = xp_ref[...] * (1.0 + sc1) + sh1
    halo_after = xn_ref[...] * (1.0 + sc1) + sh1
    ext = TILE + 2 * POOL_HALO

    def mixer(j):
        place = tile_in_seq * subs + j
        is_first = jnp.logical_or(jnp.logical_not(in_lat), place == 0)
        is_last = jnp.logical_or(jnp.logical_not(in_lat), place == TILES_PER_LAT_SEQ - 1)
        before = halo_before if j == 0 else h[j * TILE - POOL_HALO:j * TILE]
        after = halo_after if j == subs - 1 else h[(j + 1) * TILE:(j + 1) * TILE + POOL_HALO]
        hj = h[j * TILE:(j + 1) * TILE]
        hext = jnp.concatenate([jnp.where(is_first, 0.0, before), hj, jnp.where(is_last, 0.0, after)], axis=0)
        pos = jnp.where(in_lat, place * TILE, 0) + lax.broadcasted_iota(jnp.int32, (TILE, 1), 0)
        outs = []
        for g, w in enumerate(POOL_SIZES):
            lo_c, hi_c = g * POOL_GROUP_DIM, (g + 1) * POOL_GROUP_DIM
            a = hext[:, lo_c:hi_c]
            k = 1
            while k < w:
                a = a + pltpu.roll(a, ext - k, axis=0)
                k *= 2
            off = POOL_HALO - w // 2
            win = pltpu.roll(a, ext - off, axis=0)[:TILE] if off else a[:TILE]
            lo = jnp.maximum(pos - w // 2, 0)
            hi = jnp.minimum(pos - w // 2 + w, seq_len)
            cnt = (hi - lo).astype(F32)
            pooled = win / cnt - hj[:, lo_c:hi_c]
            outs.append(jnp.dot(pooled.astype(BF16), pw_ref[g], preferred_element_type=F32))
        return jnp.concatenate(outs, axis=1) * ps_ref[...]

    mix = mixer(0)
    for j in range(subs):
        nxt = mixer(j + 1) if j + 1 < subs else None
        rs = slice(j * TILE, (j + 1) * TILE)
        _epilogue(x[rs, :], mix, m, lng_ref[...], lnb_ref[...], w2_ref,
                  x1_ref.at[rs, :], h2p_ref.at[rs, :], logit_ref.at[:, rs])
        mix = nxt


def _pool_layer(x_ctx, x_lat, mods, pool_w, pool_scale, lng, lnb, w2):
    ctx_tiles = T_CTX // POOL_TILE
    halo_blocks = POOL_TILE // POOL_HALO
    last_halo = T_LAT // POOL_HALO - 1
    return pl.pallas_call(
        _pool_kernel,
        grid=(T_ALL // POOL_TILE,),
        in_specs=[
            pl.BlockSpec((POOL_TILE, D_MODEL), lambda t: (jnp.minimum(t, ctx_tiles - 1), 0)),
            pl.BlockSpec((POOL_TILE, D_MODEL), lambda t: (jnp.maximum(t - ctx_tiles, 0), 0)),
            pl.BlockSpec((POOL_HALO, D_MODEL), lambda t: (jnp.maximum((t - ctx_tiles) * halo_blocks - 1, 0), 0)),
            pl.BlockSpec((POOL_HALO, D_MODEL),
                         lambda t: (jnp.clip((t - ctx_tiles + 1) * halo_blocks, 0, last_halo), 0)),
            _mods_spec(POOL_TILE),
            pl.BlockSpec((len(POOL_SIZES), POOL_GROUP_DIM, POOL_GROUP_DIM), lambda t: (0, 0, 0)),
            pl.BlockSpec((1, D_MODEL), lambda t: (0, 0)),
        ] + _EPI_IN_SPECS,
        out_specs=[
            pl.BlockSpec((POOL_TILE, D_MODEL), lambda t: (t, 0)),
            pl.BlockSpec((POOL_TILE, D_PACK), lambda t: (t, 0)),
            pl.BlockSpec((N_EXPERTS, POOL_TILE), lambda t: (0, t)),
        ],
        out_shape=_EPI_OUT_SHAPE,
        compiler_params=_params("arbitrary"),
        name="pool_mixer",
    )(x_ctx, x_lat, x_lat, x_lat, mods, pool_w, pool_scale, lng, lnb, w2)


def _moe_rows(n):
    return 2 * n + N_EXPERTS * MOE_TILE


def _route_plan(eidx):
    n = eidx.shape[1]
    tiles = _moe_rows(n) // MOE_TILE
    eidx = eidx.reshape(2 * n)
    experts = jnp.arange(N_EXPERTS, dtype=jnp.int32)
    onehot = (eidx[:, None] == experts[None, :]).astype(jnp.int32)
    counts = jnp.sum(onehot, axis=0)
    rank = jnp.sum((jnp.cumsum(onehot, axis=0) - onehot) * onehot, axis=1)
    padded = (counts + MOE_TILE - 1) // MOE_TILE * MOE_TILE
    ends = jnp.cumsum(padded)
    starts = ends - padded
    pos = jnp.sum(onehot * starts[None, :], axis=1) + rank
    tile_start = jnp.arange(tiles, dtype=jnp.int32)[:, None] * MOE_TILE
    in_region = ((tile_start >= starts[None, :]) & (tile_start < ends[None, :])).astype(jnp.int32)
    used = jnp.sum(in_region, axis=1) > 0
    last_expert = jnp.max(jnp.where(counts > 0, experts, 0))
    tile_expert = jnp.where(used, jnp.sum(in_region * experts[None, :], axis=1), last_expert)
    first_block = starts // MOE_TILE
    block_in_expert = padded[None, :] // MOE_TILE - 1 - (tile_start - starts[None, :]) // MOE_TILE
    tile_valid = jnp.sum(in_region * jnp.clip(counts[None, :] - block_in_expert * MOE_TILE, 0, MOE_TILE), axis=1)
    tile_block = jnp.where(used, jnp.sum(in_region * (first_block[None, :] + block_in_expert), axis=1),
                           jnp.sum(jnp.where(experts == last_expert, first_block, 0)))
    return pos.astype(jnp.int32), tile_expert.astype(jnp.int32), tile_valid.astype(jnp.int32), \
        tile_block.astype(jnp.int32)


def _expert_kernel(te_ref, nv_ref, tb_ref, x_ref, wg_ref, wu_ref, wd_ref, y_ref, wgu_s, wd_s):
    i = pl.program_id(0)
    prev = te_ref[jnp.maximum(i - 1, 0)]
    changed = jnp.logical_or(i == 0, te_ref[i] != prev)

    @pl.when(changed)
    def _():
        wgu_s[:, :D_FF] = wg_ref[0, 0].astype(BF16)
        wgu_s[:, D_FF:] = wu_ref[0, 0].astype(BF16)
        wd_s[...] = wd_ref[0, 0].astype(BF16)

    nv = nv_ref[i]

    def gate_up(r0):
        rows = r0 + lax.broadcasted_iota(jnp.int32, (MOE_SUB, 1), 0)
        xp = jnp.where(rows < nv, x_ref[r0:r0 + MOE_SUB, :], jnp.uint32(0))
        left, right = _unpack_halves(xp)
        xb = jnp.concatenate([left.astype(BF16), right.astype(BF16)], axis=1)
        return jnp.dot(xb, wgu_s[...], preferred_element_type=F32)

    def down(r0, gu):
        gate, up = gu[:, :D_FF], gu[:, D_FF:]
        he = (gate * jax.nn.sigmoid(gate) * up).astype(BF16)
        y_ref[r0:r0 + MOE_SUB, :] = _pack_halves(jnp.dot(he, wd_s[...], preferred_element_type=F32).astype(BF16))

    def run(n_sub):
        gu = gate_up(0)
        for j in range(n_sub):
            nxt = gate_up((j + 1) * MOE_SUB) if j + 1 < n_sub else None
            down(j * MOE_SUB, gu)
            gu = nxt

    n_subs = MOE_TILE // MOE_SUB
    for n_sub in range(1, n_subs + 1):
        lo = (n_sub - 1) * MOE_SUB
        in_range = nv > lo if n_sub == n_subs else jnp.logical_and(nv > lo, nv <= lo + MOE_SUB)
        pl.when(in_range)(functools.partial(run, n_sub))


def _experts(xs, tile_expert, tile_valid, tile_block, layer, w_gate, w_up, w_down):
    grid_spec = pltpu.PrefetchScalarGridSpec(
        num_scalar_prefetch=3,
        grid=(xs.shape[0] // MOE_TILE,),
        in_specs=[
            pl.BlockSpec((MOE_TILE, D_PACK), lambda i, te, nv, tb: (tb[i], 0)),
            pl.BlockSpec((1, 1, D_MODEL, D_FF), lambda i, te, nv, tb: (layer, te[i], 0, 0)),
            pl.BlockSpec((1, 1, D_MODEL, D_FF), lambda i, te, nv, tb: (layer, te[i], 0, 0)),
            pl.BlockSpec((1, 1, D_FF, D_MODEL), lambda i, te, nv, tb: (layer, te[i], 0, 0)),
        ],
        out_specs=pl.BlockSpec((MOE_TILE, D_PACK), lambda i, te, nv, tb: (tb[i], 0)),
        scratch_shapes=[pltpu.VMEM((D_MODEL, 2 * D_FF), BF16), pltpu.VMEM((D_FF, D_MODEL), BF16)],
    )
    return pl.pallas_call(
        _expert_kernel,
        grid_spec=grid_spec,
        out_shape=jax.ShapeDtypeStruct(xs.shape, jnp.uint32),
        compiler_params=_params("arbitrary"),
        name="experts",
    )(tile_expert, tile_valid, tile_block, xs, w_gate, w_up, w_down)


SC_WORKERS = 32


def _sc_mesh():
    return plsc.VectorSubcoreMesh(core_axis_name="core", subcore_axis_name="subcore")


def _sc_worker():
    return lax.axis_index("subcore") * 2 + lax.axis_index("core")


def _dispatch(h2, pos):
    n = h2.shape[0]
    per_worker = n // SC_WORKERS
    pairs = per_worker // (2 * SC_WINDOW)
    one_set = [pltpu.VMEM((SC_WINDOW,), jnp.int32), pltpu.VMEM((SC_WINDOW,), jnp.int32),
               pltpu.VMEM((SC_WINDOW, D_PACK), jnp.uint32),
               pltpu.SemaphoreType.DMA, pltpu.SemaphoreType.DMA, pltpu.SemaphoreType.DMA]

    @functools.partial(pl.kernel, out_type=jax.ShapeDtypeStruct((_moe_rows(n), D_PACK), jnp.uint32),
                       mesh=_sc_mesh(), scratch_types=one_set + one_set, name="moe_dispatch")
    def scatter(x_hbm, i_hbm, o_hbm, *bufs):
        base = _sc_worker() * per_worker
        sets = (bufs[:6], bufs[6:])

        def rows_in(j, s):
            _, _, rows, sem, _, _ = sets[s]
            return pltpu.make_async_copy(x_hbm.at[pl.ds(base + j * SC_WINDOW, SC_WINDOW)], rows, sem)

        def rows_out(k, s):
            idx, rows, sem = sets[s][k], sets[s][2], sets[s][4 + k]
            return pltpu.make_async_copy(rows, o_hbm.at[idx], sem)

        def fetch(j, s):
            t0 = base + j * SC_WINDOW
            pltpu.sync_copy(i_hbm.at[pl.ds(t0, SC_WINDOW)], sets[s][0])
            pltpu.sync_copy(i_hbm.at[pl.ds(n + t0, SC_WINDOW)], sets[s][1])
            rows_in(j, s).start()

        fetch(0, 0)

        @pl.loop(0, pairs)
        def _(g):
            j0, j1 = 2 * g, 2 * g + 1
            fetch(j1, 1)
            rows_in(j0, 0).wait()
            rows_out(0, 0).start()
            rows_out(1, 0).start()
            rows_in(j1, 1).wait()
            rows_out(0, 1).start()
            rows_out(1, 1).start()
            rows_out(0, 0).wait()
            rows_out(1, 0).wait()

            @pl.when(g + 1 < pairs)
            def _():
                fetch(j0 + 2, 0)

            rows_out(0, 1).wait()
            rows_out(1, 1).wait()

    return scatter(h2, pos)


def _gather_pairs(y, pos):
    n = pos.shape[0] // 2
    per_worker = 2 * n // SC_WORKERS
    pairs = per_worker // (2 * SC_WINDOW)
    scratch = [pltpu.VMEM((per_worker,), jnp.int32),
               pltpu.VMEM((SC_WINDOW, D_PACK), jnp.uint32), pltpu.VMEM((SC_WINDOW, D_PACK), jnp.uint32),
               pltpu.SemaphoreType.DMA, pltpu.SemaphoreType.DMA, pltpu.SemaphoreType.DMA, pltpu.SemaphoreType.DMA]

    @functools.partial(pl.kernel, out_type=jax.ShapeDtypeStruct((2 * n, D_PACK), jnp.uint32),
                       mesh=_sc_mesh(), scratch_types=scratch, name="moe_gather")
    def gather(y_hbm, i_hbm, o_hbm, idx_v, rows0, rows1, read0, read1, write0, write1):
        base = _sc_worker() * per_worker
        pltpu.sync_copy(i_hbm.at[pl.ds(base, per_worker)], idx_v)

        def read(j, rows, sem):
            return pltpu.make_async_copy(y_hbm.at[idx_v.at[pl.ds(j * SC_WINDOW, SC_WINDOW)]], rows, sem)

        def write(j, rows, sem):
            return pltpu.make_async_copy(rows, o_hbm.at[pl.ds(base + j * SC_WINDOW, SC_WINDOW)], sem)

        read(0, rows0, read0).start()

        @pl.loop(0, pairs)
        def _(g):
            j0, j1 = 2 * g, 2 * g + 1
            read(j1, rows1, read1).start()
            read(j0, rows0, read0).wait()
            write(j0, rows0, write0).start()
            read(j1, rows1, read1).wait()
            write(j1, rows1, write1).start()
            write(j0, rows0, write0).wait()

            @pl.when(g + 1 < pairs)
            def _():
                read(j0 + 2, rows0, read0).start()

            write(j1, rows1, write1).wait()

    return gather(y, pos).reshape(2, n, D_PACK)


def _moe_post_norm(x_ref, yg_ref, w_ref, mods_ref, lng_ref, lnb_ref):
    g2 = mods_ref[0][5:6, :]
    groups = x_ref.shape[0] // LANES
    wt = jnp.concatenate([w_ref[0, 0], w_ref[1, 0]], axis=0).T
    moe = []
    for r in range(groups):
        rows = slice(r * LANES, (r + 1) * LANES)
        y1 = jnp.concatenate(_unpack_halves(yg_ref[0, rows, :]), axis=1)
        y2 = jnp.concatenate(_unpack_halves(yg_ref[1, rows, :]), axis=1)
        moe.append(wt[:, r:r + 1] * y1 + wt[:, groups + r:groups + r + 1] * y2)
    return _post_norm(x_ref[...], g2 * jnp.concatenate(moe, axis=0), lng_ref[...], lnb_ref[...])


def _combine_kernel(x_ref, yg_ref, w_ref, mods_ref, lng_ref, lnb_ref, o_ref):
    o_ref[...] = _moe_post_norm(x_ref, yg_ref, w_ref, mods_ref, lng_ref, lnb_ref)


def _combine(x1, yg, wsel, mods, lng, lnb, t0, n):
    off = t0 // LN_TILE
    return pl.pallas_call(
        _combine_kernel,
        grid=(n // LN_TILE,),
        in_specs=[
            pl.BlockSpec((LN_TILE, D_MODEL), lambda t: (off + t, 0)),
            pl.BlockSpec((2, LN_TILE, D_PACK), lambda t: (0, off + t, 0)),
            pl.BlockSpec((2, 1, LN_TILE // LANES, LANES), lambda t: (0, off + t, 0, 0)),
            _mods_spec(LN_TILE, off),
            pl.BlockSpec((1, D_MODEL), lambda t: (0, 0)),
            pl.BlockSpec((1, D_MODEL), lambda t: (0, 0)),
        ],
        out_specs=pl.BlockSpec((LN_TILE, D_MODEL), lambda t: (t, 0)),
        out_shape=jax.ShapeDtypeStruct((n, D_MODEL), F32),
        compiler_params=_params("arbitrary"),
        name="moe_combine",
    )(x1, yg, wsel.reshape(2, T_ALL // LN_TILE, LN_TILE // LANES, LANES), mods, lng, lnb)


def _moe_rows_of_tokens(h2p, logits, b_router, layer, w_gate, w_up, w_down):
    eidx, wsel = _route(logits, b_router)
    eidx = eidx.reshape(2, T_ALL)
    pos, tile_expert, tile_valid, tile_block = _route_plan(eidx)
    xs = _dispatch(h2p, pos)
    y = _experts(xs, tile_expert, tile_valid, tile_block, layer, w_gate, w_up, w_down)
    return _gather_pairs(y, pos), wsel


def _qkv_kernel(x1_ref, yg_ref, wsel_ref, mods0_ref, lng_ref, lnb_ref, mods_ref, w_ref,
                x_ref, qkv_ref, nk_ref, nv_ref, w_s):
    t = pl.program_id(0)

    @pl.when(t == 0)
    def _():
        w_s[...] = w_ref[0].astype(BF16)

    m = mods_ref[0]
    sh1, sc1 = m[0:1, :], m[1:2, :]

    def norm(rows):
        x = _moe_post_norm(x1_ref.at[rows], yg_ref.at[:, rows], wsel_ref, mods0_ref, lng_ref, lnb_ref)
        x_ref[rows, :] = x
        return (x * (1.0 + sc1) + sh1).astype(BF16)

    def project(rows, h):
        r = jnp.dot(h, w_s[...], preferred_element_type=F32)
        qkv_ref[rows, :D_MODEL] = (r[:, :D_MODEL] * (HEAD_DIM ** -0.5)).astype(BF16)
        qkv_ref[rows, D_MODEL:] = r[:, D_MODEL:].astype(BF16)
        return r

    is_ctx = t < T_CTX // QKV_TILE
    rows = slice(0, QKV_TILE)

    @pl.when(is_ctx)
    def _():
        r = project(rows, norm(rows))
        for out_ref, base in ((nk_ref, D_MODEL), (nv_ref, 2 * D_MODEL)):
            for b in range(QKV_TILE // SEQ):
                for p in range(HEAD_PAIRS):
                    pair = r[b * SEQ:(b + 1) * SEQ, base + p * LANES: base + (p + 1) * LANES].T
                    out_ref[b, 0, 2 * p] = pair[:HEAD_DIM]
                    out_ref[b, 0, 2 * p + 1] = pair[HEAD_DIM:]

    @pl.when(jnp.logical_not(is_ctx))
    def _():
        project(rows, norm(rows))


def _qkv(x1, yg, wsel, mods0, lng, lnb, mods1, w_qkv):
    seqs = QKV_TILE // SEQ
    row_spec = pl.BlockSpec((QKV_TILE, D_MODEL), lambda t: (t, 0))
    vec_spec = pl.BlockSpec((1, D_MODEL), lambda t: (0, 0))

    cache_spec = pl.BlockSpec((seqs, 1, N_HEADS, HEAD_DIM, SEQ),
                              lambda t: (jnp.minimum(t, T_CTX // QKV_TILE - 1), 0, 0, 0, 0))
    cache_shape = jax.ShapeDtypeStruct((BATCH, 1, N_HEADS, HEAD_DIM, SEQ), F32)
    return pl.pallas_call(
        _qkv_kernel,
        grid=(T_ALL // QKV_TILE,),
        in_specs=[
            row_spec,
            pl.BlockSpec((2, QKV_TILE, D_PACK), lambda t: (0, t, 0)),
            pl.BlockSpec((2, 1, QKV_TILE // LANES, LANES), lambda t: (0, t, 0, 0)),
            _mods_spec(QKV_TILE),
            vec_spec,
            vec_spec,
            _mods_spec(QKV_TILE),
            pl.BlockSpec((1, D_MODEL, 3 * D_MODEL), lambda t: (0, 0, 0)),
        ],
        out_specs=[row_spec, pl.BlockSpec((QKV_TILE, 3 * D_MODEL), lambda t: (t, 0)), cache_spec, cache_spec],
        out_shape=[jax.ShapeDtypeStruct((T_ALL, D_MODEL), F32),
                   jax.ShapeDtypeStruct((T_ALL, 3 * D_MODEL), BF16), cache_shape, cache_shape],
        scratch_shapes=[pltpu.VMEM((D_MODEL, 3 * D_MODEL), BF16)],
        compiler_params=_params("arbitrary"),
        name="qkv",
    )(x1, yg, wsel.reshape(2, T_ALL // QKV_TILE, QKV_TILE // LANES, LANES), mods0, lng, lnb, mods1, w_qkv)


def _dot_nt(a, b):
    return lax.dot_general(a, b, (((1,), (1,)), ((), ())), preferred_element_type=F32)


def _head_masks():
    lane = lax.broadcasted_iota(jnp.int32, (1, LANES), 1)
    return lane < HEAD_DIM, lane >= HEAD_DIM


def _ctx_attn_kernel(q_ref, k_ref, v_ref, o_ref):
    left, right = _head_masks()
    for p in range(HEAD_PAIRS):
        cols = slice(p * LANES, (p + 1) * LANES)
        q2, k2, v2 = q_ref[:, cols], k_ref[:, cols], v_ref[:, cols]
        halves = []
        for mask in (left, right):
            qh = jnp.where(mask, q2, jnp.zeros_like(q2))
            s = _dot_nt(qh, k2)
            e = jnp.exp(s - jnp.max(s, axis=-1, keepdims=True))
            o2 = jnp.dot(e.astype(BF16), v2, preferred_element_type=F32)
            halves.append(o2 / jnp.sum(e, axis=-1, keepdims=True))
        o_ref[:, cols] = jnp.where(left, halves[0], halves[1]).astype(BF16)


def _ctx_attention(qkv):
    return pl.pallas_call(
        _ctx_attn_kernel,
        grid=(BATCH,),
        in_specs=[pl.BlockSpec((SEQ, D_MODEL), lambda b, j=j: (b, j)) for j in range(3)],
        out_specs=pl.BlockSpec((SEQ, D_MODEL), lambda b: (b, 0)),
        out_shape=jax.ShapeDtypeStruct((T_ALL, D_MODEL), BF16),
        compiler_params=_params("arbitrary"),
        name="ctx_attention",
    )(qkv, qkv, qkv)


_LAT_Q_BLOCK_ROWS = 4
_LAT_KEY_ROWS = ((0, 8), (0, 12), (4, 16), (8, 16))


def _softmax_rows(s_ref, p_ref, l_ref):
    sc = s_ref[...]
    e = jnp.exp(sc - jnp.max(sc, axis=-1, keepdims=True))
    l_ref[...] = jnp.sum(e, axis=-1, keepdims=True)
    p_ref[...] = e.astype(BF16)


def _lat_attn_kernel(q_ref, k_ref, v_ref, ck_ref, cv_ref, eb_ref, o_ctx_ref, o_ref, bias_s, s_scr, p_scr, l_scr):
    del o_ctx_ref
    left, right = _head_masks()

    @pl.when(pl.program_id(1) == 0)
    def _():
        neg = jnp.full((GRID_W, LANES), NEG_INF, F32)
        for hh in range(2):
            for r in range(GRID_ROWS):
                r0 = min(max(r - WIN_H // 2, 0), GRID_ROWS - WIN_H)
                for j in range(GRID_ROWS // 2):
                    parts = []
                    for kr in (2 * j, 2 * j + 1):
                        parts.append(eb_ref[hh, kr - r + WIN_H - 1] if r0 <= kr < r0 + WIN_H else None)
                    if parts[0] is None and parts[1] is None:
                        val = neg
                    else:
                        val = jnp.where(left, neg if parts[0] is None else parts[0],
                                        neg if parts[1] is None else parts[1])
                    bias_s[hh, r * GRID_W:(r + 1) * GRID_W, j * LANES:(j + 1) * LANES] = val

    ck = jnp.concatenate([ck_ref[0, 0, 0], ck_ref[0, 0, 1]], axis=0).astype(BF16)
    cv = jnp.concatenate([cv_ref[0, 0, 0], cv_ref[0, 0, 1]], axis=0).astype(BF16)
    qrows = _LAT_Q_BLOCK_ROWS * GRID_W
    units = [(qb, hh) for qb in range(len(_LAT_KEY_ROWS)) for hh in range(2)]

    def refs(u):
        nk = (_LAT_KEY_ROWS[units[u][0]][1] - _LAT_KEY_ROWS[units[u][0]][0]) * GRID_W
        width = nk + PAST_LEN
        return nk, s_scr.at[u % 2, :, :width], p_scr.at[u % 2, :, :width], l_scr.at[u % 2]

    def scores(u):
        qb, hh = units[u]
        kr0, kr1 = _LAT_KEY_ROWS[qb]
        qs, ks = slice(qb * qrows, (qb + 1) * qrows), slice(kr0 * GRID_W, kr1 * GRID_W)
        nk, s_ref, _, _ = refs(u)
        q2 = q_ref[qs, :]
        qh = jnp.where(right if hh else left, q2, jnp.zeros_like(q2))
        s_ref[:, :nk] = _dot_nt(qh, k_ref[ks, :]) + bias_s[hh, qs, ks]
        s_ref[:, nk:] = jnp.dot(qh, ck, preferred_element_type=F32)

    def weighted_values(u):
        qb, _ = units[u]
        kr0, kr1 = _LAT_KEY_ROWS[qb]
        nk, _, p_ref, l_ref = refs(u)
        o2 = (jnp.dot(p_ref[:, :nk], v_ref[kr0 * GRID_W:kr1 * GRID_W, :], preferred_element_type=F32)
              + _dot_nt(p_ref[:, nk:], cv))
        return o2 / l_ref[...]

    scores(0)
    halves = []
    for u in range(len(units)):
        if u + 1 < len(units):
            scores(u + 1)
        _, s_ref, p_ref, l_ref = refs(u)
        _softmax_rows(s_ref, p_ref, l_ref)
        halves.append(weighted_values(u))
        if len(halves) == 2:
            qb = units[u][0]
            o_ref[qb * qrows:(qb + 1) * qrows, :] = jnp.where(left, halves[0], halves[1]).astype(BF16)
            halves = []


def _lat_attention(qkv, cache_kt, cache_vt, ebias, o_buf):
    row0 = T_CTX // DEC_SEQ
    qrows = _LAT_Q_BLOCK_ROWS * GRID_W
    max_keys = max(k1 - k0 for k0, k1 in _LAT_KEY_ROWS) * GRID_W + PAST_LEN
    cache_spec = pl.BlockSpec((1, 1, 2, HEAD_DIM, PAST_LEN), lambda p, b: (b, 0, p, 0, 0))
    return pl.pallas_call(
        _lat_attn_kernel,
        grid=(HEAD_PAIRS, DEC_BATCH),
        in_specs=[
            pl.BlockSpec((DEC_SEQ, LANES), lambda p, b: (row0 + b, p)),
            pl.BlockSpec((DEC_SEQ, LANES), lambda p, b: (row0 + b, HEAD_PAIRS + p)),
            pl.BlockSpec((DEC_SEQ, LANES), lambda p, b: (row0 + b, 2 * HEAD_PAIRS + p)),
            cache_spec,
            cache_spec,
            pl.BlockSpec((2, 2 * WIN_H - 1, GRID_W, LANES), lambda p, b: (p, 0, 0, 0)),
            pl.BlockSpec(memory_space=pl.ANY),
        ],
        out_specs=pl.BlockSpec((DEC_SEQ, LANES), lambda p, b: (row0 + b, p)),
        out_shape=jax.ShapeDtypeStruct((T_ALL, D_MODEL), BF16),
        scratch_shapes=[pltpu.VMEM((2, DEC_SEQ, DEC_SEQ), F32),
                        pltpu.VMEM((2, qrows, max_keys), F32),
                        pltpu.VMEM((2, qrows, max_keys), BF16),
                        pltpu.VMEM((2, qrows, 1), F32)],
        input_output_aliases={6: 0},
        compiler_params=_params("arbitrary", "arbitrary"),
        name="lat_attention",
    )(qkv, qkv, qkv, cache_kt, cache_vt, ebias, o_buf)


def _expanded_bias(rel_bias):
    col = jnp.arange(GRID_W)
    col_start = jnp.clip(col - WIN_W // 2, 0, GRID_W - WIN_W)
    col_ok = (col[None, :] >= col_start[:, None]) & (col[None, :] < col_start[:, None] + WIN_W)
    dx = jnp.clip(col[None, :] - col[:, None] + (WIN_W - 1), 0, 2 * WIN_W - 2)
    onehot = (dx[None] == jnp.arange(2 * WIN_W - 1)[:, None, None]).astype(F32)
    eb = jnp.einsum("hyj,jqk->hyqk", rel_bias.astype(F32), onehot, precision=lax.Precision.HIGHEST)
    eb = jnp.where(col_ok[None, None], eb, NEG_INF)
    return jnp.concatenate([eb, eb], axis=-1)


def _wo_kernel(o_ref, x_ref, mods_ref, w_ref, lng_ref, lnb_ref, w2_ref, x1_ref, h2p_ref, logit_ref, w_s):
    @pl.when(pl.program_id(0) == 0)
    def _():
        w_s[...] = w_ref[0].astype(BF16)

    subs = [slice(r0, r0 + TILE) for r0 in range(0, WO_TILE, TILE)]
    mix = jnp.dot(o_ref[subs[0], :], w_s[...], preferred_element_type=F32)
    for j, rs in enumerate(subs):
        nxt = jnp.dot(o_ref[subs[j + 1], :], w_s[...], preferred_element_type=F32) if j + 1 < len(subs) else None
        _epilogue(x_ref[rs, :], mix, mods_ref[0], lng_ref[...], lnb_ref[...], w2_ref,
                  x1_ref.at[rs, :], h2p_ref.at[rs, :], logit_ref.at[:, rs])
        mix = nxt


def _attn_out(o, x, mods, w_o, lng, lnb, w2):
    return pl.pallas_call(
        _wo_kernel,
        grid=(T_ALL // WO_TILE,),
        in_specs=[
            pl.BlockSpec((WO_TILE, D_MODEL), lambda t: (t, 0)),
            pl.BlockSpec((WO_TILE, D_MODEL), lambda t: (t, 0)),
            _mods_spec(WO_TILE),
            pl.BlockSpec((1, D_MODEL, D_MODEL), lambda t: (0, 0, 0)),
        ] + _EPI_IN_SPECS,
        out_specs=[
            pl.BlockSpec((WO_TILE, D_MODEL), lambda t: (t, 0)),
            pl.BlockSpec((WO_TILE, D_PACK), lambda t: (t, 0)),
            pl.BlockSpec((N_EXPERTS, WO_TILE), lambda t: (0, t)),
        ],
        out_shape=_EPI_OUT_SHAPE,
        scratch_shapes=[pltpu.VMEM((D_MODEL, D_MODEL), BF16)],
        compiler_params=_params("arbitrary"),
        name="attn_out",
    )(o, x, mods, w_o, lng, lnb, w2)


def kernel(x_prompt, x_sample, cache_k, cache_v, c, c_ctx, w_ada, b_ada, ln1_g, ln1_b, ln2_g, ln2_b,
           pool_w, pool_scale, w_qkv, w_o, rel_bias, w_router, b_router, w_gate, w_up, w_down):
    cond = jnp.zeros((N_COND, D_MODEL), F32).at[0].set(c_ctx).at[1:1 + DEC_BATCH].set(c)
    mods0, mods1 = _ada(cond, w_ada, b_ada, 0), _ada(cond, w_ada, b_ada, 1)
    wrt = w_router.T
    wrt_hi = wrt.astype(BF16)
    w2 = jnp.concatenate([wrt_hi, (wrt - wrt_hi.astype(F32)).astype(BF16)], axis=0)

    x, h2p, logits = _pool_layer(x_prompt.reshape(T_CTX, D_MODEL), x_sample.reshape(T_LAT, D_MODEL), mods0,
                                 pool_w[0].astype(BF16), pool_scale[0:1], ln1_g[0:1], ln1_b[0:1], w2)
    yg, wsel = _moe_rows_of_tokens(h2p, logits, b_router, 0, w_gate, w_up, w_down)

    x, qkv, new_kt, new_vt = _qkv(x, yg, wsel, mods0, ln2_g[0:1], ln2_b[0:1], mods1, w_qkv)
    o = _lat_attention(qkv, cache_k.transpose(0, 1, 2, 4, 3), cache_v.transpose(0, 1, 2, 4, 3),
                       _expanded_bias(rel_bias[0]), _ctx_attention(qkv))
    x, h2p, logits = _attn_out(o, x, mods1, w_o, ln1_g[1:2], ln1_b[1:2], w2)
    yg, wsel = _moe_rows_of_tokens(h2p, logits, b_router, 1, w_gate, w_up, w_down)
    y_ctx, y_lat = [_combine(x, yg, wsel, mods1, ln2_g[1:2], ln2_b[1:2], t0, n) for t0, n in SEGMENTS]
    return (y_ctx.reshape(BATCH, SEQ, D_MODEL), y_lat.reshape(DEC_BATCH, DEC_SEQ, D_MODEL),
            new_kt.transpose(0, 1, 2, 4, 3), new_vt.transpose(0, 1, 2, 4, 3))
```

```python
import functools

import jax
import jax.numpy as jnp
from jax import lax
from jax.experimental import pallas as pl
from jax.experimental.pallas import tpu as pltpu
from jax.experimental.pallas import tpu_sc as plsc

F32 = jnp.float32
BF16 = jnp.bfloat16

D_MODEL = 1024
BATCH = 16
SEQ = 256
DEC_BATCH = 8
DEC_SEQ = 1024
PAST_LEN = 512
GRID_W = 64
GRID_ROWS = DEC_SEQ // GRID_W
POOL_SIZES = (2, 4, 8, 16)
POOL_GROUP_DIM = D_MODEL // len(POOL_SIZES)
POOL_HALO = 8
N_HEADS = 16
HEAD_DIM = 64
WIN_H = 8
WIN_W = 16
N_EXPERTS = 16
EXPERTS_PER_GROUP = 4
N_EXPERT_GROUPS = 4
D_FF = 512
ALPHA = (2.0 * 2) ** 0.25
LN_EPS = 1e-5
NEG_INF = -1e30

T_CTX = BATCH * SEQ
T_LAT = DEC_BATCH * DEC_SEQ
T_ALL = T_CTX + T_LAT
N_COND = 16
TILE = 256
TILES_PER_LAT_SEQ = DEC_SEQ // TILE
MOE_TILE = 1024
MOE_SUB = 256
SEGMENTS = ((0, T_CTX), (T_CTX, T_LAT))
LN_TILE = 512
QKV_TILE = 512
POOL_TILE = 512
WO_TILE = 1024
SC_WINDOW = 96
D_PACK = D_MODEL // 2
ROUTE_ROWS = T_ALL // 128
HEAD_PAIRS = N_HEADS // 2
LANES = 128
VMEM_LIMIT = 56 * 1024 * 1024


def _cond_row(t, tile):
    ctx_tiles = T_CTX // tile
    per_seq = DEC_SEQ // tile
    return jnp.maximum(t - ctx_tiles + per_seq, 0) // per_seq


def _params(*sem):
    return pltpu.CompilerParams(dimension_semantics=sem, vmem_limit_bytes=VMEM_LIMIT)


def _pack_halves(v):
    half = v.shape[1] // 2
    hi = lax.bitcast_convert_type(v[:, :half].astype(F32), jnp.uint32)
    lo = lax.bitcast_convert_type(v[:, half:].astype(F32), jnp.uint32)
    return hi | (lo >> 16)


def _unpack_halves(p):
    left = lax.bitcast_convert_type(p & jnp.uint32(0xFFFF0000), F32)
    right = lax.bitcast_convert_type(p << 16, F32)
    return left, right


def _ada_kernel(cond_ref, w_ref, b_ref, o_ref):
    cnd = cond_ref[...]
    act = cnd * jax.nn.sigmoid(cnd)
    a_hi = act.astype(BF16)
    a_lo = (act - a_hi.astype(F32)).astype(BF16)
    w = w_ref[0]
    w_hi = w.astype(BF16)
    w_lo = (w - w_hi.astype(F32)).astype(BF16)
    a2 = jnp.concatenate([a_hi, a_lo], axis=0)
    p = jnp.dot(a2, w_hi, preferred_element_type=F32)
    q = jnp.dot(a_hi, w_lo, preferred_element_type=F32)
    o_ref[0] = p[:N_COND] + p[N_COND:] + q + b_ref[0]


def _ada(cond, w_ada, b_ada, layer):
    depth, d, n = w_ada.shape
    bn = 1536
    mods = pl.pallas_call(
        _ada_kernel,
        grid=(n // bn,),
        in_specs=[
            pl.BlockSpec((N_COND, d), lambda j: (0, 0)),
            pl.BlockSpec((1, d, bn), lambda j: (layer, 0, j)),
            pl.BlockSpec((1, 1, bn), lambda j: (layer, 0, j)),
        ],
        out_specs=pl.BlockSpec((1, N_COND, bn), lambda j: (0, 0, j)),
        out_shape=jax.ShapeDtypeStruct((1, N_COND, n), F32),
        compiler_params=_params("arbitrary"),
        name="ada",
    )(cond, w_ada, b_ada.reshape(depth, 1, n))
    return mods.reshape(N_COND, 6, D_MODEL)


def _post_norm(x, upd, g, b):
    y = ALPHA * x + upd
    mu = jnp.mean(y, axis=-1, keepdims=True)
    yc = y - mu
    var = jnp.mean(yc * yc, axis=-1, keepdims=True)
    return yc * lax.rsqrt(var + LN_EPS) * g + b


def _router_logits(h2, h_hi, w2_ref):
    h_lo = (h2 - h_hi.astype(F32)).astype(BF16)
    w2 = w2_ref[...]
    p = _dot_nt(w2, h_hi)
    q = _dot_nt(w2[:N_EXPERTS], h_lo)
    return p[:N_EXPERTS] + p[N_EXPERTS:] + q


def _epilogue(x, mix, m, lng, lnb, w2_ref, x1_ref, h2p_ref, logit_ref):
    g1, sh2, sc2 = m[2:3, :], m[3:4, :], m[4:5, :]
    x1 = _post_norm(x, g1 * mix, lng, lnb)
    h2 = x1 * (1.0 + sc2) + sh2
    h_hi = h2.astype(BF16)
    x1_ref[...] = x1
    h2p_ref[...] = _pack_halves(h_hi)
    logit_ref[...] = _router_logits(h2, h_hi, w2_ref)


_EPI_IN_SPECS = [
    pl.BlockSpec((1, D_MODEL), lambda t: (0, 0)),
    pl.BlockSpec((1, D_MODEL), lambda t: (0, 0)),
    pl.BlockSpec((2 * N_EXPERTS, D_MODEL), lambda t: (0, 0)),
]
_EPI_OUT_SHAPE = [
    jax.ShapeDtypeStruct((T_ALL, D_MODEL), F32),
    jax.ShapeDtypeStruct((T_ALL, D_PACK), jnp.uint32),
    jax.ShapeDtypeStruct((N_EXPERTS, T_ALL), F32),
]


def _route_kernel(logit_ref, br_ref, pos_ref, w_ref, table_ref):
    aff = jax.nn.sigmoid(logit_ref[...])
    sel = aff + br_ref[...]
    sel_rows = [sel[e] for e in range(N_EXPERTS)]
    aff_rows = [aff[e] for e in range(N_EXPERTS)]

    def group_score(g):
        r = sel_rows[g * EXPERTS_PER_GROUP:(g + 1) * EXPERTS_PER_GROUP]
        best = None
        for i in range(EXPERTS_PER_GROUP):
            for j in range(i + 1, EXPERTS_PER_GROUP):
                pair = r[i] + r[j]
                best = pair if best is None else jnp.maximum(best, pair)
        return best

    best = group_score(0)
    gidx = jnp.zeros_like(best, dtype=jnp.int32)
    for g in range(1, N_EXPERT_GROUPS):
        sc = group_score(g)
        better = sc > best
        gidx = jnp.where(better, g, gidx)
        best = jnp.where(better, sc, best)

    def pick_group(rows, j):
        out = rows[j]
        for g in range(1, N_EXPERT_GROUPS):
            out = jnp.where(gidx == g, rows[g * EXPERTS_PER_GROUP + j], out)
        return out

    cand = [pick_group(sel_rows, j) for j in range(EXPERTS_PER_GROUP)]
    cand_aff = [pick_group(aff_rows, j) for j in range(EXPERTS_PER_GROUP)]

    def argmax_first(vals):
        bv, bi, ba = vals[0], jnp.zeros_like(gidx), cand_aff[0]
        for j in range(1, EXPERTS_PER_GROUP):
            better = vals[j] > bv
            bv = jnp.where(better, vals[j], bv)
            bi = jnp.where(better, j, bi)
            ba = jnp.where(better, cand_aff[j], ba)
        return bi, ba

    i1, a1 = argmax_first(cand)
    rest = [jnp.where(i1 == j, -jnp.inf, cand[j]) for j in range(EXPERTS_PER_GROUP)]
    i2, a2 = argmax_first(rest)
    denom = a1 + a2
    base = gidx * EXPERTS_PER_GROUP
    w_ref[0] = a1 / denom
    w_ref[1] = a2 / denom
    _plan(base + i1, base + i2, pos_ref, table_ref)


def _plan(e1, e2, pos_ref, table_ref):
    rows = e1.shape[0]
    upper = (lax.broadcasted_iota(jnp.int32, (LANES, LANES), 0)
             <= lax.broadcasted_iota(jnp.int32, (LANES, LANES), 1)).astype(BF16)
    lower = (lax.broadcasted_iota(jnp.int32, (rows, rows), 1)
             < lax.broadcasted_iota(jnp.int32, (rows, rows), 0)).astype(BF16)
    masks = [(choice == e) for e in range(N_EXPERTS) for choice in (e1, e2)]
    hits = jnp.concatenate([m.astype(BF16) for m in masks], axis=0)
    within = jnp.dot(hits, upper, preferred_element_type=F32)
    totals = [within[i * rows:(i + 1) * rows, LANES - 1:LANES] for i in range(2 * N_EXPERTS)]
    spread = jnp.concatenate([jnp.broadcast_to(t, (rows, LANES)) for t in totals], axis=1).astype(BF16)
    above = jnp.dot(lower, spread, preferred_element_type=F32)

    tile_start = lax.broadcasted_iota(jnp.int32, (1, LANES), 1).astype(F32) * MOE_TILE
    start = jnp.zeros((1, 1), F32)
    pos = [jnp.zeros((rows, LANES), F32), jnp.zeros((rows, LANES), F32)]
    tile_expert = jnp.zeros((1, LANES), F32)
    tile_valid = jnp.zeros((1, LANES), F32)
    tile_block = jnp.zeros((1, LANES), F32)
    used = jnp.zeros((1, LANES), F32)
    last_expert = jnp.zeros((1, 1), F32)
    last_block = jnp.zeros((1, 1), F32)
    for e in range(N_EXPERTS):
        first = jnp.zeros((1, 1), F32)
        for c in range(2):
            i = 2 * e + c
            hit = masks[i]
            rank = (within[i * rows:(i + 1) * rows] - hit.astype(F32)
                    + above[:, i * LANES:(i + 1) * LANES] + (start + first))
            pos[c] = jnp.where(hit, rank, pos[c])
            first = first + jnp.sum(totals[i], axis=0, keepdims=True)
        count = first
        n_tiles = jnp.floor((count + (MOE_TILE - 1)) * (1.0 / MOE_TILE))
        end = start + n_tiles * MOE_TILE
        inside = jnp.logical_and(tile_start >= start, tile_start < end)
        block_in_expert = n_tiles - 1.0 - (tile_start - start) * (1.0 / MOE_TILE)
        tile_expert = jnp.where(inside, float(e), tile_expert)
        tile_valid = jnp.where(inside, jnp.clip(count - block_in_expert * MOE_TILE, 0.0, float(MOE_TILE)), tile_valid)
        tile_block = jnp.where(inside, start * (1.0 / MOE_TILE) + block_in_expert, tile_block)
        used = jnp.where(inside, 1.0, used)
        has_rows = count > 0.0
        last_expert = jnp.where(has_rows, float(e), last_expert)
        last_block = jnp.where(has_rows, start * (1.0 / MOE_TILE), last_block)
        start = end
    pos_ref[0] = pos[0].astype(jnp.int32)
    pos_ref[1] = pos[1].astype(jnp.int32)
    idle = used == 0.0
    table_ref[0:1, :] = jnp.where(idle, last_expert, tile_expert).astype(jnp.int32)
    table_ref[1:2, :] = tile_valid.astype(jnp.int32)
    table_ref[2:3, :] = jnp.where(idle, last_block, tile_block).astype(jnp.int32)
    table_ref[3:8, :] = jnp.zeros((5, LANES), jnp.int32)


def _route(logits, b_router):
    tiles = _moe_rows(T_ALL) // MOE_TILE
    pos, wsel, table = pl.pallas_call(
        _route_kernel,
        out_shape=[jax.ShapeDtypeStruct((2, ROUTE_ROWS, LANES), jnp.int32),
                   jax.ShapeDtypeStruct((2, ROUTE_ROWS, LANES), F32),
                   jax.ShapeDtypeStruct((8, LANES), jnp.int32)],
        compiler_params=pltpu.CompilerParams(vmem_limit_bytes=VMEM_LIMIT),
        name="route",
    )(logits.reshape(N_EXPERTS, ROUTE_ROWS, LANES), b_router.reshape(N_EXPERTS, 1, 1))
    return pos.reshape(2 * T_ALL), wsel, table[0, :tiles], table[1, :tiles], table[2, :tiles]


def _mods_spec(tile, first_tile=0):
    return pl.BlockSpec((1, 6, D_MODEL), lambda t: (_cond_row(first_tile + t, tile), 0, 0))


def _pool_kernel(xa_ref, xb_ref, xp_ref, xn_ref, mods_ref, pw_ref, ps_ref, lng_ref, lnb_ref, w2_ref,
                 x1_ref, h2p_ref, logit_ref):
    t = pl.program_id(0)
    m = mods_ref[0]
    sh1, sc1 = m[0:1, :], m[1:2, :]
    subs = POOL_TILE // TILE
    in_lat = t >= T_CTX // POOL_TILE
    tile_in_seq = (t - T_CTX // POOL_TILE) % (DEC_SEQ // POOL_TILE)
    seq_len = jnp.where(in_lat, DEC_SEQ, SEQ)

    x = jnp.where(in_lat, xb_ref[...], xa_ref[...])
    h = x * (1.0 + sc1) + sh1
    halo_before = xp_ref[...] * (1.0 + sc1) + sh1
    halo_after = xn_ref[...] * (1.0 + sc1) + sh1
    ext = TILE + 2 * POOL_HALO

    def mixer(j):
        place = tile_in_seq * subs + j
        is_first = jnp.logical_or(jnp.logical_not(in_lat), place == 0)
        is_last = jnp.logical_or(jnp.logical_not(in_lat), place == TILES_PER_LAT_SEQ - 1)
        before = halo_before if j == 0 else h[j * TILE - POOL_HALO:j * TILE]
        after = halo_after if j == subs - 1 else h[(j + 1) * TILE:(j + 1) * TILE + POOL_HALO]
        hj = h[j * TILE:(j + 1) * TILE]
        hext = jnp.concatenate([jnp.where(is_first, 0.0, before), hj, jnp.where(is_last, 0.0, after)], axis=0)
        pos = jnp.where(in_lat, place * TILE, 0) + lax.broadcasted_iota(jnp.int32, (TILE, 1), 0)
        outs = []
        for g, w in enumerate(POOL_SIZES):
            lo_c, hi_c = g * POOL_GROUP_DIM, (g + 1) * POOL_GROUP_DIM
            a = hext[:, lo_c:hi_c]
            k = 1
            while k < w:
                a = a + pltpu.roll(a, ext - k, axis=0)
                k *= 2
            off = POOL_HALO - w // 2
            win = pltpu.roll(a, ext - off, axis=0)[:TILE] if off else a[:TILE]
            lo = jnp.maximum(pos - w // 2, 0)
            hi = jnp.minimum(pos - w // 2 + w, seq_len)
            cnt = (hi - lo).astype(F32)
            pooled = win / cnt - hj[:, lo_c:hi_c]
            outs.append(jnp.dot(pooled.astype(BF16), pw_ref[g], preferred_element_type=F32))
        return jnp.concatenate(outs, axis=1) * ps_ref[...]

    mix = mixer(0)
    for j in range(subs):
        nxt = mixer(j + 1) if j + 1 < subs else None
        rs = slice(j * TILE, (j + 1) * TILE)
        _epilogue(x[rs, :], mix, m, lng_ref[...], lnb_ref[...], w2_ref,
                  x1_ref.at[rs, :], h2p_ref.at[rs, :], logit_ref.at[:, rs])
        mix = nxt


def _pool_layer(x_ctx, x_lat, mods, pool_w, pool_scale, lng, lnb, w2):
    ctx_tiles = T_CTX // POOL_TILE
    halo_blocks = POOL_TILE // POOL_HALO
    last_halo = T_LAT // POOL_HALO - 1
    return pl.pallas_call(
        _pool_kernel,
        grid=(T_ALL // POOL_TILE,),
        in_specs=[
            pl.BlockSpec((POOL_TILE, D_MODEL), lambda t: (jnp.minimum(t, ctx_tiles - 1), 0)),
            pl.BlockSpec((POOL_TILE, D_MODEL), lambda t: (jnp.maximum(t - ctx_tiles, 0), 0)),
            pl.BlockSpec((POOL_HALO, D_MODEL), lambda t: (jnp.maximum((t - ctx_tiles) * halo_blocks - 1, 0), 0)),
            pl.BlockSpec((POOL_HALO, D_MODEL),
                         lambda t: (jnp.clip((t - ctx_tiles + 1) * halo_blocks, 0, last_halo), 0)),
            _mods_spec(POOL_TILE),
            pl.BlockSpec((len(POOL_SIZES), POOL_GROUP_DIM, POOL_GROUP_DIM), lambda t: (0, 0, 0)),
            pl.BlockSpec((1, D_MODEL), lambda t: (0, 0)),
        ] + _EPI_IN_SPECS,
        out_specs=[
            pl.BlockSpec((POOL_TILE, D_MODEL), lambda t: (t, 0)),
            pl.BlockSpec((POOL_TILE, D_PACK), lambda t: (t, 0)),
            pl.BlockSpec((N_EXPERTS, POOL_TILE), lambda t: (0, t)),
        ],
        out_shape=_EPI_OUT_SHAPE,
        compiler_params=_params("arbitrary"),
        name="pool_mixer",
    )(x_ctx, x_lat, x_lat, x_lat, mods, pool_w, pool_scale, lng, lnb, w2)


def _moe_rows(n):
    return 2 * n + N_EXPERTS * MOE_TILE


def _expert_kernel(te_ref, nv_ref, tb_ref, x_ref, wg_ref, wu_ref, wd_ref, y_ref, wgu_s, wd_s):
    i = pl.program_id(0)
    prev = te_ref[jnp.maximum(i - 1, 0)]
    changed = jnp.logical_or(i == 0, te_ref[i] != prev)

    @pl.when(changed)
    def _():
        wgu_s[:, :D_FF] = wg_ref[0, 0].astype(BF16)
        wgu_s[:, D_FF:] = wu_ref[0, 0].astype(BF16)
        wd_s[...] = wd_ref[0, 0].astype(BF16)

    nv = nv_ref[i]

    def gate_up(r0):
        rows = r0 + lax.broadcasted_iota(jnp.int32, (MOE_SUB, 1), 0)
        xp = jnp.where(rows < nv, x_ref[r0:r0 + MOE_SUB, :], jnp.uint32(0))
        left, right = _unpack_halves(xp)
        xb = jnp.concatenate([left.astype(BF16), right.astype(BF16)], axis=1)
        return jnp.dot(xb, wgu_s[...], preferred_element_type=F32)

    def down(r0, gu):
        gate, up = gu[:, :D_FF], gu[:, D_FF:]
        he = (gate * jax.nn.sigmoid(gate) * up).astype(BF16)
        y_ref[r0:r0 + MOE_SUB, :] = _pack_halves(jnp.dot(he, wd_s[...], preferred_element_type=F32).astype(BF16))

    def run(n_sub):
        gu = gate_up(0)
        for j in range(n_sub):
            nxt = gate_up((j + 1) * MOE_SUB) if j + 1 < n_sub else None
            down(j * MOE_SUB, gu)
            gu = nxt

    n_subs = MOE_TILE // MOE_SUB
    for n_sub in range(1, n_subs + 1):
        lo = (n_sub - 1) * MOE_SUB
        in_range = nv > lo if n_sub == n_subs else jnp.logical_and(nv > lo, nv <= lo + MOE_SUB)
        pl.when(in_range)(functools.partial(run, n_sub))


def _experts(xs, tile_expert, tile_valid, tile_block, layer, w_gate, w_up, w_down):
    grid_spec = pltpu.PrefetchScalarGridSpec(
        num_scalar_prefetch=3,
        grid=(xs.shape[0] // MOE_TILE,),
        in_specs=[
            pl.BlockSpec((MOE_TILE, D_PACK), lambda i, te, nv, tb: (tb[i], 0)),
            pl.BlockSpec((1, 1, D_MODEL, D_FF), lambda i, te, nv, tb: (layer, te[i], 0, 0)),
            pl.BlockSpec((1, 1, D_MODEL, D_FF), lambda i, te, nv, tb: (layer, te[i], 0, 0)),
            pl.BlockSpec((1, 1, D_FF, D_MODEL), lambda i, te, nv, tb: (layer, te[i], 0, 0)),
        ],
        out_specs=pl.BlockSpec((MOE_TILE, D_PACK), lambda i, te, nv, tb: (tb[i], 0)),
        scratch_shapes=[pltpu.VMEM((D_MODEL, 2 * D_FF), BF16), pltpu.VMEM((D_FF, D_MODEL), BF16)],
    )
    return pl.pallas_call(
        _expert_kernel,
        grid_spec=grid_spec,
        out_shape=jax.ShapeDtypeStruct(xs.shape, jnp.uint32),
        compiler_params=_params("arbitrary"),
        name="experts",
    )(tile_expert, tile_valid, tile_block, xs, w_gate, w_up, w_down)


SC_WORKERS = 32


def _sc_mesh():
    return plsc.VectorSubcoreMesh(core_axis_name="core", subcore_axis_name="subcore")


def _sc_worker():
    return lax.axis_index("subcore") * 2 + lax.axis_index("core")


def _dispatch(h2, pos):
    n = h2.shape[0]
    per_worker = n // SC_WORKERS
    pairs = per_worker // (2 * SC_WINDOW)
    one_set = [pltpu.VMEM((SC_WINDOW,), jnp.int32), pltpu.VMEM((SC_WINDOW,), jnp.int32),
               pltpu.VMEM((SC_WINDOW, D_PACK), jnp.uint32),
               pltpu.SemaphoreType.DMA, pltpu.SemaphoreType.DMA, pltpu.SemaphoreType.DMA]

    @functools.partial(pl.kernel, out_type=jax.ShapeDtypeStruct((_moe_rows(n), D_PACK), jnp.uint32),
                       mesh=_sc_mesh(), scratch_types=one_set + one_set, name="moe_dispatch")
    def scatter(x_hbm, i_hbm, o_hbm, *bufs):
        base = _sc_worker() * per_worker
        sets = (bufs[:6], bufs[6:])

        def rows_in(j, s):
            _, _, rows, sem, _, _ = sets[s]
            return pltpu.make_async_copy(x_hbm.at[pl.ds(base + j * SC_WINDOW, SC_WINDOW)], rows, sem)

        def rows_out(k, s):
            idx, rows, sem = sets[s][k], sets[s][2], sets[s][4 + k]
            return pltpu.make_async_copy(rows, o_hbm.at[idx], sem)

        def fetch(j, s):
            t0 = base + j * SC_WINDOW
            pltpu.sync_copy(i_hbm.at[pl.ds(t0, SC_WINDOW)], sets[s][0])
            pltpu.sync_copy(i_hbm.at[pl.ds(n + t0, SC_WINDOW)], sets[s][1])
            rows_in(j, s).start()

        fetch(0, 0)

        @pl.loop(0, pairs)
        def _(g):
            j0, j1 = 2 * g, 2 * g + 1
            fetch(j1, 1)
            rows_in(j0, 0).wait()
            rows_out(0, 0).start()
            rows_out(1, 0).start()
            rows_in(j1, 1).wait()
            rows_out(0, 1).start()
            rows_out(1, 1).start()
            rows_out(0, 0).wait()
            rows_out(1, 0).wait()

            @pl.when(g + 1 < pairs)
            def _():
                fetch(j0 + 2, 0)

            rows_out(0, 1).wait()
            rows_out(1, 1).wait()

    return scatter(h2, pos)


def _gather_pairs(y, pos):
    n = pos.shape[0] // 2
    per_worker = 2 * n // SC_WORKERS
    pairs = per_worker // (2 * SC_WINDOW)
    scratch = [pltpu.VMEM((per_worker,), jnp.int32),
               pltpu.VMEM((SC_WINDOW, D_PACK), jnp.uint32), pltpu.VMEM((SC_WINDOW, D_PACK), jnp.uint32),
               pltpu.SemaphoreType.DMA, pltpu.SemaphoreType.DMA, pltpu.SemaphoreType.DMA, pltpu.SemaphoreType.DMA]

    @functools.partial(pl.kernel, out_type=jax.ShapeDtypeStruct((2 * n, D_PACK), jnp.uint32),
                       mesh=_sc_mesh(), scratch_types=scratch, name="moe_gather")
    def gather(y_hbm, i_hbm, o_hbm, idx_v, rows0, rows1, read0, read1, write0, write1):
        base = _sc_worker() * per_worker
        pltpu.sync_copy(i_hbm.at[pl.ds(base, per_worker)], idx_v)

        def read(j, rows, sem):
            return pltpu.make_async_copy(y_hbm.at[idx_v.at[pl.ds(j * SC_WINDOW, SC_WINDOW)]], rows, sem)

        def write(j, rows, sem):
            return pltpu.make_async_copy(rows, o_hbm.at[pl.ds(base + j * SC_WINDOW, SC_WINDOW)], sem)

        read(0, rows0, read0).start()

        @pl.loop(0, pairs)
        def _(g):
            j0, j1 = 2 * g, 2 * g + 1
            read(j1, rows1, read1).start()
            read(j0, rows0, read0).wait()
            write(j0, rows0, write0).start()
            read(j1, rows1, read1).wait()
            write(j1, rows1, write1).start()
            write(j0, rows0, write0).wait()

            @pl.when(g + 1 < pairs)
            def _():
                read(j0 + 2, rows0, read0).start()

            write(j1, rows1, write1).wait()

    return gather(y, pos).reshape(2, n, D_PACK)


def _moe_post_norm(x_ref, yg_ref, w_ref, mods_ref, lng_ref, lnb_ref):
    g2 = mods_ref[0][5:6, :]
    groups = x_ref.shape[0] // LANES
    wt = jnp.concatenate([w_ref[0, 0], w_ref[1, 0]], axis=0).T
    moe = []
    for r in range(groups):
        rows = slice(r * LANES, (r + 1) * LANES)
        y1 = jnp.concatenate(_unpack_halves(yg_ref[0, rows, :]), axis=1)
        y2 = jnp.concatenate(_unpack_halves(yg_ref[1, rows, :]), axis=1)
        moe.append(wt[:, r:r + 1] * y1 + wt[:, groups + r:groups + r + 1] * y2)
    return _post_norm(x_ref[...], g2 * jnp.concatenate(moe, axis=0), lng_ref[...], lnb_ref[...])


def _combine_kernel(x_ref, yg_ref, w_ref, mods_ref, lng_ref, lnb_ref, o_ref):
    o_ref[...] = _moe_post_norm(x_ref, yg_ref, w_ref, mods_ref, lng_ref, lnb_ref)


def _combine(x1, yg, wsel, mods, lng, lnb, t0, n):
    off = t0 // LN_TILE
    return pl.pallas_call(
        _combine_kernel,
        grid=(n // LN_TILE,),
        in_specs=[
            pl.BlockSpec((LN_TILE, D_MODEL), lambda t: (off + t, 0)),
            pl.BlockSpec((2, LN_TILE, D_PACK), lambda t: (0, off + t, 0)),
            pl.BlockSpec((2, 1, LN_TILE // LANES, LANES), lambda t: (0, off + t, 0, 0)),
            _mods_spec(LN_TILE, off),
            pl.BlockSpec((1, D_MODEL), lambda t: (0, 0)),
            pl.BlockSpec((1, D_MODEL), lambda t: (0, 0)),
        ],
        out_specs=pl.BlockSpec((LN_TILE, D_MODEL), lambda t: (t, 0)),
        out_shape=jax.ShapeDtypeStruct((n, D_MODEL), F32),
        compiler_params=_params("arbitrary"),
        name="moe_combine",
    )(x1, yg, wsel.reshape(2, T_ALL // LN_TILE, LN_TILE // LANES, LANES), mods, lng, lnb)


def _moe_rows_of_tokens(h2p, logits, b_router, layer, w_gate, w_up, w_down):
    pos, wsel, tile_expert, tile_valid, tile_block = _route(logits, b_router)
    xs = _dispatch(h2p, pos)
    y = _experts(xs, tile_expert, tile_valid, tile_block, layer, w_gate, w_up, w_down)
    return _gather_pairs(y, pos), wsel


def _qkv_kernel(x1_ref, yg_ref, wsel_ref, mods0_ref, lng_ref, lnb_ref, mods_ref, w_ref,
                x_ref, qkv_ref, nk_ref, nv_ref, w_s):
    t = pl.program_id(0)

    @pl.when(t == 0)
    def _():
        w_s[...] = w_ref[0].astype(BF16)

    m = mods_ref[0]
    sh1, sc1 = m[0:1, :], m[1:2, :]

    def norm(rows):
        x = _moe_post_norm(x1_ref.at[rows], yg_ref.at[:, rows], wsel_ref, mods0_ref, lng_ref, lnb_ref)
        x_ref[rows, :] = x
        return (x * (1.0 + sc1) + sh1).astype(BF16)

    def project(rows, h):
        r = jnp.dot(h, w_s[...], preferred_element_type=F32)
        qkv_ref[rows, :D_MODEL] = (r[:, :D_MODEL] * (HEAD_DIM ** -0.5)).astype(BF16)
        qkv_ref[rows, D_MODEL:] = r[:, D_MODEL:].astype(BF16)
        return r

    is_ctx = t < T_CTX // QKV_TILE
    rows = slice(0, QKV_TILE)

    @pl.when(is_ctx)
    def _():
        r = project(rows, norm(rows))
        for out_ref, base in ((nk_ref, D_MODEL), (nv_ref, 2 * D_MODEL)):
            for b in range(QKV_TILE // SEQ):
                for p in range(HEAD_PAIRS):
                    pair = r[b * SEQ:(b + 1) * SEQ, base + p * LANES: base + (p + 1) * LANES].T
                    out_ref[b, 0, 2 * p] = pair[:HEAD_DIM]
                    out_ref[b, 0, 2 * p + 1] = pair[HEAD_DIM:]

    @pl.when(jnp.logical_not(is_ctx))
    def _():
        project(rows, norm(rows))


def _qkv(x1, yg, wsel, mods0, lng, lnb, mods1, w_qkv):
    seqs = QKV_TILE // SEQ
    row_spec = pl.BlockSpec((QKV_TILE, D_MODEL), lambda t: (t, 0))
    vec_spec = pl.BlockSpec((1, D_MODEL), lambda t: (0, 0))

    cache_spec = pl.BlockSpec((seqs, 1, N_HEADS, HEAD_DIM, SEQ),
                              lambda t: (jnp.minimum(t, T_CTX // QKV_TILE - 1), 0, 0, 0, 0))
    cache_shape = jax.ShapeDtypeStruct((BATCH, 1, N_HEADS, HEAD_DIM, SEQ), F32)
    return pl.pallas_call(
        _qkv_kernel,
        grid=(T_ALL // QKV_TILE,),
        in_specs=[
            row_spec,
            pl.BlockSpec((2, QKV_TILE, D_PACK), lambda t: (0, t, 0)),
            pl.BlockSpec((2, 1, QKV_TILE // LANES, LANES), lambda t: (0, t, 0, 0)),
            _mods_spec(QKV_TILE),
            vec_spec,
            vec_spec,
            _mods_spec(QKV_TILE),
            pl.BlockSpec((1, D_MODEL, 3 * D_MODEL), lambda t: (0, 0, 0)),
        ],
        out_specs=[row_spec, pl.BlockSpec((QKV_TILE, 3 * D_MODEL), lambda t: (t, 0)), cache_spec, cache_spec],
        out_shape=[jax.ShapeDtypeStruct((T_ALL, D_MODEL), F32),
                   jax.ShapeDtypeStruct((T_ALL, 3 * D_MODEL), BF16), cache_shape, cache_shape],
        scratch_shapes=[pltpu.VMEM((D_MODEL, 3 * D_MODEL), BF16)],
        compiler_params=_params("arbitrary"),
        name="qkv",
    )(x1, yg, wsel.reshape(2, T_ALL // QKV_TILE, QKV_TILE // LANES, LANES), mods0, lng, lnb, mods1, w_qkv)


def _dot_nt(a, b):
    return lax.dot_general(a, b, (((1,), (1,)), ((), ())), preferred_element_type=F32)


def _head_masks():
    lane = lax.broadcasted_iota(jnp.int32, (1, LANES), 1)
    return lane < HEAD_DIM, lane >= HEAD_DIM


def _ctx_attn_kernel(q_ref, k_ref, v_ref, o_ref):
    left, right = _head_masks()
    units = [(p, hh) for p in range(HEAD_PAIRS) for hh in range(2)]

    def scores(u):
        p, hh = units[u]
        cols = slice(p * LANES, (p + 1) * LANES)
        q2 = q_ref[:, cols]
        qh = jnp.where(right if hh else left, q2, jnp.zeros_like(q2))
        return _dot_nt(qh, k_ref[:, cols])

    s = scores(0)
    halves = []
    for u, (p, _) in enumerate(units):
        nxt = scores(u + 1) if u + 1 < len(units) else None
        cols = slice(p * LANES, (p + 1) * LANES)
        e = jnp.exp(s - jnp.max(s, axis=-1, keepdims=True))
        o2 = jnp.dot(e.astype(BF16), v_ref[:, cols], preferred_element_type=F32)
        halves.append(o2 / jnp.sum(e, axis=-1, keepdims=True))
        if len(halves) == 2:
            o_ref[:, cols] = jnp.where(left, halves[0], halves[1]).astype(BF16)
            halves = []
        s = nxt


def _ctx_attention(qkv):
    return pl.pallas_call(
        _ctx_attn_kernel,
        grid=(BATCH,),
        in_specs=[pl.BlockSpec((SEQ, D_MODEL), lambda b, j=j: (b, j)) for j in range(3)],
        out_specs=pl.BlockSpec((SEQ, D_MODEL), lambda b: (b, 0)),
        out_shape=jax.ShapeDtypeStruct((T_ALL, D_MODEL), BF16),
        compiler_params=_params("arbitrary"),
        name="ctx_attention",
    )(qkv, qkv, qkv)


_LAT_Q_BLOCK_ROWS = 4
_LAT_KEY_ROWS = ((0, 8), (0, 12), (4, 16), (8, 16))


def _softmax_rows(s_ref, p_ref, l_ref):
    sc = s_ref[...]
    e = jnp.exp(sc - jnp.max(sc, axis=-1, keepdims=True))
    l_ref[...] = jnp.sum(e, axis=-1, keepdims=True)
    p_ref[...] = e.astype(BF16)


def _lat_attn_kernel(q_ref, k_ref, v_ref, ck_ref, cv_ref, eb_ref, o_ctx_ref, o_ref, bias_s, s_scr, p_scr, l_scr):
    del o_ctx_ref
    left, right = _head_masks()

    @pl.when(pl.program_id(1) == 0)
    def _():
        neg = jnp.full((GRID_W, LANES), NEG_INF, F32)
        for hh in range(2):
            for r in range(GRID_ROWS):
                r0 = min(max(r - WIN_H // 2, 0), GRID_ROWS - WIN_H)
                for j in range(GRID_ROWS // 2):
                    parts = []
                    for kr in (2 * j, 2 * j + 1):
                        parts.append(eb_ref[hh, kr - r + WIN_H - 1] if r0 <= kr < r0 + WIN_H else None)
                    if parts[0] is None and parts[1] is None:
                        val = neg
                    else:
                        val = jnp.where(left, neg if parts[0] is None else parts[0],
                                        neg if parts[1] is None else parts[1])
                    bias_s[hh, r * GRID_W:(r + 1) * GRID_W, j * LANES:(j + 1) * LANES] = val

    ck = jnp.concatenate([ck_ref[0, 0, 0], ck_ref[0, 0, 1]], axis=0).astype(BF16)
    cv = jnp.concatenate([cv_ref[0, 0, 0], cv_ref[0, 0, 1]], axis=0).astype(BF16)
    qrows = _LAT_Q_BLOCK_ROWS * GRID_W
    units = [(qb, hh) for qb in range(len(_LAT_KEY_ROWS)) for hh in range(2)]

    def refs(u):
        nk = (_LAT_KEY_ROWS[units[u][0]][1] - _LAT_KEY_ROWS[units[u][0]][0]) * GRID_W
        width = nk + PAST_LEN
        return nk, s_scr.at[u % 2, :, :width], p_scr.at[u % 2, :, :width], l_scr.at[u % 2]

    def scores(u):
        qb, hh = units[u]
        kr0, kr1 = _LAT_KEY_ROWS[qb]
        qs, ks = slice(qb * qrows, (qb + 1) * qrows), slice(kr0 * GRID_W, kr1 * GRID_W)
        nk, s_ref, _, _ = refs(u)
        q2 = q_ref[qs, :]
        qh = jnp.where(right if hh else left, q2, jnp.zeros_like(q2))
        s_ref[:, :nk] = _dot_nt(qh, k_ref[ks, :]) + bias_s[hh, qs, ks]
        s_ref[:, nk:] = jnp.dot(qh, ck, preferred_element_type=F32)

    def weighted_values(u):
        qb, _ = units[u]
        kr0, kr1 = _LAT_KEY_ROWS[qb]
        nk, _, p_ref, l_ref = refs(u)
        o2 = (jnp.dot(p_ref[:, :nk], v_ref[kr0 * GRID_W:kr1 * GRID_W, :], preferred_element_type=F32)
              + _dot_nt(p_ref[:, nk:], cv))
        return o2 / l_ref[...]

    scores(0)
    halves = []
    for u in range(len(units)):
        if u + 1 < len(units):
            scores(u + 1)
        _, s_ref, p_ref, l_ref = refs(u)
        _softmax_rows(s_ref, p_ref, l_ref)
        halves.append(weighted_values(u))
        if len(halves) == 2:
            qb = units[u][0]
            o_ref[qb * qrows:(qb + 1) * qrows, :] = jnp.where(left, halves[0], halves[1]).astype(BF16)
            halves = []


def _lat_attention(qkv, cache_kt, cache_vt, ebias, o_buf):
    row0 = T_CTX // DEC_SEQ
    qrows = _LAT_Q_BLOCK_ROWS * GRID_W
    max_keys = max(k1 - k0 for k0, k1 in _LAT_KEY_ROWS) * GRID_W + PAST_LEN
    cache_spec = pl.BlockSpec((1, 1, 2, HEAD_DIM, PAST_LEN), lambda p, b: (b, 0, p, 0, 0))
    return pl.pallas_call(
        _lat_attn_kernel,
        grid=(HEAD_PAIRS, DEC_BATCH),
        in_specs=[
            pl.BlockSpec((DEC_SEQ, LANES), lambda p, b: (row0 + b, p)),
            pl.BlockSpec((DEC_SEQ, LANES), lambda p, b: (row0 + b, HEAD_PAIRS + p)),
            pl.BlockSpec((DEC_SEQ, LANES), lambda p, b: (row0 + b, 2 * HEAD_PAIRS + p)),
            cache_spec,
            cache_spec,
            pl.BlockSpec((2, 2 * WIN_H - 1, GRID_W, LANES), lambda p, b: (p, 0, 0, 0)),
            pl.BlockSpec(memory_space=pl.ANY),
        ],
        out_specs=pl.BlockSpec((DEC_SEQ, LANES), lambda p, b: (row0 + b, p)),
        out_shape=jax.ShapeDtypeStruct((T_ALL, D_MODEL), BF16),
        scratch_shapes=[pltpu.VMEM((2, DEC_SEQ, DEC_SEQ), F32),
                        pltpu.VMEM((2, qrows, max_keys), F32),
                        pltpu.VMEM((2, qrows, max_keys), BF16),
                        pltpu.VMEM((2, qrows, 1), F32)],
        input_output_aliases={6: 0},
        compiler_params=_params("arbitrary", "arbitrary"),
        name="lat_attention",
    )(qkv, qkv, qkv, cache_kt, cache_vt, ebias, o_buf)


def _expanded_bias(rel_bias):
    col = jnp.arange(GRID_W)
    col_start = jnp.clip(col - WIN_W // 2, 0, GRID_W - WIN_W)
    col_ok = (col[None, :] >= col_start[:, None]) & (col[None, :] < col_start[:, None] + WIN_W)
    dx = jnp.clip(col[None, :] - col[:, None] + (WIN_W - 1), 0, 2 * WIN_W - 2)
    dx, col_ok = jnp.concatenate([dx, dx], axis=1), jnp.concatenate([col_ok, col_ok], axis=1)
    onehot = (dx[None] == jnp.arange(2 * WIN_W - 1)[:, None, None]).astype(F32)
    eb = jnp.einsum("hyj,jqk->hyqk", rel_bias.astype(F32), onehot, precision=lax.Precision.HIGHEST)
    return jnp.where(col_ok[None, None], eb, NEG_INF)


def _wo_kernel(o_ref, x_ref, mods_ref, w_ref, lng_ref, lnb_ref, w2_ref, x1_ref, h2p_ref, logit_ref, w_s):
    @pl.when(pl.program_id(0) == 0)
    def _():
        w_s[...] = w_ref[0].astype(BF16)

    subs = [slice(r0, r0 + TILE) for r0 in range(0, WO_TILE, TILE)]
    mix = jnp.dot(o_ref[subs[0], :], w_s[...], preferred_element_type=F32)
    for j, rs in enumerate(subs):
        nxt = jnp.dot(o_ref[subs[j + 1], :], w_s[...], preferred_element_type=F32) if j + 1 < len(subs) else None
        _epilogue(x_ref[rs, :], mix, mods_ref[0], lng_ref[...], lnb_ref[...], w2_ref,
                  x1_ref.at[rs, :], h2p_ref.at[rs, :], logit_ref.at[:, rs])
        mix = nxt


def _attn_out(o, x, mods, w_o, lng, lnb, w2):
    return pl.pallas_call(
        _wo_kernel,
        grid=(T_ALL // WO_TILE,),
        in_specs=[
            pl.BlockSpec((WO_TILE, D_MODEL), lambda t: (t, 0)),
            pl.BlockSpec((WO_TILE, D_MODEL), lambda t: (t, 0)),
            _mods_spec(WO_TILE),
            pl.BlockSpec((1, D_MODEL, D_MODEL), lambda t: (0, 0, 0)),
        ] + _EPI_IN_SPECS,
        out_specs=[
            pl.BlockSpec((WO_TILE, D_MODEL), lambda t: (t, 0)),
            pl.BlockSpec((WO_TILE, D_PACK), lambda t: (t, 0)),
            pl.BlockSpec((N_EXPERTS, WO_TILE), lambda t: (0, t)),
        ],
        out_shape=_EPI_OUT_SHAPE,
        scratch_shapes=[pltpu.VMEM((D_MODEL, D_MODEL), BF16)],
        compiler_params=_params("arbitrary"),
        name="attn_out",
    )(o, x, mods, w_o, lng, lnb, w2)


def kernel(x_prompt, x_sample, cache_k, cache_v, c, c_ctx, w_ada, b_ada, ln1_g, ln1_b, ln2_g, ln2_b,
           pool_w, pool_scale, w_qkv, w_o, rel_bias, w_router, b_router, w_gate, w_up, w_down):
    cond = jnp.zeros((N_COND, D_MODEL), F32).at[0].set(c_ctx).at[1:1 + DEC_BATCH].set(c)
    mods0, mods1 = _ada(cond, w_ada, b_ada, 0), _ada(cond, w_ada, b_ada, 1)
    wrt = w_router.T
    wrt_hi = wrt.astype(BF16)
    w2 = jnp.concatenate([wrt_hi, (wrt - wrt_hi.astype(F32)).astype(BF16)], axis=0)

    x, h2p, logits = _pool_layer(x_prompt.reshape(T_CTX, D_MODEL), x_sample.reshape(T_LAT, D_MODEL), mods0,
                                 pool_w[0].astype(BF16), pool_scale[0:1], ln1_g[0:1], ln1_b[0:1], w2)
    yg, wsel = _moe_rows_of_tokens(h2p, logits, b_router, 0, w_gate, w_up, w_down)

    x, qkv, new_kt, new_vt = _qkv(x, yg, wsel, mods0, ln2_g[0:1], ln2_b[0:1], mods1, w_qkv)
    o = _lat_attention(qkv, cache_k.transpose(0, 1, 2, 4, 3), cache_v.transpose(0, 1, 2, 4, 3),
                       _expanded_bias(rel_bias[0]), _ctx_attention(qkv))
    x, h2p, logits = _attn_out(o, x, mods1, w_o, ln1_g[1:2], ln1_b[1:2], w2)
    yg, wsel = _moe_rows_of_tokens(h2p, logits, b_router, 1, w_gate, w_up, w_down)
    y_ctx, y_lat = [_combine(x, yg, wsel, mods1, ln2_g[1:2], ln2_b[1:2], t0, n) for t0, n in SEGMENTS]
    return (y_ctx.reshape(BATCH, SEQ, D_MODEL), y_lat.reshape(DEC_BATCH, DEC_SEQ, D_MODEL),
            new_kt.transpose(0, 1, 2, 4, 3), new_vt.transpose(0, 1, 2, 4, 3))
```

```python
import functools

import jax
import jax.numpy as jnp
from jax import lax
from jax.experimental import pallas as pl
from jax.experimental.pallas import tpu as pltpu
from jax.experimental.pallas import tpu_sc as plsc

F32 = jnp.float32
BF16 = jnp.bfloat16

D_MODEL = 1024
BATCH = 16
SEQ = 256
DEC_BATCH = 8
DEC_SEQ = 1024
PAST_LEN = 512
GRID_W = 64
GRID_ROWS = DEC_SEQ // GRID_W
POOL_SIZES = (2, 4, 8, 16)
POOL_GROUP_DIM = D_MODEL // len(POOL_SIZES)
POOL_HALO = 8
N_HEADS = 16
HEAD_DIM = 64
WIN_H = 8
WIN_W = 16
N_EXPERTS = 16
EXPERTS_PER_GROUP = 4
N_EXPERT_GROUPS = 4
D_FF = 512
ALPHA = (2.0 * 2) ** 0.25
LN_EPS = 1e-5
NEG_INF = -1e30

T_CTX = BATCH * SEQ
T_LAT = DEC_BATCH * DEC_SEQ
T_ALL = T_CTX + T_LAT
N_COND = 16
TILE = 256
TILES_PER_LAT_SEQ = DEC_SEQ // TILE
MOE_TILE = 1024
MOE_SUB = 256
X_SLOTS = 3
SEGMENTS = ((0, T_CTX), (T_CTX, T_LAT))
LN_TILE = 512
QKV_TILE = 512
POOL_TILE = 512
WO_TILE = 1024
SC_WINDOW = 64
D_PACK = D_MODEL // 2
ROUTE_ROWS = T_ALL // 128
HEAD_PAIRS = N_HEADS // 2
LANES = 128
VMEM_LIMIT = 56 * 1024 * 1024


def _cond_row(t, tile):
    ctx_tiles = T_CTX // tile
    per_seq = DEC_SEQ // tile
    return jnp.maximum(t - ctx_tiles + per_seq, 0) // per_seq


def _params(*sem):
    return pltpu.CompilerParams(dimension_semantics=sem, vmem_limit_bytes=VMEM_LIMIT)


def _pack_halves(v):
    half = v.shape[1] // 2
    hi = lax.bitcast_convert_type(v[:, :half].astype(F32), jnp.uint32)
    lo = lax.bitcast_convert_type(v[:, half:].astype(F32), jnp.uint32)
    return hi | (lo >> 16)


def _unpack_halves(p):
    left = lax.bitcast_convert_type(p & jnp.uint32(0xFFFF0000), F32)
    right = lax.bitcast_convert_type(p << 16, F32)
    return left, right


def _ada_kernel(cond_ref, w_ref, b_ref, o_ref):
    cnd = cond_ref[...]
    act = cnd * jax.nn.sigmoid(cnd)
    a_hi = act.astype(BF16)
    a_lo = (act - a_hi.astype(F32)).astype(BF16)
    w = w_ref[0]
    w_hi = w.astype(BF16)
    w_lo = (w - w_hi.astype(F32)).astype(BF16)
    a2 = jnp.concatenate([a_hi, a_lo], axis=0)
    p = jnp.dot(a2, w_hi, preferred_element_type=F32)
    q = jnp.dot(a_hi, w_lo, preferred_element_type=F32)
    o_ref[0] = p[:N_COND] + p[N_COND:] + q + b_ref[0]


def _ada(cond, w_ada, b_ada, layer):
    depth, d, n = w_ada.shape
    bn = 1536
    mods = pl.pallas_call(
        _ada_kernel,
        grid=(n // bn,),
        in_specs=[
            pl.BlockSpec((N_COND, d), lambda j: (0, 0)),
            pl.BlockSpec((1, d, bn), lambda j: (layer, 0, j)),
            pl.BlockSpec((1, 1, bn), lambda j: (layer, 0, j)),
        ],
        out_specs=pl.BlockSpec((1, N_COND, bn), lambda j: (0, 0, j)),
        out_shape=jax.ShapeDtypeStruct((1, N_COND, n), F32),
        compiler_params=_params("arbitrary"),
        name="ada",
    )(cond, w_ada, b_ada.reshape(depth, 1, n))
    return mods.reshape(N_COND, 6, D_MODEL)


def _post_norm(x, upd, g, b):
    y = ALPHA * x + upd
    mu = jnp.mean(y, axis=-1, keepdims=True)
    yc = y - mu
    var = jnp.mean(yc * yc, axis=-1, keepdims=True)
    return yc * lax.rsqrt(var + LN_EPS) * g + b


def _router_logits(h2, h_hi, w2_ref):
    h_lo = (h2 - h_hi.astype(F32)).astype(BF16)
    w2 = w2_ref[...]
    p = _dot_nt(w2, h_hi)
    q = _dot_nt(w2[:N_EXPERTS], h_lo)
    return p[:N_EXPERTS] + p[N_EXPERTS:] + q


def _epilogue(x, mix, m, lng, lnb, w2_ref, x1_ref, h2p_ref, logit_ref):
    g1, sh2, sc2 = m[2:3, :], m[3:4, :], m[4:5, :]
    x1 = _post_norm(x, g1 * mix, lng, lnb)
    h2 = x1 * (1.0 + sc2) + sh2
    h_hi = h2.astype(BF16)
    x1_ref[...] = x1
    h2p_ref[...] = _pack_halves(h_hi)
    logit_ref[...] = _router_logits(h2, h_hi, w2_ref)


_EPI_IN_SPECS = [
    pl.BlockSpec((1, D_MODEL), lambda t: (0, 0)),
    pl.BlockSpec((1, D_MODEL), lambda t: (0, 0)),
    pl.BlockSpec((2 * N_EXPERTS, D_MODEL), lambda t: (0, 0)),
]
_EPI_OUT_SHAPE = [
    jax.ShapeDtypeStruct((T_ALL, D_MODEL), F32),
    jax.ShapeDtypeStruct((T_ALL, D_PACK), jnp.uint32),
    jax.ShapeDtypeStruct((N_EXPERTS, T_ALL), F32),
]


def _route_kernel(logit_ref, br_ref, pos_ref, w_ref, table_ref):
    aff = jax.nn.sigmoid(logit_ref[...])
    sel = aff + br_ref[...]
    sel_rows = [sel[e] for e in range(N_EXPERTS)]
    aff_rows = [aff[e] for e in range(N_EXPERTS)]

    def group_score(g):
        r = sel_rows[g * EXPERTS_PER_GROUP:(g + 1) * EXPERTS_PER_GROUP]
        best = None
        for i in range(EXPERTS_PER_GROUP):
            for j in range(i + 1, EXPERTS_PER_GROUP):
                pair = r[i] + r[j]
                best = pair if best is None else jnp.maximum(best, pair)
        return best

    best = group_score(0)
    gidx = jnp.zeros_like(best, dtype=jnp.int32)
    for g in range(1, N_EXPERT_GROUPS):
        sc = group_score(g)
        better = sc > best
        gidx = jnp.where(better, g, gidx)
        best = jnp.where(better, sc, best)

    def pick_group(rows, j):
        out = rows[j]
        for g in range(1, N_EXPERT_GROUPS):
            out = jnp.where(gidx == g, rows[g * EXPERTS_PER_GROUP + j], out)
        return out

    cand = [pick_group(sel_rows, j) for j in range(EXPERTS_PER_GROUP)]
    cand_aff = [pick_group(aff_rows, j) for j in range(EXPERTS_PER_GROUP)]

    def argmax_first(vals):
        bv, bi, ba = vals[0], jnp.zeros_like(gidx), cand_aff[0]
        for j in range(1, EXPERTS_PER_GROUP):
            better = vals[j] > bv
            bv = jnp.where(better, vals[j], bv)
            bi = jnp.where(better, j, bi)
            ba = jnp.where(better, cand_aff[j], ba)
        return bi, ba

    i1, a1 = argmax_first(cand)
    rest = [jnp.where(i1 == j, -jnp.inf, cand[j]) for j in range(EXPERTS_PER_GROUP)]
    i2, a2 = argmax_first(rest)
    denom = a1 + a2
    base = gidx * EXPERTS_PER_GROUP
    w_ref[0] = a1 / denom
    w_ref[1] = a2 / denom
    _plan(base + i1, base + i2, pos_ref, table_ref)


def _plan(e1, e2, pos_ref, table_ref):
    rows = e1.shape[0]
    upper = (lax.broadcasted_iota(jnp.int32, (LANES, LANES), 0)
             <= lax.broadcasted_iota(jnp.int32, (LANES, LANES), 1)).astype(BF16)
    lower = (lax.broadcasted_iota(jnp.int32, (rows, rows), 1)
             < lax.broadcasted_iota(jnp.int32, (rows, rows), 0)).astype(BF16)
    masks = [(choice == e) for e in range(N_EXPERTS) for choice in (e1, e2)]
    hits = jnp.concatenate([m.astype(BF16) for m in masks], axis=0)
    within = jnp.dot(hits, upper, preferred_element_type=F32)
    totals = [within[i * rows:(i + 1) * rows, LANES - 1:LANES] for i in range(2 * N_EXPERTS)]
    spread = jnp.concatenate([jnp.broadcast_to(t, (rows, LANES)) for t in totals], axis=1).astype(BF16)
    above = jnp.dot(lower, spread, preferred_element_type=F32)

    tile_start = lax.broadcasted_iota(jnp.int32, (1, LANES), 1).astype(F32) * MOE_TILE
    start = jnp.zeros((1, 1), F32)
    pos = [jnp.zeros((rows, LANES), F32), jnp.zeros((rows, LANES), F32)]
    tile_expert = jnp.zeros((1, LANES), F32)
    tile_valid = jnp.zeros((1, LANES), F32)
    tile_block = jnp.zeros((1, LANES), F32)
    used = jnp.zeros((1, LANES), F32)
    last_expert = jnp.zeros((1, 1), F32)
    last_block = jnp.zeros((1, 1), F32)
    for e in range(N_EXPERTS):
        first = jnp.zeros((1, 1), F32)
        for c in range(2):
            i = 2 * e + c
            hit = masks[i]
            rank = (within[i * rows:(i + 1) * rows] - hit.astype(F32)
                    + above[:, i * LANES:(i + 1) * LANES] + (start + first))
            pos[c] = jnp.where(hit, rank, pos[c])
            first = first + jnp.sum(totals[i], axis=0, keepdims=True)
        count = first
        n_tiles = jnp.floor((count + (MOE_TILE - 1)) * (1.0 / MOE_TILE))
        end = start + n_tiles * MOE_TILE
        inside = jnp.logical_and(tile_start >= start, tile_start < end)
        block_in_expert = n_tiles - 1.0 - (tile_start - start) * (1.0 / MOE_TILE)
        tile_expert = jnp.where(inside, float(e), tile_expert)
        tile_valid = jnp.where(inside, jnp.clip(count - block_in_expert * MOE_TILE, 0.0, float(MOE_TILE)), tile_valid)
        tile_block = jnp.where(inside, start * (1.0 / MOE_TILE) + block_in_expert, tile_block)
        used = jnp.where(inside, 1.0, used)
        has_rows = count > 0.0
        last_expert = jnp.where(has_rows, float(e), last_expert)
        last_block = jnp.where(has_rows, start * (1.0 / MOE_TILE), last_block)
        start = end
    pos_ref[0] = pos[0].astype(jnp.int32)
    pos_ref[1] = pos[1].astype(jnp.int32)
    idle = used == 0.0
    table_ref[0:1, :] = jnp.where(idle, last_expert, tile_expert).astype(jnp.int32)
    table_ref[1:2, :] = tile_valid.astype(jnp.int32)
    table_ref[2:3, :] = jnp.where(idle, last_block, tile_block).astype(jnp.int32)
    table_ref[3:8, :] = jnp.zeros((5, LANES), jnp.int32)


def _route(logits, b_router):
    tiles = _moe_rows(T_ALL) // MOE_TILE
    pos, wsel, table = pl.pallas_call(
        _route_kernel,
        out_shape=[jax.ShapeDtypeStruct((2, ROUTE_ROWS, LANES), jnp.int32),
                   jax.ShapeDtypeStruct((2, ROUTE_ROWS, LANES), F32),
                   jax.ShapeDtypeStruct((8, LANES), jnp.int32)],
        compiler_params=pltpu.CompilerParams(vmem_limit_bytes=VMEM_LIMIT),
        name="route",
    )(logits.reshape(N_EXPERTS, ROUTE_ROWS, LANES), b_router.reshape(N_EXPERTS, 1, 1))
    return pos.reshape(2 * T_ALL), wsel, table[0, :tiles], table[1, :tiles], table[2, :tiles]


def _mods_spec(tile, first_tile=0):
    return pl.BlockSpec((1, 6, D_MODEL), lambda t: (_cond_row(first_tile + t, tile), 0, 0))


def _pool_kernel(xa_ref, xb_ref, xp_ref, xn_ref, mods_ref, pw_ref, ps_ref, lng_ref, lnb_ref, w2_ref,
                 x1_ref, h2p_ref, logit_ref):
    t = pl.program_id(0)
    m = mods_ref[0]
    sh1, sc1 = m[0:1, :], m[1:2, :]
    subs = POOL_TILE // TILE
    in_lat = t >= T_CTX // POOL_TILE
    tile_in_seq = (t - T_CTX // POOL_TILE) % (DEC_SEQ // POOL_TILE)
    seq_len = jnp.where(in_lat, DEC_SEQ, SEQ)

    x = jnp.where(in_lat, xb_ref[...], xa_ref[...])
    h = x * (1.0 + sc1) + sh1
    halo_before = xp_ref[...] * (1.0 + sc1) + sh1
    halo_after = xn_ref[...] * (1.0 + sc1) + sh1
    ext = TILE + 2 * POOL_HALO

    def mixer(j):
        place = tile_in_seq * subs + j
        is_first = jnp.logical_or(jnp.logical_not(in_lat), place == 0)
        is_last = jnp.logical_or(jnp.logical_not(in_lat), place == TILES_PER_LAT_SEQ - 1)
        before = halo_before if j == 0 else h[j * TILE - POOL_HALO:j * TILE]
        after = halo_after if j == subs - 1 else h[(j + 1) * TILE:(j + 1) * TILE + POOL_HALO]
        hj = h[j * TILE:(j + 1) * TILE]
        hext = jnp.concatenate([jnp.where(is_first, 0.0, before), hj, jnp.where(is_last, 0.0, after)], axis=0)
        pos = jnp.where(in_lat, place * TILE, 0) + lax.broadcasted_iota(jnp.int32, (TILE, 1), 0)
        outs = []
        for g, w in enumerate(POOL_SIZES):
            lo_c, hi_c = g * POOL_GROUP_DIM, (g + 1) * POOL_GROUP_DIM
            a = hext[:, lo_c:hi_c]
            k = 1
            while k < w:
                a = a + pltpu.roll(a, ext - k, axis=0)
                k *= 2
            off = POOL_HALO - w // 2
            win = pltpu.roll(a, ext - off, axis=0)[:TILE] if off else a[:TILE]
            lo = jnp.maximum(pos - w // 2, 0)
            hi = jnp.minimum(pos - w // 2 + w, seq_len)
            cnt = (hi - lo).astype(F32)
            pooled = win / cnt - hj[:, lo_c:hi_c]
            outs.append(jnp.dot(pooled.astype(BF16), pw_ref[g], preferred_element_type=F32))
        return jnp.concatenate(outs, axis=1) * ps_ref[...]

    mix = mixer(0)
    for j in range(subs):
        nxt = mixer(j + 1) if j + 1 < subs else None
        rs = slice(j * TILE, (j + 1) * TILE)
        _epilogue(x[rs, :], mix, m, lng_ref[...], lnb_ref[...], w2_ref,
                  x1_ref.at[rs, :], h2p_ref.at[rs, :], logit_ref.at[:, rs])
        mix = nxt


def _pool_layer(x_ctx, x_lat, mods, pool_w, pool_scale, lng, lnb, w2):
    ctx_tiles = T_CTX // POOL_TILE
    halo_blocks = POOL_TILE // POOL_HALO
    last_halo = T_LAT // POOL_HALO - 1
    return pl.pallas_call(
        _pool_kernel,
        grid=(T_ALL // POOL_TILE,),
        in_specs=[
            pl.BlockSpec((POOL_TILE, D_MODEL), lambda t: (jnp.minimum(t, ctx_tiles - 1), 0)),
            pl.BlockSpec((POOL_TILE, D_MODEL), lambda t: (jnp.maximum(t - ctx_tiles, 0), 0)),
            pl.BlockSpec((POOL_HALO, D_MODEL), lambda t: (jnp.maximum((t - ctx_tiles) * halo_blocks - 1, 0), 0)),
            pl.BlockSpec((POOL_HALO, D_MODEL),
                         lambda t: (jnp.clip((t - ctx_tiles + 1) * halo_blocks, 0, last_halo), 0)),
            _mods_spec(POOL_TILE),
            pl.BlockSpec((len(POOL_SIZES), POOL_GROUP_DIM, POOL_GROUP_DIM), lambda t: (0, 0, 0)),
            pl.BlockSpec((1, D_MODEL), lambda t: (0, 0)),
        ] + _EPI_IN_SPECS,
        out_specs=[
            pl.BlockSpec((POOL_TILE, D_MODEL), lambda t: (t, 0)),
            pl.BlockSpec((POOL_TILE, D_PACK), lambda t: (t, 0)),
            pl.BlockSpec((N_EXPERTS, POOL_TILE), lambda t: (0, t)),
        ],
        out_shape=_EPI_OUT_SHAPE,
        compiler_params=_params("arbitrary"),
        name="pool_mixer",
    )(x_ctx, x_lat, x_lat, x_lat, mods, pool_w, pool_scale, lng, lnb, w2)


def _moe_rows(n):
    return 2 * n + N_EXPERTS * MOE_TILE


def _expert_kernel(te_ref, nv_ref, tb_ref, x_hbm, wg_ref, wu_ref, wd_ref, y_ref, wgu_s, wd_s, x_ring, x_sems):
    i = pl.program_id(0)
    steps = pl.num_programs(0)
    prev = te_ref[jnp.maximum(i - 1, 0)]
    changed = jnp.logical_or(i == 0, te_ref[i] != prev)

    def x_copy(step):
        slot = step % X_SLOTS
        return pltpu.make_async_copy(x_hbm.at[pl.ds(tb_ref[step] * MOE_TILE, MOE_TILE)], x_ring.at[slot],
                                     x_sems.at[slot])

    def start_x(step):
        @pl.when(nv_ref[step] > 0)
        def _():
            x_copy(step).start()

    @pl.when(i == 0)
    def _():
        for step in range(X_SLOTS - 1):
            start_x(step)

    @pl.when(i + (X_SLOTS - 1) < steps)
    def _():
        start_x(i + (X_SLOTS - 1))

    @pl.when(changed)
    def _():
        wgu_s[:, :D_FF] = wg_ref[0, 0].astype(BF16)
        wgu_s[:, D_FF:] = wu_ref[0, 0].astype(BF16)
        wd_s[...] = wd_ref[0, 0].astype(BF16)

    nv = nv_ref[i]
    x_ref = x_ring.at[i % X_SLOTS]

    @pl.when(nv > 0)
    def _():
        x_copy(i).wait()

    def gate_up(r0):
        rows = r0 + lax.broadcasted_iota(jnp.int32, (MOE_SUB, 1), 0)
        xp = jnp.where(rows < nv, x_ref[r0:r0 + MOE_SUB, :], jnp.uint32(0))
        left, right = _unpack_halves(xp)
        xb = jnp.concatenate([left.astype(BF16), right.astype(BF16)], axis=1)
        return jnp.dot(xb, wgu_s[...], preferred_element_type=F32)

    def down(r0, gu):
        gate, up = gu[:, :D_FF], gu[:, D_FF:]
        he = (gate * jax.nn.sigmoid(gate) * up).astype(BF16)
        y_ref[r0:r0 + MOE_SUB, :] = _pack_halves(jnp.dot(he, wd_s[...], preferred_element_type=F32).astype(BF16))

    def run(n_sub):
        gu = gate_up(0)
        for j in range(n_sub):
            nxt = gate_up((j + 1) * MOE_SUB) if j + 1 < n_sub else None
            down(j * MOE_SUB, gu)
            gu = nxt

    n_subs = MOE_TILE // MOE_SUB
    for n_sub in range(1, n_subs + 1):
        lo = (n_sub - 1) * MOE_SUB
        in_range = nv > lo if n_sub == n_subs else jnp.logical_and(nv > lo, nv <= lo + MOE_SUB)
        pl.when(in_range)(functools.partial(run, n_sub))


def _experts(xs, tile_expert, tile_valid, tile_block, layer, w_gate, w_up, w_down):
    grid_spec = pltpu.PrefetchScalarGridSpec(
        num_scalar_prefetch=3,
        grid=(xs.shape[0] // MOE_TILE,),
        in_specs=[
            pl.BlockSpec(memory_space=pl.ANY),
            pl.BlockSpec((1, 1, D_MODEL, D_FF), lambda i, te, nv, tb: (layer, te[i], 0, 0)),
            pl.BlockSpec((1, 1, D_MODEL, D_FF), lambda i, te, nv, tb: (layer, te[i], 0, 0)),
            pl.BlockSpec((1, 1, D_FF, D_MODEL), lambda i, te, nv, tb: (layer, te[i], 0, 0)),
        ],
        out_specs=pl.BlockSpec((MOE_TILE, D_PACK), lambda i, te, nv, tb: (tb[i], 0)),
        scratch_shapes=[pltpu.VMEM((D_MODEL, 2 * D_FF), BF16), pltpu.VMEM((D_FF, D_MODEL), BF16),
                        pltpu.VMEM((X_SLOTS, MOE_TILE, D_PACK), jnp.uint32), pltpu.SemaphoreType.DMA((X_SLOTS,))],
    )
    return pl.pallas_call(
        _expert_kernel,
        grid_spec=grid_spec,
        out_shape=jax.ShapeDtypeStruct(xs.shape, jnp.uint32),
        compiler_params=_params("arbitrary"),
        name="experts",
    )(tile_expert, tile_valid, tile_block, xs, w_gate, w_up, w_down)


SC_WORKERS = 32


def _sc_mesh():
    return plsc.VectorSubcoreMesh(core_axis_name="core", subcore_axis_name="subcore")


def _sc_worker():
    return lax.axis_index("subcore") * 2 + lax.axis_index("core")


def _dispatch(h2, pos):
    n = h2.shape[0]
    per_worker = n // SC_WORKERS
    pairs = per_worker // (2 * SC_WINDOW)
    one_set = [pltpu.VMEM((SC_WINDOW,), jnp.int32), pltpu.VMEM((SC_WINDOW,), jnp.int32),
               pltpu.VMEM((SC_WINDOW, D_PACK), jnp.uint32),
               pltpu.SemaphoreType.DMA, pltpu.SemaphoreType.DMA, pltpu.SemaphoreType.DMA]

    @functools.partial(pl.kernel, out_type=jax.ShapeDtypeStruct((_moe_rows(n), D_PACK), jnp.uint32),
                       mesh=_sc_mesh(), scratch_types=one_set + one_set, name="moe_dispatch")
    def scatter(x_hbm, i_hbm, o_hbm, *bufs):
        base = _sc_worker() * per_worker
        sets = (bufs[:6], bufs[6:])

        def rows_in(j, s):
            _, _, rows, sem, _, _ = sets[s]
            return pltpu.make_async_copy(x_hbm.at[pl.ds(base + j * SC_WINDOW, SC_WINDOW)], rows, sem)

        def rows_out(k, s):
            idx, rows, sem = sets[s][k], sets[s][2], sets[s][4 + k]
            return pltpu.make_async_copy(rows, o_hbm.at[idx], sem)

        def fetch(j, s):
            t0 = base + j * SC_WINDOW
            pltpu.sync_copy(i_hbm.at[pl.ds(t0, SC_WINDOW)], sets[s][0])
            pltpu.sync_copy(i_hbm.at[pl.ds(n + t0, SC_WINDOW)], sets[s][1])
            rows_in(j, s).start()

        fetch(0, 0)

        @pl.loop(0, pairs)
        def _(g):
            j0, j1 = 2 * g, 2 * g + 1
            fetch(j1, 1)
            rows_in(j0, 0).wait()
            rows_out(0, 0).start()
            rows_out(1, 0).start()
            rows_in(j1, 1).wait()
            rows_out(0, 1).start()
            rows_out(1, 1).start()
            rows_out(0, 0).wait()
            rows_out(1, 0).wait()

            @pl.when(g + 1 < pairs)
            def _():
                fetch(j0 + 2, 0)

            rows_out(0, 1).wait()
            rows_out(1, 1).wait()

    return scatter(h2, pos)


def _gather_pairs(y, pos):
    n = pos.shape[0] // 2
    per_worker = 2 * n // SC_WORKERS
    pairs = per_worker // (2 * SC_WINDOW)
    scratch = [pltpu.VMEM((per_worker,), jnp.int32),
               pltpu.VMEM((SC_WINDOW, D_PACK), jnp.uint32), pltpu.VMEM((SC_WINDOW, D_PACK), jnp.uint32),
               pltpu.SemaphoreType.DMA, pltpu.SemaphoreType.DMA, pltpu.SemaphoreType.DMA, pltpu.SemaphoreType.DMA]

    @functools.partial(pl.kernel, out_type=jax.ShapeDtypeStruct((2 * n, D_PACK), jnp.uint32),
                       mesh=_sc_mesh(), scratch_types=scratch, name="moe_gather")
    def gather(y_hbm, i_hbm, o_hbm, idx_v, rows0, rows1, read0, read1, write0, write1):
        base = _sc_worker() * per_worker
        pltpu.sync_copy(i_hbm.at[pl.ds(base, per_worker)], idx_v)

        def read(j, rows, sem):
            return pltpu.make_async_copy(y_hbm.at[idx_v.at[pl.ds(j * SC_WINDOW, SC_WINDOW)]], rows, sem)

        def write(j, rows, sem):
            return pltpu.make_async_copy(rows, o_hbm.at[pl.ds(base + j * SC_WINDOW, SC_WINDOW)], sem)

        read(0, rows0, read0).start()

        @pl.loop(0, pairs)
        def _(g):
            j0, j1 = 2 * g, 2 * g + 1
            read(j1, rows1, read1).start()
            read(j0, rows0, read0).wait()
            write(j0, rows0, write0).start()
            read(j1, rows1, read1).wait()
            write(j1, rows1, write1).start()
            write(j0, rows0, write0).wait()

            @pl.when(g + 1 < pairs)
            def _():
                read(j0 + 2, rows0, read0).start()

            write(j1, rows1, write1).wait()

    return gather(y, pos).reshape(2, n, D_PACK)


def _moe_post_norm(x_ref, yg_ref, w_ref, mods_ref, lng_ref, lnb_ref):
    g2 = mods_ref[0][5:6, :]
    groups = x_ref.shape[0] // LANES
    wt = jnp.concatenate([w_ref[0, 0], w_ref[1, 0]], axis=0).T
    moe = []
    for r in range(groups):
        rows = slice(r * LANES, (r + 1) * LANES)
        y1 = jnp.concatenate(_unpack_halves(yg_ref[0, rows, :]), axis=1)
        y2 = jnp.concatenate(_unpack_halves(yg_ref[1, rows, :]), axis=1)
        moe.append(wt[:, r:r + 1] * y1 + wt[:, groups + r:groups + r + 1] * y2)
    return _post_norm(x_ref[...], g2 * jnp.concatenate(moe, axis=0), lng_ref[...], lnb_ref[...])


def _combine_kernel(x_ref, yg_ref, w_ref, mods_ref, lng_ref, lnb_ref, o_ref):
    o_ref[...] = _moe_post_norm(x_ref, yg_ref, w_ref, mods_ref, lng_ref, lnb_ref)


def _combine(x1, yg, wsel, mods, lng, lnb, t0, n):
    off = t0 // LN_TILE
    return pl.pallas_call(
        _combine_kernel,
        grid=(n // LN_TILE,),
        in_specs=[
            pl.BlockSpec((LN_TILE, D_MODEL), lambda t: (off + t, 0)),
            pl.BlockSpec((2, LN_TILE, D_PACK), lambda t: (0, off + t, 0)),
            pl.BlockSpec((2, 1, LN_TILE // LANES, LANES), lambda t: (0, off + t, 0, 0)),
            _mods_spec(LN_TILE, off),
            pl.BlockSpec((1, D_MODEL), lambda t: (0, 0)),
            pl.BlockSpec((1, D_MODEL), lambda t: (0, 0)),
        ],
        out_specs=pl.BlockSpec((LN_TILE, D_MODEL), lambda t: (t, 0)),
        out_shape=jax.ShapeDtypeStruct((n, D_MODEL), F32),
        compiler_params=_params("arbitrary"),
        name="moe_combine",
    )(x1, yg, wsel.reshape(2, T_ALL // LN_TILE, LN_TILE // LANES, LANES), mods, lng, lnb)


def _moe_rows_of_tokens(h2p, logits, b_router, layer, w_gate, w_up, w_down):
    pos, wsel, tile_expert, tile_valid, tile_block = _route(logits, b_router)
    xs = _dispatch(h2p, pos)
    y = _experts(xs, tile_expert, tile_valid, tile_block, layer, w_gate, w_up, w_down)
    return _gather_pairs(y, pos), wsel


def _qkv_kernel(x1_ref, yg_ref, wsel_ref, mods0_ref, lng_ref, lnb_ref, mods_ref, w_ref,
                x_ref, qkv_ref, nk_ref, nv_ref, w_s):
    t = pl.program_id(0)

    @pl.when(t == 0)
    def _():
        w_s[...] = w_ref[0].astype(BF16)

    m = mods_ref[0]
    sh1, sc1 = m[0:1, :], m[1:2, :]

    def norm(rows):
        x = _moe_post_norm(x1_ref.at[rows], yg_ref.at[:, rows], wsel_ref, mods0_ref, lng_ref, lnb_ref)
        x_ref[rows, :] = x
        return (x * (1.0 + sc1) + sh1).astype(BF16)

    def project(rows, h):
        r = jnp.dot(h, w_s[...], preferred_element_type=F32)
        qkv_ref[rows, :D_MODEL] = (r[:, :D_MODEL] * (HEAD_DIM ** -0.5)).astype(BF16)
        qkv_ref[rows, D_MODEL:] = r[:, D_MODEL:].astype(BF16)
        return r

    is_ctx = t < T_CTX // QKV_TILE
    rows = slice(0, QKV_TILE)

    @pl.when(is_ctx)
    def _():
        r = project(rows, norm(rows))
        for out_ref, base in ((nk_ref, D_MODEL), (nv_ref, 2 * D_MODEL)):
            for b in range(QKV_TILE // SEQ):
                for p in range(HEAD_PAIRS):
                    pair = r[b * SEQ:(b + 1) * SEQ, base + p * LANES: base + (p + 1) * LANES].T
                    out_ref[b, 0, 2 * p] = pair[:HEAD_DIM]
                    out_ref[b, 0, 2 * p + 1] = pair[HEAD_DIM:]

    @pl.when(jnp.logical_not(is_ctx))
    def _():
        project(rows, norm(rows))


def _qkv(x1, yg, wsel, mods0, lng, lnb, mods1, w_qkv):
    seqs = QKV_TILE // SEQ
    row_spec = pl.BlockSpec((QKV_TILE, D_MODEL), lambda t: (t, 0))
    vec_spec = pl.BlockSpec((1, D_MODEL), lambda t: (0, 0))

    cache_spec = pl.BlockSpec((seqs, 1, N_HEADS, HEAD_DIM, SEQ),
                              lambda t: (jnp.minimum(t, T_CTX // QKV_TILE - 1), 0, 0, 0, 0))
    cache_shape = jax.ShapeDtypeStruct((BATCH, 1, N_HEADS, HEAD_DIM, SEQ), F32)
    return pl.pallas_call(
        _qkv_kernel,
        grid=(T_ALL // QKV_TILE,),
        in_specs=[
            row_spec,
            pl.BlockSpec((2, QKV_TILE, D_PACK), lambda t: (0, t, 0)),
            pl.BlockSpec((2, 1, QKV_TILE // LANES, LANES), lambda t: (0, t, 0, 0)),
            _mods_spec(QKV_TILE),
            vec_spec,
            vec_spec,
            _mods_spec(QKV_TILE),
            pl.BlockSpec((1, D_MODEL, 3 * D_MODEL), lambda t: (0, 0, 0)),
        ],
        out_specs=[row_spec, pl.BlockSpec((QKV_TILE, 3 * D_MODEL), lambda t: (t, 0)), cache_spec, cache_spec],
        out_shape=[jax.ShapeDtypeStruct((T_ALL, D_MODEL), F32),
                   jax.ShapeDtypeStruct((T_ALL, 3 * D_MODEL), BF16), cache_shape, cache_shape],
        scratch_shapes=[pltpu.VMEM((D_MODEL, 3 * D_MODEL), BF16)],
        compiler_params=_params("arbitrary"),
        name="qkv",
    )(x1, yg, wsel.reshape(2, T_ALL // QKV_TILE, QKV_TILE // LANES, LANES), mods0, lng, lnb, mods1, w_qkv)


def _dot_nt(a, b):
    return lax.dot_general(a, b, (((1,), (1,)), ((), ())), preferred_element_type=F32)


def _head_masks():
    lane = lax.broadcasted_iota(jnp.int32, (1, LANES), 1)
    return lane < HEAD_DIM, lane >= HEAD_DIM


def _ctx_attn_kernel(q_ref, k_ref, v_ref, o_ref):
    left, right = _head_masks()
    units = [(p, hh) for p in range(HEAD_PAIRS) for hh in range(2)]

    def scores(u):
        p, hh = units[u]
        cols = slice(p * LANES, (p + 1) * LANES)
        q2 = q_ref[:, cols]
        qh = jnp.where(right if hh else left, q2, jnp.zeros_like(q2))
        return _dot_nt(qh, k_ref[:, cols])

    s = scores(0)
    halves = []
    for u, (p, _) in enumerate(units):
        nxt = scores(u + 1) if u + 1 < len(units) else None
        cols = slice(p * LANES, (p + 1) * LANES)
        e = jnp.exp(s - jnp.max(s, axis=-1, keepdims=True))
        o2 = jnp.dot(e.astype(BF16), v_ref[:, cols], preferred_element_type=F32)
        halves.append(o2 / jnp.sum(e, axis=-1, keepdims=True))
        if len(halves) == 2:
            o_ref[:, cols] = jnp.where(left, halves[0], halves[1]).astype(BF16)
            halves = []
        s = nxt


def _ctx_attention(qkv):
    return pl.pallas_call(
        _ctx_attn_kernel,
        grid=(BATCH,),
        in_specs=[pl.BlockSpec((SEQ, D_MODEL), lambda b, j=j: (b, j)) for j in range(3)],
        out_specs=pl.BlockSpec((SEQ, D_MODEL), lambda b: (b, 0)),
        out_shape=jax.ShapeDtypeStruct((T_ALL, D_MODEL), BF16),
        compiler_params=_params("arbitrary"),
        name="ctx_attention",
    )(qkv, qkv, qkv)


_LAT_Q_BLOCK_ROWS = 4
_LAT_KEY_ROWS = ((0, 8), (0, 12), (4, 16), (8, 16))


def _softmax_rows(s_ref, p_ref, l_ref):
    sc = s_ref[...]
    e = jnp.exp(sc - jnp.max(sc, axis=-1, keepdims=True))
    l_ref[...] = jnp.sum(e, axis=-1, keepdims=True)
    p_ref[...] = e.astype(BF16)


def _lat_attn_kernel(q_ref, k_ref, v_ref, ck_ref, cv_ref, eb_ref, o_ctx_ref, o_ref, bias_s, s_scr, p_scr, l_scr):
    del o_ctx_ref
    left, right = _head_masks()

    @pl.when(pl.program_id(1) == 0)
    def _():
        neg = jnp.full((GRID_W, LANES), NEG_INF, F32)
        for hh in range(2):
            for r in range(GRID_ROWS):
                r0 = min(max(r - WIN_H // 2, 0), GRID_ROWS - WIN_H)
                for j in range(GRID_ROWS // 2):
                    parts = []
                    for kr in (2 * j, 2 * j + 1):
                        parts.append(eb_ref[hh, kr - r + WIN_H - 1] if r0 <= kr < r0 + WIN_H else None)
                    if parts[0] is None and parts[1] is None:
                        val = neg
                    else:
                        val = jnp.where(left, neg if parts[0] is None else parts[0],
                                        neg if parts[1] is None else parts[1])
                    bias_s[hh, r * GRID_W:(r + 1) * GRID_W, j * LANES:(j + 1) * LANES] = val

    ck = jnp.concatenate([ck_ref[0, 0, 0], ck_ref[0, 0, 1]], axis=0).astype(BF16)
    cv = jnp.concatenate([cv_ref[0, 0, 0], cv_ref[0, 0, 1]], axis=0).astype(BF16)
    qrows = _LAT_Q_BLOCK_ROWS * GRID_W
    units = [(qb, hh) for qb in range(len(_LAT_KEY_ROWS)) for hh in range(2)]

    def refs(u):
        nk = (_LAT_KEY_ROWS[units[u][0]][1] - _LAT_KEY_ROWS[units[u][0]][0]) * GRID_W
        width = nk + PAST_LEN
        return nk, s_scr.at[u % 2, :, :width], p_scr.at[u % 2, :, :width], l_scr.at[u % 2]

    def scores(u):
        qb, hh = units[u]
        kr0, kr1 = _LAT_KEY_ROWS[qb]
        qs, ks = slice(qb * qrows, (qb + 1) * qrows), slice(kr0 * GRID_W, kr1 * GRID_W)
        nk, s_ref, _, _ = refs(u)
        q2 = q_ref[qs, :]
        qh = jnp.where(right if hh else left, q2, jnp.zeros_like(q2))
        s_ref[:, :nk] = _dot_nt(qh, k_ref[ks, :]) + bias_s[hh, qs, ks]
        s_ref[:, nk:] = jnp.dot(qh, ck, preferred_element_type=F32)

    def weighted_values(u):
        qb, _ = units[u]
        kr0, kr1 = _LAT_KEY_ROWS[qb]
        nk, _, p_ref, l_ref = refs(u)
        o2 = (jnp.dot(p_ref[:, :nk], v_ref[kr0 * GRID_W:kr1 * GRID_W, :], preferred_element_type=F32)
              + _dot_nt(p_ref[:, nk:], cv))
        return o2 / l_ref[...]

    scores(0)
    halves = []
    for u in range(len(units)):
        if u + 1 < len(units):
            scores(u + 1)
        _, s_ref, p_ref, l_ref = refs(u)
        _softmax_rows(s_ref, p_ref, l_ref)
        halves.append(weighted_values(u))
        if len(halves) == 2:
            qb = units[u][0]
            o_ref[qb * qrows:(qb + 1) * qrows, :] = jnp.where(left, halves[0], halves[1]).astype(BF16)
            halves = []


def _lat_attention(qkv, cache_kt, cache_vt, ebias, o_buf):
    row0 = T_CTX // DEC_SEQ
    qrows = _LAT_Q_BLOCK_ROWS * GRID_W
    max_keys = max(k1 - k0 for k0, k1 in _LAT_KEY_ROWS) * GRID_W + PAST_LEN
    cache_spec = pl.BlockSpec((1, 1, 2, HEAD_DIM, PAST_LEN), lambda p, b: (b, 0, p, 0, 0))
    return pl.pallas_call(
        _lat_attn_kernel,
        grid=(HEAD_PAIRS, DEC_BATCH),
        in_specs=[
            pl.BlockSpec((DEC_SEQ, LANES), lambda p, b: (row0 + b, p)),
            pl.BlockSpec((DEC_SEQ, LANES), lambda p, b: (row0 + b, HEAD_PAIRS + p)),
            pl.BlockSpec((DEC_SEQ, LANES), lambda p, b: (row0 + b, 2 * HEAD_PAIRS + p)),
            cache_spec,
            cache_spec,
            pl.BlockSpec((2, 2 * WIN_H - 1, GRID_W, LANES), lambda p, b: (p, 0, 0, 0)),
            pl.BlockSpec(memory_space=pl.ANY),
        ],
        out_specs=pl.BlockSpec((DEC_SEQ, LANES), lambda p, b: (row0 + b, p)),
        out_shape=jax.ShapeDtypeStruct((T_ALL, D_MODEL), BF16),
        scratch_shapes=[pltpu.VMEM((2, DEC_SEQ, DEC_SEQ), F32),
                        pltpu.VMEM((2, qrows, max_keys), F32),
                        pltpu.VMEM((2, qrows, max_keys), BF16),
                        pltpu.VMEM((2, qrows, 1), F32)],
        input_output_aliases={6: 0},
        compiler_params=_params("arbitrary", "arbitrary"),
        name="lat_attention",
    )(qkv, qkv, qkv, cache_kt, cache_vt, ebias, o_buf)


def _expanded_bias(rel_bias):
    col = jnp.arange(GRID_W)
    col_start = jnp.clip(col - WIN_W // 2, 0, GRID_W - WIN_W)
    col_ok = (col[None, :] >= col_start[:, None]) & (col[None, :] < col_start[:, None] + WIN_W)
    dx = jnp.clip(col[None, :] - col[:, None] + (WIN_W - 1), 0, 2 * WIN_W - 2)
    dx, col_ok = jnp.concatenate([dx, dx], axis=1), jnp.concatenate([col_ok, col_ok], axis=1)
    onehot = (dx[None] == jnp.arange(2 * WIN_W - 1)[:, None, None]).astype(F32)
    eb = jnp.einsum("hyj,jqk->hyqk", rel_bias.astype(F32), onehot, precision=lax.Precision.HIGHEST)
    return jnp.where(col_ok[None, None], eb, NEG_INF)


def _wo_kernel(o_ref, x_ref, mods_ref, w_ref, lng_ref, lnb_ref, w2_ref, x1_ref, h2p_ref, logit_ref, w_s):
    @pl.when(pl.program_id(0) == 0)
    def _():
        w_s[...] = w_ref[0].astype(BF16)

    subs = [slice(r0, r0 + TILE) for r0 in range(0, WO_TILE, TILE)]
    mix = jnp.dot(o_ref[subs[0], :], w_s[...], preferred_element_type=F32)
    for j, rs in enumerate(subs):
        nxt = jnp.dot(o_ref[subs[j + 1], :], w_s[...], preferred_element_type=F32) if j + 1 < len(subs) else None
        _epilogue(x_ref[rs, :], mix, mods_ref[0], lng_ref[...], lnb_ref[...], w2_ref,
                  x1_ref.at[rs, :], h2p_ref.at[rs, :], logit_ref.at[:, rs])
        mix = nxt


def _attn_out(o, x, mods, w_o, lng, lnb, w2):
    return pl.pallas_call(
        _wo_kernel,
        grid=(T_ALL // WO_TILE,),
        in_specs=[
            pl.BlockSpec((WO_TILE, D_MODEL), lambda t: (t, 0)),
            pl.BlockSpec((WO_TILE, D_MODEL), lambda t: (t, 0)),
            _mods_spec(WO_TILE),
            pl.BlockSpec((1, D_MODEL, D_MODEL), lambda t: (0, 0, 0)),
        ] + _EPI_IN_SPECS,
        out_specs=[
            pl.BlockSpec((WO_TILE, D_MODEL), lambda t: (t, 0)),
            pl.BlockSpec((WO_TILE, D_PACK), lambda t: (t, 0)),
            pl.BlockSpec((N_EXPERTS, WO_TILE), lambda t: (0, t)),
        ],
        out_shape=_EPI_OUT_SHAPE,
        scratch_shapes=[pltpu.VMEM((D_MODEL, D_MODEL), BF16)],
        compiler_params=_params("arbitrary"),
        name="attn_out",
    )(o, x, mods, w_o, lng, lnb, w2)


def kernel(x_prompt, x_sample, cache_k, cache_v, c, c_ctx, w_ada, b_ada, ln1_g, ln1_b, ln2_g, ln2_b,
           pool_w, pool_scale, w_qkv, w_o, rel_bias, w_router, b_router, w_gate, w_up, w_down):
    cond = jnp.zeros((N_COND, D_MODEL), F32).at[0].set(c_ctx).at[1:1 + DEC_BATCH].set(c)
    mods0, mods1 = _ada(cond, w_ada, b_ada, 0), _ada(cond, w_ada, b_ada, 1)
    wrt = w_router.T
    wrt_hi = wrt.astype(BF16)
    w2 = jnp.concatenate([wrt_hi, (wrt - wrt_hi.astype(F32)).astype(BF16)], axis=0)

    x, h2p, logits = _pool_layer(x_prompt.reshape(T_CTX, D_MODEL), x_sample.reshape(T_LAT, D_MODEL), mods0,
                                 pool_w[0].astype(BF16), pool_scale[0:1], ln1_g[0:1], ln1_b[0:1], w2)
    yg, wsel = _moe_rows_of_tokens(h2p, logits, b_router, 0, w_gate, w_up, w_down)

    x, qkv, new_kt, new_vt = _qkv(x, yg, wsel, mods0, ln2_g[0:1], ln2_b[0:1], mods1, w_qkv)
    o = _lat_attention(qkv, cache_k.transpose(0, 1, 2, 4, 3), cache_v.transpose(0, 1, 2, 4, 3),
                       _expanded_bias(rel_bias[0]), _ctx_attention(qkv))
    x, h2p, logits = _attn_out(o, x, mods1, w_o, ln1_g[1:2], ln1_b[1:2], w2)
    yg, wsel = _moe_rows_of_tokens(h2p, logits, b_router, 1, w_gate, w_up, w_down)
    y_ctx, y_lat = [_combine(x, yg, wsel, mods1, ln2_g[1:2], ln2_b[1:2], t0, n) for t0, n in SEGMENTS]
    return (y_ctx.reshape(BATCH, SEQ, D_MODEL), y_lat.reshape(DEC_BATCH, DEC_SEQ, D_MODEL),
            new_kt.transpose(0, 1, 2, 4, 3), new_vt.transpose(0, 1, 2, 4, 3))
```

```python
import functools

import jax
import jax.numpy as jnp
from jax import lax
from jax.experimental import pallas as pl
from jax.experimental.pallas import tpu as pltpu
from jax.experimental.pallas import tpu_sc as plsc

F32 = jnp.float32
BF16 = jnp.bfloat16

D_MODEL = 1024
BATCH = 16
SEQ = 256
DEC_BATCH = 8
DEC_SEQ = 1024
PAST_LEN = 512
GRID_W = 64
GRID_ROWS = DEC_SEQ // GRID_W
POOL_SIZES = (2, 4, 8, 16)
POOL_GROUP_DIM = D_MODEL // len(POOL_SIZES)
POOL_HALO = 8
N_HEADS = 16
HEAD_DIM = 64
WIN_H = 8
WIN_W = 16
N_EXPERTS = 16
EXPERTS_PER_GROUP = 4
N_EXPERT_GROUPS = 4
D_FF = 512
ALPHA = (2.0 * 2) ** 0.25
LN_EPS = 1e-5
NEG_INF = -1e30

T_CTX = BATCH * SEQ
T_LAT = DEC_BATCH * DEC_SEQ
T_ALL = T_CTX + T_LAT
N_COND = 16
TILE = 256
TILES_PER_LAT_SEQ = DEC_SEQ // TILE
MOE_TILE = 1024
MOE_SUB = 256
X_SLOTS = 3
SEGMENTS = ((0, T_CTX), (T_CTX, T_LAT))
LN_TILE = 512
QKV_TILE = 512
POOL_TILE = 512
WO_TILE = 1024
SC_WINDOW = 64
D_PACK = D_MODEL // 2
ROUTE_ROWS = T_ALL // 128
HEAD_PAIRS = N_HEADS // 2
LANES = 128
VMEM_LIMIT = 56 * 1024 * 1024


def _cond_row(t, tile):
    ctx_tiles = T_CTX // tile
    per_seq = DEC_SEQ // tile
    return jnp.maximum(t - ctx_tiles + per_seq, 0) // per_seq


def _params(*sem):
    return pltpu.CompilerParams(dimension_semantics=sem, vmem_limit_bytes=VMEM_LIMIT)


def _pack_halves(v):
    half = v.shape[1] // 2
    hi = lax.bitcast_convert_type(v[:, :half].astype(F32), jnp.uint32)
    lo = lax.bitcast_convert_type(v[:, half:].astype(F32), jnp.uint32)
    return hi | (lo >> 16)


def _unpack_halves(p):
    left = lax.bitcast_convert_type(p & jnp.uint32(0xFFFF0000), F32)
    right = lax.bitcast_convert_type(p << 16, F32)
    return left, right


def _ada_kernel(cond_ref, w_ref, b_ref, o_ref):
    cnd = cond_ref[...]
    act = cnd * jax.nn.sigmoid(cnd)
    a_hi = act.astype(BF16)
    a_lo = (act - a_hi.astype(F32)).astype(BF16)
    w = w_ref[0]
    w_hi = w.astype(BF16)
    w_lo = (w - w_hi.astype(F32)).astype(BF16)
    a2 = jnp.concatenate([a_hi, a_lo], axis=0)
    p = jnp.dot(a2, w_hi, preferred_element_type=F32)
    q = jnp.dot(a_hi, w_lo, preferred_element_type=F32)
    o_ref[0] = p[:N_COND] + p[N_COND:] + q + b_ref[0]


def _ada(cond, w_ada, b_ada, layer):
    depth, d, n = w_ada.shape
    bn = 1536
    mods = pl.pallas_call(
        _ada_kernel,
        grid=(n // bn,),
        in_specs=[
            pl.BlockSpec((N_COND, d), lambda j: (0, 0)),
            pl.BlockSpec((1, d, bn), lambda j: (layer, 0, j)),
            pl.BlockSpec((1, 1, bn), lambda j: (layer, 0, j)),
        ],
        out_specs=pl.BlockSpec((1, N_COND, bn), lambda j: (0, 0, j)),
        out_shape=jax.ShapeDtypeStruct((1, N_COND, n), F32),
        compiler_params=_params("arbitrary"),
        name="ada",
    )(cond, w_ada, b_ada.reshape(depth, 1, n))
    return mods.reshape(N_COND, 6, D_MODEL)


def _post_norm(x, upd, g, b):
    y = ALPHA * x + upd
    mu = jnp.mean(y, axis=-1, keepdims=True)
    yc = y - mu
    var = jnp.mean(yc * yc, axis=-1, keepdims=True)
    return yc * lax.rsqrt(var + LN_EPS) * g + b


def _router_logits(h2, h_hi, w2_ref):
    h_lo = (h2 - h_hi.astype(F32)).astype(BF16)
    w2 = w2_ref[...]
    p = _dot_nt(w2, h_hi)
    q = _dot_nt(w2[:N_EXPERTS], h_lo)
    return p[:N_EXPERTS] + p[N_EXPERTS:] + q


def _epilogue(x, mix, m, lng, lnb, w2_ref, x1_ref, h2p_ref, logit_ref):
    g1, sh2, sc2 = m[2:3, :], m[3:4, :], m[4:5, :]
    x1 = _post_norm(x, g1 * mix, lng, lnb)
    h2 = x1 * (1.0 + sc2) + sh2
    h_hi = h2.astype(BF16)
    x1_ref[...] = x1
    h2p_ref[...] = _pack_halves(h_hi)
    logit_ref[...] = _router_logits(h2, h_hi, w2_ref)


_EPI_IN_SPECS = [
    pl.BlockSpec((1, D_MODEL), lambda t: (0, 0)),
    pl.BlockSpec((1, D_MODEL), lambda t: (0, 0)),
    pl.BlockSpec((2 * N_EXPERTS, D_MODEL), lambda t: (0, 0)),
]
_EPI_OUT_SHAPE = [
    jax.ShapeDtypeStruct((T_ALL, D_MODEL), F32),
    jax.ShapeDtypeStruct((T_ALL, D_PACK), jnp.uint32),
    jax.ShapeDtypeStruct((N_EXPERTS, T_ALL), F32),
]


def _route_kernel(logit_ref, br_ref, pos_ref, w_ref, table_ref):
    aff = jax.nn.sigmoid(logit_ref[...])
    sel = aff + br_ref[...]
    sel_rows = [sel[e] for e in range(N_EXPERTS)]
    aff_rows = [aff[e] for e in range(N_EXPERTS)]

    def group_score(g):
        r = sel_rows[g * EXPERTS_PER_GROUP:(g + 1) * EXPERTS_PER_GROUP]
        best = None
        for i in range(EXPERTS_PER_GROUP):
            for j in range(i + 1, EXPERTS_PER_GROUP):
                pair = r[i] + r[j]
                best = pair if best is None else jnp.maximum(best, pair)
        return best

    best = group_score(0)
    gidx = jnp.zeros_like(best, dtype=jnp.int32)
    for g in range(1, N_EXPERT_GROUPS):
        sc = group_score(g)
        better = sc > best
        gidx = jnp.where(better, g, gidx)
        best = jnp.where(better, sc, best)

    def pick_group(rows, j):
        out = rows[j]
        for g in range(1, N_EXPERT_GROUPS):
            out = jnp.where(gidx == g, rows[g * EXPERTS_PER_GROUP + j], out)
        return out

    cand = [pick_group(sel_rows, j) for j in range(EXPERTS_PER_GROUP)]
    cand_aff = [pick_group(aff_rows, j) for j in range(EXPERTS_PER_GROUP)]

    def argmax_first(vals):
        bv, bi, ba = vals[0], jnp.zeros_like(gidx), cand_aff[0]
        for j in range(1, EXPERTS_PER_GROUP):
            better = vals[j] > bv
            bv = jnp.where(better, vals[j], bv)
            bi = jnp.where(better, j, bi)
            ba = jnp.where(better, cand_aff[j], ba)
        return bi, ba

    i1, a1 = argmax_first(cand)
    rest = [jnp.where(i1 == j, -jnp.inf, cand[j]) for j in range(EXPERTS_PER_GROUP)]
    i2, a2 = argmax_first(rest)
    denom = a1 + a2
    base = gidx * EXPERTS_PER_GROUP
    w_ref[0] = a1 / denom
    w_ref[1] = a2 / denom
    _plan(base + i1, base + i2, pos_ref, table_ref)


def _plan(e1, e2, pos_ref, table_ref):
    rows = e1.shape[0]
    upper = (lax.broadcasted_iota(jnp.int32, (LANES, LANES), 0)
             <= lax.broadcasted_iota(jnp.int32, (LANES, LANES), 1)).astype(BF16)
    lower = (lax.broadcasted_iota(jnp.int32, (rows, rows), 1)
             < lax.broadcasted_iota(jnp.int32, (rows, rows), 0)).astype(BF16)
    masks = [(choice == e) for e in range(N_EXPERTS) for choice in (e1, e2)]
    hits = jnp.concatenate([m.astype(BF16) for m in masks], axis=0)
    within = jnp.dot(hits, upper, preferred_element_type=F32)
    totals = [within[i * rows:(i + 1) * rows, LANES - 1:LANES] for i in range(2 * N_EXPERTS)]
    spread = jnp.concatenate([jnp.broadcast_to(t, (rows, LANES)) for t in totals], axis=1).astype(BF16)
    above = jnp.dot(lower, spread, preferred_element_type=F32)

    tile_start = lax.broadcasted_iota(jnp.int32, (1, LANES), 1).astype(F32) * MOE_TILE
    start = jnp.zeros((1, 1), F32)
    pos = [jnp.zeros((rows, LANES), F32), jnp.zeros((rows, LANES), F32)]
    tile_expert = jnp.zeros((1, LANES), F32)
    tile_valid = jnp.zeros((1, LANES), F32)
    tile_block = jnp.zeros((1, LANES), F32)
    used = jnp.zeros((1, LANES), F32)
    last_expert = jnp.zeros((1, 1), F32)
    last_block = jnp.zeros((1, 1), F32)
    for e in range(N_EXPERTS):
        first = jnp.zeros((1, 1), F32)
        for c in range(2):
            i = 2 * e + c
            hit = masks[i]
            rank = (within[i * rows:(i + 1) * rows] - hit.astype(F32)
                    + above[:, i * LANES:(i + 1) * LANES] + (start + first))
            pos[c] = jnp.where(hit, rank, pos[c])
            first = first + jnp.sum(totals[i], axis=0, keepdims=True)
        count = first
        n_tiles = jnp.floor((count + (MOE_TILE - 1)) * (1.0 / MOE_TILE))
        end = start + n_tiles * MOE_TILE
        inside = jnp.logical_and(tile_start >= start, tile_start < end)
        block_in_expert = n_tiles - 1.0 - (tile_start - start) * (1.0 / MOE_TILE)
        tile_expert = jnp.where(inside, float(e), tile_expert)
        tile_valid = jnp.where(inside, jnp.clip(count - block_in_expert * MOE_TILE, 0.0, float(MOE_TILE)), tile_valid)
        tile_block = jnp.where(inside, start * (1.0 / MOE_TILE) + block_in_expert, tile_block)
        used = jnp.where(inside, 1.0, used)
        has_rows = count > 0.0
        last_expert = jnp.where(has_rows, float(e), last_expert)
        last_block = jnp.where(has_rows, start * (1.0 / MOE_TILE), last_block)
        start = end
    pos_ref[0] = pos[0].astype(jnp.int32)
    pos_ref[1] = pos[1].astype(jnp.int32)
    idle = used == 0.0
    table_ref[0:1, :] = jnp.where(idle, last_expert, tile_expert).astype(jnp.int32)
    table_ref[1:2, :] = tile_valid.astype(jnp.int32)
    table_ref[2:3, :] = jnp.where(idle, last_block, tile_block).astype(jnp.int32)
    table_ref[3:8, :] = jnp.zeros((5, LANES), jnp.int32)


def _route(logits, b_router):
    tiles = _moe_rows(T_ALL) // MOE_TILE
    pos, wsel, table = pl.pallas_call(
        _route_kernel,
        out_shape=[jax.ShapeDtypeStruct((2, ROUTE_ROWS, LANES), jnp.int32),
                   jax.ShapeDtypeStruct((2, ROUTE_ROWS, LANES), F32),
                   jax.ShapeDtypeStruct((8, LANES), jnp.int32)],
        compiler_params=pltpu.CompilerParams(vmem_limit_bytes=VMEM_LIMIT),
        name="route",
    )(logits.reshape(N_EXPERTS, ROUTE_ROWS, LANES), b_router.reshape(N_EXPERTS, 1, 1))
    return pos.reshape(2 * T_ALL), wsel, table[0, :tiles], table[1, :tiles], table[2, :tiles]


def _mods_spec(tile, first_tile=0):
    return pl.BlockSpec((1, 6, D_MODEL), lambda t: (_cond_row(first_tile + t, tile), 0, 0))


def _pool_kernel(xa_ref, xb_ref, xp_ref, xn_ref, mods_ref, pw_ref, ps_ref, lng_ref, lnb_ref, w2_ref,
                 x1_ref, h2p_ref, logit_ref):
    t = pl.program_id(0)
    m = mods_ref[0]
    sh1, sc1 = m[0:1, :], m[1:2, :]
    subs = POOL_TILE // TILE
    in_lat = t >= T_CTX // POOL_TILE
    tile_in_seq = (t - T_CTX // POOL_TILE) % (DEC_SEQ // POOL_TILE)
    seq_len = jnp.where(in_lat, DEC_SEQ, SEQ)

    x = jnp.where(in_lat, xb_ref[...], xa_ref[...])
    h = x * (1.0 + sc1) + sh1
    halo_before = xp_ref[...] * (1.0 + sc1) + sh1
    halo_after = xn_ref[...] * (1.0 + sc1) + sh1
    ext = TILE + 2 * POOL_HALO

    def mixer(j):
        place = tile_in_seq * subs + j
        is_first = jnp.logical_or(jnp.logical_not(in_lat), place == 0)
        is_last = jnp.logical_or(jnp.logical_not(in_lat), place == TILES_PER_LAT_SEQ - 1)
        before = halo_before if j == 0 else h[j * TILE - POOL_HALO:j * TILE]
        after = halo_after if j == subs - 1 else h[(j + 1) * TILE:(j + 1) * TILE + POOL_HALO]
        hj = h[j * TILE:(j + 1) * TILE]
        hext = jnp.concatenate([jnp.where(is_first, 0.0, before), hj, jnp.where(is_last, 0.0, after)], axis=0)
        pos = jnp.where(in_lat, place * TILE, 0) + lax.broadcasted_iota(jnp.int32, (TILE, 1), 0)
        outs = []
        for g, w in enumerate(POOL_SIZES):
            lo_c, hi_c = g * POOL_GROUP_DIM, (g + 1) * POOL_GROUP_DIM
            a = hext[:, lo_c:hi_c]
            k = 1
            while k < w:
                a = a + pltpu.roll(a, ext - k, axis=0)
                k *= 2
            off = POOL_HALO - w // 2
            win = pltpu.roll(a, ext - off, axis=0)[:TILE] if off else a[:TILE]
            lo = jnp.maximum(pos - w // 2, 0)
            hi = jnp.minimum(pos - w // 2 + w, seq_len)
            cnt = (hi - lo).astype(F32)
            pooled = win / cnt - hj[:, lo_c:hi_c]
            outs.append(jnp.dot(pooled.astype(BF16), pw_ref[g], preferred_element_type=F32))
        return jnp.concatenate(outs, axis=1) * ps_ref[...]

    mix = mixer(0)
    for j in range(subs):
        nxt = mixer(j + 1) if j + 1 < subs else None
        rs = slice(j * TILE, (j + 1) * TILE)
        _epilogue(x[rs, :], mix, m, lng_ref[...], lnb_ref[...], w2_ref,
                  x1_ref.at[rs, :], h2p_ref.at[rs, :], logit_ref.at[:, rs])
        mix = nxt


def _pool_layer(x_ctx, x_lat, mods, pool_w, pool_scale, lng, lnb, w2):
    ctx_tiles = T_CTX // POOL_TILE
    halo_blocks = POOL_TILE // POOL_HALO
    last_halo = T_LAT // POOL_HALO - 1
    return pl.pallas_call(
        _pool_kernel,
        grid=(T_ALL // POOL_TILE,),
        in_specs=[
            pl.BlockSpec((POOL_TILE, D_MODEL), lambda t: (jnp.minimum(t, ctx_tiles - 1), 0)),
            pl.BlockSpec((POOL_TILE, D_MODEL), lambda t: (jnp.maximum(t - ctx_tiles, 0), 0)),
            pl.BlockSpec((POOL_HALO, D_MODEL), lambda t: (jnp.maximum((t - ctx_tiles) * halo_blocks - 1, 0), 0)),
            pl.BlockSpec((POOL_HALO, D_MODEL),
                         lambda t: (jnp.clip((t - ctx_tiles + 1) * halo_blocks, 0, last_halo), 0)),
            _mods_spec(POOL_TILE),
            pl.BlockSpec((len(POOL_SIZES), POOL_GROUP_DIM, POOL_GROUP_DIM), lambda t: (0, 0, 0)),
            pl.BlockSpec((1, D_MODEL), lambda t: (0, 0)),
        ] + _EPI_IN_SPECS,
        out_specs=[
            pl.BlockSpec((POOL_TILE, D_MODEL), lambda t: (t, 0)),
            pl.BlockSpec((POOL_TILE, D_PACK), lambda t: (t, 0)),
            pl.BlockSpec((N_EXPERTS, POOL_TILE), lambda t: (0, t)),
        ],
        out_shape=_EPI_OUT_SHAPE,
        compiler_params=_params("arbitrary"),
        name="pool_mixer",
    )(x_ctx, x_lat, x_lat, x_lat, mods, pool_w, pool_scale, lng, lnb, w2)


def _moe_rows(n):
    return 2 * n + N_EXPERTS * MOE_TILE


def _expert_kernel(te_ref, nv_ref, tb_ref, x_hbm, wg_ref, wu_ref, wd_ref, y_ref, wgu_s, wd_s, x_ring, x_sems):
    i = pl.program_id(0)
    steps = pl.num_programs(0)
    prev = te_ref[jnp.maximum(i - 1, 0)]
    changed = jnp.logical_or(i == 0, te_ref[i] != prev)

    def x_copy(step):
        slot = step % X_SLOTS
        return pltpu.make_async_copy(x_hbm.at[pl.ds(tb_ref[step] * MOE_TILE, MOE_TILE)], x_ring.at[slot],
                                     x_sems.at[slot])

    def start_x(step):
        @pl.when(nv_ref[step] > 0)
        def _():
            x_copy(step).start()

    @pl.when(i == 0)
    def _():
        for step in range(X_SLOTS - 1):
            start_x(step)

    @pl.when(i + (X_SLOTS - 1) < steps)
    def _():
        start_x(i + (X_SLOTS - 1))

    @pl.when(changed)
    def _():
        wgu_s[:, :D_FF] = wg_ref[0, 0].astype(BF16)
        wgu_s[:, D_FF:] = wu_ref[0, 0].astype(BF16)
        wd_s[...] = wd_ref[0, 0].astype(BF16)

    nv = nv_ref[i]
    x_ref = x_ring.at[i % X_SLOTS]

    @pl.when(nv > 0)
    def _():
        x_copy(i).wait()

    def gate_up(r0):
        rows = r0 + lax.broadcasted_iota(jnp.int32, (MOE_SUB, 1), 0)
        xp = jnp.where(rows < nv, x_ref[r0:r0 + MOE_SUB, :], jnp.uint32(0))
        left, right = _unpack_halves(xp)
        xb = jnp.concatenate([left.astype(BF16), right.astype(BF16)], axis=1)
        return jnp.dot(xb, wgu_s[...], preferred_element_type=F32)

    def down(r0, gu):
        gate, up = gu[:, :D_FF], gu[:, D_FF:]
        he = (gate * jax.nn.sigmoid(gate) * up).astype(BF16)
        y_ref[r0:r0 + MOE_SUB, :] = _pack_halves(jnp.dot(he, wd_s[...], preferred_element_type=F32).astype(BF16))

    def run(n_sub):
        gu = gate_up(0)
        for j in range(n_sub):
            nxt = gate_up((j + 1) * MOE_SUB) if j + 1 < n_sub else None
            down(j * MOE_SUB, gu)
            gu = nxt

    n_subs = MOE_TILE // MOE_SUB
    for n_sub in range(1, n_subs + 1):
        lo = (n_sub - 1) * MOE_SUB
        in_range = nv > lo if n_sub == n_subs else jnp.logical_and(nv > lo, nv <= lo + MOE_SUB)
        pl.when(in_range)(functools.partial(run, n_sub))


def _experts(xs, tile_expert, tile_valid, tile_block, layer, w_gate, w_up, w_down):
    grid_spec = pltpu.PrefetchScalarGridSpec(
        num_scalar_prefetch=3,
        grid=(xs.shape[0] // MOE_TILE,),
        in_specs=[
            pl.BlockSpec(memory_space=pl.ANY),
            pl.BlockSpec((1, 1, D_MODEL, D_FF), lambda i, te, nv, tb: (layer, te[i], 0, 0)),
            pl.BlockSpec((1, 1, D_MODEL, D_FF), lambda i, te, nv, tb: (layer, te[i], 0, 0)),
            pl.BlockSpec((1, 1, D_FF, D_MODEL), lambda i, te, nv, tb: (layer, te[i], 0, 0)),
        ],
        out_specs=pl.BlockSpec((MOE_TILE, D_PACK), lambda i, te, nv, tb: (tb[i], 0)),
        scratch_shapes=[pltpu.VMEM((D_MODEL, 2 * D_FF), BF16), pltpu.VMEM((D_FF, D_MODEL), BF16),
                        pltpu.VMEM((X_SLOTS, MOE_TILE, D_PACK), jnp.uint32), pltpu.SemaphoreType.DMA((X_SLOTS,))],
    )
    return pl.pallas_call(
        _expert_kernel,
        grid_spec=grid_spec,
        out_shape=jax.ShapeDtypeStruct(xs.shape, jnp.uint32),
        compiler_params=_params("arbitrary"),
        name="experts",
    )(tile_expert, tile_valid, tile_block, xs, w_gate, w_up, w_down)


SC_WORKERS = 32


def _sc_mesh():
    return plsc.VectorSubcoreMesh(core_axis_name="core", subcore_axis_name="subcore")


def _sc_worker():
    return lax.axis_index("subcore") * 2 + lax.axis_index("core")


def _dispatch(h2, pos):
    n = h2.shape[0]
    per_worker = n // SC_WORKERS
    pairs = per_worker // (2 * SC_WINDOW)
    one_set = [pltpu.VMEM((SC_WINDOW,), jnp.int32), pltpu.VMEM((SC_WINDOW,), jnp.int32),
               pltpu.VMEM((SC_WINDOW, D_PACK), jnp.uint32),
               pltpu.SemaphoreType.DMA, pltpu.SemaphoreType.DMA, pltpu.SemaphoreType.DMA]

    @functools.partial(pl.kernel, out_type=jax.ShapeDtypeStruct((_moe_rows(n), D_PACK), jnp.uint32),
                       mesh=_sc_mesh(), scratch_types=one_set + one_set, name="moe_dispatch")
    def scatter(x_hbm, i_hbm, o_hbm, *bufs):
        base = _sc_worker() * per_worker
        sets = (bufs[:6], bufs[6:])

        def rows_in(j, s):
            _, _, rows, sem, _, _ = sets[s]
            return pltpu.make_async_copy(x_hbm.at[pl.ds(base + j * SC_WINDOW, SC_WINDOW)], rows, sem)

        def rows_out(k, s):
            idx, rows, sem = sets[s][k], sets[s][2], sets[s][4 + k]
            return pltpu.make_async_copy(rows, o_hbm.at[idx], sem)

        def fetch(j, s):
            t0 = base + j * SC_WINDOW
            pltpu.sync_copy(i_hbm.at[pl.ds(t0, SC_WINDOW)], sets[s][0])
            pltpu.sync_copy(i_hbm.at[pl.ds(n + t0, SC_WINDOW)], sets[s][1])
            rows_in(j, s).start()

        fetch(0, 0)

        @pl.loop(0, pairs)
        def _(g):
            j0, j1 = 2 * g, 2 * g + 1
            fetch(j1, 1)
            rows_in(j0, 0).wait()
            rows_out(0, 0).start()
            rows_out(1, 0).start()
            rows_in(j1, 1).wait()
            rows_out(0, 1).start()
            rows_out(1, 1).start()
            rows_out(0, 0).wait()
            rows_out(1, 0).wait()

            @pl.when(g + 1 < pairs)
            def _():
                fetch(j0 + 2, 0)

            rows_out(0, 1).wait()
            rows_out(1, 1).wait()

    return scatter(h2, pos)


def _gather_pairs(y, pos):
    n = pos.shape[0] // 2
    per_worker = 2 * n // SC_WORKERS
    pairs = per_worker // (2 * SC_WINDOW)
    scratch = [pltpu.VMEM((per_worker,), jnp.int32),
               pltpu.VMEM((SC_WINDOW, D_PACK), jnp.uint32), pltpu.VMEM((SC_WINDOW, D_PACK), jnp.uint32),
               pltpu.SemaphoreType.DMA, pltpu.SemaphoreType.DMA, pltpu.SemaphoreType.DMA, pltpu.SemaphoreType.DMA]

    @functools.partial(pl.kernel, out_type=jax.ShapeDtypeStruct((2 * n, D_PACK), jnp.uint32),
                       mesh=_sc_mesh(), scratch_types=scratch, name="moe_gather")
    def gather(y_hbm, i_hbm, o_hbm, idx_v, rows0, rows1, read0, read1, write0, write1):
        base = _sc_worker() * per_worker
        pltpu.sync_copy(i_hbm.at[pl.ds(base, per_worker)], idx_v)

        def read(j, rows, sem):
            return pltpu.make_async_copy(y_hbm.at[idx_v.at[pl.ds(j * SC_WINDOW, SC_WINDOW)]], rows, sem)

        def write(j, rows, sem):
            return pltpu.make_async_copy(rows, o_hbm.at[pl.ds(base + j * SC_WINDOW, SC_WINDOW)], sem)

        read(0, rows0, read0).start()

        @pl.loop(0, pairs)
        def _(g):
            j0, j1 = 2 * g, 2 * g + 1
            read(j1, rows1, read1).start()
            read(j0, rows0, read0).wait()
            write(j0, rows0, write0).start()
            read(j1, rows1, read1).wait()
            write(j1, rows1, write1).start()
            write(j0, rows0, write0).wait()

            @pl.when(g + 1 < pairs)
            def _():
                read(j0 + 2, rows0, read0).start()

            write(j1, rows1, write1).wait()

    return gather(y, pos).reshape(2, n, D_PACK)


def _moe_post_norm(x_ref, yg_ref, w_ref, mods_ref, lng_ref, lnb_ref):
    g2 = mods_ref[0][5:6, :]
    groups = x_ref.shape[0] // LANES
    wt = jnp.concatenate([w_ref[0, 0], w_ref[1, 0]], axis=0).T
    moe = []
    for r in range(groups):
        rows = slice(r * LANES, (r + 1) * LANES)
        y1 = jnp.concatenate(_unpack_halves(yg_ref[0, rows, :]), axis=1)
        y2 = jnp.concatenate(_unpack_halves(yg_ref[1, rows, :]), axis=1)
        moe.append(wt[:, r:r + 1] * y1 + wt[:, groups + r:groups + r + 1] * y2)
    return _post_norm(x_ref[...], g2 * jnp.concatenate(moe, axis=0), lng_ref[...], lnb_ref[...])


def _combine_kernel(x_ref, yg_ref, w_ref, mods_ref, lng_ref, lnb_ref, o_ref):
    o_ref[...] = _moe_post_norm(x_ref, yg_ref, w_ref, mods_ref, lng_ref, lnb_ref)


def _combine(x1, yg, wsel, mods, lng, lnb, t0, n):
    off = t0 // LN_TILE
    return pl.pallas_call(
        _combine_kernel,
        grid=(n // LN_TILE,),
        in_specs=[
            pl.BlockSpec((LN_TILE, D_MODEL), lambda t: (off + t, 0)),
            pl.BlockSpec((2, LN_TILE, D_PACK), lambda t: (0, off + t, 0)),
            pl.BlockSpec((2, 1, LN_TILE // LANES, LANES), lambda t: (0, off + t, 0, 0)),
            _mods_spec(LN_TILE, off),
            pl.BlockSpec((1, D_MODEL), lambda t: (0, 0)),
            pl.BlockSpec((1, D_MODEL), lambda t: (0, 0)),
        ],
        out_specs=pl.BlockSpec((LN_TILE, D_MODEL), lambda t: (t, 0)),
        out_shape=jax.ShapeDtypeStruct((n, D_MODEL), F32),
        compiler_params=_params("arbitrary"),
        name="moe_combine",
    )(x1, yg, wsel.reshape(2, T_ALL // LN_TILE, LN_TILE // LANES, LANES), mods, lng, lnb)


def _moe_rows_of_tokens(h2p, logits, b_router, layer, w_gate, w_up, w_down):
    pos, wsel, tile_expert, tile_valid, tile_block = _route(logits, b_router)
    xs = _dispatch(h2p, pos)
    y = _experts(xs, tile_expert, tile_valid, tile_block, layer, w_gate, w_up, w_down)
    return _gather_pairs(y, pos), wsel


def _qkv_kernel(x1_ref, yg_ref, wsel_ref, mods0_ref, lng_ref, lnb_ref, mods_ref, w_ref,
                x_ref, qkv_ref, nk_ref, nv_ref, w_s):
    t = pl.program_id(0)

    @pl.when(t == 0)
    def _():
        w_s[...] = w_ref[0].astype(BF16)

    m = mods_ref[0]
    sh1, sc1 = m[0:1, :], m[1:2, :]

    def norm(rows):
        x = _moe_post_norm(x1_ref.at[rows], yg_ref.at[:, rows], wsel_ref, mods0_ref, lng_ref, lnb_ref)
        x_ref[rows, :] = x
        return (x * (1.0 + sc1) + sh1).astype(BF16)

    def project(rows, h):
        r = jnp.dot(h, w_s[...], preferred_element_type=F32)
        qkv_ref[rows, :D_MODEL] = (r[:, :D_MODEL] * (HEAD_DIM ** -0.5)).astype(BF16)
        qkv_ref[rows, D_MODEL:] = r[:, D_MODEL:].astype(BF16)
        return r

    is_ctx = t < T_CTX // QKV_TILE
    rows = slice(0, QKV_TILE)

    @pl.when(is_ctx)
    def _():
        r = project(rows, norm(rows))
        for out_ref, base in ((nk_ref, D_MODEL), (nv_ref, 2 * D_MODEL)):
            for b in range(QKV_TILE // SEQ):
                for p in range(HEAD_PAIRS):
                    pair = r[b * SEQ:(b + 1) * SEQ, base + p * LANES: base + (p + 1) * LANES].T
                    out_ref[b, 0, 2 * p] = pair[:HEAD_DIM]
                    out_ref[b, 0, 2 * p + 1] = pair[HEAD_DIM:]

    @pl.when(jnp.logical_not(is_ctx))
    def _():
        project(rows, norm(rows))


def _qkv(x1, yg, wsel, mods0, lng, lnb, mods1, w_qkv):
    seqs = QKV_TILE // SEQ
    row_spec = pl.BlockSpec((QKV_TILE, D_MODEL), lambda t: (t, 0))
    vec_spec = pl.BlockSpec((1, D_MODEL), lambda t: (0, 0))

    cache_spec = pl.BlockSpec((seqs, 1, N_HEADS, HEAD_DIM, SEQ),
                              lambda t: (jnp.minimum(t, T_CTX // QKV_TILE - 1), 0, 0, 0, 0))
    cache_shape = jax.ShapeDtypeStruct((BATCH, 1, N_HEADS, HEAD_DIM, SEQ), F32)
    return pl.pallas_call(
        _qkv_kernel,
        grid=(T_ALL // QKV_TILE,),
        in_specs=[
            row_spec,
            pl.BlockSpec((2, QKV_TILE, D_PACK), lambda t: (0, t, 0)),
            pl.BlockSpec((2, 1, QKV_TILE // LANES, LANES), lambda t: (0, t, 0, 0)),
            _mods_spec(QKV_TILE),
            vec_spec,
            vec_spec,
            _mods_spec(QKV_TILE),
            pl.BlockSpec((1, D_MODEL, 3 * D_MODEL), lambda t: (0, 0, 0)),
        ],
        out_specs=[row_spec, pl.BlockSpec((QKV_TILE, 3 * D_MODEL), lambda t: (t, 0)), cache_spec, cache_spec],
        out_shape=[jax.ShapeDtypeStruct((T_ALL, D_MODEL), F32),
                   jax.ShapeDtypeStruct((T_ALL, 3 * D_MODEL), BF16), cache_shape, cache_shape],
        scratch_shapes=[pltpu.VMEM((D_MODEL, 3 * D_MODEL), BF16)],
        compiler_params=_params("arbitrary"),
        name="qkv",
    )(x1, yg, wsel.reshape(2, T_ALL // QKV_TILE, QKV_TILE // LANES, LANES), mods0, lng, lnb, mods1, w_qkv)


def _dot_nt(a, b):
    return lax.dot_general(a, b, (((1,), (1,)), ((), ())), preferred_element_type=F32)


def _head_masks():
    lane = lax.broadcasted_iota(jnp.int32, (1, LANES), 1)
    return lane < HEAD_DIM, lane >= HEAD_DIM


def _ctx_attn_kernel(q_ref, k_ref, v_ref, o_ref):
    left, right = _head_masks()
    units = [(p, hh) for p in range(HEAD_PAIRS) for hh in range(2)]

    def scores(u):
        p, hh = units[u]
        cols = slice(p * LANES, (p + 1) * LANES)
        q2 = q_ref[:, cols]
        qh = jnp.where(right if hh else left, q2, jnp.zeros_like(q2))
        return _dot_nt(qh, k_ref[:, cols])

    s = scores(0)
    halves = []
    for u, (p, _) in enumerate(units):
        nxt = scores(u + 1) if u + 1 < len(units) else None
        cols = slice(p * LANES, (p + 1) * LANES)
        e = jnp.exp(s - jnp.max(s, axis=-1, keepdims=True))
        o2 = jnp.dot(e.astype(BF16), v_ref[:, cols], preferred_element_type=F32)
        halves.append(o2 / jnp.sum(e, axis=-1, keepdims=True))
        if len(halves) == 2:
            o_ref[:, cols] = jnp.where(left, halves[0], halves[1]).astype(BF16)
            halves = []
        s = nxt


def _ctx_attention(qkv):
    return pl.pallas_call(
        _ctx_attn_kernel,
        grid=(BATCH,),
        in_specs=[pl.BlockSpec((SEQ, D_MODEL), lambda b, j=j: (b, j)) for j in range(3)],
        out_specs=pl.BlockSpec((SEQ, D_MODEL), lambda b: (b, 0)),
        out_shape=jax.ShapeDtypeStruct((T_ALL, D_MODEL), BF16),
        compiler_params=_params("arbitrary"),
        name="ctx_attention",
    )(qkv, qkv, qkv)


_LAT_Q_BLOCK_ROWS = 4
_LAT_KEY_ROWS = ((0, 8), (0, 12), (4, 16), (8, 16))


def _softmax_rows(s_ref, p_ref, l_ref):
    sc = s_ref[...]
    e = jnp.exp(sc - jnp.max(sc, axis=-1, keepdims=True))
    l_ref[...] = jnp.sum(e, axis=-1, keepdims=True)
    p_ref[...] = e.astype(BF16)


def _lat_attn_kernel(q_ref, k_ref, v_ref, ck_ref, cv_ref, eb_ref, o_ctx_ref, o_ref, bias_s, s_scr, p_scr, l_scr):
    del o_ctx_ref
    left, right = _head_masks()

    @pl.when(pl.program_id(1) == 0)
    def _():
        neg = jnp.full((GRID_W, LANES), NEG_INF, F32)
        for hh in range(2):
            for r in range(GRID_ROWS):
                r0 = min(max(r - WIN_H // 2, 0), GRID_ROWS - WIN_H)
                for j in range(GRID_ROWS // 2):
                    parts = []
                    for kr in (2 * j, 2 * j + 1):
                        parts.append(eb_ref[hh, kr - r + WIN_H - 1] if r0 <= kr < r0 + WIN_H else None)
                    if parts[0] is None and parts[1] is None:
                        val = neg
                    else:
                        val = jnp.where(left, neg if parts[0] is None else parts[0],
                                        neg if parts[1] is None else parts[1])
                    bias_s[hh, r * GRID_W:(r + 1) * GRID_W, j * LANES:(j + 1) * LANES] = val

    ck = jnp.concatenate([ck_ref[0, 0, 0], ck_ref[0, 0, 1]], axis=0).astype(BF16)
    cv = jnp.concatenate([cv_ref[0, 0, 0], cv_ref[0, 0, 1]], axis=0).astype(BF16)
    qrows = _LAT_Q_BLOCK_ROWS * GRID_W
    units = [(qb, hh) for qb in range(len(_LAT_KEY_ROWS)) for hh in range(2)]

    def refs(u):
        nk = (_LAT_KEY_ROWS[units[u][0]][1] - _LAT_KEY_ROWS[units[u][0]][0]) * GRID_W
        width = nk + PAST_LEN
        return nk, s_scr.at[u % 2, :, :width], p_scr.at[u % 2, :, :width], l_scr.at[u % 2]

    def scores(u):
        qb, hh = units[u]
        kr0, kr1 = _LAT_KEY_ROWS[qb]
        qs, ks = slice(qb * qrows, (qb + 1) * qrows), slice(kr0 * GRID_W, kr1 * GRID_W)
        nk, s_ref, _, _ = refs(u)
        q2 = q_ref[qs, :]
        qh = jnp.where(right if hh else left, q2, jnp.zeros_like(q2))
        s_ref[:, :nk] = _dot_nt(qh, k_ref[ks, :]) + bias_s[hh, qs, ks]
        s_ref[:, nk:] = jnp.dot(qh, ck, preferred_element_type=F32)

    def weighted_values(u):
        qb, _ = units[u]
        kr0, kr1 = _LAT_KEY_ROWS[qb]
        nk, _, p_ref, l_ref = refs(u)
        o2 = (jnp.dot(p_ref[:, :nk], v_ref[kr0 * GRID_W:kr1 * GRID_W, :], preferred_element_type=F32)
              + _dot_nt(p_ref[:, nk:], cv))
        return o2 / l_ref[...]

    scores(0)
    halves = []
    for u in range(len(units)):
        if u + 1 < len(units):
            scores(u + 1)
        _, s_ref, p_ref, l_ref = refs(u)
        _softmax_rows(s_ref, p_ref, l_ref)
        halves.append(weighted_values(u))
        if len(halves) == 2:
            qb = units[u][0]
            o_ref[qb * qrows:(qb + 1) * qrows, :] = jnp.where(left, halves[0], halves[1]).astype(BF16)
            halves = []


def _lat_attention(qkv, cache_kt, cache_vt, ebias, o_buf):
    row0 = T_CTX // DEC_SEQ
    qrows = _LAT_Q_BLOCK_ROWS * GRID_W
    max_keys = max(k1 - k0 for k0, k1 in _LAT_KEY_ROWS) * GRID_W + PAST_LEN
    cache_spec = pl.BlockSpec((1, 1, 2, HEAD_DIM, PAST_LEN), lambda p, b: (b, 0, p, 0, 0))
    return pl.pallas_call(
        _lat_attn_kernel,
        grid=(HEAD_PAIRS, DEC_BATCH),
        in_specs=[
            pl.BlockSpec((DEC_SEQ, LANES), lambda p, b: (row0 + b, p)),
            pl.BlockSpec((DEC_SEQ, LANES), lambda p, b: (row0 + b, HEAD_PAIRS + p)),
            pl.BlockSpec((DEC_SEQ, LANES), lambda p, b: (row0 + b, 2 * HEAD_PAIRS + p)),
            cache_spec,
            cache_spec,
            pl.BlockSpec((2, 2 * WIN_H - 1, GRID_W, LANES), lambda p, b: (p, 0, 0, 0)),
            pl.BlockSpec(memory_space=pl.ANY),
        ],
        out_specs=pl.BlockSpec((DEC_SEQ, LANES), lambda p, b: (row0 + b, p)),
        out_shape=jax.ShapeDtypeStruct((T_ALL, D_MODEL), BF16),
        scratch_shapes=[pltpu.VMEM((2, DEC_SEQ, DEC_SEQ), F32),
                        pltpu.VMEM((2, qrows, max_keys), F32),
                        pltpu.VMEM((2, qrows, max_keys), BF16),
                        pltpu.VMEM((2, qrows, 1), F32)],
        input_output_aliases={6: 0},
        compiler_params=_params("arbitrary", "arbitrary"),
        name="lat_attention",
    )(qkv, qkv, qkv, cache_kt, cache_vt, ebias, o_buf)


def _bias_kernel(r_ref, o_ref):
    q = lax.broadcasted_iota(jnp.int32, (GRID_W, LANES), 0)
    k = lax.broadcasted_iota(jnp.int32, (GRID_W, LANES), 1) & (GRID_W - 1)
    start = jnp.clip(q - WIN_W // 2, 0, GRID_W - WIN_W)
    col_ok = (k >= start) & (k < start + WIN_W)
    for y in range(2 * WIN_H - 1):
        rows = jnp.broadcast_to(r_ref[0, y:y + 1, :], (GRID_W, LANES))
        rows = pltpu.roll(rows, LANES - (WIN_W - 1), 1, stride=1, stride_axis=0)
        o_ref[0, y] = jnp.where(col_ok, rows, NEG_INF)


def _expanded_bias(rel_bias):
    rel_h, rel_w = 2 * WIN_H - 1, 2 * WIN_W - 1
    r = jnp.pad(rel_bias.astype(F32), ((0, 0), (0, 0), (0, GRID_W - rel_w)))
    r = jnp.concatenate([r, r], axis=2)
    return pl.pallas_call(
        _bias_kernel,
        grid=(N_HEADS,),
        in_specs=[pl.BlockSpec((1, rel_h, LANES), lambda h: (h, 0, 0))],
        out_specs=pl.BlockSpec((1, rel_h, GRID_W, LANES), lambda h: (h, 0, 0, 0)),
        out_shape=jax.ShapeDtypeStruct((N_HEADS, rel_h, GRID_W, LANES), F32),
        compiler_params=_params("arbitrary"),
        name="rel_bias_tiles",
    )(r)


def _wo_kernel(o_ref, x_ref, mods_ref, w_ref, lng_ref, lnb_ref, w2_ref, x1_ref, h2p_ref, logit_ref, w_s):
    @pl.when(pl.program_id(0) == 0)
    def _():
        w_s[...] = w_ref[0].astype(BF16)

    subs = [slice(r0, r0 + TILE) for r0 in range(0, WO_TILE, TILE)]
    mix = jnp.dot(o_ref[subs[0], :], w_s[...], preferred_element_type=F32)
    for j, rs in enumerate(subs):
        nxt = jnp.dot(o_ref[subs[j + 1], :], w_s[...], preferred_element_type=F32) if j + 1 < len(subs) else None
        _epilogue(x_ref[rs, :], mix, mods_ref[0], lng_ref[...], lnb_ref[...], w2_ref,
                  x1_ref.at[rs, :], h2p_ref.at[rs, :], logit_ref.at[:, rs])
        mix = nxt


def _attn_out(o, x, mods, w_o, lng, lnb, w2):
    return pl.pallas_call(
        _wo_kernel,
        grid=(T_ALL // WO_TILE,),
        in_specs=[
            pl.BlockSpec((WO_TILE, D_MODEL), lambda t: (t, 0)),
            pl.BlockSpec((WO_TILE, D_MODEL), lambda t: (t, 0)),
            _mods_spec(WO_TILE),
            pl.BlockSpec((1, D_MODEL, D_MODEL), lambda t: (0, 0, 0)),
        ] + _EPI_IN_SPECS,
        out_specs=[
            pl.BlockSpec((WO_TILE, D_MODEL), lambda t: (t, 0)),
            pl.BlockSpec((WO_TILE, D_PACK), lambda t: (t, 0)),
            pl.BlockSpec((N_EXPERTS, WO_TILE), lambda t: (0, t)),
        ],
        out_shape=_EPI_OUT_SHAPE,
        scratch_shapes=[pltpu.VMEM((D_MODEL, D_MODEL), BF16)],
        compiler_params=_params("arbitrary"),
        name="attn_out",
    )(o, x, mods, w_o, lng, lnb, w2)


def kernel(x_prompt, x_sample, cache_k, cache_v, c, c_ctx, w_ada, b_ada, ln1_g, ln1_b, ln2_g, ln2_b,
           pool_w, pool_scale, w_qkv, w_o, rel_bias, w_router, b_router, w_gate, w_up, w_down):
    cond = jnp.zeros((N_COND, D_MODEL), F32).at[0].set(c_ctx).at[1:1 + DEC_BATCH].set(c)
    mods0, mods1 = _ada(cond, w_ada, b_ada, 0), _ada(cond, w_ada, b_ada, 1)
    wrt = w_router.T
    wrt_hi = wrt.astype(BF16)
    w2 = jnp.concatenate([wrt_hi, (wrt - wrt_hi.astype(F32)).astype(BF16)], axis=0)

    x, h2p, logits = _pool_layer(x_prompt.reshape(T_CTX, D_MODEL), x_sample.reshape(T_LAT, D_MODEL), mods0,
                                 pool_w[0].astype(BF16), pool_scale[0:1], ln1_g[0:1], ln1_b[0:1], w2)
    yg, wsel = _moe_rows_of_tokens(h2p, logits, b_router, 0, w_gate, w_up, w_down)

    x, qkv, new_kt, new_vt = _qkv(x, yg, wsel, mods0, ln2_g[0:1], ln2_b[0:1], mods1, w_qkv)
    o = _lat_attention(qkv, cache_k.transpose(0, 1, 2, 4, 3), cache_v.transpose(0, 1, 2, 4, 3),
                       _expanded_bias(rel_bias[0]), _ctx_attention(qkv))
    x, h2p, logits = _attn_out(o, x, mods1, w_o, ln1_g[1:2], ln1_b[1:2], w2)
    yg, wsel = _moe_rows_of_tokens(h2p, logits, b_router, 1, w_gate, w_up, w_down)
    y_ctx, y_lat = [_combine(x, yg, wsel, mods1, ln2_g[1:2], ln2_b[1:2], t0, n) for t0, n in SEGMENTS]
    return (y_ctx.reshape(BATCH, SEQ, D_MODEL), y_lat.reshape(DEC_BATCH, DEC_SEQ, D_MODEL),
            new_kt.transpose(0, 1, 2, 4, 3), new_vt.transpose(0, 1, 2, 4, 3))
```

```python
import functools
import itertools

import jax
import jax.numpy as jnp
from jax import lax
from jax.experimental import pallas as pl
from jax.experimental.pallas import tpu as pltpu
from jax.experimental.pallas import tpu_sc as plsc

F32 = jnp.float32
BF16 = jnp.bfloat16

D_MODEL = 1024
BATCH = 16
SEQ = 256
DEC_BATCH = 8
DEC_SEQ = 1024
PAST_LEN = 512
GRID_W = 64
GRID_ROWS = DEC_SEQ // GRID_W
POOL_SIZES = (2, 4, 8, 16)
POOL_GROUP_DIM = D_MODEL // len(POOL_SIZES)
POOL_HALO = 8
N_HEADS = 16
HEAD_DIM = 64
WIN_H = 8
WIN_W = 16
N_EXPERTS = 16
EXPERTS_PER_GROUP = 4
N_EXPERT_GROUPS = 4
D_FF = 512
ALPHA = (2.0 * 2) ** 0.25
LN_EPS = 1e-5
NEG_INF = -1e30

T_CTX = BATCH * SEQ
T_LAT = DEC_BATCH * DEC_SEQ
T_ALL = T_CTX + T_LAT
N_COND = 16
TILE = 256
TILES_PER_LAT_SEQ = DEC_SEQ // TILE
MOE_TILE = 1024
MOE_SUB = 256
X_SLOTS = 3
SEGMENTS = ((0, T_CTX), (T_CTX, T_LAT))
LN_TILE = 512
QKV_TILE = 512
POOL_TILE = 512
WO_TILE = 1024
SC_WINDOW = 64
D_PACK = D_MODEL // 2
ROUTE_ROWS = T_ALL // 128
HEAD_PAIRS = N_HEADS // 2
LANES = 128
VMEM_LIMIT = 56 * 1024 * 1024


def _cond_row(t, tile):
    ctx_tiles = T_CTX // tile
    per_seq = DEC_SEQ // tile
    return jnp.maximum(t - ctx_tiles + per_seq, 0) // per_seq


def _params(*sem):
    return pltpu.CompilerParams(dimension_semantics=sem, vmem_limit_bytes=VMEM_LIMIT)


def _pack_halves(v):
    half = v.shape[1] // 2
    hi = lax.bitcast_convert_type(v[:, :half].astype(F32), jnp.uint32)
    lo = lax.bitcast_convert_type(v[:, half:].astype(F32), jnp.uint32)
    return hi | (lo >> 16)


def _unpack_halves(p):
    left = lax.bitcast_convert_type(p & jnp.uint32(0xFFFF0000), F32)
    right = lax.bitcast_convert_type(p << 16, F32)
    return left, right


def _ada_kernel(cond_ref, w_ref, b_ref, o_ref):
    cnd = cond_ref[...]
    act = cnd * jax.nn.sigmoid(cnd)
    a_hi = act.astype(BF16)
    a_lo = (act - a_hi.astype(F32)).astype(BF16)
    w = w_ref[0]
    w_hi = w.astype(BF16)
    w_lo = (w - w_hi.astype(F32)).astype(BF16)
    a2 = jnp.concatenate([a_hi, a_lo], axis=0)
    p = jnp.dot(a2, w_hi, preferred_element_type=F32)
    q = jnp.dot(a_hi, w_lo, preferred_element_type=F32)
    o_ref[0] = p[:N_COND] + p[N_COND:] + q + b_ref[0]


def _ada(cond, w_ada, b_ada, layer):
    depth, d, n = w_ada.shape
    bn = 1536
    mods = pl.pallas_call(
        _ada_kernel,
        grid=(n // bn,),
        in_specs=[
            pl.BlockSpec((N_COND, d), lambda j: (0, 0)),
            pl.BlockSpec((1, d, bn), lambda j: (layer, 0, j)),
            pl.BlockSpec((1, 1, bn), lambda j: (layer, 0, j)),
        ],
        out_specs=pl.BlockSpec((1, N_COND, bn), lambda j: (0, 0, j)),
        out_shape=jax.ShapeDtypeStruct((1, N_COND, n), F32),
        compiler_params=_params("arbitrary"),
        name="ada",
    )(cond, w_ada, b_ada.reshape(depth, 1, n))
    return mods.reshape(N_COND, 6, D_MODEL)


def _post_norm(x, upd, g, b):
    y = ALPHA * x + upd
    mu = jnp.mean(y, axis=-1, keepdims=True)
    yc = y - mu
    var = jnp.mean(yc * yc, axis=-1, keepdims=True)
    return yc * lax.rsqrt(var + LN_EPS) * g + b


def _router_logits(h2, h_hi, w2_ref):
    h_lo = (h2 - h_hi.astype(F32)).astype(BF16)
    w2 = w2_ref[...]
    p = _dot_nt(w2, h_hi)
    q = _dot_nt(w2[:N_EXPERTS], h_lo)
    return p[:N_EXPERTS] + p[N_EXPERTS:] + q


def _epilogue(x, mix, m, lng, lnb, w2_ref, x1_ref, h2p_ref, logit_ref):
    g1, sh2, sc2 = m[2:3, :], m[3:4, :], m[4:5, :]
    x1 = _post_norm(x, g1 * mix, lng, lnb)
    h2 = x1 * (1.0 + sc2) + sh2
    h_hi = h2.astype(BF16)
    x1_ref[...] = x1
    h2p_ref[...] = _pack_halves(h_hi)
    logit_ref[...] = _router_logits(h2, h_hi, w2_ref)


_EPI_IN_SPECS = [
    pl.BlockSpec((1, D_MODEL), lambda t: (0, 0)),
    pl.BlockSpec((1, D_MODEL), lambda t: (0, 0)),
    pl.BlockSpec((2 * N_EXPERTS, D_MODEL), lambda t: (0, 0)),
]
_EPI_OUT_SHAPE = [
    jax.ShapeDtypeStruct((T_ALL, D_MODEL), F32),
    jax.ShapeDtypeStruct((T_ALL, D_PACK), jnp.uint32),
    jax.ShapeDtypeStruct((N_EXPERTS, T_ALL), F32),
]


def _route_kernel(logit_ref, br_ref, pos_ref, w_ref, table_ref):
    aff = jax.nn.sigmoid(logit_ref[...])
    sel = aff + br_ref[...]
    sel_rows = [sel[e] for e in range(N_EXPERTS)]
    aff_rows = [aff[e] for e in range(N_EXPERTS)]

    def group_score(g):
        r = sel_rows[g * EXPERTS_PER_GROUP:(g + 1) * EXPERTS_PER_GROUP]
        best = None
        for i in range(EXPERTS_PER_GROUP):
            for j in range(i + 1, EXPERTS_PER_GROUP):
                pair = r[i] + r[j]
                best = pair if best is None else jnp.maximum(best, pair)
        return best

    best = group_score(0)
    gidx = jnp.zeros_like(best, dtype=jnp.int32)
    for g in range(1, N_EXPERT_GROUPS):
        sc = group_score(g)
        better = sc > best
        gidx = jnp.where(better, g, gidx)
        best = jnp.where(better, sc, best)

    def pick_group(rows, j):
        out = rows[j]
        for g in range(1, N_EXPERT_GROUPS):
            out = jnp.where(gidx == g, rows[g * EXPERTS_PER_GROUP + j], out)
        return out

    cand = [pick_group(sel_rows, j) for j in range(EXPERTS_PER_GROUP)]
    cand_aff = [pick_group(aff_rows, j) for j in range(EXPERTS_PER_GROUP)]

    def argmax_first(vals):
        bv, bi, ba = vals[0], jnp.zeros_like(gidx), cand_aff[0]
        for j in range(1, EXPERTS_PER_GROUP):
            better = vals[j] > bv
            bv = jnp.where(better, vals[j], bv)
            bi = jnp.where(better, j, bi)
            ba = jnp.where(better, cand_aff[j], ba)
        return bi, ba

    i1, a1 = argmax_first(cand)
    rest = [jnp.where(i1 == j, -jnp.inf, cand[j]) for j in range(EXPERTS_PER_GROUP)]
    i2, a2 = argmax_first(rest)
    denom = a1 + a2
    base = gidx * EXPERTS_PER_GROUP
    w_ref[0] = a1 / denom
    w_ref[1] = a2 / denom
    _plan(base + i1, base + i2, pos_ref, table_ref)


def _plan(e1, e2, pos_ref, table_ref):
    rows = e1.shape[0]
    upper = (lax.broadcasted_iota(jnp.int32, (LANES, LANES), 0)
             <= lax.broadcasted_iota(jnp.int32, (LANES, LANES), 1)).astype(BF16)
    lower = (lax.broadcasted_iota(jnp.int32, (rows, rows), 1)
             < lax.broadcasted_iota(jnp.int32, (rows, rows), 0)).astype(BF16)
    masks = [(choice == e) for e in range(N_EXPERTS) for choice in (e1, e2)]
    hits = jnp.concatenate([m.astype(BF16) for m in masks], axis=0)
    within = jnp.dot(hits, upper, preferred_element_type=F32)
    totals = [within[i * rows:(i + 1) * rows, LANES - 1:LANES] for i in range(2 * N_EXPERTS)]
    spread = jnp.concatenate([jnp.broadcast_to(t, (rows, LANES)) for t in totals], axis=1).astype(BF16)
    above = jnp.dot(lower, spread, preferred_element_type=F32)

    tile_start = lax.broadcasted_iota(jnp.int32, (1, LANES), 1).astype(F32) * MOE_TILE
    start = jnp.zeros((1, 1), F32)
    pos = [jnp.zeros((rows, LANES), F32), jnp.zeros((rows, LANES), F32)]
    tile_expert = jnp.zeros((1, LANES), F32)
    tile_valid = jnp.zeros((1, LANES), F32)
    tile_block = jnp.zeros((1, LANES), F32)
    used = jnp.zeros((1, LANES), F32)
    last_expert = jnp.zeros((1, 1), F32)
    last_block = jnp.zeros((1, 1), F32)
    for e in range(N_EXPERTS):
        first = jnp.zeros((1, 1), F32)
        for c in range(2):
            i = 2 * e + c
            hit = masks[i]
            rank = (within[i * rows:(i + 1) * rows] - hit.astype(F32)
                    + above[:, i * LANES:(i + 1) * LANES] + (start + first))
            pos[c] = jnp.where(hit, rank, pos[c])
            first = first + jnp.sum(totals[i], axis=0, keepdims=True)
        count = first
        n_tiles = jnp.floor((count + (MOE_TILE - 1)) * (1.0 / MOE_TILE))
        end = start + n_tiles * MOE_TILE
        inside = jnp.logical_and(tile_start >= start, tile_start < end)
        block_in_expert = n_tiles - 1.0 - (tile_start - start) * (1.0 / MOE_TILE)
        tile_expert = jnp.where(inside, float(e), tile_expert)
        tile_valid = jnp.where(inside, jnp.clip(count - block_in_expert * MOE_TILE, 0.0, float(MOE_TILE)), tile_valid)
        tile_block = jnp.where(inside, start * (1.0 / MOE_TILE) + block_in_expert, tile_block)
        used = jnp.where(inside, 1.0, used)
        has_rows = count > 0.0
        last_expert = jnp.where(has_rows, float(e), last_expert)
        last_block = jnp.where(has_rows, start * (1.0 / MOE_TILE), last_block)
        start = end
    pos_ref[0] = pos[0].astype(jnp.int32)
    pos_ref[1] = pos[1].astype(jnp.int32)
    idle = used == 0.0
    table_ref[0:1, :] = jnp.where(idle, last_expert, tile_expert).astype(jnp.int32)
    table_ref[1:2, :] = tile_valid.astype(jnp.int32)
    table_ref[2:3, :] = jnp.where(idle, last_block, tile_block).astype(jnp.int32)
    table_ref[3:8, :] = jnp.zeros((5, LANES), jnp.int32)


def _route(logits, b_router):
    tiles = _moe_rows(T_ALL) // MOE_TILE
    pos, wsel, table = pl.pallas_call(
        _route_kernel,
        out_shape=[jax.ShapeDtypeStruct((2, ROUTE_ROWS, LANES), jnp.int32),
                   jax.ShapeDtypeStruct((2, ROUTE_ROWS, LANES), F32),
                   jax.ShapeDtypeStruct((8, LANES), jnp.int32)],
        compiler_params=pltpu.CompilerParams(vmem_limit_bytes=VMEM_LIMIT),
        name="route",
    )(logits.reshape(N_EXPERTS, ROUTE_ROWS, LANES), b_router.reshape(N_EXPERTS, 1, 1))
    return pos.reshape(2 * T_ALL), wsel, table[0, :tiles], table[1, :tiles], table[2, :tiles]


def _mods_spec(tile, first_tile=0):
    return pl.BlockSpec((1, 6, D_MODEL), lambda t: (_cond_row(first_tile + t, tile), 0, 0))


def _pool_kernel(xa_ref, xb_ref, xp_ref, xn_ref, mods_ref, pw_ref, ps_ref, lng_ref, lnb_ref, w2_ref,
                 x1_ref, h2p_ref, logit_ref):
    t = pl.program_id(0)
    m = mods_ref[0]
    sh1, sc1 = m[0:1, :], m[1:2, :]
    subs = POOL_TILE // TILE
    in_lat = t >= T_CTX // POOL_TILE
    tile_in_seq = (t - T_CTX // POOL_TILE) % (DEC_SEQ // POOL_TILE)
    seq_len = jnp.where(in_lat, DEC_SEQ, SEQ)

    x = jnp.where(in_lat, xb_ref[...], xa_ref[...])
    h = x * (1.0 + sc1) + sh1
    halo_before = xp_ref[...] * (1.0 + sc1) + sh1
    halo_after = xn_ref[...] * (1.0 + sc1) + sh1
    ext = TILE + 2 * POOL_HALO

    def mixer(j):
        place = tile_in_seq * subs + j
        is_first = jnp.logical_or(jnp.logical_not(in_lat), place == 0)
        is_last = jnp.logical_or(jnp.logical_not(in_lat), place == TILES_PER_LAT_SEQ - 1)
        before = halo_before if j == 0 else h[j * TILE - POOL_HALO:j * TILE]
        after = halo_after if j == subs - 1 else h[(j + 1) * TILE:(j + 1) * TILE + POOL_HALO]
        hj = h[j * TILE:(j + 1) * TILE]
        hext = jnp.concatenate([jnp.where(is_first, 0.0, before), hj, jnp.where(is_last, 0.0, after)], axis=0)
        pos = jnp.where(in_lat, place * TILE, 0) + lax.broadcasted_iota(jnp.int32, (TILE, 1), 0)
        outs = []
        for g, w in enumerate(POOL_SIZES):
            lo_c, hi_c = g * POOL_GROUP_DIM, (g + 1) * POOL_GROUP_DIM
            a = hext[:, lo_c:hi_c]
            k = 1
            while k < w:
                a = a + pltpu.roll(a, ext - k, axis=0)
                k *= 2
            off = POOL_HALO - w // 2
            win = pltpu.roll(a, ext - off, axis=0)[:TILE] if off else a[:TILE]
            lo = jnp.maximum(pos - w // 2, 0)
            hi = jnp.minimum(pos - w // 2 + w, seq_len)
            cnt = (hi - lo).astype(F32)
            pooled = win / cnt - hj[:, lo_c:hi_c]
            outs.append(jnp.dot(pooled.astype(BF16), pw_ref[g], preferred_element_type=F32))
        return jnp.concatenate(outs, axis=1) * ps_ref[...]

    mix = mixer(0)
    for j in range(subs):
        nxt = mixer(j + 1) if j + 1 < subs else None
        rs = slice(j * TILE, (j + 1) * TILE)
        _epilogue(x[rs, :], mix, m, lng_ref[...], lnb_ref[...], w2_ref,
                  x1_ref.at[rs, :], h2p_ref.at[rs, :], logit_ref.at[:, rs])
        mix = nxt


def _pool_layer(x_ctx, x_lat, mods, pool_w, pool_scale, lng, lnb, w2):
    ctx_tiles = T_CTX // POOL_TILE
    halo_blocks = POOL_TILE // POOL_HALO
    last_halo = T_LAT // POOL_HALO - 1
    return pl.pallas_call(
        _pool_kernel,
        grid=(T_ALL // POOL_TILE,),
        in_specs=[
            pl.BlockSpec((POOL_TILE, D_MODEL), lambda t: (jnp.minimum(t, ctx_tiles - 1), 0)),
            pl.BlockSpec((POOL_TILE, D_MODEL), lambda t: (jnp.maximum(t - ctx_tiles, 0), 0)),
            pl.BlockSpec((POOL_HALO, D_MODEL), lambda t: (jnp.maximum((t - ctx_tiles) * halo_blocks - 1, 0), 0)),
            pl.BlockSpec((POOL_HALO, D_MODEL),
                         lambda t: (jnp.clip((t - ctx_tiles + 1) * halo_blocks, 0, last_halo), 0)),
            _mods_spec(POOL_TILE),
            pl.BlockSpec((len(POOL_SIZES), POOL_GROUP_DIM, POOL_GROUP_DIM), lambda t: (0, 0, 0)),
            pl.BlockSpec((1, D_MODEL), lambda t: (0, 0)),
        ] + _EPI_IN_SPECS,
        out_specs=[
            pl.BlockSpec((POOL_TILE, D_MODEL), lambda t: (t, 0)),
            pl.BlockSpec((POOL_TILE, D_PACK), lambda t: (t, 0)),
            pl.BlockSpec((N_EXPERTS, POOL_TILE), lambda t: (0, t)),
        ],
        out_shape=_EPI_OUT_SHAPE,
        compiler_params=_params("arbitrary"),
        name="pool_mixer",
    )(x_ctx, x_lat, x_lat, x_lat, mods, pool_w, pool_scale, lng, lnb, w2)


def _moe_rows(n):
    return 2 * n + N_EXPERTS * MOE_TILE


def _expert_kernel(te_ref, nv_ref, tb_ref, x_hbm, wg_ref, wu_ref, wd_ref, after_ref, y_ref,
                   wgu_s, wd_s, x_ring, x_sems):
    del after_ref
    i = pl.program_id(0)
    steps = pl.num_programs(0)
    prev = te_ref[jnp.maximum(i - 1, 0)]
    changed = jnp.logical_or(i == 0, te_ref[i] != prev)

    def x_copy(step):
        slot = step % X_SLOTS
        return pltpu.make_async_copy(x_hbm.at[pl.ds(tb_ref[step] * MOE_TILE, MOE_TILE)], x_ring.at[slot],
                                     x_sems.at[slot])

    def start_x(step):
        @pl.when(nv_ref[step] > 0)
        def _():
            x_copy(step).start()

    @pl.when(i == 0)
    def _():
        for step in range(X_SLOTS - 1):
            start_x(step)

    @pl.when(i + (X_SLOTS - 1) < steps)
    def _():
        start_x(i + (X_SLOTS - 1))

    @pl.when(changed)
    def _():
        wgu_s[:, :D_FF] = wg_ref[0, 0].astype(BF16)
        wgu_s[:, D_FF:] = wu_ref[0, 0].astype(BF16)
        wd_s[...] = wd_ref[0, 0].astype(BF16)

    nv = nv_ref[i]
    x_ref = x_ring.at[i % X_SLOTS]

    @pl.when(nv > 0)
    def _():
        x_copy(i).wait()

    def gate_up(r0):
        rows = r0 + lax.broadcasted_iota(jnp.int32, (MOE_SUB, 1), 0)
        xp = jnp.where(rows < nv, x_ref[r0:r0 + MOE_SUB, :], jnp.uint32(0))
        left, right = _unpack_halves(xp)
        xb = jnp.concatenate([left.astype(BF16), right.astype(BF16)], axis=1)
        return jnp.dot(xb, wgu_s[...], preferred_element_type=F32)

    def down(r0, gu):
        gate, up = gu[:, :D_FF], gu[:, D_FF:]
        he = (gate * jax.nn.sigmoid(gate) * up).astype(BF16)
        y_ref[r0:r0 + MOE_SUB, :] = _pack_halves(jnp.dot(he, wd_s[...], preferred_element_type=F32).astype(BF16))

    def run(n_sub):
        gu = gate_up(0)
        for j in range(n_sub):
            nxt = gate_up((j + 1) * MOE_SUB) if j + 1 < n_sub else None
            down(j * MOE_SUB, gu)
            gu = nxt

    n_subs = MOE_TILE // MOE_SUB
    for n_sub in range(1, n_subs + 1):
        lo = (n_sub - 1) * MOE_SUB
        in_range = nv > lo if n_sub == n_subs else jnp.logical_and(nv > lo, nv <= lo + MOE_SUB)
        pl.when(in_range)(functools.partial(run, n_sub))


def _experts(xs, tile_expert, tile_valid, tile_block, layer, w_gate, w_up, w_down, after):
    grid_spec = pltpu.PrefetchScalarGridSpec(
        num_scalar_prefetch=3,
        grid=(xs.shape[0] // MOE_TILE,),
        in_specs=[
            pl.BlockSpec(memory_space=pl.ANY),
            pl.BlockSpec((1, 1, D_MODEL, D_FF), lambda i, te, nv, tb: (layer, te[i], 0, 0)),
            pl.BlockSpec((1, 1, D_MODEL, D_FF), lambda i, te, nv, tb: (layer, te[i], 0, 0)),
            pl.BlockSpec((1, 1, D_FF, D_MODEL), lambda i, te, nv, tb: (layer, te[i], 0, 0)),
            pl.BlockSpec(memory_space=pl.ANY),
        ],
        out_specs=pl.BlockSpec((MOE_TILE, D_PACK), lambda i, te, nv, tb: (tb[i], 0)),
        scratch_shapes=[pltpu.VMEM((D_MODEL, 2 * D_FF), BF16), pltpu.VMEM((D_FF, D_MODEL), BF16),
                        pltpu.VMEM((X_SLOTS, MOE_TILE, D_PACK), jnp.uint32), pltpu.SemaphoreType.DMA((X_SLOTS,))],
    )
    return pl.pallas_call(
        _expert_kernel,
        grid_spec=grid_spec,
        out_shape=jax.ShapeDtypeStruct(xs.shape, jnp.uint32),
        compiler_params=_params("arbitrary"),
        name="experts",
    )(tile_expert, tile_valid, tile_block, xs, w_gate, w_up, w_down, after)


SC_WORKERS = 32


def _sc_mesh():
    return plsc.VectorSubcoreMesh(core_axis_name="core", subcore_axis_name="subcore")


def _sc_worker():
    return lax.axis_index("subcore") * 2 + lax.axis_index("core")


def _dispatch(h2, pos):
    n = h2.shape[0]
    per_worker = n // SC_WORKERS
    pairs = per_worker // (2 * SC_WINDOW)
    one_set = [pltpu.VMEM((SC_WINDOW,), jnp.int32), pltpu.VMEM((SC_WINDOW,), jnp.int32),
               pltpu.VMEM((SC_WINDOW, D_PACK), jnp.uint32),
               pltpu.SemaphoreType.DMA, pltpu.SemaphoreType.DMA, pltpu.SemaphoreType.DMA]

    @functools.partial(pl.kernel, out_type=jax.ShapeDtypeStruct((_moe_rows(n), D_PACK), jnp.uint32),
                       mesh=_sc_mesh(), scratch_types=one_set + one_set, name="moe_dispatch")
    def scatter(x_hbm, i_hbm, o_hbm, *bufs):
        base = _sc_worker() * per_worker
        sets = (bufs[:6], bufs[6:])

        def rows_in(j, s):
            _, _, rows, sem, _, _ = sets[s]
            return pltpu.make_async_copy(x_hbm.at[pl.ds(base + j * SC_WINDOW, SC_WINDOW)], rows, sem)

        def rows_out(k, s):
            idx, rows, sem = sets[s][k], sets[s][2], sets[s][4 + k]
            return pltpu.make_async_copy(rows, o_hbm.at[idx], sem)

        def fetch(j, s):
            t0 = base + j * SC_WINDOW
            pltpu.sync_copy(i_hbm.at[pl.ds(t0, SC_WINDOW)], sets[s][0])
            pltpu.sync_copy(i_hbm.at[pl.ds(n + t0, SC_WINDOW)], sets[s][1])
            rows_in(j, s).start()

        fetch(0, 0)

        @pl.loop(0, pairs)
        def _(g):
            j0, j1 = 2 * g, 2 * g + 1
            fetch(j1, 1)
            rows_in(j0, 0).wait()
            rows_out(0, 0).start()
            rows_out(1, 0).start()
            rows_in(j1, 1).wait()
            rows_out(0, 1).start()
            rows_out(1, 1).start()
            rows_out(0, 0).wait()
            rows_out(1, 0).wait()

            @pl.when(g + 1 < pairs)
            def _():
                fetch(j0 + 2, 0)

            rows_out(0, 1).wait()
            rows_out(1, 1).wait()

    return scatter(h2, pos)


def _gather_pairs(y, pos):
    n = pos.shape[0] // 2
    per_worker = 2 * n // SC_WORKERS
    pairs = per_worker // (2 * SC_WINDOW)
    scratch = [pltpu.VMEM((per_worker,), jnp.int32),
               pltpu.VMEM((SC_WINDOW, D_PACK), jnp.uint32), pltpu.VMEM((SC_WINDOW, D_PACK), jnp.uint32),
               pltpu.SemaphoreType.DMA, pltpu.SemaphoreType.DMA, pltpu.SemaphoreType.DMA, pltpu.SemaphoreType.DMA]

    @functools.partial(pl.kernel, out_type=jax.ShapeDtypeStruct((2 * n, D_PACK), jnp.uint32),
                       mesh=_sc_mesh(), scratch_types=scratch, name="moe_gather")
    def gather(y_hbm, i_hbm, o_hbm, idx_v, rows0, rows1, read0, read1, write0, write1):
        base = _sc_worker() * per_worker
        pltpu.sync_copy(i_hbm.at[pl.ds(base, per_worker)], idx_v)

        def read(j, rows, sem):
            return pltpu.make_async_copy(y_hbm.at[idx_v.at[pl.ds(j * SC_WINDOW, SC_WINDOW)]], rows, sem)

        def write(j, rows, sem):
            return pltpu.make_async_copy(rows, o_hbm.at[pl.ds(base + j * SC_WINDOW, SC_WINDOW)], sem)

        read(0, rows0, read0).start()

        @pl.loop(0, pairs)
        def _(g):
            j0, j1 = 2 * g, 2 * g + 1
            read(j1, rows1, read1).start()
            read(j0, rows0, read0).wait()
            write(j0, rows0, write0).start()
            read(j1, rows1, read1).wait()
            write(j1, rows1, write1).start()
            write(j0, rows0, write0).wait()

            @pl.when(g + 1 < pairs)
            def _():
                read(j0 + 2, rows0, read0).start()

            write(j1, rows1, write1).wait()

    return gather(y, pos).reshape(2, n, D_PACK)


def _moe_post_norm(x_ref, yg_ref, w_ref, mods_ref, lng_ref, lnb_ref):
    g2 = mods_ref[0][5:6, :]
    groups = x_ref.shape[0] // LANES
    wt = jnp.concatenate([w_ref[0, 0], w_ref[1, 0]], axis=0).T
    moe = []
    for r in range(groups):
        rows = slice(r * LANES, (r + 1) * LANES)
        y1 = jnp.concatenate(_unpack_halves(yg_ref[0, rows, :]), axis=1)
        y2 = jnp.concatenate(_unpack_halves(yg_ref[1, rows, :]), axis=1)
        moe.append(wt[:, r:r + 1] * y1 + wt[:, groups + r:groups + r + 1] * y2)
    return _post_norm(x_ref[...], g2 * jnp.concatenate(moe, axis=0), lng_ref[...], lnb_ref[...])


def _combine_kernel(x_ref, yg_ref, w_ref, mods_ref, lng_ref, lnb_ref, o_ref):
    o_ref[...] = _moe_post_norm(x_ref, yg_ref, w_ref, mods_ref, lng_ref, lnb_ref)


def _combine(x1, yg, wsel, mods, lng, lnb, t0, n):
    off = t0 // LN_TILE
    return pl.pallas_call(
        _combine_kernel,
        grid=(n // LN_TILE,),
        in_specs=[
            pl.BlockSpec((LN_TILE, D_MODEL), lambda t: (off + t, 0)),
            pl.BlockSpec((2, LN_TILE, D_PACK), lambda t: (0, off + t, 0)),
            pl.BlockSpec((2, 1, LN_TILE // LANES, LANES), lambda t: (0, off + t, 0, 0)),
            _mods_spec(LN_TILE, off),
            pl.BlockSpec((1, D_MODEL), lambda t: (0, 0)),
            pl.BlockSpec((1, D_MODEL), lambda t: (0, 0)),
        ],
        out_specs=pl.BlockSpec((LN_TILE, D_MODEL), lambda t: (t, 0)),
        out_shape=jax.ShapeDtypeStruct((n, D_MODEL), F32),
        compiler_params=_params("arbitrary"),
        name="moe_combine",
    )(x1, yg, wsel.reshape(2, T_ALL // LN_TILE, LN_TILE // LANES, LANES), mods, lng, lnb)


def _moe_rows_of_tokens(h2p, logits, b_router, layer, w_gate, w_up, w_down, after):
    pos, wsel, tile_expert, tile_valid, tile_block = _route(logits, b_router)
    xs = _dispatch(h2p, pos)
    y = _experts(xs, tile_expert, tile_valid, tile_block, layer, w_gate, w_up, w_down, after)
    return _gather_pairs(y, pos), wsel


def _qkv_kernel(x1_ref, yg_ref, wsel_ref, mods0_ref, lng_ref, lnb_ref, mods_ref, w_ref,
                x_ref, qkv_ref, nk_ref, nv_ref, w_s):
    t = pl.program_id(0)

    @pl.when(t == 0)
    def _():
        w_s[...] = w_ref[0].astype(BF16)

    m = mods_ref[0]
    sh1, sc1 = m[0:1, :], m[1:2, :]

    def norm(rows):
        x = _moe_post_norm(x1_ref.at[rows], yg_ref.at[:, rows], wsel_ref, mods0_ref, lng_ref, lnb_ref)
        x_ref[rows, :] = x
        return (x * (1.0 + sc1) + sh1).astype(BF16)

    def project(rows, h):
        r = jnp.dot(h, w_s[...], preferred_element_type=F32)
        qkv_ref[rows, :D_MODEL] = (r[:, :D_MODEL] * (HEAD_DIM ** -0.5)).astype(BF16)
        qkv_ref[rows, D_MODEL:] = r[:, D_MODEL:].astype(BF16)
        return r

    is_ctx = t < T_CTX // QKV_TILE
    rows = slice(0, QKV_TILE)

    @pl.when(is_ctx)
    def _():
        r = project(rows, norm(rows))
        for out_ref, base in ((nk_ref, D_MODEL), (nv_ref, 2 * D_MODEL)):
            for b in range(QKV_TILE // SEQ):
                for p in range(HEAD_PAIRS):
                    pair = r[b * SEQ:(b + 1) * SEQ, base + p * LANES: base + (p + 1) * LANES].T
                    out_ref[b, 0, 2 * p] = pair[:HEAD_DIM]
                    out_ref[b, 0, 2 * p + 1] = pair[HEAD_DIM:]

    @pl.when(jnp.logical_not(is_ctx))
    def _():
        project(rows, norm(rows))


def _qkv(x1, yg, wsel, mods0, lng, lnb, mods1, w_qkv):
    seqs = QKV_TILE // SEQ
    row_spec = pl.BlockSpec((QKV_TILE, D_MODEL), lambda t: (t, 0))
    vec_spec = pl.BlockSpec((1, D_MODEL), lambda t: (0, 0))

    cache_spec = pl.BlockSpec((seqs, 1, N_HEADS, HEAD_DIM, SEQ),
                              lambda t: (jnp.minimum(t, T_CTX // QKV_TILE - 1), 0, 0, 0, 0))
    cache_shape = jax.ShapeDtypeStruct((BATCH, 1, N_HEADS, HEAD_DIM, SEQ), F32)
    return pl.pallas_call(
        _qkv_kernel,
        grid=(T_ALL // QKV_TILE,),
        in_specs=[
            row_spec,
            pl.BlockSpec((2, QKV_TILE, D_PACK), lambda t: (0, t, 0)),
            pl.BlockSpec((2, 1, QKV_TILE // LANES, LANES), lambda t: (0, t, 0, 0)),
            _mods_spec(QKV_TILE),
            vec_spec,
            vec_spec,
            _mods_spec(QKV_TILE),
            pl.BlockSpec((1, D_MODEL, 3 * D_MODEL), lambda t: (0, 0, 0)),
        ],
        out_specs=[row_spec, pl.BlockSpec((QKV_TILE, 3 * D_MODEL), lambda t: (t, 0)), cache_spec, cache_spec],
        out_shape=[jax.ShapeDtypeStruct((T_ALL, D_MODEL), F32),
                   jax.ShapeDtypeStruct((T_ALL, 3 * D_MODEL), BF16), cache_shape, cache_shape],
        scratch_shapes=[pltpu.VMEM((D_MODEL, 3 * D_MODEL), BF16)],
        compiler_params=_params("arbitrary"),
        name="qkv",
    )(x1, yg, wsel.reshape(2, T_ALL // QKV_TILE, QKV_TILE // LANES, LANES), mods0, lng, lnb, mods1, w_qkv)


def _dot_nt(a, b):
    return lax.dot_general(a, b, (((1,), (1,)), ((), ())), preferred_element_type=F32)


def _head_masks():
    lane = lax.broadcasted_iota(jnp.int32, (1, LANES), 1)
    return lane < HEAD_DIM, lane >= HEAD_DIM


def _ctx_attn_kernel(q_ref, k_ref, v_ref, o_ref):
    left, right = _head_masks()
    units = [(p, hh) for p in range(HEAD_PAIRS) for hh in range(2)]

    def scores(u):
        p, hh = units[u]
        cols = slice(p * LANES, (p + 1) * LANES)
        q2 = q_ref[:, cols]
        qh = jnp.where(right if hh else left, q2, jnp.zeros_like(q2))
        return _dot_nt(qh, k_ref[:, cols])

    s = scores(0)
    halves = []
    for u, (p, _) in enumerate(units):
        nxt = scores(u + 1) if u + 1 < len(units) else None
        cols = slice(p * LANES, (p + 1) * LANES)
        e = jnp.exp(s - jnp.max(s, axis=-1, keepdims=True))
        o2 = jnp.dot(e.astype(BF16), v_ref[:, cols], preferred_element_type=F32)
        halves.append(o2 / jnp.sum(e, axis=-1, keepdims=True))
        if len(halves) == 2:
            o_ref[:, cols] = jnp.where(left, halves[0], halves[1]).astype(BF16)
            halves = []
        s = nxt


def _ctx_attention(qkv):
    return pl.pallas_call(
        _ctx_attn_kernel,
        grid=(BATCH,),
        in_specs=[pl.BlockSpec((SEQ, D_MODEL), lambda b, j=j: (b, j)) for j in range(3)],
        out_specs=pl.BlockSpec((SEQ, D_MODEL), lambda b: (b, 0)),
        out_shape=jax.ShapeDtypeStruct((T_ALL, D_MODEL), BF16),
        compiler_params=_params("arbitrary"),
        name="ctx_attention",
    )(qkv, qkv, qkv)


_LAT_Q_BLOCK_ROWS = 4
_LAT_KEY_ROWS = ((0, 8), (0, 12), (4, 16), (8, 16))


def _softmax_rows(s_ref, p_ref, l_ref):
    sc = s_ref[...]
    e = jnp.exp(sc - jnp.max(sc, axis=-1, keepdims=True))
    l_ref[...] = jnp.sum(e, axis=-1, keepdims=True)
    p_ref[...] = e.astype(BF16)


def _lat_attn_kernel(q_ref, k_ref, v_ref, ck_ref, cv_ref, eb_ref, o_ctx_ref, o_ref, bias_s, s_scr, p_scr, l_scr):
    del o_ctx_ref
    left, right = _head_masks()

    @pl.when(pl.program_id(1) == 0)
    def _():
        neg = jnp.full((GRID_W, LANES), NEG_INF, F32)
        for hh in range(2):
            for r in range(GRID_ROWS):
                r0 = min(max(r - WIN_H // 2, 0), GRID_ROWS - WIN_H)
                for j in range(GRID_ROWS // 2):
                    parts = []
                    for kr in (2 * j, 2 * j + 1):
                        parts.append(eb_ref[hh, kr - r + WIN_H - 1] if r0 <= kr < r0 + WIN_H else None)
                    if parts[0] is None and parts[1] is None:
                        val = neg
                    else:
                        val = jnp.where(left, neg if parts[0] is None else parts[0],
                                        neg if parts[1] is None else parts[1])
                    bias_s[hh, r * GRID_W:(r + 1) * GRID_W, j * LANES:(j + 1) * LANES] = val

    ck = jnp.concatenate([ck_ref[0, 0, 0], ck_ref[0, 0, 1]], axis=0).astype(BF16)
    cv = jnp.concatenate([cv_ref[0, 0, 0], cv_ref[0, 0, 1]], axis=0).astype(BF16)
    qrows = _LAT_Q_BLOCK_ROWS * GRID_W
    units = [(qb, hh) for qb in range(len(_LAT_KEY_ROWS)) for hh in range(2)]

    def refs(u):
        nk = (_LAT_KEY_ROWS[units[u][0]][1] - _LAT_KEY_ROWS[units[u][0]][0]) * GRID_W
        width = nk + PAST_LEN
        return nk, s_scr.at[u % 2, :, :width], p_scr.at[u % 2, :, :width], l_scr.at[u % 2]

    def scores(u):
        qb, hh = units[u]
        kr0, kr1 = _LAT_KEY_ROWS[qb]
        qs, ks = slice(qb * qrows, (qb + 1) * qrows), slice(kr0 * GRID_W, kr1 * GRID_W)
        nk, s_ref, _, _ = refs(u)
        q2 = q_ref[qs, :]
        qh = jnp.where(right if hh else left, q2, jnp.zeros_like(q2))
        s_ref[:, :nk] = _dot_nt(qh, k_ref[ks, :]) + bias_s[hh, qs, ks]
        s_ref[:, nk:] = jnp.dot(qh, ck, preferred_element_type=F32)

    def weighted_values(u):
        qb, _ = units[u]
        kr0, kr1 = _LAT_KEY_ROWS[qb]
        nk, _, p_ref, l_ref = refs(u)
        o2 = (jnp.dot(p_ref[:, :nk], v_ref[kr0 * GRID_W:kr1 * GRID_W, :], preferred_element_type=F32)
              + _dot_nt(p_ref[:, nk:], cv))
        return o2 / l_ref[...]

    scores(0)
    halves = []
    for u in range(len(units)):
        if u + 1 < len(units):
            scores(u + 1)
        _, s_ref, p_ref, l_ref = refs(u)
        _softmax_rows(s_ref, p_ref, l_ref)
        halves.append(weighted_values(u))
        if len(halves) == 2:
            qb = units[u][0]
            o_ref[qb * qrows:(qb + 1) * qrows, :] = jnp.where(left, halves[0], halves[1]).astype(BF16)
            halves = []


def _lat_attention(qkv, cache_kt, cache_vt, ebias, o_buf):
    row0 = T_CTX // DEC_SEQ
    qrows = _LAT_Q_BLOCK_ROWS * GRID_W
    max_keys = max(k1 - k0 for k0, k1 in _LAT_KEY_ROWS) * GRID_W + PAST_LEN
    cache_spec = pl.BlockSpec((1, 1, 2, HEAD_DIM, PAST_LEN), lambda p, b: (b, 0, p, 0, 0))
    return pl.pallas_call(
        _lat_attn_kernel,
        grid=(HEAD_PAIRS, DEC_BATCH),
        in_specs=[
            pl.BlockSpec((DEC_SEQ, LANES), lambda p, b: (row0 + b, p)),
            pl.BlockSpec((DEC_SEQ, LANES), lambda p, b: (row0 + b, HEAD_PAIRS + p)),
            pl.BlockSpec((DEC_SEQ, LANES), lambda p, b: (row0 + b, 2 * HEAD_PAIRS + p)),
            cache_spec,
            cache_spec,
            pl.BlockSpec((2, 2 * WIN_H - 1, GRID_W, LANES), lambda p, b: (p, 0, 0, 0)),
            pl.BlockSpec(memory_space=pl.ANY),
        ],
        out_specs=pl.BlockSpec((DEC_SEQ, LANES), lambda p, b: (row0 + b, p)),
        out_shape=jax.ShapeDtypeStruct((T_ALL, D_MODEL), BF16),
        scratch_shapes=[pltpu.VMEM((2, DEC_SEQ, DEC_SEQ), F32),
                        pltpu.VMEM((2, qrows, max_keys), F32),
                        pltpu.VMEM((2, qrows, max_keys), BF16),
                        pltpu.VMEM((2, qrows, 1), F32)],
        input_output_aliases={6: 0},
        compiler_params=_params("arbitrary", "arbitrary"),
        name="lat_attention",
    )(qkv, qkv, qkv, cache_kt, cache_vt, ebias, o_buf)


def _bias_kernel(r_ref, o_ref):
    q = lax.broadcasted_iota(jnp.int32, (GRID_W, LANES), 0)
    k = lax.broadcasted_iota(jnp.int32, (GRID_W, LANES), 1) & (GRID_W - 1)
    start = jnp.clip(q - WIN_W // 2, 0, GRID_W - WIN_W)
    col_ok = (k >= start) & (k < start + WIN_W)
    heads = o_ref.shape[0]
    for i, y in itertools.product(range(heads), range(2 * WIN_H - 1)):
        base = jnp.broadcast_to(r_ref[pl.program_id(0) * heads + i, y:y + 1, :], (8, LANES))
        base = pltpu.roll(base, LANES - (WIN_W - 1), 1, stride=1, stride_axis=0)
        for v in range(GRID_W // 8):
            rs = slice(8 * v, 8 * v + 8)
            o_ref[i, y, rs, :] = jnp.where(col_ok[rs], pltpu.roll(base, 8 * v, 1) if v else base, NEG_INF)


def _expanded_bias(rel_bias):
    rel_h, rel_w = 2 * WIN_H - 1, 2 * WIN_W - 1
    r = jnp.pad(rel_bias.astype(F32), ((0, 0), (0, 0), (0, GRID_W - rel_w)))
    r = jnp.concatenate([r, r], axis=2)
    heads = 4
    return pl.pallas_call(
        _bias_kernel,
        grid=(N_HEADS // heads,),
        in_specs=[pl.BlockSpec((N_HEADS, rel_h, LANES), lambda h: (0, 0, 0))],
        out_specs=pl.BlockSpec((heads, rel_h, GRID_W, LANES), lambda h: (h, 0, 0, 0)),
        out_shape=jax.ShapeDtypeStruct((N_HEADS, rel_h, GRID_W, LANES), F32),
        compiler_params=_params("arbitrary"),
        name="rel_bias_tiles",
    )(r)


def _wo_kernel(o_ref, x_ref, mods_ref, w_ref, lng_ref, lnb_ref, w2_ref, x1_ref, h2p_ref, logit_ref, w_s):
    @pl.when(pl.program_id(0) == 0)
    def _():
        w_s[...] = w_ref[0].astype(BF16)

    subs = [slice(r0, r0 + TILE) for r0 in range(0, WO_TILE, TILE)]
    mix = jnp.dot(o_ref[subs[0], :], w_s[...], preferred_element_type=F32)
    for j, rs in enumerate(subs):
        nxt = jnp.dot(o_ref[subs[j + 1], :], w_s[...], preferred_element_type=F32) if j + 1 < len(subs) else None
        _epilogue(x_ref[rs, :], mix, mods_ref[0], lng_ref[...], lnb_ref[...], w2_ref,
                  x1_ref.at[rs, :], h2p_ref.at[rs, :], logit_ref.at[:, rs])
        mix = nxt


def _attn_out(o, x, mods, w_o, lng, lnb, w2):
    return pl.pallas_call(
        _wo_kernel,
        grid=(T_ALL // WO_TILE,),
        in_specs=[
            pl.BlockSpec((WO_TILE, D_MODEL), lambda t: (t, 0)),
            pl.BlockSpec((WO_TILE, D_MODEL), lambda t: (t, 0)),
            _mods_spec(WO_TILE),
            pl.BlockSpec((1, D_MODEL, D_MODEL), lambda t: (0, 0, 0)),
        ] + _EPI_IN_SPECS,
        out_specs=[
            pl.BlockSpec((WO_TILE, D_MODEL), lambda t: (t, 0)),
            pl.BlockSpec((WO_TILE, D_PACK), lambda t: (t, 0)),
            pl.BlockSpec((N_EXPERTS, WO_TILE), lambda t: (0, t)),
        ],
        out_shape=_EPI_OUT_SHAPE,
        scratch_shapes=[pltpu.VMEM((D_MODEL, D_MODEL), BF16)],
        compiler_params=_params("arbitrary"),
        name="attn_out",
    )(o, x, mods, w_o, lng, lnb, w2)


def kernel(x_prompt, x_sample, cache_k, cache_v, c, c_ctx, w_ada, b_ada, ln1_g, ln1_b, ln2_g, ln2_b,
           pool_w, pool_scale, w_qkv, w_o, rel_bias, w_router, b_router, w_gate, w_up, w_down):
    cond = jnp.zeros((N_COND, D_MODEL), F32).at[0].set(c_ctx).at[1:1 + DEC_BATCH].set(c)
    mods0, mods1 = _ada(cond, w_ada, b_ada, 0), _ada(cond, w_ada, b_ada, 1)
    wrt = w_router.T
    wrt_hi = wrt.astype(BF16)
    w2 = jnp.concatenate([wrt_hi, (wrt - wrt_hi.astype(F32)).astype(BF16)], axis=0)

    x, h2p, logits = _pool_layer(x_prompt.reshape(T_CTX, D_MODEL), x_sample.reshape(T_LAT, D_MODEL), mods0,
                                 pool_w[0].astype(BF16), pool_scale[0:1], ln1_g[0:1], ln1_b[0:1], w2)
    yg, wsel = _moe_rows_of_tokens(h2p, logits, b_router, 0, w_gate, w_up, w_down, mods1)

    x, qkv, new_kt, new_vt = _qkv(x, yg, wsel, mods0, ln2_g[0:1], ln2_b[0:1], mods1, w_qkv)
    o = _lat_attention(qkv, cache_k.transpose(0, 1, 2, 4, 3), cache_v.transpose(0, 1, 2, 4, 3),
                       _expanded_bias(rel_bias[0]), _ctx_attention(qkv))
    x, h2p, logits = _attn_out(o, x, mods1, w_o, ln1_g[1:2], ln1_b[1:2], w2)
    yg, wsel = _moe_rows_of_tokens(h2p, logits, b_router, 1, w_gate, w_up, w_down, mods1)
    y_ctx, y_lat = [_combine(x, yg, wsel, mods1, ln2_g[1:2], ln2_b[1:2], t0, n) for t0, n in SEGMENTS]
    return (y_ctx.reshape(BATCH, SEQ, D_MODEL), y_lat.reshape(DEC_BATCH, DEC_SEQ, D_MODEL),
            new_kt.transpose(0, 1, 2, 4, 3), new_vt.transpose(0, 1, 2, 4, 3))
```

```python
import functools

import jax
import jax.numpy as jnp
from jax import lax
from jax.experimental import pallas as pl
from jax.experimental.pallas import tpu as pltpu
from jax.experimental.pallas import tpu_sc as plsc

F32 = jnp.float32
BF16 = jnp.bfloat16

D_MODEL = 1024
BATCH = 16
SEQ = 256
DEC_BATCH = 8
DEC_SEQ = 1024
PAST_LEN = 512
GRID_W = 64
GRID_ROWS = DEC_SEQ // GRID_W
POOL_SIZES = (2, 4, 8, 16)
POOL_GROUP_DIM = D_MODEL // len(POOL_SIZES)
POOL_HALO = 8
N_HEADS = 16
HEAD_DIM = 64
WIN_H = 8
WIN_W = 16
N_EXPERTS = 16
EXPERTS_PER_GROUP = 4
N_EXPERT_GROUPS = 4
D_FF = 512
ALPHA = (2.0 * 2) ** 0.25
LN_EPS = 1e-5
NEG_INF = -1e30

T_CTX = BATCH * SEQ
T_LAT = DEC_BATCH * DEC_SEQ
T_ALL = T_CTX + T_LAT
N_COND = 16
TILE = 256
TILES_PER_LAT_SEQ = DEC_SEQ // TILE
MOE_TILE = 1024
MOE_SUB = 256
X_SLOTS = 3
SEGMENTS = ((0, T_CTX), (T_CTX, T_LAT))
LN_TILE = 512
QKV_TILE = 512
POOL_TILE = 512
WO_TILE = 1024
SC_WINDOW = 64
D_PACK = D_MODEL // 2
ROUTE_ROWS = T_ALL // 128
HEAD_PAIRS = N_HEADS // 2
LANES = 128
VMEM_LIMIT = 56 * 1024 * 1024


def _cond_row(t, tile):
    ctx_tiles = T_CTX // tile
    per_seq = DEC_SEQ // tile
    return jnp.maximum(t - ctx_tiles + per_seq, 0) // per_seq


def _params(*sem):
    return pltpu.CompilerParams(dimension_semantics=sem, vmem_limit_bytes=VMEM_LIMIT)


def _pack_halves(v):
    half = v.shape[1] // 2
    hi = lax.bitcast_convert_type(v[:, :half].astype(F32), jnp.uint32)
    lo = lax.bitcast_convert_type(v[:, half:].astype(F32), jnp.uint32)
    return hi | (lo >> 16)


def _unpack_halves(p):
    left = lax.bitcast_convert_type(p & jnp.uint32(0xFFFF0000), F32)
    right = lax.bitcast_convert_type(p << 16, F32)
    return left, right


def _ada_kernel(layer, cctx_ref, c_ref, w_ref, b_ref, o_ref, cond_s):
    cond_s[...] = jnp.zeros_like(cond_s)
    cond_s[0:1, :] = cctx_ref[...]
    cond_s[1:1 + DEC_BATCH, :] = c_ref[...]
    cnd = cond_s[...]
    act = cnd * jax.nn.sigmoid(cnd)
    a_hi = act.astype(BF16)
    a_lo = (act - a_hi.astype(F32)).astype(BF16)
    w = w_ref[0]
    w_hi = w.astype(BF16)
    w_lo = (w - w_hi.astype(F32)).astype(BF16)
    a2 = jnp.concatenate([a_hi, a_lo], axis=0)
    p = jnp.dot(a2, w_hi, preferred_element_type=F32)
    q = jnp.dot(a_hi, w_lo, preferred_element_type=F32)
    val = p[:N_COND] + p[N_COND:] + q + b_ref[layer:layer + 1, :]

    def put(j):
        for c in range(N_COND):
            o_ref[c, j:j + 1, :] = val[c:c + 1, :]

    for j in range(o_ref.shape[1]):
        pl.when(pl.program_id(0) == j)(functools.partial(put, j))


def _ada(c_ctx, c, w_ada, b_ada, layer):
    depth, d, n = w_ada.shape
    return pl.pallas_call(
        functools.partial(_ada_kernel, layer),
        grid=(n // d,),
        in_specs=[
            pl.BlockSpec((1, d), lambda j: (0, 0)),
            pl.BlockSpec((DEC_BATCH, d), lambda j: (0, 0)),
            pl.BlockSpec((1, d, d), lambda j: (layer, 0, j)),
            pl.BlockSpec((depth, d), lambda j: (0, j)),
        ],
        out_specs=pl.BlockSpec((N_COND, n // d, d), lambda j: (0, 0, 0)),
        out_shape=jax.ShapeDtypeStruct((N_COND, n // d, d), F32),
        scratch_shapes=[pltpu.VMEM((N_COND, d), F32)],
        compiler_params=_params("arbitrary"),
        name="ada",
    )(c_ctx.reshape(1, d), c, w_ada, b_ada)


def _post_norm(x, upd, g, b):
    y = ALPHA * x + upd
    mu = jnp.mean(y, axis=-1, keepdims=True)
    yc = y - mu
    var = jnp.mean(yc * yc, axis=-1, keepdims=True)
    return yc * lax.rsqrt(var + LN_EPS) * g + b


def _router_logits(h2, h_hi, w2_ref):
    h_lo = (h2 - h_hi.astype(F32)).astype(BF16)
    w2 = w2_ref[...]
    p = _dot_nt(w2, h_hi)
    q = _dot_nt(w2[:N_EXPERTS], h_lo)
    return p[:N_EXPERTS] + p[N_EXPERTS:] + q


def _epilogue(x, mix, m, lng, lnb, w2_ref, x1_ref, h2p_ref, logit_ref):
    g1, sh2, sc2 = m[2:3, :], m[3:4, :], m[4:5, :]
    x1 = _post_norm(x, g1 * mix, lng, lnb)
    h2 = x1 * (1.0 + sc2) + sh2
    h_hi = h2.astype(BF16)
    x1_ref[...] = x1
    h2p_ref[...] = _pack_halves(h_hi)
    logit_ref[...] = _router_logits(h2, h_hi, w2_ref)


_EPI_IN_SPECS = [
    pl.BlockSpec((1, D_MODEL), lambda t: (0, 0)),
    pl.BlockSpec((1, D_MODEL), lambda t: (0, 0)),
    pl.BlockSpec((2 * N_EXPERTS, D_MODEL), lambda t: (0, 0)),
]
_EPI_OUT_SHAPE = [
    jax.ShapeDtypeStruct((T_ALL, D_MODEL), F32),
    jax.ShapeDtypeStruct((T_ALL, D_PACK), jnp.uint32),
    jax.ShapeDtypeStruct((N_EXPERTS, T_ALL), F32),
]


def _route_kernel(logit_ref, br_ref, pos_ref, w_ref, table_ref):
    aff = jax.nn.sigmoid(logit_ref[...])
    sel = aff + br_ref[...]
    sel_rows = [sel[e] for e in range(N_EXPERTS)]
    aff_rows = [aff[e] for e in range(N_EXPERTS)]

    def group_score(g):
        r = sel_rows[g * EXPERTS_PER_GROUP:(g + 1) * EXPERTS_PER_GROUP]
        best = None
        for i in range(EXPERTS_PER_GROUP):
            for j in range(i + 1, EXPERTS_PER_GROUP):
                pair = r[i] + r[j]
                best = pair if best is None else jnp.maximum(best, pair)
        return best

    best = group_score(0)
    gidx = jnp.zeros_like(best, dtype=jnp.int32)
    for g in range(1, N_EXPERT_GROUPS):
        sc = group_score(g)
        better = sc > best
        gidx = jnp.where(better, g, gidx)
        best = jnp.where(better, sc, best)

    def pick_group(rows, j):
        out = rows[j]
        for g in range(1, N_EXPERT_GROUPS):
            out = jnp.where(gidx == g, rows[g * EXPERTS_PER_GROUP + j], out)
        return out

    cand = [pick_group(sel_rows, j) for j in range(EXPERTS_PER_GROUP)]
    cand_aff = [pick_group(aff_rows, j) for j in range(EXPERTS_PER_GROUP)]

    def argmax_first(vals):
        bv, bi, ba = vals[0], jnp.zeros_like(gidx), cand_aff[0]
        for j in range(1, EXPERTS_PER_GROUP):
            better = vals[j] > bv
            bv = jnp.where(better, vals[j], bv)
            bi = jnp.where(better, j, bi)
            ba = jnp.where(better, cand_aff[j], ba)
        return bi, ba

    i1, a1 = argmax_first(cand)
    rest = [jnp.where(i1 == j, -jnp.inf, cand[j]) for j in range(EXPERTS_PER_GROUP)]
    i2, a2 = argmax_first(rest)
    denom = a1 + a2
    base = gidx * EXPERTS_PER_GROUP
    w_ref[0] = a1 / denom
    w_ref[1] = a2 / denom
    _plan(base + i1, base + i2, pos_ref, table_ref)


def _plan(e1, e2, pos_ref, table_ref):
    rows = e1.shape[0]
    upper = (lax.broadcasted_iota(jnp.int32, (LANES, LANES), 0)
             <= lax.broadcasted_iota(jnp.int32, (LANES, LANES), 1)).astype(BF16)
    lower = (lax.broadcasted_iota(jnp.int32, (rows, rows), 1)
             < lax.broadcasted_iota(jnp.int32, (rows, rows), 0)).astype(BF16)
    masks = [(choice == e) for e in range(N_EXPERTS) for choice in (e1, e2)]
    hits = jnp.concatenate([m.astype(BF16) for m in masks], axis=0)
    within = jnp.dot(hits, upper, preferred_element_type=F32)
    totals = [within[i * rows:(i + 1) * rows, LANES - 1:LANES] for i in range(2 * N_EXPERTS)]
    spread = jnp.concatenate([jnp.broadcast_to(t, (rows, LANES)) for t in totals], axis=1).astype(BF16)
    above = jnp.dot(lower, spread, preferred_element_type=F32)

    tile_start = lax.broadcasted_iota(jnp.int32, (1, LANES), 1).astype(F32) * MOE_TILE
    start = jnp.zeros((1, 1), F32)
    pos = [jnp.zeros((rows, LANES), F32), jnp.zeros((rows, LANES), F32)]
    tile_expert = jnp.zeros((1, LANES), F32)
    tile_valid = jnp.zeros((1, LANES), F32)
    tile_block = jnp.zeros((1, LANES), F32)
    used = jnp.zeros((1, LANES), F32)
    last_expert = jnp.zeros((1, 1), F32)
    last_block = jnp.zeros((1, 1), F32)
    for e in range(N_EXPERTS):
        first = jnp.zeros((1, 1), F32)
        for c in range(2):
            i = 2 * e + c
            hit = masks[i]
            rank = (within[i * rows:(i + 1) * rows] - hit.astype(F32)
                    + above[:, i * LANES:(i + 1) * LANES] + (start + first))
            pos[c] = jnp.where(hit, rank, pos[c])
            first = first + jnp.sum(totals[i], axis=0, keepdims=True)
        count = first
        n_tiles = jnp.floor((count + (MOE_TILE - 1)) * (1.0 / MOE_TILE))
        end = start + n_tiles * MOE_TILE
        inside = jnp.logical_and(tile_start >= start, tile_start < end)
        block_in_expert = n_tiles - 1.0 - (tile_start - start) * (1.0 / MOE_TILE)
        tile_expert = jnp.where(inside, float(e), tile_expert)
        tile_valid = jnp.where(inside, jnp.clip(count - block_in_expert * MOE_TILE, 0.0, float(MOE_TILE)), tile_valid)
        tile_block = jnp.where(inside, start * (1.0 / MOE_TILE) + block_in_expert, tile_block)
        used = jnp.where(inside, 1.0, used)
        has_rows = count > 0.0
        last_expert = jnp.where(has_rows, float(e), last_expert)
        last_block = jnp.where(has_rows, start * (1.0 / MOE_TILE), last_block)
        start = end
    pos_ref[0] = pos[0].astype(jnp.int32)
    pos_ref[1] = pos[1].astype(jnp.int32)
    idle = used == 0.0
    table_ref[0:1, :] = jnp.where(idle, last_expert, tile_expert).astype(jnp.int32)
    table_ref[1:2, :] = tile_valid.astype(jnp.int32)
    table_ref[2:3, :] = jnp.where(idle, last_block, tile_block).astype(jnp.int32)
    table_ref[3:8, :] = jnp.zeros((5, LANES), jnp.int32)


def _route(logits, b_router):
    tiles = _moe_rows(T_ALL) // MOE_TILE
    pos, wsel, table = pl.pallas_call(
        _route_kernel,
        out_shape=[jax.ShapeDtypeStruct((2, ROUTE_ROWS, LANES), jnp.int32),
                   jax.ShapeDtypeStruct((2, ROUTE_ROWS, LANES), F32),
                   jax.ShapeDtypeStruct((8, LANES), jnp.int32)],
        compiler_params=pltpu.CompilerParams(vmem_limit_bytes=VMEM_LIMIT),
        name="route",
    )(logits.reshape(N_EXPERTS, ROUTE_ROWS, LANES), b_router.reshape(N_EXPERTS, 1, 1))
    return pos.reshape(2 * T_ALL), wsel, table[0, :tiles], table[1, :tiles], table[2, :tiles]


def _mods_spec(tile, first_tile=0):
    return pl.BlockSpec((1, 6, D_MODEL), lambda t: (_cond_row(first_tile + t, tile), 0, 0))


def _pool_kernel(xa_ref, xb_ref, xp_ref, xn_ref, mods_ref, pw_ref, ps_ref, lng_ref, lnb_ref, w2_ref,
                 x1_ref, h2p_ref, logit_ref):
    t = pl.program_id(0)
    m = mods_ref[0]
    sh1, sc1 = m[0:1, :], m[1:2, :]
    subs = POOL_TILE // TILE
    in_lat = t >= T_CTX // POOL_TILE
    tile_in_seq = (t - T_CTX // POOL_TILE) % (DEC_SEQ // POOL_TILE)
    seq_len = jnp.where(in_lat, DEC_SEQ, SEQ)

    x = jnp.where(in_lat, xb_ref[...], xa_ref[...])
    h = x * (1.0 + sc1) + sh1
    halo_before = xp_ref[...] * (1.0 + sc1) + sh1
    halo_after = xn_ref[...] * (1.0 + sc1) + sh1
    ext = TILE + 2 * POOL_HALO

    def mixer(j):
        place = tile_in_seq * subs + j
        is_first = jnp.logical_or(jnp.logical_not(in_lat), place == 0)
        is_last = jnp.logical_or(jnp.logical_not(in_lat), place == TILES_PER_LAT_SEQ - 1)
        before = halo_before if j == 0 else h[j * TILE - POOL_HALO:j * TILE]
        after = halo_after if j == subs - 1 else h[(j + 1) * TILE:(j + 1) * TILE + POOL_HALO]
        hj = h[j * TILE:(j + 1) * TILE]
        hext = jnp.concatenate([jnp.where(is_first, 0.0, before), hj, jnp.where(is_last, 0.0, after)], axis=0)
        pos = jnp.where(in_lat, place * TILE, 0) + lax.broadcasted_iota(jnp.int32, (TILE, 1), 0)
        outs = []
        for g, w in enumerate(POOL_SIZES):
            lo_c, hi_c = g * POOL_GROUP_DIM, (g + 1) * POOL_GROUP_DIM
            a = hext[:, lo_c:hi_c]
            k = 1
            while k < w:
                a = a + pltpu.roll(a, ext - k, axis=0)
                k *= 2
            off = POOL_HALO - w // 2
            win = pltpu.roll(a, ext - off, axis=0)[:TILE] if off else a[:TILE]
            lo = jnp.maximum(pos - w // 2, 0)
            hi = jnp.minimum(pos - w // 2 + w, seq_len)
            cnt = (hi - lo).astype(F32)
            pooled = win / cnt - hj[:, lo_c:hi_c]
            outs.append(jnp.dot(pooled.astype(BF16), pw_ref[g], preferred_element_type=F32))
        return jnp.concatenate(outs, axis=1) * ps_ref[...]

    mix = mixer(0)
    for j in range(subs):
        nxt = mixer(j + 1) if j + 1 < subs else None
        rs = slice(j * TILE, (j + 1) * TILE)
        _epilogue(x[rs, :], mix, m, lng_ref[...], lnb_ref[...], w2_ref,
                  x1_ref.at[rs, :], h2p_ref.at[rs, :], logit_ref.at[:, rs])
        mix = nxt


def _pool_layer(x_ctx, x_lat, mods, pool_w, pool_scale, lng, lnb, w2):
    ctx_tiles = T_CTX // POOL_TILE
    halo_blocks = POOL_TILE // POOL_HALO
    last_halo = T_LAT // POOL_HALO - 1
    return pl.pallas_call(
        _pool_kernel,
        grid=(T_ALL // POOL_TILE,),
        in_specs=[
            pl.BlockSpec((POOL_TILE, D_MODEL), lambda t: (jnp.minimum(t, ctx_tiles - 1), 0)),
            pl.BlockSpec((POOL_TILE, D_MODEL), lambda t: (jnp.maximum(t - ctx_tiles, 0), 0)),
            pl.BlockSpec((POOL_HALO, D_MODEL), lambda t: (jnp.maximum((t - ctx_tiles) * halo_blocks - 1, 0), 0)),
            pl.BlockSpec((POOL_HALO, D_MODEL),
                         lambda t: (jnp.clip((t - ctx_tiles + 1) * halo_blocks, 0, last_halo), 0)),
            _mods_spec(POOL_TILE),
            pl.BlockSpec((len(POOL_SIZES), POOL_GROUP_DIM, POOL_GROUP_DIM), lambda t: (0, 0, 0)),
            pl.BlockSpec((1, D_MODEL), lambda t: (0, 0)),
        ] + _EPI_IN_SPECS,
        out_specs=[
            pl.BlockSpec((POOL_TILE, D_MODEL), lambda t: (t, 0)),
            pl.BlockSpec((POOL_TILE, D_PACK), lambda t: (t, 0)),
            pl.BlockSpec((N_EXPERTS, POOL_TILE), lambda t: (0, t)),
        ],
        out_shape=_EPI_OUT_SHAPE,
        compiler_params=_params("arbitrary"),
        name="pool_mixer",
    )(x_ctx, x_lat, x_lat, x_lat, mods, pool_w, pool_scale, lng, lnb, w2)


def _moe_rows(n):
    return 2 * n + N_EXPERTS * MOE_TILE


def _expert_kernel(te_ref, nv_ref, tb_ref, x_hbm, wg_ref, wu_ref, wd_ref, y_ref, wgu_s, wd_s, x_ring, x_sems):
    i = pl.program_id(0)
    steps = pl.num_programs(0)
    prev = te_ref[jnp.maximum(i - 1, 0)]
    changed = jnp.logical_or(i == 0, te_ref[i] != prev)

    def x_copy(step):
        slot = step % X_SLOTS
        return pltpu.make_async_copy(x_hbm.at[pl.ds(tb_ref[step] * MOE_TILE, MOE_TILE)], x_ring.at[slot],
                                     x_sems.at[slot])

    def start_x(step):
        @pl.when(nv_ref[step] > 0)
        def _():
            x_copy(step).start()

    @pl.when(i == 0)
    def _():
        for step in range(X_SLOTS - 1):
            start_x(step)

    @pl.when(i + (X_SLOTS - 1) < steps)
    def _():
        start_x(i + (X_SLOTS - 1))

    @pl.when(changed)
    def _():
        wgu_s[:, :D_FF] = wg_ref[0, 0].astype(BF16)
        wgu_s[:, D_FF:] = wu_ref[0, 0].astype(BF16)
        wd_s[...] = wd_ref[0, 0].astype(BF16)

    nv = nv_ref[i]
    x_ref = x_ring.at[i % X_SLOTS]

    @pl.when(nv > 0)
    def _():
        x_copy(i).wait()

    def gate_up(r0):
        rows = r0 + lax.broadcasted_iota(jnp.int32, (MOE_SUB, 1), 0)
        xp = jnp.where(rows < nv, x_ref[r0:r0 + MOE_SUB, :], jnp.uint32(0))
        left, right = _unpack_halves(xp)
        xb = jnp.concatenate([left.astype(BF16), right.astype(BF16)], axis=1)
        return jnp.dot(xb, wgu_s[...], preferred_element_type=F32)

    def down(r0, gu):
        gate, up = gu[:, :D_FF], gu[:, D_FF:]
        he = (gate * jax.nn.sigmoid(gate) * up).astype(BF16)
        y_ref[r0:r0 + MOE_SUB, :] = _pack_halves(jnp.dot(he, wd_s[...], preferred_element_type=F32).astype(BF16))

    def run(n_sub):
        gu = gate_up(0)
        for j in range(n_sub):
            nxt = gate_up((j + 1) * MOE_SUB) if j + 1 < n_sub else None
            down(j * MOE_SUB, gu)
            gu = nxt

    n_subs = MOE_TILE // MOE_SUB
    for n_sub in range(1, n_subs + 1):
        lo = (n_sub - 1) * MOE_SUB
        in_range = nv > lo if n_sub == n_subs else jnp.logical_and(nv > lo, nv <= lo + MOE_SUB)
        pl.when(in_range)(functools.partial(run, n_sub))


def _experts(xs, tile_expert, tile_valid, tile_block, layer, w_gate, w_up, w_down):
    grid_spec = pltpu.PrefetchScalarGridSpec(
        num_scalar_prefetch=3,
        grid=(xs.shape[0] // MOE_TILE,),
        in_specs=[
            pl.BlockSpec(memory_space=pl.ANY),
            pl.BlockSpec((1, 1, D_MODEL, D_FF), lambda i, te, nv, tb: (layer, te[i], 0, 0)),
            pl.BlockSpec((1, 1, D_MODEL, D_FF), lambda i, te, nv, tb: (layer, te[i], 0, 0)),
            pl.BlockSpec((1, 1, D_FF, D_MODEL), lambda i, te, nv, tb: (layer, te[i], 0, 0)),
        ],
        out_specs=pl.BlockSpec((MOE_TILE, D_PACK), lambda i, te, nv, tb: (tb[i], 0)),
        scratch_shapes=[pltpu.VMEM((D_MODEL, 2 * D_FF), BF16), pltpu.VMEM((D_FF, D_MODEL), BF16),
                        pltpu.VMEM((X_SLOTS, MOE_TILE, D_PACK), jnp.uint32), pltpu.SemaphoreType.DMA((X_SLOTS,))],
    )
    return pl.pallas_call(
        _expert_kernel,
        grid_spec=grid_spec,
        out_shape=jax.ShapeDtypeStruct(xs.shape, jnp.uint32),
        compiler_params=_params("arbitrary"),
        name="experts",
    )(tile_expert, tile_valid, tile_block, xs, w_gate, w_up, w_down)


SC_WORKERS = 32


def _sc_mesh():
    return plsc.VectorSubcoreMesh(core_axis_name="core", subcore_axis_name="subcore")


def _sc_worker():
    return lax.axis_index("subcore") * 2 + lax.axis_index("core")


def _dispatch(h2, pos):
    n = h2.shape[0]
    per_worker = n // SC_WORKERS
    pairs = per_worker // (2 * SC_WINDOW)
    one_set = [pltpu.VMEM((SC_WINDOW,), jnp.int32), pltpu.VMEM((SC_WINDOW,), jnp.int32),
               pltpu.VMEM((SC_WINDOW, D_PACK), jnp.uint32),
               pltpu.SemaphoreType.DMA, pltpu.SemaphoreType.DMA, pltpu.SemaphoreType.DMA]

    @functools.partial(pl.kernel, out_type=jax.ShapeDtypeStruct((_moe_rows(n), D_PACK), jnp.uint32),
                       mesh=_sc_mesh(), scratch_types=one_set + one_set, name="moe_dispatch")
    def scatter(x_hbm, i_hbm, o_hbm, *bufs):
        base = _sc_worker() * per_worker
        sets = (bufs[:6], bufs[6:])

        def rows_in(j, s):
            _, _, rows, sem, _, _ = sets[s]
            return pltpu.make_async_copy(x_hbm.at[pl.ds(base + j * SC_WINDOW, SC_WINDOW)], rows, sem)

        def rows_out(k, s):
            idx, rows, sem = sets[s][k], sets[s][2], sets[s][4 + k]
            return pltpu.make_async_copy(rows, o_hbm.at[idx], sem)

        def fetch(j, s):
            t0 = base + j * SC_WINDOW
            pltpu.sync_copy(i_hbm.at[pl.ds(t0, SC_WINDOW)], sets[s][0])
            pltpu.sync_copy(i_hbm.at[pl.ds(n + t0, SC_WINDOW)], sets[s][1])
            rows_in(j, s).start()

        fetch(0, 0)

        @pl.loop(0, pairs)
        def _(g):
            j0, j1 = 2 * g, 2 * g + 1
            fetch(j1, 1)
            rows_in(j0, 0).wait()
            rows_out(0, 0).start()
            rows_out(1, 0).start()
            rows_in(j1, 1).wait()
            rows_out(0, 1).start()
            rows_out(1, 1).start()
            rows_out(0, 0).wait()
            rows_out(1, 0).wait()

            @pl.when(g + 1 < pairs)
            def _():
                fetch(j0 + 2, 0)

            rows_out(0, 1).wait()
            rows_out(1, 1).wait()

    return scatter(h2, pos)


def _gather_pairs(y, pos):
    n = pos.shape[0] // 2
    per_worker = 2 * n // SC_WORKERS
    pairs = per_worker // (2 * SC_WINDOW)
    scratch = [pltpu.VMEM((per_worker,), jnp.int32),
               pltpu.VMEM((SC_WINDOW, D_PACK), jnp.uint32), pltpu.VMEM((SC_WINDOW, D_PACK), jnp.uint32),
               pltpu.SemaphoreType.DMA, pltpu.SemaphoreType.DMA, pltpu.SemaphoreType.DMA, pltpu.SemaphoreType.DMA]

    @functools.partial(pl.kernel, out_type=jax.ShapeDtypeStruct((2 * n, D_PACK), jnp.uint32),
                       mesh=_sc_mesh(), scratch_types=scratch, name="moe_gather")
    def gather(y_hbm, i_hbm, o_hbm, idx_v, rows0, rows1, read0, read1, write0, write1):
        base = _sc_worker() * per_worker
        pltpu.sync_copy(i_hbm.at[pl.ds(base, per_worker)], idx_v)

        def read(j, rows, sem):
            return pltpu.make_async_copy(y_hbm.at[idx_v.at[pl.ds(j * SC_WINDOW, SC_WINDOW)]], rows, sem)

        def write(j, rows, sem):
            return pltpu.make_async_copy(rows, o_hbm.at[pl.ds(base + j * SC_WINDOW, SC_WINDOW)], sem)

        read(0, rows0, read0).start()

        @pl.loop(0, pairs)
        def _(g):
            j0, j1 = 2 * g, 2 * g + 1
            read(j1, rows1, read1).start()
            read(j0, rows0, read0).wait()
            write(j0, rows0, write0).start()
            read(j1, rows1, read1).wait()
            write(j1, rows1, write1).start()
            write(j0, rows0, write0).wait()

            @pl.when(g + 1 < pairs)
            def _():
                read(j0 + 2, rows0, read0).start()

            write(j1, rows1, write1).wait()

    return gather(y, pos).reshape(2, n, D_PACK)


def _moe_post_norm(x_ref, yg_ref, w_ref, mods_ref, lng_ref, lnb_ref):
    g2 = mods_ref[0][5:6, :]
    groups = x_ref.shape[0] // LANES
    wt = jnp.concatenate([w_ref[0, 0], w_ref[1, 0]], axis=0).T
    moe = []
    for r in range(groups):
        rows = slice(r * LANES, (r + 1) * LANES)
        y1 = jnp.concatenate(_unpack_halves(yg_ref[0, rows, :]), axis=1)
        y2 = jnp.concatenate(_unpack_halves(yg_ref[1, rows, :]), axis=1)
        moe.append(wt[:, r:r + 1] * y1 + wt[:, groups + r:groups + r + 1] * y2)
    return _post_norm(x_ref[...], g2 * jnp.concatenate(moe, axis=0), lng_ref[...], lnb_ref[...])


def _combine_kernel(x_ref, yg_ref, w_ref, mods_ref, lng_ref, lnb_ref, o_ref):
    o_ref[...] = _moe_post_norm(x_ref, yg_ref, w_ref, mods_ref, lng_ref, lnb_ref)


def _combine(x1, yg, wsel, mods, lng, lnb, t0, n):
    off = t0 // LN_TILE
    return pl.pallas_call(
        _combine_kernel,
        grid=(n // LN_TILE,),
        in_specs=[
            pl.BlockSpec((LN_TILE, D_MODEL), lambda t: (off + t, 0)),
            pl.BlockSpec((2, LN_TILE, D_PACK), lambda t: (0, off + t, 0)),
            pl.BlockSpec((2, 1, LN_TILE // LANES, LANES), lambda t: (0, off + t, 0, 0)),
            _mods_spec(LN_TILE, off),
            pl.BlockSpec((1, D_MODEL), lambda t: (0, 0)),
            pl.BlockSpec((1, D_MODEL), lambda t: (0, 0)),
        ],
        out_specs=pl.BlockSpec((LN_TILE, D_MODEL), lambda t: (t, 0)),
        out_shape=jax.ShapeDtypeStruct((n, D_MODEL), F32),
        compiler_params=_params("arbitrary"),
        name="moe_combine",
    )(x1, yg, wsel.reshape(2, T_ALL // LN_TILE, LN_TILE // LANES, LANES), mods, lng, lnb)


def _moe_rows_of_tokens(h2p, logits, b_router, layer, w_gate, w_up, w_down):
    pos, wsel, tile_expert, tile_valid, tile_block = _route(logits, b_router)
    xs = _dispatch(h2p, pos)
    y = _experts(xs, tile_expert, tile_valid, tile_block, layer, w_gate, w_up, w_down)
    return _gather_pairs(y, pos), wsel


def _qkv_kernel(x1_ref, yg_ref, wsel_ref, mods0_ref, lng_ref, lnb_ref, mods_ref, w_ref,
                x_ref, qkv_ref, nk_ref, nv_ref, w_s):
    t = pl.program_id(0)

    @pl.when(t == 0)
    def _():
        w_s[...] = w_ref[0].astype(BF16)

    m = mods_ref[0]
    sh1, sc1 = m[0:1, :], m[1:2, :]

    def norm(rows):
        x = _moe_post_norm(x1_ref.at[rows], yg_ref.at[:, rows], wsel_ref, mods0_ref, lng_ref, lnb_ref)
        x_ref[rows, :] = x
        return (x * (1.0 + sc1) + sh1).astype(BF16)

    def project(rows, h):
        r = jnp.dot(h, w_s[...], preferred_element_type=F32)
        qkv_ref[rows, :D_MODEL] = (r[:, :D_MODEL] * (HEAD_DIM ** -0.5)).astype(BF16)
        qkv_ref[rows, D_MODEL:] = r[:, D_MODEL:].astype(BF16)
        return r

    is_ctx = t < T_CTX // QKV_TILE
    rows = slice(0, QKV_TILE)

    @pl.when(is_ctx)
    def _():
        r = project(rows, norm(rows))
        for out_ref, base in ((nk_ref, D_MODEL), (nv_ref, 2 * D_MODEL)):
            for b in range(QKV_TILE // SEQ):
                for p in range(HEAD_PAIRS):
                    pair = r[b * SEQ:(b + 1) * SEQ, base + p * LANES: base + (p + 1) * LANES].T
                    out_ref[b, 0, 2 * p] = pair[:HEAD_DIM]
                    out_ref[b, 0, 2 * p + 1] = pair[HEAD_DIM:]

    @pl.when(jnp.logical_not(is_ctx))
    def _():
        project(rows, norm(rows))


def _qkv(x1, yg, wsel, mods0, lng, lnb, mods1, w_qkv):
    seqs = QKV_TILE // SEQ
    row_spec = pl.BlockSpec((QKV_TILE, D_MODEL), lambda t: (t, 0))
    vec_spec = pl.BlockSpec((1, D_MODEL), lambda t: (0, 0))

    cache_spec = pl.BlockSpec((seqs, 1, N_HEADS, HEAD_DIM, SEQ),
                              lambda t: (jnp.minimum(t, T_CTX // QKV_TILE - 1), 0, 0, 0, 0))
    cache_shape = jax.ShapeDtypeStruct((BATCH, 1, N_HEADS, HEAD_DIM, SEQ), F32)
    return pl.pallas_call(
        _qkv_kernel,
        grid=(T_ALL // QKV_TILE,),
        in_specs=[
            row_spec,
            pl.BlockSpec((2, QKV_TILE, D_PACK), lambda t: (0, t, 0)),
            pl.BlockSpec((2, 1, QKV_TILE // LANES, LANES), lambda t: (0, t, 0, 0)),
            _mods_spec(QKV_TILE),
            vec_spec,
            vec_spec,
            _mods_spec(QKV_TILE),
            pl.BlockSpec((1, D_MODEL, 3 * D_MODEL), lambda t: (0, 0, 0)),
        ],
        out_specs=[row_spec, pl.BlockSpec((QKV_TILE, 3 * D_MODEL), lambda t: (t, 0)), cache_spec, cache_spec],
        out_shape=[jax.ShapeDtypeStruct((T_ALL, D_MODEL), F32),
                   jax.ShapeDtypeStruct((T_ALL, 3 * D_MODEL), BF16), cache_shape, cache_shape],
        scratch_shapes=[pltpu.VMEM((D_MODEL, 3 * D_MODEL), BF16)],
        compiler_params=_params("arbitrary"),
        name="qkv",
    )(x1, yg, wsel.reshape(2, T_ALL // QKV_TILE, QKV_TILE // LANES, LANES), mods0, lng, lnb, mods1, w_qkv)


def _dot_nt(a, b):
    return lax.dot_general(a, b, (((1,), (1,)), ((), ())), preferred_element_type=F32)


def _head_masks():
    lane = lax.broadcasted_iota(jnp.int32, (1, LANES), 1)
    return lane < HEAD_DIM, lane >= HEAD_DIM


def _ctx_attn_kernel(q_ref, k_ref, v_ref, o_ref):
    left, right = _head_masks()
    units = [(p, hh) for p in range(HEAD_PAIRS) for hh in range(2)]

    def scores(u):
        p, hh = units[u]
        cols = slice(p * LANES, (p + 1) * LANES)
        q2 = q_ref[:, cols]
        qh = jnp.where(right if hh else left, q2, jnp.zeros_like(q2))
        return _dot_nt(qh, k_ref[:, cols])

    s = scores(0)
    halves = []
    for u, (p, _) in enumerate(units):
        nxt = scores(u + 1) if u + 1 < len(units) else None
        cols = slice(p * LANES, (p + 1) * LANES)
        e = jnp.exp(s - jnp.max(s, axis=-1, keepdims=True))
        o2 = jnp.dot(e.astype(BF16), v_ref[:, cols], preferred_element_type=F32)
        halves.append(o2 / jnp.sum(e, axis=-1, keepdims=True))
        if len(halves) == 2:
            o_ref[:, cols] = jnp.where(left, halves[0], halves[1]).astype(BF16)
            halves = []
        s = nxt


def _ctx_attention(qkv):
    return pl.pallas_call(
        _ctx_attn_kernel,
        grid=(BATCH,),
        in_specs=[pl.BlockSpec((SEQ, D_MODEL), lambda b, j=j: (b, j)) for j in range(3)],
        out_specs=pl.BlockSpec((SEQ, D_MODEL), lambda b: (b, 0)),
        out_shape=jax.ShapeDtypeStruct((T_ALL, D_MODEL), BF16),
        compiler_params=_params("arbitrary"),
        name="ctx_attention",
    )(qkv, qkv, qkv)


_LAT_Q_BLOCK_ROWS = 4
_LAT_KEY_ROWS = ((0, 8), (0, 12), (4, 16), (8, 16))


def _softmax_rows(s_ref, p_ref, l_ref):
    sc = s_ref[...]
    e = jnp.exp(sc - jnp.max(sc, axis=-1, keepdims=True))
    l_ref[...] = jnp.sum(e, axis=-1, keepdims=True)
    p_ref[...] = e.astype(BF16)


def _lat_attn_kernel(q_ref, k_ref, v_ref, ck_ref, cv_ref, eb_ref, o_ctx_ref, o_ref, bias_s, s_scr, p_scr, l_scr):
    del o_ctx_ref
    left, right = _head_masks()

    @pl.when(pl.program_id(1) == 0)
    def _():
        neg = jnp.full((GRID_W, LANES), NEG_INF, F32)
        for hh in range(2):
            for r in range(GRID_ROWS):
                r0 = min(max(r - WIN_H // 2, 0), GRID_ROWS - WIN_H)
                for j in range(GRID_ROWS // 2):
                    parts = []
                    for kr in (2 * j, 2 * j + 1):
                        parts.append(eb_ref[hh, kr - r + WIN_H - 1] if r0 <= kr < r0 + WIN_H else None)
                    if parts[0] is None and parts[1] is None:
                        val = neg
                    else:
                        val = jnp.where(left, neg if parts[0] is None else parts[0],
                                        neg if parts[1] is None else parts[1])
                    bias_s[hh, r * GRID_W:(r + 1) * GRID_W, j * LANES:(j + 1) * LANES] = val

    ck = jnp.concatenate([ck_ref[0, 0, 0], ck_ref[0, 0, 1]], axis=0).astype(BF16)
    cv = jnp.concatenate([cv_ref[0, 0, 0], cv_ref[0, 0, 1]], axis=0).astype(BF16)
    qrows = _LAT_Q_BLOCK_ROWS * GRID_W
    units = [(qb, hh) for qb in range(len(_LAT_KEY_ROWS)) for hh in range(2)]

    def refs(u):
        nk = (_LAT_KEY_ROWS[units[u][0]][1] - _LAT_KEY_ROWS[units[u][0]][0]) * GRID_W
        width = nk + PAST_LEN
        return nk, s_scr.at[u % 2, :, :width], p_scr.at[u % 2, :, :width], l_scr.at[u % 2]

    def scores(u):
        qb, hh = units[u]
        kr0, kr1 = _LAT_KEY_ROWS[qb]
        qs, ks = slice(qb * qrows, (qb + 1) * qrows), slice(kr0 * GRID_W, kr1 * GRID_W)
        nk, s_ref, _, _ = refs(u)
        q2 = q_ref[qs, :]
        qh = jnp.where(right if hh else left, q2, jnp.zeros_like(q2))
        s_ref[:, :nk] = _dot_nt(qh, k_ref[ks, :]) + bias_s[hh, qs, ks]
        s_ref[:, nk:] = jnp.dot(qh, ck, preferred_element_type=F32)

    def weighted_values(u):
        qb, _ = units[u]
        kr0, kr1 = _LAT_KEY_ROWS[qb]
        nk, _, p_ref, l_ref = refs(u)
        o2 = (jnp.dot(p_ref[:, :nk], v_ref[kr0 * GRID_W:kr1 * GRID_W, :], preferred_element_type=F32)
              + _dot_nt(p_ref[:, nk:], cv))
        return o2 / l_ref[...]

    scores(0)
    halves = []
    for u in range(len(units)):
        if u + 1 < len(units):
            scores(u + 1)
        _, s_ref, p_ref, l_ref = refs(u)
        _softmax_rows(s_ref, p_ref, l_ref)
        halves.append(weighted_values(u))
        if len(halves) == 2:
            qb = units[u][0]
            o_ref[qb * qrows:(qb + 1) * qrows, :] = jnp.where(left, halves[0], halves[1]).astype(BF16)
            halves = []


def _lat_attention(qkv, cache_kt, cache_vt, ebias, o_buf):
    row0 = T_CTX // DEC_SEQ
    qrows = _LAT_Q_BLOCK_ROWS * GRID_W
    max_keys = max(k1 - k0 for k0, k1 in _LAT_KEY_ROWS) * GRID_W + PAST_LEN
    cache_spec = pl.BlockSpec((1, 1, 2, HEAD_DIM, PAST_LEN), lambda p, b: (b, 0, p, 0, 0))
    return pl.pallas_call(
        _lat_attn_kernel,
        grid=(HEAD_PAIRS, DEC_BATCH),
        in_specs=[
            pl.BlockSpec((DEC_SEQ, LANES), lambda p, b: (row0 + b, p)),
            pl.BlockSpec((DEC_SEQ, LANES), lambda p, b: (row0 + b, HEAD_PAIRS + p)),
            pl.BlockSpec((DEC_SEQ, LANES), lambda p, b: (row0 + b, 2 * HEAD_PAIRS + p)),
            cache_spec,
            cache_spec,
            pl.BlockSpec((2, 2 * WIN_H - 1, GRID_W, LANES), lambda p, b: (p, 0, 0, 0)),
            pl.BlockSpec(memory_space=pl.ANY),
        ],
        out_specs=pl.BlockSpec((DEC_SEQ, LANES), lambda p, b: (row0 + b, p)),
        out_shape=jax.ShapeDtypeStruct((T_ALL, D_MODEL), BF16),
        scratch_shapes=[pltpu.VMEM((2, DEC_SEQ, DEC_SEQ), F32),
                        pltpu.VMEM((2, qrows, max_keys), F32),
                        pltpu.VMEM((2, qrows, max_keys), BF16),
                        pltpu.VMEM((2, qrows, 1), F32)],
        input_output_aliases={6: 0},
        compiler_params=_params("arbitrary", "arbitrary"),
        name="lat_attention",
    )(qkv, qkv, qkv, cache_kt, cache_vt, ebias, o_buf)


def _bias_kernel(r_ref, o_ref):
    q = lax.broadcasted_iota(jnp.int32, (GRID_W, LANES), 0)
    k = lax.broadcasted_iota(jnp.int32, (GRID_W, LANES), 1) & (GRID_W - 1)
    start = jnp.clip(q - WIN_W // 2, 0, GRID_W - WIN_W)
    col_ok = (k >= start) & (k < start + WIN_W)
    for y in range(2 * WIN_H - 1):
        rows = jnp.broadcast_to(r_ref[0, y:y + 1, :], (GRID_W, LANES))
        rows = pltpu.roll(rows, LANES - (WIN_W - 1), 1, stride=1, stride_axis=0)
        o_ref[0, y] = jnp.where(col_ok, rows, NEG_INF)


def _expanded_bias(rel_bias):
    rel_h, rel_w = 2 * WIN_H - 1, 2 * WIN_W - 1
    r = jnp.pad(rel_bias.astype(F32), ((0, 0), (0, 0), (0, GRID_W - rel_w)))
    r = jnp.concatenate([r, r], axis=2)
    return pl.pallas_call(
        _bias_kernel,
        grid=(N_HEADS,),
        in_specs=[pl.BlockSpec((1, rel_h, LANES), lambda h: (h, 0, 0))],
        out_specs=pl.BlockSpec((1, rel_h, GRID_W, LANES), lambda h: (h, 0, 0, 0)),
        out_shape=jax.ShapeDtypeStruct((N_HEADS, rel_h, GRID_W, LANES), F32),
        compiler_params=_params("arbitrary"),
        name="rel_bias_tiles",
    )(r)


def _wo_kernel(o_ref, x_ref, mods_ref, w_ref, lng_ref, lnb_ref, w2_ref, x1_ref, h2p_ref, logit_ref, w_s):
    @pl.when(pl.program_id(0) == 0)
    def _():
        w_s[...] = w_ref[0].astype(BF16)

    subs = [slice(r0, r0 + TILE) for r0 in range(0, WO_TILE, TILE)]
    mix = jnp.dot(o_ref[subs[0], :], w_s[...], preferred_element_type=F32)
    for j, rs in enumerate(subs):
        nxt = jnp.dot(o_ref[subs[j + 1], :], w_s[...], preferred_element_type=F32) if j + 1 < len(subs) else None
        _epilogue(x_ref[rs, :], mix, mods_ref[0], lng_ref[...], lnb_ref[...], w2_ref,
                  x1_ref.at[rs, :], h2p_ref.at[rs, :], logit_ref.at[:, rs])
        mix = nxt


def _attn_out(o, x, mods, w_o, lng, lnb, w2):
    return pl.pallas_call(
        _wo_kernel,
        grid=(T_ALL // WO_TILE,),
        in_specs=[
            pl.BlockSpec((WO_TILE, D_MODEL), lambda t: (t, 0)),
            pl.BlockSpec((WO_TILE, D_MODEL), lambda t: (t, 0)),
            _mods_spec(WO_TILE),
            pl.BlockSpec((1, D_MODEL, D_MODEL), lambda t: (0, 0, 0)),
        ] + _EPI_IN_SPECS,
        out_specs=[
            pl.BlockSpec((WO_TILE, D_MODEL), lambda t: (t, 0)),
            pl.BlockSpec((WO_TILE, D_PACK), lambda t: (t, 0)),
            pl.BlockSpec((N_EXPERTS, WO_TILE), lambda t: (0, t)),
        ],
        out_shape=_EPI_OUT_SHAPE,
        scratch_shapes=[pltpu.VMEM((D_MODEL, D_MODEL), BF16)],
        compiler_params=_params("arbitrary"),
        name="attn_out",
    )(o, x, mods, w_o, lng, lnb, w2)


def kernel(x_prompt, x_sample, cache_k, cache_v, c, c_ctx, w_ada, b_ada, ln1_g, ln1_b, ln2_g, ln2_b,
           pool_w, pool_scale, w_qkv, w_o, rel_bias, w_router, b_router, w_gate, w_up, w_down):
    mods0, mods1 = _ada(c_ctx, c, w_ada, b_ada, 0), _ada(c_ctx, c, w_ada, b_ada, 1)
    wrt = w_router.T
    wrt_hi = wrt.astype(BF16)
    w2 = jnp.concatenate([wrt_hi, (wrt - wrt_hi.astype(F32)).astype(BF16)], axis=0)

    x, h2p, logits = _pool_layer(x_prompt.reshape(T_CTX, D_MODEL), x_sample.reshape(T_LAT, D_MODEL), mods0,
                                 pool_w[0].astype(BF16), pool_scale[0:1], ln1_g[0:1], ln1_b[0:1], w2)
    yg, wsel = _moe_rows_of_tokens(h2p, logits, b_router, 0, w_gate, w_up, w_down)

    x, qkv, new_kt, new_vt = _qkv(x, yg, wsel, mods0, ln2_g[0:1], ln2_b[0:1], mods1, w_qkv)
    o = _lat_attention(qkv, cache_k.transpose(0, 1, 2, 4, 3), cache_v.transpose(0, 1, 2, 4, 3),
                       _expanded_bias(rel_bias[0]), _ctx_attention(qkv))
    x, h2p, logits = _attn_out(o, x, mods1, w_o, ln1_g[1:2], ln1_b[1:2], w2)
    yg, wsel = _moe_rows_of_tokens(h2p, logits, b_router, 1, w_gate, w_up, w_down)
    y_ctx, y_lat = [_combine(x, yg, wsel, mods1, ln2_g[1:2], ln2_b[1:2], t0, n) for t0, n in SEGMENTS]
    return (y_ctx.reshape(BATCH, SEQ, D_MODEL), y_lat.reshape(DEC_BATCH, DEC_SEQ, D_MODEL),
            new_kt.transpose(0, 1, 2, 4, 3), new_vt.transpose(0, 1, 2, 4, 3))
```

```python
import functools

import jax
import jax.numpy as jnp
from jax import lax
from jax.experimental import pallas as pl
from jax.experimental.pallas import tpu as pltpu
from jax.experimental.pallas import tpu_sc as plsc

F32 = jnp.float32
BF16 = jnp.bfloat16

D_MODEL = 1024
BATCH = 16
SEQ = 256
DEC_BATCH = 8
DEC_SEQ = 1024
PAST_LEN = 512
GRID_W = 64
GRID_ROWS = DEC_SEQ // GRID_W
POOL_SIZES = (2, 4, 8, 16)
POOL_GROUP_DIM = D_MODEL // len(POOL_SIZES)
POOL_HALO = 8
N_HEADS = 16
HEAD_DIM = 64
WIN_H = 8
WIN_W = 16
N_EXPERTS = 16
EXPERTS_PER_GROUP = 4
N_EXPERT_GROUPS = 4
D_FF = 512
ALPHA = (2.0 * 2) ** 0.25
LN_EPS = 1e-5
NEG_INF = -1e30

T_CTX = BATCH * SEQ
T_LAT = DEC_BATCH * DEC_SEQ
T_ALL = T_CTX + T_LAT
N_COND = 16
TILE = 256
TILES_PER_LAT_SEQ = DEC_SEQ // TILE
MOE_TILE = 1024
MOE_SUB = 256
X_SLOTS = 3
SEGMENTS = ((0, T_CTX), (T_CTX, T_LAT))
LN_TILE = 512
QKV_TILE = 512
POOL_TILE = 512
WO_TILE = 1024
SC_WINDOW = 64
D_PACK = D_MODEL // 2
ROUTE_ROWS = T_ALL // 128
HEAD_PAIRS = N_HEADS // 2
LANES = 128
VMEM_LIMIT = 56 * 1024 * 1024


def _cond_row(t, tile):
    ctx_tiles = T_CTX // tile
    per_seq = DEC_SEQ // tile
    return jnp.maximum(t - ctx_tiles + per_seq, 0) // per_seq


def _params(*sem):
    return pltpu.CompilerParams(dimension_semantics=sem, vmem_limit_bytes=VMEM_LIMIT)


def _pack_halves(v):
    half = v.shape[1] // 2
    hi = lax.bitcast_convert_type(v[:, :half].astype(F32), jnp.uint32)
    lo = lax.bitcast_convert_type(v[:, half:].astype(F32), jnp.uint32)
    return hi | (lo >> 16)


def _unpack_halves(p):
    left = lax.bitcast_convert_type(p & jnp.uint32(0xFFFF0000), F32)
    right = lax.bitcast_convert_type(p << 16, F32)
    return left, right


def _ada_kernel(layer, cctx_ref, c_ref, w_ref, b_ref, o_ref, cond_s):
    cond_s[...] = jnp.zeros_like(cond_s)
    cond_s[0:1, :] = cctx_ref[...]
    cond_s[1:1 + DEC_BATCH, :] = c_ref[...]
    cnd = cond_s[...]
    act = cnd * jax.nn.sigmoid(cnd)
    a_hi = act.astype(BF16)
    a_lo = (act - a_hi.astype(F32)).astype(BF16)
    w = w_ref[0]
    w_hi = w.astype(BF16)
    w_lo = (w - w_hi.astype(F32)).astype(BF16)
    a2 = jnp.concatenate([a_hi, a_lo], axis=0)
    p = jnp.dot(a2, w_hi, preferred_element_type=F32)
    q = jnp.dot(a_hi, w_lo, preferred_element_type=F32)
    val = p[:N_COND] + p[N_COND:] + q + b_ref[layer:layer + 1, :]

    def put(j):
        for c in range(N_COND):
            o_ref[c, j:j + 1, :] = val[c:c + 1, :]

    for j in range(o_ref.shape[1]):
        pl.when(pl.program_id(0) == j)(functools.partial(put, j))


def _ada(c_ctx, c, w_ada, b_ada, layer):
    depth, d, n = w_ada.shape
    return pl.pallas_call(
        functools.partial(_ada_kernel, layer),
        grid=(n // d,),
        in_specs=[
            pl.BlockSpec((1, d), lambda j: (0, 0)),
            pl.BlockSpec((DEC_BATCH, d), lambda j: (0, 0)),
            pl.BlockSpec((1, d, d), lambda j: (layer, 0, j)),
            pl.BlockSpec((depth, d), lambda j: (0, j)),
        ],
        out_specs=pl.BlockSpec((N_COND, n // d, d), lambda j: (0, 0, 0)),
        out_shape=jax.ShapeDtypeStruct((N_COND, n // d, d), F32),
        scratch_shapes=[pltpu.VMEM((N_COND, d), F32)],
        compiler_params=_params("arbitrary"),
        name="ada",
    )(c_ctx.reshape(1, d), c, w_ada, b_ada)


def _post_norm(x, upd, g, b):
    y = ALPHA * x + upd
    mu = jnp.mean(y, axis=-1, keepdims=True)
    yc = y - mu
    var = jnp.mean(yc * yc, axis=-1, keepdims=True)
    return yc * lax.rsqrt(var + LN_EPS) * g + b


def _router_logits(h2, h_hi, w2_ref):
    h_lo = (h2 - h_hi.astype(F32)).astype(BF16)
    w2 = w2_ref[...]
    p = _dot_nt(w2, h_hi)
    q = _dot_nt(w2[:N_EXPERTS], h_lo)
    return p[:N_EXPERTS] + p[N_EXPERTS:] + q


def _epilogue(x, mix, m, lng, lnb, w2_ref, x1_ref, h2p_ref, logit_ref):
    g1, sh2, sc2 = m[2:3, :], m[3:4, :], m[4:5, :]
    x1 = _post_norm(x, g1 * mix, lng, lnb)
    h2 = x1 * (1.0 + sc2) + sh2
    h_hi = h2.astype(BF16)
    x1_ref[...] = x1
    h2p_ref[...] = _pack_halves(h_hi)
    logit_ref[...] = _router_logits(h2, h_hi, w2_ref)


_EPI_IN_SPECS = [
    pl.BlockSpec((1, D_MODEL), lambda t: (0, 0)),
    pl.BlockSpec((1, D_MODEL), lambda t: (0, 0)),
    pl.BlockSpec((2 * N_EXPERTS, D_MODEL), lambda t: (0, 0)),
]
_EPI_OUT_SHAPE = [
    jax.ShapeDtypeStruct((T_ALL, D_MODEL), F32),
    jax.ShapeDtypeStruct((T_ALL, D_PACK), jnp.uint32),
    jax.ShapeDtypeStruct((N_EXPERTS, T_ALL), F32),
]


def _route_kernel(logit_ref, br_ref, pos_ref, w_ref, table_ref):
    aff = jax.nn.sigmoid(logit_ref[...])
    sel = aff + br_ref[...]
    sel_rows = [sel[e] for e in range(N_EXPERTS)]
    aff_rows = [aff[e] for e in range(N_EXPERTS)]

    def group_score(g):
        r = sel_rows[g * EXPERTS_PER_GROUP:(g + 1) * EXPERTS_PER_GROUP]
        best = None
        for i in range(EXPERTS_PER_GROUP):
            for j in range(i + 1, EXPERTS_PER_GROUP):
                pair = r[i] + r[j]
                best = pair if best is None else jnp.maximum(best, pair)
        return best

    best = group_score(0)
    gidx = jnp.zeros_like(best, dtype=jnp.int32)
    for g in range(1, N_EXPERT_GROUPS):
        sc = group_score(g)
        better = sc > best
        gidx = jnp.where(better, g, gidx)
        best = jnp.where(better, sc, best)

    def pick_group(rows, j):
        out = rows[j]
        for g in range(1, N_EXPERT_GROUPS):
            out = jnp.where(gidx == g, rows[g * EXPERTS_PER_GROUP + j], out)
        return out

    cand = [pick_group(sel_rows, j) for j in range(EXPERTS_PER_GROUP)]
    cand_aff = [pick_group(aff_rows, j) for j in range(EXPERTS_PER_GROUP)]

    def argmax_first(vals):
        bv, bi, ba = vals[0], jnp.zeros_like(gidx), cand_aff[0]
        for j in range(1, EXPERTS_PER_GROUP):
            better = vals[j] > bv
            bv = jnp.where(better, vals[j], bv)
            bi = jnp.where(better, j, bi)
            ba = jnp.where(better, cand_aff[j], ba)
        return bi, ba

    i1, a1 = argmax_first(cand)
    rest = [jnp.where(i1 == j, -jnp.inf, cand[j]) for j in range(EXPERTS_PER_GROUP)]
    i2, a2 = argmax_first(rest)
    denom = a1 + a2
    base = gidx * EXPERTS_PER_GROUP
    w_ref[0] = a1 / denom
    w_ref[1] = a2 / denom
    _plan(base + i1, base + i2, pos_ref, table_ref)


def _plan(e1, e2, pos_ref, table_ref):
    rows = e1.shape[0]
    upper = (lax.broadcasted_iota(jnp.int32, (LANES, LANES), 0)
             <= lax.broadcasted_iota(jnp.int32, (LANES, LANES), 1)).astype(BF16)
    lower = (lax.broadcasted_iota(jnp.int32, (rows, rows), 1)
             < lax.broadcasted_iota(jnp.int32, (rows, rows), 0)).astype(BF16)
    masks = [(choice == e) for e in range(N_EXPERTS) for choice in (e1, e2)]
    hits = jnp.concatenate([m.astype(BF16) for m in masks], axis=0)
    within = jnp.dot(hits, upper, preferred_element_type=F32)
    totals = [within[i * rows:(i + 1) * rows, LANES - 1:LANES] for i in range(2 * N_EXPERTS)]
    spread = jnp.concatenate([jnp.broadcast_to(t, (rows, LANES)) for t in totals], axis=1).astype(BF16)
    above = jnp.dot(lower, spread, preferred_element_type=F32)

    tile_start = lax.broadcasted_iota(jnp.int32, (1, LANES), 1).astype(F32) * MOE_TILE
    start = jnp.zeros((1, 1), F32)
    pos = [jnp.zeros((rows, LANES), F32), jnp.zeros((rows, LANES), F32)]
    tile_expert = jnp.zeros((1, LANES), F32)
    tile_valid = jnp.zeros((1, LANES), F32)
    tile_block = jnp.zeros((1, LANES), F32)
    used = jnp.zeros((1, LANES), F32)
    last_expert = jnp.zeros((1, 1), F32)
    last_block = jnp.zeros((1, 1), F32)
    for e in range(N_EXPERTS):
        first = jnp.zeros((1, 1), F32)
        for c in range(2):
            i = 2 * e + c
            hit = masks[i]
            rank = (within[i * rows:(i + 1) * rows] - hit.astype(F32)
                    + above[:, i * LANES:(i + 1) * LANES] + (start + first))
            pos[c] = jnp.where(hit, rank, pos[c])
            first = first + jnp.sum(totals[i], axis=0, keepdims=True)
        count = first
        n_tiles = jnp.floor((count + (MOE_TILE - 1)) * (1.0 / MOE_TILE))
        end = start + n_tiles * MOE_TILE
        inside = jnp.logical_and(tile_start >= start, tile_start < end)
        block_in_expert = n_tiles - 1.0 - (tile_start - start) * (1.0 / MOE_TILE)
        tile_expert = jnp.where(inside, float(e), tile_expert)
        tile_valid = jnp.where(inside, jnp.clip(count - block_in_expert * MOE_TILE, 0.0, float(MOE_TILE)), tile_valid)
        tile_block = jnp.where(inside, start * (1.0 / MOE_TILE) + block_in_expert, tile_block)
        used = jnp.where(inside, 1.0, used)
        has_rows = count > 0.0
        last_expert = jnp.where(has_rows, float(e), last_expert)
        last_block = jnp.where(has_rows, start * (1.0 / MOE_TILE), last_block)
        start = end
    pos_ref[0] = pos[0].astype(jnp.int32)
    pos_ref[1] = pos[1].astype(jnp.int32)
    idle = used == 0.0
    table_ref[0:1, :] = jnp.where(idle, last_expert, tile_expert).astype(jnp.int32)
    table_ref[1:2, :] = tile_valid.astype(jnp.int32)
    table_ref[2:3, :] = jnp.where(idle, last_block, tile_block).astype(jnp.int32)
    table_ref[3:8, :] = jnp.zeros((5, LANES), jnp.int32)


def _route(logits, b_router):
    tiles = _moe_rows(T_ALL) // MOE_TILE
    pos, wsel, table = pl.pallas_call(
        _route_kernel,
        out_shape=[jax.ShapeDtypeStruct((2, ROUTE_ROWS, LANES), jnp.int32),
                   jax.ShapeDtypeStruct((2, ROUTE_ROWS, LANES), F32),
                   jax.ShapeDtypeStruct((8, LANES), jnp.int32)],
        compiler_params=pltpu.CompilerParams(vmem_limit_bytes=VMEM_LIMIT),
        name="route",
    )(logits.reshape(N_EXPERTS, ROUTE_ROWS, LANES), b_router.reshape(N_EXPERTS, 1, 1))
    return pos.reshape(2 * T_ALL), wsel, table[0, :tiles], table[1, :tiles], table[2, :tiles]


def _mods_spec(tile, first_tile=0):
    return pl.BlockSpec((1, 6, D_MODEL), lambda t: (_cond_row(first_tile + t, tile), 0, 0))


def _pool_kernel(xa_ref, xb_ref, xp_ref, xn_ref, mods_ref, pw_ref, ps_ref, lng_ref, lnb_ref, w2_ref,
                 x1_ref, h2p_ref, logit_ref, pw_s):
    t = pl.program_id(0)

    @pl.when(t == 0)
    def _():
        pw_s[...] = pw_ref[0].astype(BF16)

    m = mods_ref[0]
    sh1, sc1 = m[0:1, :], m[1:2, :]
    subs = POOL_TILE // TILE
    in_lat = t >= T_CTX // POOL_TILE
    tile_in_seq = (t - T_CTX // POOL_TILE) % (DEC_SEQ // POOL_TILE)
    seq_len = jnp.where(in_lat, DEC_SEQ, SEQ)

    x = jnp.where(in_lat, xb_ref[...], xa_ref[...])
    h = x * (1.0 + sc1) + sh1
    halo_before = xp_ref[...] * (1.0 + sc1) + sh1
    halo_after = xn_ref[...] * (1.0 + sc1) + sh1
    ext = TILE + 2 * POOL_HALO

    def mixer(j):
        place = tile_in_seq * subs + j
        is_first = jnp.logical_or(jnp.logical_not(in_lat), place == 0)
        is_last = jnp.logical_or(jnp.logical_not(in_lat), place == TILES_PER_LAT_SEQ - 1)
        before = halo_before if j == 0 else h[j * TILE - POOL_HALO:j * TILE]
        after = halo_after if j == subs - 1 else h[(j + 1) * TILE:(j + 1) * TILE + POOL_HALO]
        hj = h[j * TILE:(j + 1) * TILE]
        hext = jnp.concatenate([jnp.where(is_first, 0.0, before), hj, jnp.where(is_last, 0.0, after)], axis=0)
        pos = jnp.where(in_lat, place * TILE, 0) + lax.broadcasted_iota(jnp.int32, (TILE, 1), 0)
        outs = []
        for g, w in enumerate(POOL_SIZES):
            lo_c, hi_c = g * POOL_GROUP_DIM, (g + 1) * POOL_GROUP_DIM
            a = hext[:, lo_c:hi_c]
            k = 1
            while k < w:
                a = a + pltpu.roll(a, ext - k, axis=0)
                k *= 2
            off = POOL_HALO - w // 2
            win = pltpu.roll(a, ext - off, axis=0)[:TILE] if off else a[:TILE]
            lo = jnp.maximum(pos - w // 2, 0)
            hi = jnp.minimum(pos - w // 2 + w, seq_len)
            cnt = (hi - lo).astype(F32)
            pooled = win / cnt - hj[:, lo_c:hi_c]
            outs.append(jnp.dot(pooled.astype(BF16), pw_s[g], preferred_element_type=F32))
        return jnp.concatenate(outs, axis=1) * ps_ref[...]

    mix = mixer(0)
    for j in range(subs):
        nxt = mixer(j + 1) if j + 1 < subs else None
        rs = slice(j * TILE, (j + 1) * TILE)
        _epilogue(x[rs, :], mix, m, lng_ref[...], lnb_ref[...], w2_ref,
                  x1_ref.at[rs, :], h2p_ref.at[rs, :], logit_ref.at[:, rs])
        mix = nxt


def _pool_layer(x_ctx, x_lat, mods, pool_w, pool_scale, lng, lnb, w2):
    ctx_tiles = T_CTX // POOL_TILE
    halo_blocks = POOL_TILE // POOL_HALO
    last_halo = T_LAT // POOL_HALO - 1
    return pl.pallas_call(
        _pool_kernel,
        grid=(T_ALL // POOL_TILE,),
        in_specs=[
            pl.BlockSpec((POOL_TILE, D_MODEL), lambda t: (jnp.minimum(t, ctx_tiles - 1), 0)),
            pl.BlockSpec((POOL_TILE, D_MODEL), lambda t: (jnp.maximum(t - ctx_tiles, 0), 0)),
            pl.BlockSpec((POOL_HALO, D_MODEL), lambda t: (jnp.maximum((t - ctx_tiles) * halo_blocks - 1, 0), 0)),
            pl.BlockSpec((POOL_HALO, D_MODEL),
                         lambda t: (jnp.clip((t - ctx_tiles + 1) * halo_blocks, 0, last_halo), 0)),
            _mods_spec(POOL_TILE),
            pl.BlockSpec((1, len(POOL_SIZES), POOL_GROUP_DIM, POOL_GROUP_DIM), lambda t: (0, 0, 0, 0)),
            pl.BlockSpec((1, D_MODEL), lambda t: (0, 0)),
        ] + _EPI_IN_SPECS,
        out_specs=[
            pl.BlockSpec((POOL_TILE, D_MODEL), lambda t: (t, 0)),
            pl.BlockSpec((POOL_TILE, D_PACK), lambda t: (t, 0)),
            pl.BlockSpec((N_EXPERTS, POOL_TILE), lambda t: (0, t)),
        ],
        out_shape=_EPI_OUT_SHAPE,
        scratch_shapes=[pltpu.VMEM((len(POOL_SIZES), POOL_GROUP_DIM, POOL_GROUP_DIM), BF16)],
        compiler_params=_params("arbitrary"),
        name="pool_mixer",
    )(x_ctx, x_lat, x_lat, x_lat, mods, pool_w, pool_scale, lng, lnb, w2)


def _moe_rows(n):
    return 2 * n + N_EXPERTS * MOE_TILE


def _expert_kernel(te_ref, nv_ref, tb_ref, x_hbm, wg_ref, wu_ref, wd_ref, y_ref, wgu_s, wd_s, x_ring, x_sems):
    i = pl.program_id(0)
    steps = pl.num_programs(0)
    prev = te_ref[jnp.maximum(i - 1, 0)]
    changed = jnp.logical_or(i == 0, te_ref[i] != prev)

    def x_copy(step):
        slot = step % X_SLOTS
        return pltpu.make_async_copy(x_hbm.at[pl.ds(tb_ref[step] * MOE_TILE, MOE_TILE)], x_ring.at[slot],
                                     x_sems.at[slot])

    def start_x(step):
        @pl.when(nv_ref[step] > 0)
        def _():
            x_copy(step).start()

    @pl.when(i == 0)
    def _():
        for step in range(X_SLOTS - 1):
            start_x(step)

    @pl.when(i + (X_SLOTS - 1) < steps)
    def _():
        start_x(i + (X_SLOTS - 1))

    @pl.when(changed)
    def _():
        wgu_s[:, :D_FF] = wg_ref[0, 0].astype(BF16)
        wgu_s[:, D_FF:] = wu_ref[0, 0].astype(BF16)
        wd_s[...] = wd_ref[0, 0].astype(BF16)

    nv = nv_ref[i]
    x_ref = x_ring.at[i % X_SLOTS]

    @pl.when(nv > 0)
    def _():
        x_copy(i).wait()

    def gate_up(r0):
        rows = r0 + lax.broadcasted_iota(jnp.int32, (MOE_SUB, 1), 0)
        xp = jnp.where(rows < nv, x_ref[r0:r0 + MOE_SUB, :], jnp.uint32(0))
        left, right = _unpack_halves(xp)
        xb = jnp.concatenate([left.astype(BF16), right.astype(BF16)], axis=1)
        return jnp.dot(xb, wgu_s[...], preferred_element_type=F32)

    def down(r0, gu):
        gate, up = gu[:, :D_FF], gu[:, D_FF:]
        he = (gate * jax.nn.sigmoid(gate) * up).astype(BF16)
        y_ref[r0:r0 + MOE_SUB, :] = _pack_halves(jnp.dot(he, wd_s[...], preferred_element_type=F32).astype(BF16))

    def run(n_sub):
        gu = gate_up(0)
        for j in range(n_sub):
            nxt = gate_up((j + 1) * MOE_SUB) if j + 1 < n_sub else None
            down(j * MOE_SUB, gu)
            gu = nxt

    n_subs = MOE_TILE // MOE_SUB
    for n_sub in range(1, n_subs + 1):
        lo = (n_sub - 1) * MOE_SUB
        in_range = nv > lo if n_sub == n_subs else jnp.logical_and(nv > lo, nv <= lo + MOE_SUB)
        pl.when(in_range)(functools.partial(run, n_sub))


def _experts(xs, tile_expert, tile_valid, tile_block, layer, w_gate, w_up, w_down):
    grid_spec = pltpu.PrefetchScalarGridSpec(
        num_scalar_prefetch=3,
        grid=(xs.shape[0] // MOE_TILE,),
        in_specs=[
            pl.BlockSpec(memory_space=pl.ANY),
            pl.BlockSpec((1, 1, D_MODEL, D_FF), lambda i, te, nv, tb: (layer, te[i], 0, 0)),
            pl.BlockSpec((1, 1, D_MODEL, D_FF), lambda i, te, nv, tb: (layer, te[i], 0, 0)),
            pl.BlockSpec((1, 1, D_FF, D_MODEL), lambda i, te, nv, tb: (layer, te[i], 0, 0)),
        ],
        out_specs=pl.BlockSpec((MOE_TILE, D_PACK), lambda i, te, nv, tb: (tb[i], 0)),
        scratch_shapes=[pltpu.VMEM((D_MODEL, 2 * D_FF), BF16), pltpu.VMEM((D_FF, D_MODEL), BF16),
                        pltpu.VMEM((X_SLOTS, MOE_TILE, D_PACK), jnp.uint32), pltpu.SemaphoreType.DMA((X_SLOTS,))],
    )
    return pl.pallas_call(
        _expert_kernel,
        grid_spec=grid_spec,
        out_shape=jax.ShapeDtypeStruct(xs.shape, jnp.uint32),
        compiler_params=_params("arbitrary"),
        name="experts",
    )(tile_expert, tile_valid, tile_block, xs, w_gate, w_up, w_down)


SC_WORKERS = 32


def _sc_mesh():
    return plsc.VectorSubcoreMesh(core_axis_name="core", subcore_axis_name="subcore")


def _sc_worker():
    return lax.axis_index("subcore") * 2 + lax.axis_index("core")


def _dispatch(h2, pos):
    n = h2.shape[0]
    per_worker = n // SC_WORKERS
    pairs = per_worker // (2 * SC_WINDOW)
    one_set = [pltpu.VMEM((SC_WINDOW,), jnp.int32), pltpu.VMEM((SC_WINDOW,), jnp.int32),
               pltpu.VMEM((SC_WINDOW, D_PACK), jnp.uint32),
               pltpu.SemaphoreType.DMA, pltpu.SemaphoreType.DMA, pltpu.SemaphoreType.DMA]

    @functools.partial(pl.kernel, out_type=jax.ShapeDtypeStruct((_moe_rows(n), D_PACK), jnp.uint32),
                       mesh=_sc_mesh(), scratch_types=one_set + one_set, name="moe_dispatch")
    def scatter(x_hbm, i_hbm, o_hbm, *bufs):
        base = _sc_worker() * per_worker
        sets = (bufs[:6], bufs[6:])

        def rows_in(j, s):
            _, _, rows, sem, _, _ = sets[s]
            return pltpu.make_async_copy(x_hbm.at[pl.ds(base + j * SC_WINDOW, SC_WINDOW)], rows, sem)

        def rows_out(k, s):
            idx, rows, sem = sets[s][k], sets[s][2], sets[s][4 + k]
            return pltpu.make_async_copy(rows, o_hbm.at[idx], sem)

        def fetch(j, s):
            t0 = base + j * SC_WINDOW
            pltpu.sync_copy(i_hbm.at[pl.ds(t0, SC_WINDOW)], sets[s][0])
            pltpu.sync_copy(i_hbm.at[pl.ds(n + t0, SC_WINDOW)], sets[s][1])
            rows_in(j, s).start()

        fetch(0, 0)

        @pl.loop(0, pairs)
        def _(g):
            j0, j1 = 2 * g, 2 * g + 1
            fetch(j1, 1)
            rows_in(j0, 0).wait()
            rows_out(0, 0).start()
            rows_out(1, 0).start()
            rows_in(j1, 1).wait()
            rows_out(0, 1).start()
            rows_out(1, 1).start()
            rows_out(0, 0).wait()
            rows_out(1, 0).wait()

            @pl.when(g + 1 < pairs)
            def _():
                fetch(j0 + 2, 0)

            rows_out(0, 1).wait()
            rows_out(1, 1).wait()

    return scatter(h2, pos)


def _gather_pairs(y, pos):
    n = pos.shape[0] // 2
    per_worker = 2 * n // SC_WORKERS
    pairs = per_worker // (2 * SC_WINDOW)
    scratch = [pltpu.VMEM((per_worker,), jnp.int32),
               pltpu.VMEM((SC_WINDOW, D_PACK), jnp.uint32), pltpu.VMEM((SC_WINDOW, D_PACK), jnp.uint32),
               pltpu.SemaphoreType.DMA, pltpu.SemaphoreType.DMA, pltpu.SemaphoreType.DMA, pltpu.SemaphoreType.DMA]

    @functools.partial(pl.kernel, out_type=jax.ShapeDtypeStruct((2 * n, D_PACK), jnp.uint32),
                       mesh=_sc_mesh(), scratch_types=scratch, name="moe_gather")
    def gather(y_hbm, i_hbm, o_hbm, idx_v, rows0, rows1, read0, read1, write0, write1):
        base = _sc_worker() * per_worker
        pltpu.sync_copy(i_hbm.at[pl.ds(base, per_worker)], idx_v)

        def read(j, rows, sem):
            return pltpu.make_async_copy(y_hbm.at[idx_v.at[pl.ds(j * SC_WINDOW, SC_WINDOW)]], rows, sem)

        def write(j, rows, sem):
            return pltpu.make_async_copy(rows, o_hbm.at[pl.ds(base + j * SC_WINDOW, SC_WINDOW)], sem)

        read(0, rows0, read0).start()

        @pl.loop(0, pairs)
        def _(g):
            j0, j1 = 2 * g, 2 * g + 1
            read(j1, rows1, read1).start()
            read(j0, rows0, read0).wait()
            write(j0, rows0, write0).start()
            read(j1, rows1, read1).wait()
            write(j1, rows1, write1).start()
            write(j0, rows0, write0).wait()

            @pl.when(g + 1 < pairs)
            def _():
                read(j0 + 2, rows0, read0).start()

            write(j1, rows1, write1).wait()

    return gather(y, pos).reshape(2, n, D_PACK)


def _moe_post_norm(x_ref, yg_ref, w_ref, mods_ref, lng_ref, lnb_ref):
    g2 = mods_ref[0][5:6, :]
    groups = x_ref.shape[0] // LANES
    wt = jnp.concatenate([w_ref[0, 0], w_ref[1, 0]], axis=0).T
    moe = []
    for r in range(groups):
        rows = slice(r * LANES, (r + 1) * LANES)
        y1 = jnp.concatenate(_unpack_halves(yg_ref[0, rows, :]), axis=1)
        y2 = jnp.concatenate(_unpack_halves(yg_ref[1, rows, :]), axis=1)
        moe.append(wt[:, r:r + 1] * y1 + wt[:, groups + r:groups + r + 1] * y2)
    return _post_norm(x_ref[...], g2 * jnp.concatenate(moe, axis=0), lng_ref[...], lnb_ref[...])


def _combine_kernel(x_ref, yg_ref, w_ref, mods_ref, lng_ref, lnb_ref, o_ref):
    o_ref[...] = _moe_post_norm(x_ref, yg_ref, w_ref, mods_ref, lng_ref, lnb_ref)


def _combine(x1, yg, wsel, mods, lng, lnb, t0, n):
    off = t0 // LN_TILE
    return pl.pallas_call(
        _combine_kernel,
        grid=(n // LN_TILE,),
        in_specs=[
            pl.BlockSpec((LN_TILE, D_MODEL), lambda t: (off + t, 0)),
            pl.BlockSpec((2, LN_TILE, D_PACK), lambda t: (0, off + t, 0)),
            pl.BlockSpec((2, 1, LN_TILE // LANES, LANES), lambda t: (0, off + t, 0, 0)),
            _mods_spec(LN_TILE, off),
            pl.BlockSpec((1, D_MODEL), lambda t: (0, 0)),
            pl.BlockSpec((1, D_MODEL), lambda t: (0, 0)),
        ],
        out_specs=pl.BlockSpec((LN_TILE, D_MODEL), lambda t: (t, 0)),
        out_shape=jax.ShapeDtypeStruct((n, D_MODEL), F32),
        compiler_params=_params("arbitrary"),
        name="moe_combine",
    )(x1, yg, wsel.reshape(2, T_ALL // LN_TILE, LN_TILE // LANES, LANES), mods, lng, lnb)


def _moe_rows_of_tokens(h2p, logits, b_router, layer, w_gate, w_up, w_down):
    pos, wsel, tile_expert, tile_valid, tile_block = _route(logits, b_router)
    xs = _dispatch(h2p, pos)
    y = _experts(xs, tile_expert, tile_valid, tile_block, layer, w_gate, w_up, w_down)
    return _gather_pairs(y, pos), wsel


def _qkv_kernel(x1_ref, yg_ref, wsel_ref, mods0_ref, lng_ref, lnb_ref, mods_ref, w_ref,
                x_ref, qkv_ref, nk_ref, nv_ref, w_s):
    t = pl.program_id(0)

    @pl.when(t == 0)
    def _():
        w_s[...] = w_ref[0].astype(BF16)

    m = mods_ref[0]
    sh1, sc1 = m[0:1, :], m[1:2, :]

    def norm(rows):
        x = _moe_post_norm(x1_ref.at[rows], yg_ref.at[:, rows], wsel_ref, mods0_ref, lng_ref, lnb_ref)
        x_ref[rows, :] = x
        return (x * (1.0 + sc1) + sh1).astype(BF16)

    def project(rows, h):
        r = jnp.dot(h, w_s[...], preferred_element_type=F32)
        qkv_ref[rows, :D_MODEL] = (r[:, :D_MODEL] * (HEAD_DIM ** -0.5)).astype(BF16)
        qkv_ref[rows, D_MODEL:] = r[:, D_MODEL:].astype(BF16)
        return r

    is_ctx = t < T_CTX // QKV_TILE
    rows = slice(0, QKV_TILE)

    @pl.when(is_ctx)
    def _():
        r = project(rows, norm(rows))
        for out_ref, base in ((nk_ref, D_MODEL), (nv_ref, 2 * D_MODEL)):
            for b in range(QKV_TILE // SEQ):
                for p in range(HEAD_PAIRS):
                    pair = r[b * SEQ:(b + 1) * SEQ, base + p * LANES: base + (p + 1) * LANES].T
                    out_ref[b, 0, 2 * p] = pair[:HEAD_DIM]
                    out_ref[b, 0, 2 * p + 1] = pair[HEAD_DIM:]

    @pl.when(jnp.logical_not(is_ctx))
    def _():
        project(rows, norm(rows))


def _qkv(x1, yg, wsel, mods0, lng, lnb, mods1, w_qkv):
    seqs = QKV_TILE // SEQ
    row_spec = pl.BlockSpec((QKV_TILE, D_MODEL), lambda t: (t, 0))
    vec_spec = pl.BlockSpec((1, D_MODEL), lambda t: (0, 0))

    cache_spec = pl.BlockSpec((seqs, 1, N_HEADS, HEAD_DIM, SEQ),
                              lambda t: (jnp.minimum(t, T_CTX // QKV_TILE - 1), 0, 0, 0, 0))
    cache_shape = jax.ShapeDtypeStruct((BATCH, 1, N_HEADS, HEAD_DIM, SEQ), F32)
    return pl.pallas_call(
        _qkv_kernel,
        grid=(T_ALL // QKV_TILE,),
        in_specs=[
            row_spec,
            pl.BlockSpec((2, QKV_TILE, D_PACK), lambda t: (0, t, 0)),
            pl.BlockSpec((2, 1, QKV_TILE // LANES, LANES), lambda t: (0, t, 0, 0)),
            _mods_spec(QKV_TILE),
            vec_spec,
            vec_spec,
            _mods_spec(QKV_TILE),
            pl.BlockSpec((1, D_MODEL, 3 * D_MODEL), lambda t: (0, 0, 0)),
        ],
        out_specs=[row_spec, pl.BlockSpec((QKV_TILE, 3 * D_MODEL), lambda t: (t, 0)), cache_spec, cache_spec],
        out_shape=[jax.ShapeDtypeStruct((T_ALL, D_MODEL), F32),
                   jax.ShapeDtypeStruct((T_ALL, 3 * D_MODEL), BF16), cache_shape, cache_shape],
        scratch_shapes=[pltpu.VMEM((D_MODEL, 3 * D_MODEL), BF16)],
        compiler_params=_params("arbitrary"),
        name="qkv",
    )(x1, yg, wsel.reshape(2, T_ALL // QKV_TILE, QKV_TILE // LANES, LANES), mods0, lng, lnb, mods1, w_qkv)


def _dot_nt(a, b):
    return lax.dot_general(a, b, (((1,), (1,)), ((), ())), preferred_element_type=F32)


def _head_masks():
    lane = lax.broadcasted_iota(jnp.int32, (1, LANES), 1)
    return lane < HEAD_DIM, lane >= HEAD_DIM


def _ctx_attn_kernel(q_ref, k_ref, v_ref, o_ref):
    left, right = _head_masks()
    units = [(p, hh) for p in range(HEAD_PAIRS) for hh in range(2)]

    def scores(u):
        p, hh = units[u]
        cols = slice(p * LANES, (p + 1) * LANES)
        q2 = q_ref[:, cols]
        qh = jnp.where(right if hh else left, q2, jnp.zeros_like(q2))
        return _dot_nt(qh, k_ref[:, cols])

    s = scores(0)
    halves = []
    for u, (p, _) in enumerate(units):
        nxt = scores(u + 1) if u + 1 < len(units) else None
        cols = slice(p * LANES, (p + 1) * LANES)
        e = jnp.exp(s - jnp.max(s, axis=-1, keepdims=True))
        o2 = jnp.dot(e.astype(BF16), v_ref[:, cols], preferred_element_type=F32)
        halves.append(o2 / jnp.sum(e, axis=-1, keepdims=True))
        if len(halves) == 2:
            o_ref[:, cols] = jnp.where(left, halves[0], halves[1]).astype(BF16)
            halves = []
        s = nxt


def _ctx_attention(qkv):
    return pl.pallas_call(
        _ctx_attn_kernel,
        grid=(BATCH,),
        in_specs=[pl.BlockSpec((SEQ, D_MODEL), lambda b, j=j: (b, j)) for j in range(3)],
        out_specs=pl.BlockSpec((SEQ, D_MODEL), lambda b: (b, 0)),
        out_shape=jax.ShapeDtypeStruct((T_ALL, D_MODEL), BF16),
        compiler_params=_params("arbitrary"),
        name="ctx_attention",
    )(qkv, qkv, qkv)


_LAT_Q_BLOCK_ROWS = 2
_LAT_KEY_ROWS = ((0, 8), (0, 8), (0, 10), (2, 12), (4, 14), (6, 16), (8, 16), (8, 16))


def _softmax_rows(s_ref, p_ref, l_ref):
    sc = s_ref[...]
    e = jnp.exp(sc - jnp.max(sc, axis=-1, keepdims=True))
    l_ref[...] = jnp.sum(e, axis=-1, keepdims=True)
    p_ref[...] = e.astype(BF16)


def _lat_attn_kernel(q_ref, k_ref, v_ref, ck_ref, cv_ref, eb_ref, o_ctx_ref, o_ref, bias_s, s_scr, p_scr, l_scr):
    del o_ctx_ref
    left, right = _head_masks()

    @pl.when(pl.program_id(1) == 0)
    def _():
        neg = jnp.full((GRID_W, LANES), NEG_INF, F32)
        for hh in range(2):
            for r in range(GRID_ROWS):
                r0 = min(max(r - WIN_H // 2, 0), GRID_ROWS - WIN_H)
                for j in range(GRID_ROWS // 2):
                    parts = []
                    for kr in (2 * j, 2 * j + 1):
                        parts.append(eb_ref[hh, kr - r + WIN_H - 1] if r0 <= kr < r0 + WIN_H else None)
                    if parts[0] is None and parts[1] is None:
                        val = neg
                    else:
                        val = jnp.where(left, neg if parts[0] is None else parts[0],
                                        neg if parts[1] is None else parts[1])
                    bias_s[hh, r * GRID_W:(r + 1) * GRID_W, j * LANES:(j + 1) * LANES] = val

    ck = jnp.concatenate([ck_ref[0, 0, 0], ck_ref[0, 0, 1]], axis=0).astype(BF16)
    cv = jnp.concatenate([cv_ref[0, 0, 0], cv_ref[0, 0, 1]], axis=0).astype(BF16)
    qrows = _LAT_Q_BLOCK_ROWS * GRID_W
    units = [(qb, hh) for qb in range(len(_LAT_KEY_ROWS)) for hh in range(2)]

    def refs(u):
        nk = (_LAT_KEY_ROWS[units[u][0]][1] - _LAT_KEY_ROWS[units[u][0]][0]) * GRID_W
        width = nk + PAST_LEN
        return nk, s_scr.at[u % 2, :, :width], p_scr.at[u % 2, :, :width], l_scr.at[u % 2]

    def scores(u):
        qb, hh = units[u]
        kr0, kr1 = _LAT_KEY_ROWS[qb]
        qs, ks = slice(qb * qrows, (qb + 1) * qrows), slice(kr0 * GRID_W, kr1 * GRID_W)
        nk, s_ref, _, _ = refs(u)
        q2 = q_ref[qs, :]
        qh = jnp.where(right if hh else left, q2, jnp.zeros_like(q2))
        s_ref[:, :nk] = _dot_nt(qh, k_ref[ks, :]) + bias_s[hh, qs, ks]
        s_ref[:, nk:] = jnp.dot(qh, ck, preferred_element_type=F32)

    def weighted_values(u):
        qb, _ = units[u]
        kr0, kr1 = _LAT_KEY_ROWS[qb]
        nk, _, p_ref, l_ref = refs(u)
        o2 = (jnp.dot(p_ref[:, :nk], v_ref[kr0 * GRID_W:kr1 * GRID_W, :], preferred_element_type=F32)
              + _dot_nt(p_ref[:, nk:], cv))
        return o2 / l_ref[...]

    scores(0)
    halves = []
    for u in range(len(units)):
        if u + 1 < len(units):
            scores(u + 1)
        _, s_ref, p_ref, l_ref = refs(u)
        _softmax_rows(s_ref, p_ref, l_ref)
        halves.append(weighted_values(u))
        if len(halves) == 2:
            qb = units[u][0]
            o_ref[qb * qrows:(qb + 1) * qrows, :] = jnp.where(left, halves[0], halves[1]).astype(BF16)
            halves = []


def _lat_attention(qkv, cache_kt, cache_vt, ebias, o_buf):
    row0 = T_CTX // DEC_SEQ
    qrows = _LAT_Q_BLOCK_ROWS * GRID_W
    max_keys = max(k1 - k0 for k0, k1 in _LAT_KEY_ROWS) * GRID_W + PAST_LEN
    cache_spec = pl.BlockSpec((1, 1, 2, HEAD_DIM, PAST_LEN), lambda p, b: (b, 0, p, 0, 0))
    return pl.pallas_call(
        _lat_attn_kernel,
        grid=(HEAD_PAIRS, DEC_BATCH),
        in_specs=[
            pl.BlockSpec((DEC_SEQ, LANES), lambda p, b: (row0 + b, p)),
            pl.BlockSpec((DEC_SEQ, LANES), lambda p, b: (row0 + b, HEAD_PAIRS + p)),
            pl.BlockSpec((DEC_SEQ, LANES), lambda p, b: (row0 + b, 2 * HEAD_PAIRS + p)),
            cache_spec,
            cache_spec,
            pl.BlockSpec((2, 2 * WIN_H - 1, GRID_W, LANES), lambda p, b: (p, 0, 0, 0)),
            pl.BlockSpec(memory_space=pl.ANY),
        ],
        out_specs=pl.BlockSpec((DEC_SEQ, LANES), lambda p, b: (row0 + b, p)),
        out_shape=jax.ShapeDtypeStruct((T_ALL, D_MODEL), BF16),
        scratch_shapes=[pltpu.VMEM((2, DEC_SEQ, DEC_SEQ), F32),
                        pltpu.VMEM((2, qrows, max_keys), F32),
                        pltpu.VMEM((2, qrows, max_keys), BF16),
                        pltpu.VMEM((2, qrows, 1), F32)],
        input_output_aliases={6: 0},
        compiler_params=_params("arbitrary", "arbitrary"),
        name="lat_attention",
    )(qkv, qkv, qkv, cache_kt, cache_vt, ebias, o_buf)


def _bias_kernel(r_ref, o_ref):
    q = lax.broadcasted_iota(jnp.int32, (GRID_W, LANES), 0)
    k = lax.broadcasted_iota(jnp.int32, (GRID_W, LANES), 1) & (GRID_W - 1)
    start = jnp.clip(q - WIN_W // 2, 0, GRID_W - WIN_W)
    col_ok = (k >= start) & (k < start + WIN_W)
    for y in range(2 * WIN_H - 1):
        rows = jnp.broadcast_to(r_ref[0, y:y + 1, :], (GRID_W, LANES))
        rows = pltpu.roll(rows, LANES - (WIN_W - 1), 1, stride=1, stride_axis=0)
        o_ref[0, y] = jnp.where(col_ok, rows, NEG_INF)


def _expanded_bias(rel_bias):
    rel_h, rel_w = 2 * WIN_H - 1, 2 * WIN_W - 1
    r = jnp.pad(rel_bias.astype(F32), ((0, 0), (0, 0), (0, GRID_W - rel_w)))
    r = jnp.concatenate([r, r], axis=2)
    return pl.pallas_call(
        _bias_kernel,
        grid=(N_HEADS,),
        in_specs=[pl.BlockSpec((1, rel_h, LANES), lambda h: (h, 0, 0))],
        out_specs=pl.BlockSpec((1, rel_h, GRID_W, LANES), lambda h: (h, 0, 0, 0)),
        out_shape=jax.ShapeDtypeStruct((N_HEADS, rel_h, GRID_W, LANES), F32),
        compiler_params=_params("arbitrary"),
        name="rel_bias_tiles",
    )(r)


def _wo_kernel(o_ref, x_ref, mods_ref, w_ref, lng_ref, lnb_ref, w2_ref, x1_ref, h2p_ref, logit_ref, w_s):
    @pl.when(pl.program_id(0) == 0)
    def _():
        w_s[...] = w_ref[0].astype(BF16)

    subs = [slice(r0, r0 + TILE) for r0 in range(0, WO_TILE, TILE)]
    mix = jnp.dot(o_ref[subs[0], :], w_s[...], preferred_element_type=F32)
    for j, rs in enumerate(subs):
        nxt = jnp.dot(o_ref[subs[j + 1], :], w_s[...], preferred_element_type=F32) if j + 1 < len(subs) else None
        _epilogue(x_ref[rs, :], mix, mods_ref[0], lng_ref[...], lnb_ref[...], w2_ref,
                  x1_ref.at[rs, :], h2p_ref.at[rs, :], logit_ref.at[:, rs])
        mix = nxt


def _attn_out(o, x, mods, w_o, lng, lnb, w2):
    return pl.pallas_call(
        _wo_kernel,
        grid=(T_ALL // WO_TILE,),
        in_specs=[
            pl.BlockSpec((WO_TILE, D_MODEL), lambda t: (t, 0)),
            pl.BlockSpec((WO_TILE, D_MODEL), lambda t: (t, 0)),
            _mods_spec(WO_TILE),
            pl.BlockSpec((1, D_MODEL, D_MODEL), lambda t: (0, 0, 0)),
        ] + _EPI_IN_SPECS,
        out_specs=[
            pl.BlockSpec((WO_TILE, D_MODEL), lambda t: (t, 0)),
            pl.BlockSpec((WO_TILE, D_PACK), lambda t: (t, 0)),
            pl.BlockSpec((N_EXPERTS, WO_TILE), lambda t: (0, t)),
        ],
        out_shape=_EPI_OUT_SHAPE,
        scratch_shapes=[pltpu.VMEM((D_MODEL, D_MODEL), BF16)],
        compiler_params=_params("arbitrary"),
        name="attn_out",
    )(o, x, mods, w_o, lng, lnb, w2)


def kernel(x_prompt, x_sample, cache_k, cache_v, c, c_ctx, w_ada, b_ada, ln1_g, ln1_b, ln2_g, ln2_b,
           pool_w, pool_scale, w_qkv, w_o, rel_bias, w_router, b_router, w_gate, w_up, w_down):
    mods0, mods1 = _ada(c_ctx, c, w_ada, b_ada, 0), _ada(c_ctx, c, w_ada, b_ada, 1)
    wrt = w_router.T
    wrt_hi = wrt.astype(BF16)
    w2 = jnp.concatenate([wrt_hi, (wrt - wrt_hi.astype(F32)).astype(BF16)], axis=0)

    x, h2p, logits = _pool_layer(x_prompt.reshape(T_CTX, D_MODEL), x_sample.reshape(T_LAT, D_MODEL), mods0,
                                 pool_w, pool_scale[0:1], ln1_g[0:1], ln1_b[0:1], w2)
    yg, wsel = _moe_rows_of_tokens(h2p, logits, b_router, 0, w_gate, w_up, w_down)

    x, qkv, new_kt, new_vt = _qkv(x, yg, wsel, mods0, ln2_g[0:1], ln2_b[0:1], mods1, w_qkv)
    o = _lat_attention(qkv, cache_k.transpose(0, 1, 2, 4, 3), cache_v.transpose(0, 1, 2, 4, 3),
                       _expanded_bias(rel_bias[0]), _ctx_attention(qkv))
    x, h2p, logits = _attn_out(o, x, mods1, w_o, ln1_g[1:2], ln1_b[1:2], w2)
    yg, wsel = _moe_rows_of_tokens(h2p, logits, b_router, 1, w_gate, w_up, w_down)
    y_ctx, y_lat = [_combine(x, yg, wsel, mods1, ln2_g[1:2], ln2_b[1:2], t0, n) for t0, n in SEGMENTS]
    return (y_ctx.reshape(BATCH, SEQ, D_MODEL), y_lat.reshape(DEC_BATCH, DEC_SEQ, D_MODEL),
            new_kt.transpose(0, 1, 2, 4, 3), new_vt.transpose(0, 1, 2, 4, 3))
```
